```python
import math
import jax, jax.numpy as jnp
from jax import lax
import numpy as np

D_MODEL = 1024
BATCH = 16
SEQ = 256
DEPTH = 1
DEC_BATCH = 2
DEC_SEQ = 1024
PAST_LEN = 512

GRID_W = 64
N_POOL_GROUPS = 4
POOL_GROUP_DIM = 128
POOL_DIM = N_POOL_GROUPS * POOL_GROUP_DIM
POOL_WINDOWS = (2, 4, 8, 16)
N_HEADS = 8
HEAD_DIM = 64
ATTN_DIM = N_HEADS * HEAD_DIM
WIN_R = 8
WIN_C = 16
N_EXPERTS = 32
TOP_K = 4
D_FF = 1024
SWIGLU_LIMIT = 7.0
SWIGLU_ALPHA = 1.702
N_MOD = 6
Q_BLOCK = 128
RMS_EPS = 1e-6
NEG_INF = -1e30
ATTN_SCALE = HEAD_DIM ** -0.5
IN_DIM = POOL_DIM + 3 * ATTN_DIM + 2 * D_MODEL

kernel_name = "hybrid_pool_natten_moe_diffusion_step"


def rmsnorm(x, gain):
    x32 = x.astype(jnp.float32)
    y = x32 * lax.rsqrt(jnp.mean(x32 * x32, axis=-1, keepdims=True) + RMS_EPS)
    return (y * gain.astype(jnp.float32)).astype(x.dtype)


def modulation(cvec, w_ada, b_ada):
    return jnp.split(jax.nn.silu(cvec) @ w_ada + b_ada, N_MOD, axis=-1)


def pool_mixer(u, w_pool, pool_scale):
    B, T, _ = u.shape
    u32 = u.reshape(B, T, N_POOL_GROUPS, POOL_GROUP_DIM).astype(jnp.float32)
    cs = jnp.concatenate([jnp.zeros_like(u32[:, :1]), lax.cumsum(u32, axis=1)], axis=1)
    t = jnp.arange(T)
    outs = []
    for g, w in enumerate(POOL_WINDOWS):
        lo = jnp.clip(t - w // 2, 0, T)
        hi = jnp.clip(t - w // 2 + w, 0, T)
        s = cs[:, hi, g] - cs[:, lo, g]
        cnt = (hi - lo).astype(jnp.float32)[None, :, None]
        outs.append(s / cnt - u32[:, :, g])
    p = jnp.stack(outs, axis=2).astype(u.dtype)
    y = jnp.einsum('btgc,gcd->btgd', p, w_pool)
    return y.reshape(B, T, POOL_DIM) * pool_scale


def split_heads(z):
    B, T, _ = z.shape
    return z.reshape(B, T, N_HEADS, HEAD_DIM).transpose(0, 2, 1, 3)


def merge_heads(o):
    B, H, T, dh = o.shape
    return o.transpose(0, 2, 1, 3).reshape(B, T, H * dh)


def ctx_attention(q, k, v):
    B, H, L, dh = q.shape
    nb = L // Q_BLOCK
    qb = q.reshape(B, H, nb, Q_BLOCK, dh).transpose(2, 0, 1, 3, 4)

    def block(qi):
        s = jnp.einsum('bhqd,bhkd->bhqk', qi, k).astype(jnp.float32) * ATTN_SCALE
        p = jax.nn.softmax(s, axis=-1).astype(v.dtype)
        return jnp.einsum('bhqk,bhkd->bhqd', p, v)

    o = lax.map(block, qb)
    return o.transpose(1, 2, 0, 3, 4).reshape(B, H, L, dh)


def na_latent_attention(q, k, v, kc, vc, rpb):
    B, H, T, dh = q.shape
    rows = T // GRID_W
    kr = min(WIN_R, rows)
    r = jnp.arange(rows)
    row_start = jnp.clip(r - kr // 2, 0, rows - kr)
    row_idx = row_start[:, None] + jnp.arange(kr)[None, :]
    col = jnp.arange(GRID_W)
    col_start = jnp.clip(col - WIN_C // 2, 0, GRID_W - WIN_C)
    nk = kr * GRID_W

    qg = q.reshape(B, H, rows, GRID_W, dh)
    kg = k.reshape(B, H, rows, GRID_W, dh)[:, :, row_idx].reshape(B, H, rows, nk, dh)
    vg = v.reshape(B, H, rows, GRID_W, dh)[:, :, row_idx].reshape(B, H, rows, nk, dh)

    key_col = jnp.tile(col, kr)
    key_row = row_idx[:, jnp.repeat(jnp.arange(kr), GRID_W)]
    dr = key_row - r[:, None]
    dc = key_col[None, :] - col[:, None]
    in_win = (key_col[None, :] >= col_start[:, None]) & (key_col[None, :] < col_start[:, None] + WIN_C)
    bias = rpb[:, (dr + WIN_R - 1)[:, None, :],
               jnp.clip(dc + WIN_C - 1, 0, 2 * WIN_C - 2)[None, :, :]].astype(jnp.float32)

    s_loc = jnp.einsum('bhrqd,bhrkd->bhrqk', qg, kg).astype(jnp.float32) * ATTN_SCALE + bias[None]
    s_loc = jnp.where(in_win[None, None, None], s_loc, NEG_INF)
    s_ctx = jnp.einsum('bhrqd,bhld->bhrql', qg, kc).astype(jnp.float32) * ATTN_SCALE
    p = jax.nn.softmax(jnp.concatenate([s_loc, s_ctx], axis=-1), axis=-1).astype(v.dtype)
    o = (jnp.einsum('bhrqk,bhrkd->bhrqd', p[..., :nk], vg)
         + jnp.einsum('bhrql,bhld->bhrqd', p[..., nk:], vc))
    return o.reshape(B, H, T, dh)


def mixing_branches(h, w_in, w_pool, pool_scale, w_pa, w_pb, w_out, attend):
    z = h @ w_in
    u_pool, q, k, v, ga, gb = jnp.split(
        z, [POOL_DIM, POOL_DIM + ATTN_DIM, POOL_DIM + 2 * ATTN_DIM,
            POOL_DIM + 3 * ATTN_DIM, POOL_DIM + 3 * ATTN_DIM + D_MODEL], axis=-1)
    a = pool_mixer(u_pool, w_pool, pool_scale) @ w_pa
    kh, vh = split_heads(k), split_heads(v)
    o = merge_heads(attend(split_heads(q), kh, vh)) @ w_pb
    merged = jax.nn.sigmoid(ga) * a + jax.nn.sigmoid(gb) * o
    return merged @ w_out, kh, vh


def moe(h, w_router, b_router, w_up, b_up, w_down, b_down):
    shp = h.shape
    x = h.reshape(-1, shp[-1])
    logits = (x @ w_router + b_router).astype(jnp.float32)
    top_vals, top_idx = lax.top_k(logits, TOP_K)
    wts = jax.nn.softmax(top_vals, axis=-1)
    combine = jnp.sum(jax.nn.one_hot(top_idx, N_EXPERTS, dtype=jnp.float32) * wts[..., None], axis=1)
    combine = combine.astype(x.dtype)

    def expert(acc, xs):
        wu, bu, wd, bd, g = xs
        gu = x @ wu + bu
        gate, up = gu[:, :D_FF], gu[:, D_FF:]
        gate = jnp.minimum(gate, SWIGLU_LIMIT)
        up = jnp.clip(up, -SWIGLU_LIMIT, SWIGLU_LIMIT)
        glu = gate * jax.nn.sigmoid(SWIGLU_ALPHA * gate)
        y = ((up + 1.0) * glu) @ wd + bd
        return acc + g[:, None] * y, None

    acc, _ = lax.scan(expert, jnp.zeros_like(x), (w_up, b_up, w_down, b_down, combine.T))
    return acc.reshape(shp)


def setup_inputs(seed: int = 0) -> dict:
    key = jax.random.key(seed)
    ks = jax.random.split(key, 32)
    n = jax.random.normal
    f = jnp.float32
    D = D_MODEL
    return {
        "x_prompt": n(ks[0], (BATCH, SEQ, D), f),
        "x_sample": n(ks[1], (DEC_BATCH, DEC_SEQ, D), f),
        "cache_k": n(ks[2], (DEC_BATCH, DEPTH, N_HEADS, PAST_LEN, HEAD_DIM), f),
        "cache_v": n(ks[3], (DEC_BATCH, DEPTH, N_HEADS, PAST_LEN, HEAD_DIM), f),
        "c": n(ks[4], (DEC_BATCH, D), f),
        "c_ctx": n(ks[5], (D,), f),
        "w_ada": n(ks[6], (DEPTH, D, N_MOD * D), f) * (0.5 * D ** -0.5),
        "b_ada": n(ks[7], (DEPTH, N_MOD * D), f) * 0.02,
        "g_mix": 1.0 + 0.05 * n(ks[8], (DEPTH, D), f),
        "w_in": n(ks[9], (DEPTH, D, IN_DIM), f) * D ** -0.5,
        "w_pool": n(ks[10], (DEPTH, N_POOL_GROUPS, POOL_GROUP_DIM, POOL_GROUP_DIM), f) * POOL_GROUP_DIM ** -0.5,
        "pool_scale": 1.0 + 0.05 * n(ks[11], (DEPTH, POOL_DIM), f),
        "w_pa": n(ks[12], (DEPTH, POOL_DIM, D), f) * POOL_DIM ** -0.5,
        "w_pb": n(ks[13], (DEPTH, ATTN_DIM, D), f) * ATTN_DIM ** -0.5,
        "rpb": n(ks[14], (DEPTH, N_HEADS, 2 * WIN_R - 1, 2 * WIN_C - 1), f) * 0.2,
        "w_out": n(ks[15], (DEPTH, D, D), f) * D ** -0.5,
        "g_ffn": 1.0 + 0.05 * n(ks[16], (DEPTH, D), f),
        "w_router": n(ks[17], (DEPTH, D, N_EXPERTS), f) * D ** -0.5,
        "b_router": n(ks[18], (DEPTH, N_EXPERTS), f) * 0.01,
        "w_up": n(ks[19], (DEPTH, N_EXPERTS, D, 2 * D_FF), f) * D ** -0.5,
        "b_up": n(ks[20], (DEPTH, N_EXPERTS, 2 * D_FF), f) * 0.02,
        "w_down": n(ks[21], (DEPTH, N_EXPERTS, D_FF, D), f) * D_FF ** -0.5,
        "b_down": n(ks[22], (DEPTH, N_EXPERTS, D), f) * 0.02,
        "g_final": 1.0 + 0.05 * n(ks[23], (D,), f),
    }


def reference(x_prompt, x_sample, cache_k, cache_v, c, c_ctx, w_ada, b_ada, g_mix, w_in, w_pool,
              pool_scale, w_pa, w_pb, rpb, w_out, g_ffn, w_router, b_router, w_up, b_up, w_down,
              b_down, g_final):
    xp = x_prompt
    xs = x_sample
    new_k, new_v = [], []
    for l in range(DEPTH):
        moe_args = (w_router[l], b_router[l], w_up[l], b_up[l], w_down[l], b_down[l])
        mix_args = (w_in[l], w_pool[l], pool_scale[l], w_pa[l], w_pb[l], w_out[l])

        sh_m, sc_m, gt_m, sh_f, sc_f, gt_f = modulation(c_ctx[None, None, :], w_ada[l], b_ada[l])
        h = rmsnorm(xp, g_mix[l]) * (1.0 + sc_m) + sh_m
        mix, kh, vh = mixing_branches(h, *mix_args, ctx_attention)
        xp = xp + gt_m * mix
        h = rmsnorm(xp, g_ffn[l]) * (1.0 + sc_f) + sh_f
        xp = xp + gt_f * moe(h, *moe_args)
        new_k.append(kh)
        new_v.append(vh)

        kc, vc, rpb_l = cache_k[:, l], cache_v[:, l], rpb[l]
        sh_m, sc_m, gt_m, sh_f, sc_f, gt_f = modulation(c[:, None, :], w_ada[l], b_ada[l])
        h = rmsnorm(xs, g_mix[l]) * (1.0 + sc_m) + sh_m
        mix, _, _ = mixing_branches(
            h, *mix_args, lambda q, k, v: na_latent_attention(q, k, v, kc, vc, rpb_l))
        xs = xs + gt_m * mix
        h = rmsnorm(xs, g_ffn[l]) * (1.0 + sc_f) + sh_f
        xs = xs + gt_f * moe(h, *moe_args)

    y_prompt = rmsnorm(xp, g_final)
    y_sample = rmsnorm(xs, g_final)
    new_cache_k = jnp.stack(new_k, axis=1)
    new_cache_v = jnp.stack(new_v, axis=1)
    return (y_prompt, y_sample, new_cache_k, new_cache_v)
```

```python
import functools

import jax
import jax.numpy as jnp
import numpy as np
from jax import lax
from jax.experimental import pallas as pl
from jax.experimental.pallas import tpu as pltpu

F32 = jnp.float32
BF16 = jnp.bfloat16

D_MODEL = 1024
SEQ = 256
DEC_SEQ = 1024
GRID_W = 64
ROWS = DEC_SEQ // GRID_W
N_HEADS = 8
HEAD_DIM = 64
N_PAIRS = N_HEADS // 2
PAIR_W = 2 * HEAD_DIM
PAST_LEN = 512
POOL_DIM = 512
POOL_WINDOWS = (2, 4, 8, 16)
POOL_GROUP_DIM = 128
ATTN_DIM = 512
WIN_R = 8
WIN_C = 16
N_EXPERTS = 32
TOP_K = 4
D_FF = 1024
SWIGLU_LIMIT = 7.0
SWIGLU_ALPHA = 1.702
N_MOD = 6
RMS_EPS = 1e-6
NEG_INF = -1e30
ATTN_SCALE = HEAD_DIM ** -0.5
IN_DIM = POOL_DIM + 3 * ATTN_DIM + 2 * D_MODEL

TOK_BLOCK = 512
CHUNK = 16
BLOCK_CAP = TOK_BLOCK * TOP_K + N_EXPERTS * CHUNK
EXPERT_ROW_TILE = 256
V7X_VMEM_LIMIT = 60 * 1024 * 1024

_N_CTX_BLOCKS = 8
_N_LAT_BLOCKS = 4
_N_BLOCKS = _N_CTX_BLOCKS + _N_LAT_BLOCKS

_NT = (((1,), (1,)), ((), ()))


def _dot(a, b):
    return jnp.dot(a, b, preferred_element_type=F32)


def _dot_nt(a, b):
    return lax.dot_general(a, b, _NT, preferred_element_type=F32)


def _sigmoid(x):
    return 1.0 / (1.0 + jnp.exp(-x))


def _norm_mod(x, gain, scale, shift):
    ms = jnp.mean(x * x, axis=-1, keepdims=True)
    return (x * lax.rsqrt(ms + RMS_EPS) * gain) * (1.0 + scale) + shift


def _const_spec(shape):
    zeros = (0,) * len(shape)
    return pl.BlockSpec(shape, lambda *_: zeros, pipeline_mode=pl.Buffered(1))


MOD_COLS = 1536


def _mod_kernel(c_ref, w_ref, b_ref, o_ref):
    c = c_ref[...]
    s = c * _sigmoid(c)
    o_ref[...] = jnp.dot(s, w_ref[...], preferred_element_type=F32,
                         precision=lax.Precision.HIGHEST) + b_ref[...]


def _modulation(cmat, w_ada, b_ada):
    n = w_ada.shape[1]
    return pl.pallas_call(
        _mod_kernel,
        grid=(n // MOD_COLS,),
        in_specs=[pl.BlockSpec((8, D_MODEL), lambda i: (0, 0)),
                  pl.BlockSpec((D_MODEL, MOD_COLS), lambda i: (0, i)),
                  pl.BlockSpec((1, MOD_COLS), lambda i: (0, i))],
        out_specs=pl.BlockSpec((8, MOD_COLS), lambda i: (0, i)),
        out_shape=jax.ShapeDtypeStruct((8, n), F32),
        name="modulation",
    )(cmat, w_ada, b_ada.reshape(1, n))


def _pool_mix(u, pos, seq):
    n = u.shape[0]

    def down(x, d):
        return jnp.where(pos >= d, pltpu.roll(x, d, 0), 0.0)

    def up(x, d):
        return jnp.where(pos < seq - d, pltpu.roll(x, n - d, 0), 0.0)

    return down, up


def _pool_group(u, pos, seq, w):
    down, up = _pool_mix(u, pos, seq)
    hw = w // 2
    back = u
    fwd = u
    d = 1
    while d < hw:
        back = back + down(back, d)
        fwd = fwd + up(fwd, d)
        d *= 2
    s = down(back, 1) + fwd
    posf = pos.astype(F32)
    cnt = jnp.minimum(posf + hw, float(seq)) - jnp.maximum(posf - hw, 0.0)
    return s / cnt - u


def _mixer_front(x_ref, mod_ref, gmix_ref, win_ref, hb_s, u_s, q_s, k_s, v_s):
    x = x_ref[...]
    shift, scale = mod_ref[0, 0:1, :], mod_ref[0, 1:2, :]
    hb = _norm_mod(x, gmix_ref[...], scale, shift).astype(BF16)
    hb_s[...] = hb
    u_s[...] = _dot(hb, win_ref[:, 0:POOL_DIM])
    for dst, base in ((q_s, POOL_DIM), (k_s, POOL_DIM + ATTN_DIM), (v_s, POOL_DIM + 2 * ATTN_DIM)):
        z = _dot(hb, win_ref[:, base:base + ATTN_DIM])
        for g in range(N_PAIRS):
            dst[g] = z[:, g * PAIR_W:(g + 1) * PAIR_W].astype(dst.dtype)


def _mixer_back(x_ref, mod_ref, win_ref, wpool_ref, ps_ref, wpa_ref, wout_ref, x1_ref,
                hb_s, u_s, ob_s, pg_s, seq):
    tile = x_ref.shape[0]
    gate = mod_ref[0, 2:3, :]
    pos = lax.broadcasted_iota(jnp.int32, (tile, 1), 0) % seq
    for g, w in enumerate(POOL_WINDOWS):
        cols = slice(g * POOL_GROUP_DIM, (g + 1) * POOL_GROUP_DIM)
        pg_s[:, cols] = _pool_group(u_s[:, cols], pos, seq, w).astype(BF16)
    for c in range(tile // TOK_BLOCK):
        rows = slice(c * TOK_BLOCK, (c + 1) * TOK_BLOCK)
        a = jnp.zeros((TOK_BLOCK, D_MODEL), F32)
        for g in range(len(POOL_WINDOWS)):
            cols = slice(g * POOL_GROUP_DIM, (g + 1) * POOL_GROUP_DIM)
            yg = _dot(pg_s[rows, cols], wpool_ref[g]) * ps_ref[:, cols]
            a = a + _dot(yg.astype(BF16), wpa_ref[cols, :])
        gab = _dot(hb_s[rows, :], win_ref[:, POOL_DIM + 3 * ATTN_DIM:IN_DIM])
        merged = _sigmoid(gab[:, :D_MODEL]) * a + _sigmoid(gab[:, D_MODEL:]) * ob_s[rows, :]
        mix = _dot(merged.astype(BF16), wout_ref[...])
        x1_ref[rows, :] = x_ref[rows, :] + gate * mix


def _ctx_mixer_kernel(x_ref, mod_ref, gmix_ref, win_ref, wpool_ref, ps_ref, wpa_ref, wpb_ref, wout_ref,
                      x1_ref, ko_ref, vo_ref, hb_s, u_s, q_s, k_s, v_s, ob_s, pg_s):
    _mixer_front(x_ref, mod_ref, gmix_ref, win_ref, hb_s, u_s, q_s, k_s, v_s)
    even = lax.broadcasted_iota(jnp.int32, (1, PAIR_W), 1) < HEAD_DIM
    tile = x_ref.shape[0]
    for s in range(tile // SEQ):
        rows = slice(s * SEQ, (s + 1) * SEQ)
        ob = jnp.zeros((SEQ, D_MODEL), F32)
        for g in range(N_PAIRS):
            q2, k2, v2 = q_s[g, rows, :], k_s[g, rows, :], v_s[g, rows, :]
            ko_ref[s, 0, 2 * g] = k2[:, :HEAD_DIM]
            ko_ref[s, 0, 2 * g + 1] = k2[:, HEAD_DIM:]
            vo_ref[s, 0, 2 * g] = v2[:, :HEAD_DIM]
            vo_ref[s, 0, 2 * g + 1] = v2[:, HEAD_DIM:]
            kb, vb = k2.astype(BF16), v2.astype(BF16)
            outs = []
            for par in range(2):
                qm = jnp.where(even if par == 0 else jnp.logical_not(even), q2, 0.0).astype(BF16)
                sc = _dot_nt(qm, kb) * ATTN_SCALE
                m = jnp.max(sc, axis=-1, keepdims=True)
                p = jnp.exp(sc - m)
                l = jnp.sum(p, axis=-1, keepdims=True)
                outs.append(_dot(p.astype(BF16), vb) / l)
            o2 = jnp.where(even, outs[0], outs[1])
            ob = ob + _dot(o2.astype(BF16), wpb_ref[g])
        ob_s[rows, :] = ob
    _mixer_back(x_ref, mod_ref, win_ref, wpool_ref, ps_ref, wpa_ref, wout_ref, x1_ref,
                hb_s, u_s, ob_s, pg_s, SEQ)


def _row_window(r):
    rs = min(max(r - WIN_R // 2, 0), ROWS - WIN_R)
    return rs, rs - r + WIN_R - 1


def _lat_mixer_kernel(x_ref, mod_ref, gmix_ref, win_ref, wpool_ref, ps_ref, wpa_ref, wpb_ref, wout_ref,
                      kc_ref, vc_ref, tb_ref, x1_ref, hb_s, u_s, q_s, k_s, v_s, ob_s, pg_s):
    _mixer_front(x_ref, mod_ref, gmix_ref, win_ref, hb_s, u_s, q_s, k_s, v_s)
    even = lax.broadcasted_iota(jnp.int32, (1, PAIR_W), 1) < HEAD_DIM
    nk = WIN_R * GRID_W
    ob_s[...] = jnp.zeros_like(ob_s)

    def pair_body(g, carry):
        q2 = q_s[g]
        kb, vb = k_s[g].astype(BF16), v_s[g].astype(BF16)
        kcb, vcb = kc_ref[0, g].astype(BF16), vc_ref[0, g].astype(BF16)
        outs = []
        for par in range(2):
            qm = jnp.where(even if par == 0 else jnp.logical_not(even), q2, 0.0).astype(BF16)
            s_ctx = _dot_nt(qm, kcb) * ATTN_SCALE
            slabs = []
            for r in range(ROWS):
                rs, rho = _row_window(r)
                bias = tb_ref[2 * g + par, rho % 2, :, (rho - rho % 2) * GRID_W:(rho - rho % 2) * GRID_W + nk]
                sl = _dot_nt(qm[r * GRID_W:(r + 1) * GRID_W, :], kb[rs * GRID_W:rs * GRID_W + nk, :])
                slabs.append(sl * ATTN_SCALE + bias)
            s_loc = jnp.concatenate(slabs, axis=0)
            m = jnp.maximum(jnp.max(s_loc, axis=-1, keepdims=True), jnp.max(s_ctx, axis=-1, keepdims=True))
            p_loc = jnp.exp(s_loc - m)
            p_ctx = jnp.exp(s_ctx - m)
            l = jnp.sum(p_loc, axis=-1, keepdims=True) + jnp.sum(p_ctx, axis=-1, keepdims=True)
            p_locb = p_loc.astype(BF16)
            o_rows = []
            for r in range(ROWS):
                rs, _ = _row_window(r)
                o_rows.append(_dot(p_locb[r * GRID_W:(r + 1) * GRID_W, :], vb[rs * GRID_W:rs * GRID_W + nk, :]))
            o = jnp.concatenate(o_rows, axis=0) + _dot(p_ctx.astype(BF16), vcb)
            outs.append(o / l)
        o2 = jnp.where(even, outs[0], outs[1])
        ob_s[...] += _dot(o2.astype(BF16), wpb_ref[g])
        return carry

    lax.fori_loop(0, N_PAIRS, pair_body, 0)
    _mixer_back(x_ref, mod_ref, win_ref, wpool_ref, ps_ref, wpa_ref, wout_ref, x1_ref,
                hb_s, u_s, ob_s, pg_s, DEC_SEQ)


def _mixer_scratch(tile, kv_dtype):
    return [pltpu.VMEM((tile, D_MODEL), BF16),
            pltpu.VMEM((tile, POOL_DIM), F32),
            pltpu.VMEM((N_PAIRS, tile, PAIR_W), BF16),
            pltpu.VMEM((N_PAIRS, tile, PAIR_W), kv_dtype),
            pltpu.VMEM((N_PAIRS, tile, PAIR_W), kv_dtype),
            pltpu.VMEM((tile, D_MODEL), F32),
            pltpu.VMEM((tile, POOL_DIM), BF16)]


def _weight_specs():
    return [_const_spec((1, D_MODEL)),
            _const_spec((D_MODEL, IN_DIM)),
            _const_spec((len(POOL_WINDOWS), POOL_GROUP_DIM, POOL_GROUP_DIM)),
            _const_spec((1, POOL_DIM)),
            _const_spec((POOL_DIM, D_MODEL)),
            _const_spec((N_PAIRS, PAIR_W, D_MODEL)),
            _const_spec((D_MODEL, D_MODEL))]


def _ctx_mixer(x, modv, weights):
    n = x.shape[0]
    nseq = TOK_BLOCK // SEQ
    cache = jax.ShapeDtypeStruct((n // SEQ, 1, N_HEADS, SEQ, HEAD_DIM), F32)
    cache_spec = pl.BlockSpec((nseq, 1, N_HEADS, SEQ, HEAD_DIM), lambda i: (i, 0, 0, 0, 0))
    return pl.pallas_call(
        _ctx_mixer_kernel,
        grid=(n // TOK_BLOCK,),
        in_specs=[pl.BlockSpec((TOK_BLOCK, D_MODEL), lambda i: (i, 0)),
                  pl.BlockSpec((1, N_MOD, D_MODEL), lambda i: (0, 0, 0))] + _weight_specs(),
        out_specs=[pl.BlockSpec((TOK_BLOCK, D_MODEL), lambda i: (i, 0)), cache_spec, cache_spec],
        out_shape=[jax.ShapeDtypeStruct((n, D_MODEL), F32), cache, cache],
        scratch_shapes=_mixer_scratch(TOK_BLOCK, F32),
        compiler_params=pltpu.CompilerParams(dimension_semantics=("arbitrary",),
                                             vmem_limit_bytes=V7X_VMEM_LIMIT),
        name="ctx_mixer",
    )(x, modv, *weights)


def _lat_mixer(x, modv, weights, kc, vc, tb):
    n = x.shape[0]
    return pl.pallas_call(
        _lat_mixer_kernel,
        grid=(n // DEC_SEQ,),
        in_specs=[pl.BlockSpec((DEC_SEQ, D_MODEL), lambda i: (i, 0)),
                  pl.BlockSpec((1, N_MOD, D_MODEL), lambda i: (i + 1, 0, 0))] + _weight_specs() + [
                  pl.BlockSpec((1, N_PAIRS, PAST_LEN, PAIR_W), lambda i: (i, 0, 0, 0)),
                  pl.BlockSpec((1, N_PAIRS, PAST_LEN, PAIR_W), lambda i: (i, 0, 0, 0)),
                  _const_spec((N_HEADS, 2, GRID_W, ROWS * GRID_W))],
        out_specs=pl.BlockSpec((DEC_SEQ, D_MODEL), lambda i: (i, 0)),
        out_shape=jax.ShapeDtypeStruct((n, D_MODEL), F32),
        scratch_shapes=_mixer_scratch(DEC_SEQ, BF16),
        compiler_params=pltpu.CompilerParams(dimension_semantics=("arbitrary",),
                                             vmem_limit_bytes=V7X_VMEM_LIMIT),
        name="lat_mixer",
    )(x, modv, *weights, kc, vc, tb)


def _bias_tables(rpb):
    col = np.arange(GRID_W)
    dc = col[None, :] - col[:, None]
    cs = np.clip(col - WIN_C // 2, 0, GRID_W - WIN_C)
    in_win = (col[None, :] >= cs[:, None]) & (col[None, :] < cs[:, None] + WIN_C)
    idx = np.clip(dc + WIN_C - 1, 0, 2 * WIN_C - 2)
    tm = jnp.where(in_win[None, None], rpb[:, :, idx].astype(F32), NEG_INF)
    neg = jnp.full((N_HEADS, 1, GRID_W, GRID_W), NEG_INF, F32)
    even = jnp.concatenate([tm, neg], axis=1)
    odd = jnp.concatenate([tm[:, 1:], neg, neg], axis=1)
    both = jnp.stack([even, odd], axis=1)
    return both.transpose(0, 1, 3, 2, 4).reshape(N_HEADS, 2, GRID_W, ROWS * GRID_W)


SORT_ROWS = 512


def _sort_kernel(xc_ref, xl_ref, mod_ref, gffn_ref, wr_ref, br_ref,
                 xs_ref, dk_ref, np_ref, off_ref):
    j = pl.program_id(0)
    x = jnp.where(j < _N_CTX_BLOCKS, xc_ref[...], xl_ref[...])
    shift, scale = mod_ref[0, 3:4, :], mod_ref[0, 4:5, :]
    hb = _norm_mod(x, gffn_ref[...], scale, shift).astype(BF16)
    logits = _dot_nt(wr_ref[...], hb) + br_ref[...]
    eio = lax.broadcasted_iota(jnp.int32, logits.shape, 0)
    work = logits
    sels, vals = [], []
    for _ in range(TOP_K):
        m = jnp.max(work, axis=0, keepdims=True)
        idx = jnp.min(jnp.where(work == m, eio, N_EXPERTS), axis=0, keepdims=True)
        sel = eio == idx
        sels.append(sel)
        vals.append(m)
        work = jnp.where(sel, -jnp.inf, work)
    exps = [jnp.exp(v - vals[0]) for v in vals]
    den = exps[0] + exps[1] + exps[2] + exps[3]
    mask = jnp.zeros(logits.shape, F32)
    for sel in sels:
        mask = mask + jnp.where(sel, 1.0, 0.0)
    t_row = lax.broadcasted_iota(jnp.int32, (TOK_BLOCK, TOK_BLOCK), 0)
    t_col = lax.broadcasted_iota(jnp.int32, (TOK_BLOCK, TOK_BLOCK), 1)
    before = jnp.where(t_row < t_col, 1.0, 0.0).astype(BF16)
    rank = _dot(mask.astype(BF16), before)
    cnt = jnp.sum(mask, axis=1, keepdims=True)
    np16 = jnp.floor((cnt + (CHUNK - 1.0)) * (1.0 / CHUNK))
    e_row = lax.broadcasted_iota(jnp.int32, (N_EXPERTS, N_EXPERTS), 0)
    e_col = lax.broadcasted_iota(jnp.int32, (N_EXPERTS, N_EXPERTS), 1)
    lower = jnp.where(e_col < e_row, 1.0, 0.0).astype(BF16)
    np16_b = jnp.broadcast_to(np16, (N_EXPERTS, 128))
    off16 = _dot(lower, np16_b.astype(BF16))
    np_ref[0] = np16_b
    off_ref[0] = off16
    dest = off16[:, 0:1] * float(CHUNK) + rank
    dests = []
    for k in range(TOP_K):
        dk = jnp.sum(jnp.where(sels[k], dest, 0.0), axis=0, keepdims=True)
        dk_ref[0, k:k + 1, :] = dk
        dk_ref[0, TOP_K + k:TOP_K + k + 1, :] = exps[k] / den
        dests.append(dk.astype(jnp.int32))
    for c in range(BLOCK_CAP // SORT_ROWS):
        rio = lax.broadcasted_iota(jnp.int32, (SORT_ROWS, TOK_BLOCK), 0) + c * SORT_ROWS
        hit = (rio == dests[0]) | (rio == dests[1]) | (rio == dests[2]) | (rio == dests[3])
        onehot = jnp.where(hit, 1.0, 0.0).astype(BF16)
        xs_ref[c * SORT_ROWS:(c + 1) * SORT_ROWS, :] = _dot(onehot, hb).astype(BF16)


def _block_mod_index(j):
    return jnp.where(j < _N_CTX_BLOCKS, 0, 1 + (j - _N_CTX_BLOCKS) // (DEC_SEQ // TOK_BLOCK))


def _token_specs():
    return [pl.BlockSpec((TOK_BLOCK, D_MODEL), lambda j: (jnp.minimum(j, _N_CTX_BLOCKS - 1), 0)),
            pl.BlockSpec((TOK_BLOCK, D_MODEL), lambda j: (jnp.maximum(j - _N_CTX_BLOCKS, 0), 0)),
            pl.BlockSpec((1, N_MOD, D_MODEL), lambda j: (_block_mod_index(j), 0, 0))]


def _route_sort(x1c, x1l, modv, g_ffn, wr_t, br):
    tbl = jax.ShapeDtypeStruct((_N_BLOCKS, N_EXPERTS, 128), F32)
    tbl_spec = pl.BlockSpec((1, N_EXPERTS, 128), lambda j: (j, 0, 0))
    return pl.pallas_call(
        _sort_kernel,
        grid=(_N_BLOCKS,),
        in_specs=_token_specs() + [_const_spec((1, D_MODEL)),
                                   _const_spec((N_EXPERTS, D_MODEL)),
                                   _const_spec((N_EXPERTS, 1))],
        out_specs=[pl.BlockSpec((BLOCK_CAP, D_MODEL), lambda j: (j, 0)),
                   pl.BlockSpec((1, 2 * TOP_K, TOK_BLOCK), lambda j: (j, 0, 0)),
                   tbl_spec, tbl_spec],
        out_shape=[jax.ShapeDtypeStruct((_N_BLOCKS * BLOCK_CAP, D_MODEL), BF16),
                   jax.ShapeDtypeStruct((_N_BLOCKS, 2 * TOP_K, TOK_BLOCK), F32),
                   tbl, tbl],
        compiler_params=pltpu.CompilerParams(dimension_semantics=("arbitrary",),
                                             vmem_limit_bytes=V7X_VMEM_LIMIT),
        name="route_sort",
    )(x1c, x1l, modv, g_ffn, wr_t, br)


EXPERT_CAP = _N_BLOCKS * TOK_BLOCK


def _expert_kernel(np_ref, off_ref, wu_ref, bu_ref, wd_ref, bd_ref, xs_hbm, ys_hbm,
                   buf, wu_s, wd_s, sem):
    del xs_hbm
    e = pl.program_id(0)

    @pl.when(e == 0)
    def _():
        buf[...] = jnp.zeros_like(buf)

    def chunk_copy(j, c, q, gather):
        row = pl.multiple_of(j * BLOCK_CAP + (off_ref[j * N_EXPERTS + e] + c) * CHUNK, CHUNK)
        hbm = ys_hbm.at[pl.ds(row, CHUNK), :]
        vm = buf.at[pl.ds(pl.multiple_of(q * CHUNK, CHUNK), CHUNK), :]
        return pltpu.make_async_copy(hbm, vm, sem) if gather else pltpu.make_async_copy(vm, hbm, sem)

    def for_each_chunk(fn):
        def block_body(j, q0):
            n = np_ref[j * N_EXPERTS + e]

            def chunk_body(c, carry):
                fn(j, c, q0 + c)
                return carry

            lax.fori_loop(0, n, chunk_body, 0)
            return q0 + n

        return lax.fori_loop(0, _N_BLOCKS, block_body, 0)

    n_chunks = for_each_chunk(lambda j, c, q: chunk_copy(j, c, q, True).start())
    wu_s[...] = wu_ref[0].astype(BF16)
    wd_s[...] = wd_ref[0].astype(BF16)
    for_each_chunk(lambda j, c, q: chunk_copy(j, c, q, True).wait())

    n_tiles = (n_chunks * CHUNK + EXPERT_ROW_TILE - 1) // EXPERT_ROW_TILE

    def tile_body(t, carry):
        r0 = pl.multiple_of(t * EXPERT_ROW_TILE, EXPERT_ROW_TILE)
        x = buf[pl.ds(r0, EXPERT_ROW_TILE), :]
        gu = _dot(x, wu_s[...]) + bu_ref[0]
        gate = jnp.minimum(gu[:, :D_FF], SWIGLU_LIMIT)
        up = jnp.clip(gu[:, D_FF:], -SWIGLU_LIMIT, SWIGLU_LIMIT)
        glu = gate * _sigmoid(SWIGLU_ALPHA * gate)
        y = _dot(((up + 1.0) * glu).astype(BF16), wd_s[...]) + bd_ref[0]
        buf[pl.ds(r0, EXPERT_ROW_TILE), :] = y.astype(BF16)
        return carry

    lax.fori_loop(0, n_tiles, tile_body, 0)
    for_each_chunk(lambda j, c, q: chunk_copy(j, c, q, False).start())
    for_each_chunk(lambda j, c, q: chunk_copy(j, c, q, False).wait())


def _experts(np16, off16, w_up, b_up, w_down, b_down, xs):
    cap = (EXPERT_CAP + EXPERT_ROW_TILE - 1) // EXPERT_ROW_TILE * EXPERT_ROW_TILE
    grid_spec = pltpu.PrefetchScalarGridSpec(
        num_scalar_prefetch=2,
        grid=(N_EXPERTS,),
        in_specs=[pl.BlockSpec((1, D_MODEL, 2 * D_FF), lambda e, *_: (e, 0, 0)),
                  pl.BlockSpec((1, 1, 2 * D_FF), lambda e, *_: (e, 0, 0)),
                  pl.BlockSpec((1, D_FF, D_MODEL), lambda e, *_: (e, 0, 0)),
                  pl.BlockSpec((1, 1, D_MODEL), lambda e, *_: (e, 0, 0)),
                  pl.BlockSpec(memory_space=pl.ANY)],
        out_specs=pl.BlockSpec(memory_space=pl.ANY),
        scratch_shapes=[pltpu.VMEM((cap, D_MODEL), BF16),
                        pltpu.VMEM((D_MODEL, 2 * D_FF), BF16),
                        pltpu.VMEM((D_FF, D_MODEL), BF16),
                        pltpu.SemaphoreType.DMA(())],
    )
    return pl.pallas_call(
        _expert_kernel,
        grid_spec=grid_spec,
        out_shape=jax.ShapeDtypeStruct(xs.shape, xs.dtype),
        input_output_aliases={6: 0},
        compiler_params=pltpu.CompilerParams(dimension_semantics=("arbitrary",),
                                             vmem_limit_bytes=V7X_VMEM_LIMIT),
        name="experts",
    )(np16, off16, w_up, b_up.reshape(N_EXPERTS, 1, 2 * D_FF), w_down,
      b_down.reshape(N_EXPERTS, 1, D_MODEL), xs)


def _combine_kernel(ys_ref, dk_ref, xc_ref, xl_ref, mod_ref, gfin_ref, yc_ref, yl_ref):
    j = pl.program_id(0)
    d = dk_ref[0]
    acc = jnp.zeros((TOK_BLOCK, D_MODEL), F32)
    for c in range(BLOCK_CAP // SORT_ROWS):
        cio = (lax.broadcasted_iota(jnp.int32, (TOK_BLOCK, SORT_ROWS), 1) + c * SORT_ROWS).astype(F32)
        w = jnp.zeros((TOK_BLOCK, SORT_ROWS), F32)
        for k in range(TOP_K):
            w = w + jnp.where(cio == d[:, k:k + 1], d[:, TOP_K + k:TOP_K + k + 1], 0.0)
        acc = acc + _dot(w.astype(BF16), ys_ref[c * SORT_ROWS:(c + 1) * SORT_ROWS, :])
    x1 = jnp.where(j < _N_CTX_BLOCKS, xc_ref[...], xl_ref[...])
    x2 = x1 + mod_ref[0, 5:6, :] * acc
    ms = jnp.mean(x2 * x2, axis=-1, keepdims=True)
    y = x2 * lax.rsqrt(ms + RMS_EPS) * gfin_ref[...]

    @pl.when(j < _N_CTX_BLOCKS)
    def _():
        yc_ref[...] = y

    @pl.when(j >= _N_CTX_BLOCKS)
    def _():
        yl_ref[...] = y


def _combine(ys, dk_t, x1c, x1l, modv, g_final):
    return pl.pallas_call(
        _combine_kernel,
        grid=(_N_BLOCKS,),
        in_specs=[pl.BlockSpec((BLOCK_CAP, D_MODEL), lambda j: (j, 0)),
                  pl.BlockSpec((1, TOK_BLOCK, 2 * TOP_K), lambda j: (j, 0, 0))] + _token_specs() + [
                  _const_spec((1, D_MODEL))],
        out_specs=[pl.BlockSpec((TOK_BLOCK, D_MODEL), lambda j: (jnp.minimum(j, _N_CTX_BLOCKS - 1), 0)),
                   pl.BlockSpec((TOK_BLOCK, D_MODEL), lambda j: (jnp.maximum(j - _N_CTX_BLOCKS, 0), 0))],
        out_shape=[jax.ShapeDtypeStruct(x1c.shape, F32), jax.ShapeDtypeStruct(x1l.shape, F32)],
        compiler_params=pltpu.CompilerParams(dimension_semantics=("arbitrary",),
                                             vmem_limit_bytes=V7X_VMEM_LIMIT),
        name="combine",
    )(ys, dk_t, x1c, x1l, modv, g_final)


def kernel(x_prompt, x_sample, cache_k, cache_v, c, c_ctx, w_ada, b_ada, g_mix, w_in, w_pool, pool_scale,
           w_pa, w_pb, rpb, w_out, g_ffn, w_router, b_router, w_up, b_up, w_down, b_down, g_final):
    assert w_ada.shape[0] == 1, "single trunk layer"
    batch, seq, d = x_prompt.shape
    dec_batch, dec_seq, _ = x_sample.shape
    assert (seq, dec_seq, d) == (SEQ, DEC_SEQ, D_MODEL)
    assert batch * seq == _N_CTX_BLOCKS * TOK_BLOCK and dec_batch * dec_seq == _N_LAT_BLOCKS * TOK_BLOCK

    cmat = jnp.concatenate([c_ctx[None, :], c, jnp.zeros((8 - 1 - dec_batch, d), F32)], axis=0)
    modv = _modulation(cmat, w_ada[0], b_ada[0]).reshape(8, N_MOD, d)

    weights = (g_mix[0][None, :], w_in[0].astype(BF16), w_pool[0].astype(BF16), pool_scale[0][None, :],
               w_pa[0].astype(BF16), w_pb[0].astype(BF16).reshape(N_PAIRS, PAIR_W, d), w_out[0].astype(BF16))

    def by_pair(cache):
        z = cache[:, 0].reshape(dec_batch, N_PAIRS, 2, PAST_LEN, HEAD_DIM)
        return z.transpose(0, 1, 3, 2, 4).reshape(dec_batch, N_PAIRS, PAST_LEN, PAIR_W)

    x1c, new_k, new_v = _ctx_mixer(x_prompt.reshape(batch * seq, d), modv, weights)
    x1l = _lat_mixer(x_sample.reshape(dec_batch * dec_seq, d), modv, weights,
                     by_pair(cache_k), by_pair(cache_v), _bias_tables(rpb[0]))

    xs, dk, np16, off16 = _route_sort(x1c, x1l, modv, g_ffn[0][None, :],
                                      w_router[0].T.astype(BF16), b_router[0][:, None])
    np16_i = np16[:, :, 0].astype(jnp.int32).reshape(-1)
    off16_i = off16[:, :, 0].astype(jnp.int32).reshape(-1)
    ys = _experts(np16_i, off16_i, w_up[0], b_up[0], w_down[0], b_down[0], xs)
    yc, yl = _combine(ys, dk.transpose(0, 2, 1), x1c, x1l, modv, g_final[None, :])
    return (yc.reshape(batch, seq, d), yl.reshape(dec_batch, dec_seq, d), new_k, new_v)
```

```python
import functools

import jax
import jax.numpy as jnp
import numpy as np
from jax import lax
from jax.experimental import pallas as pl
from jax.experimental.pallas import tpu as pltpu

F32 = jnp.float32
BF16 = jnp.bfloat16

D_MODEL = 1024
SEQ = 256
DEC_SEQ = 1024
GRID_W = 64
ROWS = DEC_SEQ // GRID_W
N_HEADS = 8
HEAD_DIM = 64
N_PAIRS = N_HEADS // 2
PAIR_W = 2 * HEAD_DIM
PAST_LEN = 512
POOL_DIM = 512
POOL_WINDOWS = (2, 4, 8, 16)
POOL_GROUP_DIM = 128
ATTN_DIM = 512
WIN_R = 8
WIN_C = 16
N_EXPERTS = 32
TOP_K = 4
D_FF = 1024
SWIGLU_LIMIT = 7.0
SWIGLU_ALPHA = 1.702
N_MOD = 6
RMS_EPS = 1e-6
NEG_INF = -1e30
ATTN_SCALE = HEAD_DIM ** -0.5
IN_DIM = POOL_DIM + 3 * ATTN_DIM + 2 * D_MODEL

TOK_BLOCK = 512
CHUNK = 16
BLOCK_CAP = TOK_BLOCK * TOP_K + N_EXPERTS * CHUNK
EXPERT_ROW_TILE = 256
V7X_VMEM_LIMIT = 60 * 1024 * 1024

_N_CTX_BLOCKS = 8
_N_LAT_BLOCKS = 4
_N_BLOCKS = _N_CTX_BLOCKS + _N_LAT_BLOCKS

_NT = (((1,), (1,)), ((), ()))


def _dot(a, b):
    return jnp.dot(a, b, preferred_element_type=F32)


def _dot_nt(a, b):
    return lax.dot_general(a, b, _NT, preferred_element_type=F32)


def _sigmoid(x):
    return 1.0 / (1.0 + jnp.exp(-x))


def _norm_mod(x, gain, scale, shift):
    ms = jnp.mean(x * x, axis=-1, keepdims=True)
    return (x * lax.rsqrt(ms + RMS_EPS) * gain) * (1.0 + scale) + shift


def _const_spec(shape):
    zeros = (0,) * len(shape)
    return pl.BlockSpec(shape, lambda *_: zeros, pipeline_mode=pl.Buffered(1))


MOD_COLS = 1536


def _mod_kernel(c_ref, w_ref, b_ref, o_ref):
    c = c_ref[...]
    s = c * _sigmoid(c)
    o_ref[...] = jnp.dot(s, w_ref[...], preferred_element_type=F32,
                         precision=lax.Precision.HIGHEST) + b_ref[...]


def _modulation(cmat, w_ada, b_ada):
    n = w_ada.shape[1]
    return pl.pallas_call(
        _mod_kernel,
        grid=(n // MOD_COLS,),
        in_specs=[pl.BlockSpec((8, D_MODEL), lambda i: (0, 0)),
                  pl.BlockSpec((D_MODEL, MOD_COLS), lambda i: (0, i)),
                  pl.BlockSpec((1, MOD_COLS), lambda i: (0, i))],
        out_specs=pl.BlockSpec((8, MOD_COLS), lambda i: (0, i)),
        out_shape=jax.ShapeDtypeStruct((8, n), F32),
        name="modulation",
    )(cmat, w_ada, b_ada.reshape(1, n))


def _pool_mix(u, pos, seq):
    n = u.shape[0]

    def down(x, d):
        return jnp.where(pos >= d, pltpu.roll(x, d, 0), 0.0)

    def up(x, d):
        return jnp.where(pos < seq - d, pltpu.roll(x, n - d, 0), 0.0)

    return down, up


def _pool_group(u, pos, seq, w):
    down, up = _pool_mix(u, pos, seq)
    hw = w // 2
    back = u
    fwd = u
    d = 1
    while d < hw:
        back = back + down(back, d)
        fwd = fwd + up(fwd, d)
        d *= 2
    s = down(back, 1) + fwd
    posf = pos.astype(F32)
    cnt = jnp.minimum(posf + hw, float(seq)) - jnp.maximum(posf - hw, 0.0)
    return s / cnt - u


def _mixer_front(x_ref, mod_ref, gmix_ref, win_ref, hb_s, u_s, q_s, k_s, v_s):
    x = x_ref[...]
    shift, scale = mod_ref[0, 0:1, :], mod_ref[0, 1:2, :]
    hb = _norm_mod(x, gmix_ref[...], scale, shift).astype(BF16)
    hb_s[...] = hb
    u_s[...] = _dot(hb, win_ref[:, 0:POOL_DIM])
    for dst, base in ((q_s, POOL_DIM), (k_s, POOL_DIM + ATTN_DIM), (v_s, POOL_DIM + 2 * ATTN_DIM)):
        z = _dot(hb, win_ref[:, base:base + ATTN_DIM])
        for g in range(N_PAIRS):
            dst[g] = z[:, g * PAIR_W:(g + 1) * PAIR_W].astype(dst.dtype)


def _mixer_back(x_ref, mod_ref, win_ref, wpool_ref, ps_ref, wpa_ref, wout_ref, x1_ref,
                hb_s, u_s, ob_s, pg_s, seq):
    tile = x_ref.shape[0]
    gate = mod_ref[0, 2:3, :]
    pos = lax.broadcasted_iota(jnp.int32, (tile, 1), 0) % seq
    for g, w in enumerate(POOL_WINDOWS):
        cols = slice(g * POOL_GROUP_DIM, (g + 1) * POOL_GROUP_DIM)
        pg_s[:, cols] = _pool_group(u_s[:, cols], pos, seq, w).astype(BF16)
    for c in range(tile // TOK_BLOCK):
        rows = slice(c * TOK_BLOCK, (c + 1) * TOK_BLOCK)
        a = jnp.zeros((TOK_BLOCK, D_MODEL), F32)
        for g in range(len(POOL_WINDOWS)):
            cols = slice(g * POOL_GROUP_DIM, (g + 1) * POOL_GROUP_DIM)
            yg = _dot(pg_s[rows, cols], wpool_ref[g]) * ps_ref[:, cols]
            a = a + _dot(yg.astype(BF16), wpa_ref[cols, :])
        gab = _dot(hb_s[rows, :], win_ref[:, POOL_DIM + 3 * ATTN_DIM:IN_DIM])
        merged = _sigmoid(gab[:, :D_MODEL]) * a + _sigmoid(gab[:, D_MODEL:]) * ob_s[rows, :]
        mix = _dot(merged.astype(BF16), wout_ref[...])
        x1_ref[rows, :] = x_ref[rows, :] + gate * mix


def _ctx_mixer_kernel(x_ref, mod_ref, gmix_ref, win_ref, wpool_ref, ps_ref, wpa_ref, wpb_ref, wout_ref,
                      x1_ref, ko_ref, vo_ref, hb_s, u_s, q_s, k_s, v_s, ob_s, pg_s):
    _mixer_front(x_ref, mod_ref, gmix_ref, win_ref, hb_s, u_s, q_s, k_s, v_s)
    even = lax.broadcasted_iota(jnp.int32, (1, PAIR_W), 1) < HEAD_DIM
    tile = x_ref.shape[0]
    for s in range(tile // SEQ):
        rows = slice(s * SEQ, (s + 1) * SEQ)
        ob = jnp.zeros((SEQ, D_MODEL), F32)
        for g in range(N_PAIRS):
            q2, k2, v2 = q_s[g, rows, :], k_s[g, rows, :], v_s[g, rows, :]
            ko_ref[s, 0, 2 * g] = k2[:, :HEAD_DIM]
            ko_ref[s, 0, 2 * g + 1] = k2[:, HEAD_DIM:]
            vo_ref[s, 0, 2 * g] = v2[:, :HEAD_DIM]
            vo_ref[s, 0, 2 * g + 1] = v2[:, HEAD_DIM:]
            kb, vb = k2.astype(BF16), v2.astype(BF16)
            outs = []
            for par in range(2):
                qm = jnp.where(even if par == 0 else jnp.logical_not(even), q2, 0.0).astype(BF16)
                sc = _dot_nt(qm, kb) * ATTN_SCALE
                m = jnp.max(sc, axis=-1, keepdims=True)
                p = jnp.exp(sc - m)
                l = jnp.sum(p, axis=-1, keepdims=True)
                outs.append(_dot(p.astype(BF16), vb) / l)
            o2 = jnp.where(even, outs[0], outs[1])
            ob = ob + _dot(o2.astype(BF16), wpb_ref[g])
        ob_s[rows, :] = ob
    _mixer_back(x_ref, mod_ref, win_ref, wpool_ref, ps_ref, wpa_ref, wout_ref, x1_ref,
                hb_s, u_s, ob_s, pg_s, SEQ)


def _row_window(r):
    rs = min(max(r - WIN_R // 2, 0), ROWS - WIN_R)
    return rs, rs - r + WIN_R - 1


def _lat_mixer_kernel(x_ref, mod_ref, gmix_ref, win_ref, wpool_ref, ps_ref, wpa_ref, wpb_ref, wout_ref,
                      kc_ref, vc_ref, tb_ref, x1_ref, hb_s, u_s, q_s, k_s, v_s, ob_s, pg_s):
    _mixer_front(x_ref, mod_ref, gmix_ref, win_ref, hb_s, u_s, q_s, k_s, v_s)
    even = lax.broadcasted_iota(jnp.int32, (1, PAIR_W), 1) < HEAD_DIM
    nk = WIN_R * GRID_W
    ob_s[...] = jnp.zeros_like(ob_s)

    def pair_body(g, carry):
        q2 = q_s[g]
        kb, vb = k_s[g].astype(BF16), v_s[g].astype(BF16)
        kcb, vcb = kc_ref[0, g].astype(BF16), vc_ref[0, g].astype(BF16)
        outs = []
        for par in range(2):
            qm = jnp.where(even if par == 0 else jnp.logical_not(even), q2, 0.0).astype(BF16)
            s_ctx = _dot_nt(qm, kcb) * ATTN_SCALE
            slabs = []
            for r in range(ROWS):
                rs, rho = _row_window(r)
                bias = tb_ref[2 * g + par, rho % 2, :, (rho - rho % 2) * GRID_W:(rho - rho % 2) * GRID_W + nk]
                sl = _dot_nt(qm[r * GRID_W:(r + 1) * GRID_W, :], kb[rs * GRID_W:rs * GRID_W + nk, :])
                slabs.append(sl * ATTN_SCALE + bias)
            s_loc = jnp.concatenate(slabs, axis=0)
            m = jnp.maximum(jnp.max(s_loc, axis=-1, keepdims=True), jnp.max(s_ctx, axis=-1, keepdims=True))
            p_loc = jnp.exp(s_loc - m)
            p_ctx = jnp.exp(s_ctx - m)
            l = jnp.sum(p_loc, axis=-1, keepdims=True) + jnp.sum(p_ctx, axis=-1, keepdims=True)
            p_locb = p_loc.astype(BF16)
            o_rows = []
            for r in range(ROWS):
                rs, _ = _row_window(r)
                o_rows.append(_dot(p_locb[r * GRID_W:(r + 1) * GRID_W, :], vb[rs * GRID_W:rs * GRID_W + nk, :]))
            o = jnp.concatenate(o_rows, axis=0) + _dot(p_ctx.astype(BF16), vcb)
            outs.append(o / l)
        o2 = jnp.where(even, outs[0], outs[1])
        ob_s[...] += _dot(o2.astype(BF16), wpb_ref[g])
        return carry

    lax.fori_loop(0, N_PAIRS, pair_body, 0)
    _mixer_back(x_ref, mod_ref, win_ref, wpool_ref, ps_ref, wpa_ref, wout_ref, x1_ref,
                hb_s, u_s, ob_s, pg_s, DEC_SEQ)


def _mixer_scratch(tile, kv_dtype):
    return [pltpu.VMEM((tile, D_MODEL), BF16),
            pltpu.VMEM((tile, POOL_DIM), F32),
            pltpu.VMEM((N_PAIRS, tile, PAIR_W), BF16),
            pltpu.VMEM((N_PAIRS, tile, PAIR_W), kv_dtype),
            pltpu.VMEM((N_PAIRS, tile, PAIR_W), kv_dtype),
            pltpu.VMEM((tile, D_MODEL), F32),
            pltpu.VMEM((tile, POOL_DIM), BF16)]


def _weight_specs():
    return [_const_spec((1, D_MODEL)),
            _const_spec((D_MODEL, IN_DIM)),
            _const_spec((len(POOL_WINDOWS), POOL_GROUP_DIM, POOL_GROUP_DIM)),
            _const_spec((1, POOL_DIM)),
            _const_spec((POOL_DIM, D_MODEL)),
            _const_spec((N_PAIRS, PAIR_W, D_MODEL)),
            _const_spec((D_MODEL, D_MODEL))]


def _ctx_mixer(x, modv, weights):
    n = x.shape[0]
    nseq = TOK_BLOCK // SEQ
    cache = jax.ShapeDtypeStruct((n // SEQ, 1, N_HEADS, SEQ, HEAD_DIM), F32)
    cache_spec = pl.BlockSpec((nseq, 1, N_HEADS, SEQ, HEAD_DIM), lambda i: (i, 0, 0, 0, 0))
    return pl.pallas_call(
        _ctx_mixer_kernel,
        grid=(n // TOK_BLOCK,),
        in_specs=[pl.BlockSpec((TOK_BLOCK, D_MODEL), lambda i: (i, 0)),
                  pl.BlockSpec((1, N_MOD, D_MODEL), lambda i: (0, 0, 0))] + _weight_specs(),
        out_specs=[pl.BlockSpec((TOK_BLOCK, D_MODEL), lambda i: (i, 0)), cache_spec, cache_spec],
        out_shape=[jax.ShapeDtypeStruct((n, D_MODEL), F32), cache, cache],
        scratch_shapes=_mixer_scratch(TOK_BLOCK, F32),
        compiler_params=pltpu.CompilerParams(dimension_semantics=("arbitrary",),
                                             vmem_limit_bytes=V7X_VMEM_LIMIT),
        name="ctx_mixer",
    )(x, modv, *weights)


def _lat_mixer(x, modv, weights, kc, vc, tb):
    n = x.shape[0]
    return pl.pallas_call(
        _lat_mixer_kernel,
        grid=(n // DEC_SEQ,),
        in_specs=[pl.BlockSpec((DEC_SEQ, D_MODEL), lambda i: (i, 0)),
                  pl.BlockSpec((1, N_MOD, D_MODEL), lambda i: (i + 1, 0, 0))] + _weight_specs() + [
                  pl.BlockSpec((1, N_PAIRS, PAST_LEN, PAIR_W), lambda i: (i, 0, 0, 0)),
                  pl.BlockSpec((1, N_PAIRS, PAST_LEN, PAIR_W), lambda i: (i, 0, 0, 0)),
                  _const_spec((N_HEADS, 2, GRID_W, ROWS * GRID_W))],
        out_specs=pl.BlockSpec((DEC_SEQ, D_MODEL), lambda i: (i, 0)),
        out_shape=jax.ShapeDtypeStruct((n, D_MODEL), F32),
        scratch_shapes=_mixer_scratch(DEC_SEQ, BF16),
        compiler_params=pltpu.CompilerParams(dimension_semantics=("arbitrary",),
                                             vmem_limit_bytes=V7X_VMEM_LIMIT),
        name="lat_mixer",
    )(x, modv, *weights, kc, vc, tb)


N_RPB_ROWS = 2 * WIN_R - 1
N_RPB_COLS = 2 * WIN_C - 1


def _bias_kernel(rpb_ref, onehot_ref, inwin_ref, o_ref):
    t = jnp.dot(rpb_ref[...], onehot_ref[...], preferred_element_type=F32, precision=lax.Precision.HIGHEST)
    o_ref[...] = jnp.where(inwin_ref[...] > 0.0, t, NEG_INF)


def _bias_tables(rpb):
    col = np.arange(GRID_W)
    dc = col[None, :] - col[:, None]
    cs = np.clip(col - WIN_C // 2, 0, GRID_W - WIN_C)
    in_win = (col[None, :] >= cs[:, None]) & (col[None, :] < cs[:, None] + WIN_C)
    idx = np.clip(dc + WIN_C - 1, 0, N_RPB_COLS - 1)
    onehot = (np.arange(32)[:, None, None] == idx[None]).astype(np.float32).reshape(32, GRID_W * GRID_W)
    rows = N_HEADS * (N_RPB_ROWS + 1)
    rpb_p = jnp.pad(rpb.astype(F32), ((0, 0), (0, 1), (0, 32 - N_RPB_COLS))).reshape(rows, 32)
    keep = np.ones((N_HEADS, N_RPB_ROWS + 1, 1), np.float32)
    keep[:, N_RPB_ROWS] = 0.0
    inwin = (keep * in_win.astype(np.float32).reshape(1, 1, -1)).reshape(rows, GRID_W * GRID_W)
    tm = pl.pallas_call(
        _bias_kernel,
        out_shape=jax.ShapeDtypeStruct((rows, GRID_W * GRID_W), F32),
        name="bias_tables",
    )(rpb_p, jnp.asarray(onehot), jnp.asarray(inwin))
    even = tm.reshape(N_HEADS, N_RPB_ROWS + 1, GRID_W, GRID_W)
    odd = jnp.concatenate([even[:, 1:], even[:, N_RPB_ROWS:]], axis=1)
    both = jnp.stack([even, odd], axis=1)
    return both.transpose(0, 1, 3, 2, 4).reshape(N_HEADS, 2, GRID_W, ROWS * GRID_W)


SORT_ROWS = 512


def _sort_kernel(xc_ref, xl_ref, mod_ref, gffn_ref, wr_ref, br_ref,
                 xs_ref, dk_ref, np_ref, off_ref):
    j = pl.program_id(0)
    x = jnp.where(j < _N_CTX_BLOCKS, xc_ref[...], xl_ref[...])
    shift, scale = mod_ref[0, 3:4, :], mod_ref[0, 4:5, :]
    hb = _norm_mod(x, gffn_ref[...], scale, shift).astype(BF16)
    logits = _dot_nt(wr_ref[...], hb) + br_ref[...]
    eio = lax.broadcasted_iota(jnp.int32, logits.shape, 0)
    work = logits
    sels, vals = [], []
    for _ in range(TOP_K):
        m = jnp.max(work, axis=0, keepdims=True)
        idx = jnp.min(jnp.where(work == m, eio, N_EXPERTS), axis=0, keepdims=True)
        sel = eio == idx
        sels.append(sel)
        vals.append(m)
        work = jnp.where(sel, -jnp.inf, work)
    exps = [jnp.exp(v - vals[0]) for v in vals]
    den = exps[0] + exps[1] + exps[2] + exps[3]
    mask = jnp.zeros(logits.shape, F32)
    for sel in sels:
        mask = mask + jnp.where(sel, 1.0, 0.0)
    t_row = lax.broadcasted_iota(jnp.int32, (TOK_BLOCK, TOK_BLOCK), 0)
    t_col = lax.broadcasted_iota(jnp.int32, (TOK_BLOCK, TOK_BLOCK), 1)
    before = jnp.where(t_row < t_col, 1.0, 0.0).astype(BF16)
    rank = _dot(mask.astype(BF16), before)
    cnt = jnp.sum(mask, axis=1, keepdims=True)
    np16 = jnp.floor((cnt + (CHUNK - 1.0)) * (1.0 / CHUNK))
    e_row = lax.broadcasted_iota(jnp.int32, (N_EXPERTS, N_EXPERTS), 0)
    e_col = lax.broadcasted_iota(jnp.int32, (N_EXPERTS, N_EXPERTS), 1)
    lower = jnp.where(e_col < e_row, 1.0, 0.0).astype(BF16)
    np16_b = jnp.broadcast_to(np16, (N_EXPERTS, 128))
    off16 = _dot(lower, np16_b.astype(BF16))
    np_ref[0] = np16_b
    off_ref[0] = off16
    dest = off16[:, 0:1] * float(CHUNK) + rank
    dests = []
    for k in range(TOP_K):
        dk = jnp.sum(jnp.where(sels[k], dest, 0.0), axis=0, keepdims=True)
        dk_ref[0, k:k + 1, :] = dk
        dk_ref[0, TOP_K + k:TOP_K + k + 1, :] = exps[k] / den
        dests.append(dk.astype(jnp.int32))
    for c in range(BLOCK_CAP // SORT_ROWS):
        rio = lax.broadcasted_iota(jnp.int32, (SORT_ROWS, TOK_BLOCK), 0) + c * SORT_ROWS
        hit = (rio == dests[0]) | (rio == dests[1]) | (rio == dests[2]) | (rio == dests[3])
        onehot = jnp.where(hit, 1.0, 0.0).astype(BF16)
        xs_ref[c * SORT_ROWS:(c + 1) * SORT_ROWS, :] = _dot(onehot, hb).astype(BF16)


def _block_mod_index(j):
    return jnp.where(j < _N_CTX_BLOCKS, 0, 1 + (j - _N_CTX_BLOCKS) // (DEC_SEQ // TOK_BLOCK))


def _token_specs():
    return [pl.BlockSpec((TOK_BLOCK, D_MODEL), lambda j: (jnp.minimum(j, _N_CTX_BLOCKS - 1), 0)),
            pl.BlockSpec((TOK_BLOCK, D_MODEL), lambda j: (jnp.maximum(j - _N_CTX_BLOCKS, 0), 0)),
            pl.BlockSpec((1, N_MOD, D_MODEL), lambda j: (_block_mod_index(j), 0, 0))]


def _route_sort(x1c, x1l, modv, g_ffn, wr_t, br):
    tbl = jax.ShapeDtypeStruct((_N_BLOCKS, N_EXPERTS, 128), F32)
    tbl_spec = pl.BlockSpec((1, N_EXPERTS, 128), lambda j: (j, 0, 0))
    return pl.pallas_call(
        _sort_kernel,
        grid=(_N_BLOCKS,),
        in_specs=_token_specs() + [_const_spec((1, D_MODEL)),
                                   _const_spec((N_EXPERTS, D_MODEL)),
                                   _const_spec((N_EXPERTS, 1))],
        out_specs=[pl.BlockSpec((BLOCK_CAP, D_MODEL), lambda j: (j, 0)),
                   pl.BlockSpec((1, 2 * TOP_K, TOK_BLOCK), lambda j: (j, 0, 0)),
                   tbl_spec, tbl_spec],
        out_shape=[jax.ShapeDtypeStruct((_N_BLOCKS * BLOCK_CAP, D_MODEL), BF16),
                   jax.ShapeDtypeStruct((_N_BLOCKS, 2 * TOP_K, TOK_BLOCK), F32),
                   tbl, tbl],
        compiler_params=pltpu.CompilerParams(dimension_semantics=("arbitrary",),
                                             vmem_limit_bytes=V7X_VMEM_LIMIT),
        name="route_sort",
    )(x1c, x1l, modv, g_ffn, wr_t, br)


EXPERT_BUF_ROWS = 2048
PASS_CHUNKS = EXPERT_BUF_ROWS // CHUNK


def _expert_kernel(np_ref, off_ref, wu_ref, bu_ref, wd_ref, bd_ref, xs_hbm, ys_hbm,
                   buf, wu_s, wd_s, gsem, ssem, pend):
    del xs_hbm
    e = pl.program_id(0)
    last = pl.num_programs(0) - 1
    slot = e % 2
    other = 1 - slot

    def for_chunks(ex, q_lo, slot_, gather, act):
        def block_body(j, q0):
            n = np_ref[j * N_EXPERTS + ex]
            off = off_ref[j * N_EXPERTS + ex]
            c_lo = jnp.clip(q_lo - q0, 0, n)
            c_hi = jnp.clip(q_lo + PASS_CHUNKS - q0, 0, n)

            def chunk_body(c, carry):
                row = pl.multiple_of(j * BLOCK_CAP + (off + c) * CHUNK, CHUNK)
                brow = pl.multiple_of((q0 + c - q_lo) * CHUNK, CHUNK)
                hbm = ys_hbm.at[pl.ds(row, CHUNK), :]
                vm = buf.at[slot_, pl.ds(brow, CHUNK), :]
                if gather:
                    act(pltpu.make_async_copy(hbm, vm, gsem.at[slot_]))
                else:
                    act(pltpu.make_async_copy(vm, hbm, ssem.at[slot_]))
                return carry

            lax.fori_loop(c_lo, c_hi, chunk_body, 0)
            return q0 + n

        return lax.fori_loop(0, _N_BLOCKS, block_body, 0)

    def wait_chunks(sem, n, slot_):
        def body(i, carry):
            pltpu.make_async_copy(ys_hbm.at[pl.ds(0, CHUNK), :], buf.at[slot_, pl.ds(0, CHUNK), :], sem).wait()
            return carry

        lax.fori_loop(0, n, body, 0)

    def start(copy):
        copy.start()

    def compute(n):
        def tile_body(t, carry):
            r0 = pl.multiple_of(t * EXPERT_ROW_TILE, EXPERT_ROW_TILE)
            x = buf[slot, pl.ds(r0, EXPERT_ROW_TILE), :]
            gu = _dot(x, wu_s[...]) + bu_ref[0]
            gate = jnp.minimum(gu[:, :D_FF], SWIGLU_LIMIT)
            up = jnp.clip(gu[:, D_FF:], -SWIGLU_LIMIT, SWIGLU_LIMIT)
            glu = gate * _sigmoid(SWIGLU_ALPHA * gate)
            y = _dot(((up + 1.0) * glu).astype(BF16), wd_s[...]) + bd_ref[0]
            buf[slot, pl.ds(r0, EXPERT_ROW_TILE), :] = y.astype(BF16)
            return carry

        lax.fori_loop(0, (n * CHUNK + EXPERT_ROW_TILE - 1) // EXPERT_ROW_TILE, tile_body, 0)

    @pl.when(e == 0)
    def _():
        buf[...] = jnp.zeros_like(buf)
        pend[0] = 0
        pend[1] = 0
        for_chunks(0, 0, 0, True, start)

    wait_chunks(ssem.at[other], pend[other], other)
    pend[other] = 0

    @pl.when(e < last)
    def _():
        for_chunks(e + 1, 0, other, True, start)

    wu_s[...] = wu_ref[0].astype(BF16)
    wd_s[...] = wd_ref[0].astype(BF16)

    total = lax.fori_loop(0, _N_BLOCKS, lambda j, acc: acc + np_ref[j * N_EXPERTS + e], 0)
    n0 = jnp.minimum(total, PASS_CHUNKS)
    wait_chunks(gsem.at[slot], n0, slot)
    compute(n0)
    for_chunks(e, 0, slot, False, start)
    pend[slot] = n0

    def pass_body(p, carry):
        wait_chunks(ssem.at[slot], pend[slot], slot)
        lo = p * PASS_CHUNKS
        n = jnp.minimum(total - lo, PASS_CHUNKS)
        for_chunks(e, lo, slot, True, start)
        wait_chunks(gsem.at[slot], n, slot)
        compute(n)
        for_chunks(e, lo, slot, False, start)
        pend[slot] = n
        return carry

    lax.fori_loop(1, (total + PASS_CHUNKS - 1) // PASS_CHUNKS, pass_body, 0)

    @pl.when(e == last)
    def _():
        wait_chunks(ssem.at[slot], pend[slot], slot)
        pend[slot] = 0


def _experts(np16, off16, w_up, b_up, w_down, b_down, xs):
    grid_spec = pltpu.PrefetchScalarGridSpec(
        num_scalar_prefetch=2,
        grid=(N_EXPERTS,),
        in_specs=[pl.BlockSpec((1, D_MODEL, 2 * D_FF), lambda e, *_: (e, 0, 0)),
                  pl.BlockSpec((1, 1, 2 * D_FF), lambda e, *_: (e, 0, 0)),
                  pl.BlockSpec((1, D_FF, D_MODEL), lambda e, *_: (e, 0, 0)),
                  pl.BlockSpec((1, 1, D_MODEL), lambda e, *_: (e, 0, 0)),
                  pl.BlockSpec(memory_space=pl.ANY)],
        out_specs=pl.BlockSpec(memory_space=pl.ANY),
        scratch_shapes=[pltpu.VMEM((2, EXPERT_BUF_ROWS, D_MODEL), BF16),
                        pltpu.VMEM((D_MODEL, 2 * D_FF), BF16),
                        pltpu.VMEM((D_FF, D_MODEL), BF16),
                        pltpu.SemaphoreType.DMA((2,)),
                        pltpu.SemaphoreType.DMA((2,)),
                        pltpu.SMEM((2,), jnp.int32)],
    )
    return pl.pallas_call(
        _expert_kernel,
        grid_spec=grid_spec,
        out_shape=jax.ShapeDtypeStruct(xs.shape, xs.dtype),
        input_output_aliases={6: 0},
        compiler_params=pltpu.CompilerParams(dimension_semantics=("arbitrary",),
                                             vmem_limit_bytes=V7X_VMEM_LIMIT),
        name="experts",
    )(np16, off16, w_up, b_up.reshape(N_EXPERTS, 1, 2 * D_FF), w_down,
      b_down.reshape(N_EXPERTS, 1, D_MODEL), xs)


def _combine_kernel(ys_ref, dk_ref, xc_ref, xl_ref, mod_ref, gfin_ref, yc_ref, yl_ref):
    j = pl.program_id(0)
    d = dk_ref[0]
    acc = jnp.zeros((TOK_BLOCK, D_MODEL), F32)
    for c in range(BLOCK_CAP // SORT_ROWS):
        cio = (lax.broadcasted_iota(jnp.int32, (TOK_BLOCK, SORT_ROWS), 1) + c * SORT_ROWS).astype(F32)
        w = jnp.zeros((TOK_BLOCK, SORT_ROWS), F32)
        for k in range(TOP_K):
            w = w + jnp.where(cio == d[:, k:k + 1], d[:, TOP_K + k:TOP_K + k + 1], 0.0)
        acc = acc + _dot(w.astype(BF16), ys_ref[c * SORT_ROWS:(c + 1) * SORT_ROWS, :])
    x1 = jnp.where(j < _N_CTX_BLOCKS, xc_ref[...], xl_ref[...])
    x2 = x1 + mod_ref[0, 5:6, :] * acc
    ms = jnp.mean(x2 * x2, axis=-1, keepdims=True)
    y = x2 * lax.rsqrt(ms + RMS_EPS) * gfin_ref[...]

    @pl.when(j < _N_CTX_BLOCKS)
    def _():
        yc_ref[...] = y

    @pl.when(j >= _N_CTX_BLOCKS)
    def _():
        yl_ref[...] = y


def _combine(ys, dk_t, x1c, x1l, modv, g_final):
    return pl.pallas_call(
        _combine_kernel,
        grid=(_N_BLOCKS,),
        in_specs=[pl.BlockSpec((BLOCK_CAP, D_MODEL), lambda j: (j, 0)),
                  pl.BlockSpec((1, TOK_BLOCK, 2 * TOP_K), lambda j: (j, 0, 0))] + _token_specs() + [
                  _const_spec((1, D_MODEL))],
        out_specs=[pl.BlockSpec((TOK_BLOCK, D_MODEL), lambda j: (jnp.minimum(j, _N_CTX_BLOCKS - 1), 0)),
                   pl.BlockSpec((TOK_BLOCK, D_MODEL), lambda j: (jnp.maximum(j - _N_CTX_BLOCKS, 0), 0))],
        out_shape=[jax.ShapeDtypeStruct(x1c.shape, F32), jax.ShapeDtypeStruct(x1l.shape, F32)],
        compiler_params=pltpu.CompilerParams(dimension_semantics=("arbitrary",),
                                             vmem_limit_bytes=V7X_VMEM_LIMIT),
        name="combine",
    )(ys, dk_t, x1c, x1l, modv, g_final)


def kernel(x_prompt, x_sample, cache_k, cache_v, c, c_ctx, w_ada, b_ada, g_mix, w_in, w_pool, pool_scale,
           w_pa, w_pb, rpb, w_out, g_ffn, w_router, b_router, w_up, b_up, w_down, b_down, g_final):
    assert w_ada.shape[0] == 1, "single trunk layer"
    batch, seq, d = x_prompt.shape
    dec_batch, dec_seq, _ = x_sample.shape
    assert (seq, dec_seq, d) == (SEQ, DEC_SEQ, D_MODEL)
    assert batch * seq == _N_CTX_BLOCKS * TOK_BLOCK and dec_batch * dec_seq == _N_LAT_BLOCKS * TOK_BLOCK

    cmat = jnp.concatenate([c_ctx[None, :], c, jnp.zeros((8 - 1 - dec_batch, d), F32)], axis=0)
    modv = _modulation(cmat, w_ada[0], b_ada[0]).reshape(8, N_MOD, d)

    weights = (g_mix[0][None, :], w_in[0].astype(BF16), w_pool[0].astype(BF16), pool_scale[0][None, :],
               w_pa[0].astype(BF16), w_pb[0].astype(BF16).reshape(N_PAIRS, PAIR_W, d), w_out[0].astype(BF16))

    def by_pair(cache):
        z = cache[:, 0].reshape(dec_batch, N_PAIRS, 2, PAST_LEN, HEAD_DIM)
        return z.transpose(0, 1, 3, 2, 4).reshape(dec_batch, N_PAIRS, PAST_LEN, PAIR_W)

    x1c, new_k, new_v = _ctx_mixer(x_prompt.reshape(batch * seq, d), modv, weights)
    x1l = _lat_mixer(x_sample.reshape(dec_batch * dec_seq, d), modv, weights,
                     by_pair(cache_k), by_pair(cache_v), _bias_tables(rpb[0]))

    xs, dk, np16, off16 = _route_sort(x1c, x1l, modv, g_ffn[0][None, :],
                                      w_router[0].T.astype(BF16), b_router[0][:, None])
    np16_i = np16[:, :, 0].astype(jnp.int32).reshape(-1)
    off16_i = off16[:, :, 0].astype(jnp.int32).reshape(-1)
    ys = _experts(np16_i, off16_i, w_up[0], b_up[0], w_down[0], b_down[0], xs)
    yc, yl = _combine(ys, dk.transpose(0, 2, 1), x1c, x1l, modv, g_final[None, :])
    return (yc.reshape(batch, seq, d), yl.reshape(dec_batch, dec_seq, d), new_k, new_v)
```

```python
import functools

import jax
import jax.numpy as jnp
import numpy as np
from jax import lax
from jax.experimental import pallas as pl
from jax.experimental.pallas import tpu as pltpu

F32 = jnp.float32
BF16 = jnp.bfloat16

D_MODEL = 1024
SEQ = 256
DEC_SEQ = 1024
GRID_W = 64
ROWS = DEC_SEQ // GRID_W
N_HEADS = 8
HEAD_DIM = 64
N_PAIRS = N_HEADS // 2
PAIR_W = 2 * HEAD_DIM
PAST_LEN = 512
POOL_DIM = 512
POOL_WINDOWS = (2, 4, 8, 16)
POOL_GROUP_DIM = 128
ATTN_DIM = 512
WIN_R = 8
WIN_C = 16
N_EXPERTS = 32
TOP_K = 4
D_FF = 1024
SWIGLU_LIMIT = 7.0
SWIGLU_ALPHA = 1.702
N_MOD = 6
RMS_EPS = 1e-6
NEG_INF = -1e30
ATTN_SCALE = HEAD_DIM ** -0.5
IN_DIM = POOL_DIM + 3 * ATTN_DIM + 2 * D_MODEL

TOK_BLOCK = 512
CHUNK = 16
BLOCK_CAP = TOK_BLOCK * TOP_K + N_EXPERTS * CHUNK
EXPERT_ROW_TILE = 256
V7X_VMEM_LIMIT = 60 * 1024 * 1024

_N_CTX_BLOCKS = 8
_N_LAT_BLOCKS = 4
_N_BLOCKS = _N_CTX_BLOCKS + _N_LAT_BLOCKS

_NT = (((1,), (1,)), ((), ()))


def _dot(a, b):
    return jnp.dot(a, b, preferred_element_type=F32)


def _dot_nt(a, b):
    return lax.dot_general(a, b, _NT, preferred_element_type=F32)


def _sigmoid(x):
    return 1.0 / (1.0 + jnp.exp(-x))


def _norm_mod(x, gain, scale, shift):
    ms = jnp.mean(x * x, axis=-1, keepdims=True)
    return (x * lax.rsqrt(ms + RMS_EPS) * gain) * (1.0 + scale) + shift


def _const_spec(shape):
    zeros = (0,) * len(shape)
    return pl.BlockSpec(shape, lambda *_: zeros, pipeline_mode=pl.Buffered(1))


MOD_COLS = 1536


def _mod_kernel(c_ref, w_ref, b_ref, o_ref):
    c = c_ref[...]
    s = c * _sigmoid(c)
    s_hi = s.astype(BF16)
    s_lo = (s - s_hi.astype(F32)).astype(BF16)
    w = w_ref[...]
    w_hi = w.astype(BF16)
    w_lo = (w - w_hi.astype(F32)).astype(BF16)
    o_ref[...] = (_dot(s_hi, w_hi) + _dot(s_lo, w_hi) + _dot(s_hi, w_lo)) + b_ref[...]


def _modulation(cmat, w_ada, b_ada):
    n = w_ada.shape[1]
    return pl.pallas_call(
        _mod_kernel,
        grid=(n // MOD_COLS,),
        in_specs=[pl.BlockSpec((8, D_MODEL), lambda i: (0, 0)),
                  pl.BlockSpec((D_MODEL, MOD_COLS), lambda i: (0, i)),
                  pl.BlockSpec((1, MOD_COLS), lambda i: (0, i))],
        out_specs=pl.BlockSpec((8, MOD_COLS), lambda i: (0, i)),
        out_shape=jax.ShapeDtypeStruct((8, n), F32),
        name="modulation",
    )(cmat, w_ada, b_ada.reshape(1, n))


def _pool_mix(u, pos, seq):
    n = u.shape[0]

    def down(x, d):
        return jnp.where(pos >= d, pltpu.roll(x, d, 0), 0.0)

    def up(x, d):
        return jnp.where(pos < seq - d, pltpu.roll(x, n - d, 0), 0.0)

    return down, up


def _pool_group(u, pos, seq, w):
    down, up = _pool_mix(u, pos, seq)
    hw = w // 2
    back = u
    fwd = u
    d = 1
    while d < hw:
        back = back + down(back, d)
        fwd = fwd + up(fwd, d)
        d *= 2
    s = down(back, 1) + fwd
    posf = pos.astype(F32)
    cnt = jnp.minimum(posf + hw, float(seq)) - jnp.maximum(posf - hw, 0.0)
    return s / cnt - u


def _mixer_front(x_ref, mod_ref, gmix_ref, win_ref, hb_s, u_s, q_s, k_s, v_s):
    x = x_ref[...]
    shift, scale = mod_ref[0, 0:1, :], mod_ref[0, 1:2, :]
    hb = _norm_mod(x, gmix_ref[...], scale, shift).astype(BF16)
    hb_s[...] = hb
    u_s[...] = _dot(hb, win_ref[:, 0:POOL_DIM])
    for dst, base in ((q_s, POOL_DIM), (k_s, POOL_DIM + ATTN_DIM), (v_s, POOL_DIM + 2 * ATTN_DIM)):
        z = _dot(hb, win_ref[:, base:base + ATTN_DIM])
        for g in range(N_PAIRS):
            dst[g] = z[:, g * PAIR_W:(g + 1) * PAIR_W].astype(dst.dtype)


def _mixer_back(x_ref, mod_ref, win_ref, wpool_ref, ps_ref, wpa_ref, wpb_ref, wout_ref, x1_ref,
                hb_s, u_s, o_s, pg_s, seq):
    tile = x_ref.shape[0]
    gate = mod_ref[0, 2:3, :]
    pos = lax.broadcasted_iota(jnp.int32, (tile, 1), 0) % seq
    for g, w in enumerate(POOL_WINDOWS):
        cols = slice(g * POOL_GROUP_DIM, (g + 1) * POOL_GROUP_DIM)
        pg_s[:, cols] = _pool_group(u_s[:, cols], pos, seq, w).astype(BF16)
    for c in range(tile // TOK_BLOCK):
        rows = slice(c * TOK_BLOCK, (c + 1) * TOK_BLOCK)
        ys = []
        for g in range(len(POOL_WINDOWS)):
            cols = slice(g * POOL_GROUP_DIM, (g + 1) * POOL_GROUP_DIM)
            ys.append((_dot(pg_s[rows, cols], wpool_ref[g]) * ps_ref[:, cols]).astype(BF16))
        a = _dot(jnp.concatenate(ys, axis=1), wpa_ref[...])
        ob = _dot(jnp.concatenate([o_s[g, rows, :] for g in range(N_PAIRS)], axis=1), wpb_ref[...])
        gab = _dot(hb_s[rows, :], win_ref[:, POOL_DIM + 3 * ATTN_DIM:IN_DIM])
        merged = _sigmoid(gab[:, :D_MODEL]) * a + _sigmoid(gab[:, D_MODEL:]) * ob
        mix = _dot(merged.astype(BF16), wout_ref[...])
        x1_ref[rows, :] = x_ref[rows, :] + gate * mix


def _ctx_mixer_kernel(x_ref, mod_ref, gmix_ref, win_ref, wpool_ref, ps_ref, wpa_ref, wpb_ref, wout_ref,
                      x1_ref, ko_ref, vo_ref, hb_s, u_s, q_s, k_s, v_s, o_s, pg_s):
    _mixer_front(x_ref, mod_ref, gmix_ref, win_ref, hb_s, u_s, q_s, k_s, v_s)
    even = lax.broadcasted_iota(jnp.int32, (1, PAIR_W), 1) < HEAD_DIM
    tile = x_ref.shape[0]
    for s in range(tile // SEQ):
        rows = slice(s * SEQ, (s + 1) * SEQ)
        for g in range(N_PAIRS):
            q2, k2, v2 = q_s[g, rows, :], k_s[g, rows, :], v_s[g, rows, :]
            ko_ref[s, 0, 2 * g] = k2[:, :HEAD_DIM]
            ko_ref[s, 0, 2 * g + 1] = k2[:, HEAD_DIM:]
            vo_ref[s, 0, 2 * g] = v2[:, :HEAD_DIM]
            vo_ref[s, 0, 2 * g + 1] = v2[:, HEAD_DIM:]
            kb, vb = k2.astype(BF16), v2.astype(BF16)
            outs = []
            for par in range(2):
                qm = jnp.where(even if par == 0 else jnp.logical_not(even), q2, 0.0).astype(BF16)
                sc = _dot_nt(qm, kb) * ATTN_SCALE
                m = jnp.max(sc, axis=-1, keepdims=True)
                p = jnp.exp(sc - m)
                l = jnp.sum(p, axis=-1, keepdims=True)
                outs.append(_dot(p.astype(BF16), vb) / l)
            o_s[g, rows, :] = jnp.where(even, outs[0], outs[1]).astype(BF16)
    _mixer_back(x_ref, mod_ref, win_ref, wpool_ref, ps_ref, wpa_ref, wpb_ref, wout_ref, x1_ref,
                hb_s, u_s, o_s, pg_s, SEQ)


def _row_window(r):
    rs = min(max(r - WIN_R // 2, 0), ROWS - WIN_R)
    return rs, rs - r + WIN_R - 1


def _lat_mixer_kernel(x_ref, mod_ref, gmix_ref, win_ref, wpool_ref, ps_ref, wpa_ref, wpb_ref, wout_ref,
                      kc_ref, vc_ref, tb_ref, x1_ref, hb_s, u_s, q_s, k_s, v_s, o_s, pg_s):
    _mixer_front(x_ref, mod_ref, gmix_ref, win_ref, hb_s, u_s, q_s, k_s, v_s)
    even = lax.broadcasted_iota(jnp.int32, (1, PAIR_W), 1) < HEAD_DIM
    nk = WIN_R * GRID_W

    def pair_body(g, carry):
        q2 = q_s[g]
        kb, vb = k_s[g].astype(BF16), v_s[g].astype(BF16)
        kcb, vcb = kc_ref[0, g].astype(BF16), vc_ref[0, g].astype(BF16)
        outs = []
        for par in range(2):
            qm = jnp.where(even if par == 0 else jnp.logical_not(even), q2, 0.0).astype(BF16)
            s_ctx = _dot_nt(qm, kcb) * ATTN_SCALE
            slabs = []
            for r in range(ROWS):
                rs, rho = _row_window(r)
                bias = tb_ref[2 * g + par, rho % 2, :, (rho - rho % 2) * GRID_W:(rho - rho % 2) * GRID_W + nk]
                sl = _dot_nt(qm[r * GRID_W:(r + 1) * GRID_W, :], kb[rs * GRID_W:rs * GRID_W + nk, :])
                slabs.append(sl * ATTN_SCALE + bias)
            s_loc = jnp.concatenate(slabs, axis=0)
            m = jnp.maximum(jnp.max(s_loc, axis=-1, keepdims=True), jnp.max(s_ctx, axis=-1, keepdims=True))
            p_loc = jnp.exp(s_loc - m)
            p_ctx = jnp.exp(s_ctx - m)
            l = jnp.sum(p_loc, axis=-1, keepdims=True) + jnp.sum(p_ctx, axis=-1, keepdims=True)
            p_locb = p_loc.astype(BF16)
            o_rows = []
            for r in range(ROWS):
                rs, _ = _row_window(r)
                o_rows.append(_dot(p_locb[r * GRID_W:(r + 1) * GRID_W, :], vb[rs * GRID_W:rs * GRID_W + nk, :]))
            o = jnp.concatenate(o_rows, axis=0) + _dot(p_ctx.astype(BF16), vcb)
            outs.append(o / l)
        o_s[g] = jnp.where(even, outs[0], outs[1]).astype(BF16)
        return carry

    lax.fori_loop(0, N_PAIRS, pair_body, 0)
    _mixer_back(x_ref, mod_ref, win_ref, wpool_ref, ps_ref, wpa_ref, wpb_ref, wout_ref, x1_ref,
                hb_s, u_s, o_s, pg_s, DEC_SEQ)


def _mixer_scratch(tile, kv_dtype):
    return [pltpu.VMEM((tile, D_MODEL), BF16),
            pltpu.VMEM((tile, POOL_DIM), F32),
            pltpu.VMEM((N_PAIRS, tile, PAIR_W), BF16),
            pltpu.VMEM((N_PAIRS, tile, PAIR_W), kv_dtype),
            pltpu.VMEM((N_PAIRS, tile, PAIR_W), kv_dtype),
            pltpu.VMEM((N_PAIRS, tile, PAIR_W), BF16),
            pltpu.VMEM((tile, POOL_DIM), BF16)]


def _weight_specs():
    return [_const_spec((1, D_MODEL)),
            _const_spec((D_MODEL, IN_DIM)),
            _const_spec((len(POOL_WINDOWS), POOL_GROUP_DIM, POOL_GROUP_DIM)),
            _const_spec((1, POOL_DIM)),
            _const_spec((POOL_DIM, D_MODEL)),
            _const_spec((ATTN_DIM, D_MODEL)),
            _const_spec((D_MODEL, D_MODEL))]


def _ctx_mixer(x, modv, weights):
    n = x.shape[0]
    nseq = TOK_BLOCK // SEQ
    cache = jax.ShapeDtypeStruct((n // SEQ, 1, N_HEADS, SEQ, HEAD_DIM), F32)
    cache_spec = pl.BlockSpec((nseq, 1, N_HEADS, SEQ, HEAD_DIM), lambda i: (i, 0, 0, 0, 0))
    return pl.pallas_call(
        _ctx_mixer_kernel,
        grid=(n // TOK_BLOCK,),
        in_specs=[pl.BlockSpec((TOK_BLOCK, D_MODEL), lambda i: (i, 0)),
                  pl.BlockSpec((1, N_MOD, D_MODEL), lambda i: (0, 0, 0))] + _weight_specs(),
        out_specs=[pl.BlockSpec((TOK_BLOCK, D_MODEL), lambda i: (i, 0)), cache_spec, cache_spec],
        out_shape=[jax.ShapeDtypeStruct((n, D_MODEL), F32), cache, cache],
        scratch_shapes=_mixer_scratch(TOK_BLOCK, F32),
        compiler_params=pltpu.CompilerParams(dimension_semantics=("arbitrary",),
                                             vmem_limit_bytes=V7X_VMEM_LIMIT),
        name="ctx_mixer",
    )(x, modv, *weights)


def _lat_mixer(x, modv, weights, kc, vc, tb):
    n = x.shape[0]
    return pl.pallas_call(
        _lat_mixer_kernel,
        grid=(n // DEC_SEQ,),
        in_specs=[pl.BlockSpec((DEC_SEQ, D_MODEL), lambda i: (i, 0)),
                  pl.BlockSpec((1, N_MOD, D_MODEL), lambda i: (i + 1, 0, 0))] + _weight_specs() + [
                  pl.BlockSpec((1, N_PAIRS, PAST_LEN, PAIR_W), lambda i: (i, 0, 0, 0)),
                  pl.BlockSpec((1, N_PAIRS, PAST_LEN, PAIR_W), lambda i: (i, 0, 0, 0)),
                  _const_spec((N_HEADS, 2, GRID_W, ROWS * GRID_W))],
        out_specs=pl.BlockSpec((DEC_SEQ, D_MODEL), lambda i: (i, 0)),
        out_shape=jax.ShapeDtypeStruct((n, D_MODEL), F32),
        scratch_shapes=_mixer_scratch(DEC_SEQ, BF16),
        compiler_params=pltpu.CompilerParams(dimension_semantics=("arbitrary",),
                                             vmem_limit_bytes=V7X_VMEM_LIMIT),
        name="lat_mixer",
    )(x, modv, *weights, kc, vc, tb)


N_RPB_ROWS = 2 * WIN_R - 1
N_RPB_COLS = 2 * WIN_C - 1
TABLE_W = ROWS * GRID_W


def _bias_kernel(v_ref, keep_ref, o_ref):
    for h in range(N_HEADS):
        for par in range(2):
            x = jnp.broadcast_to(v_ref[h, par:par + 1, :], (GRID_W, TABLE_W))
            shifted = pltpu.roll(x, TABLE_W - (WIN_C - 1), 1, stride=1, stride_axis=0)
            o_ref[h, par] = jnp.where(keep_ref[par] > 0.0, shifted, NEG_INF)


def _bias_tables(rpb):
    col = np.arange(GRID_W)
    cs = np.clip(col - WIN_C // 2, 0, GRID_W - WIN_C)
    in_win = (col[None, :] >= cs[:, None]) & (col[None, :] < cs[:, None] + WIN_C)
    keep = np.tile(in_win.astype(np.float32), (2, 1, ROWS))
    keep[0, :, N_RPB_ROWS * GRID_W:] = 0.0
    keep[1, :, (N_RPB_ROWS - 1) * GRID_W:] = 0.0
    rp = jnp.pad(rpb.astype(F32), ((0, 0), (0, ROWS + 1 - N_RPB_ROWS), (0, GRID_W - N_RPB_COLS)))
    v = jnp.stack([rp[:, :ROWS].reshape(N_HEADS, TABLE_W), rp[:, 1:].reshape(N_HEADS, TABLE_W)], axis=1)
    return pl.pallas_call(
        _bias_kernel,
        out_shape=jax.ShapeDtypeStruct((N_HEADS, 2, GRID_W, TABLE_W), F32),
        name="bias_tables",
    )(v, jnp.asarray(keep))


SORT_ROWS = 512


def _sort_kernel(xc_ref, xl_ref, mod_ref, gffn_ref, wr_ref, br_ref,
                 xs_ref, dk_ref, np_ref, off_ref):
    j = pl.program_id(0)
    x = jnp.where(j < _N_CTX_BLOCKS, xc_ref[...], xl_ref[...])
    shift, scale = mod_ref[0, 3:4, :], mod_ref[0, 4:5, :]
    hb = _norm_mod(x, gffn_ref[...], scale, shift).astype(BF16)
    logits = _dot_nt(wr_ref[...], hb) + br_ref[...]
    eio = lax.broadcasted_iota(jnp.int32, logits.shape, 0)
    work = logits
    sels, vals = [], []
    for _ in range(TOP_K):
        m = jnp.max(work, axis=0, keepdims=True)
        idx = jnp.min(jnp.where(work == m, eio, N_EXPERTS), axis=0, keepdims=True)
        sel = eio == idx
        sels.append(sel)
        vals.append(m)
        work = jnp.where(sel, -jnp.inf, work)
    exps = [jnp.exp(v - vals[0]) for v in vals]
    den = exps[0] + exps[1] + exps[2] + exps[3]
    mask = jnp.zeros(logits.shape, F32)
    for sel in sels:
        mask = mask + jnp.where(sel, 1.0, 0.0)
    t_row = lax.broadcasted_iota(jnp.int32, (TOK_BLOCK, TOK_BLOCK), 0)
    t_col = lax.broadcasted_iota(jnp.int32, (TOK_BLOCK, TOK_BLOCK), 1)
    before = jnp.where(t_row < t_col, 1.0, 0.0).astype(BF16)
    rank = _dot(mask.astype(BF16), before)
    cnt = jnp.sum(mask, axis=1, keepdims=True)
    np16 = jnp.floor((cnt + (CHUNK - 1.0)) * (1.0 / CHUNK))
    e_row = lax.broadcasted_iota(jnp.int32, (N_EXPERTS, N_EXPERTS), 0)
    e_col = lax.broadcasted_iota(jnp.int32, (N_EXPERTS, N_EXPERTS), 1)
    lower = jnp.where(e_col < e_row, 1.0, 0.0).astype(BF16)
    np16_b = jnp.broadcast_to(np16, (N_EXPERTS, 128))
    off16 = _dot(lower, np16_b.astype(BF16))
    np_ref[0] = np16_b
    off_ref[0] = off16
    dest = off16[:, 0:1] * float(CHUNK) + rank
    dests = []
    for k in range(TOP_K):
        dk = jnp.sum(jnp.where(sels[k], dest, 0.0), axis=0, keepdims=True)
        dk_ref[0, k:k + 1, :] = dk
        dk_ref[0, TOP_K + k:TOP_K + k + 1, :] = exps[k] / den
        dests.append(dk.astype(jnp.int32))
    for c in range(BLOCK_CAP // SORT_ROWS):
        rio = lax.broadcasted_iota(jnp.int32, (SORT_ROWS, TOK_BLOCK), 0) + c * SORT_ROWS
        hit = (rio == dests[0]) | (rio == dests[1]) | (rio == dests[2]) | (rio == dests[3])
        onehot = jnp.where(hit, 1.0, 0.0).astype(BF16)
        xs_ref[c * SORT_ROWS:(c + 1) * SORT_ROWS, :] = _dot(onehot, hb).astype(BF16)


def _block_mod_index(j):
    return jnp.where(j < _N_CTX_BLOCKS, 0, 1 + (j - _N_CTX_BLOCKS) // (DEC_SEQ // TOK_BLOCK))


def _token_specs():
    return [pl.BlockSpec((TOK_BLOCK, D_MODEL), lambda j: (jnp.minimum(j, _N_CTX_BLOCKS - 1), 0)),
            pl.BlockSpec((TOK_BLOCK, D_MODEL), lambda j: (jnp.maximum(j - _N_CTX_BLOCKS, 0), 0)),
            pl.BlockSpec((1, N_MOD, D_MODEL), lambda j: (_block_mod_index(j), 0, 0))]


def _route_sort(x1c, x1l, modv, g_ffn, wr_t, br):
    tbl = jax.ShapeDtypeStruct((_N_BLOCKS, N_EXPERTS, 128), F32)
    tbl_spec = pl.BlockSpec((1, N_EXPERTS, 128), lambda j: (j, 0, 0))
    return pl.pallas_call(
        _sort_kernel,
        grid=(_N_BLOCKS,),
        in_specs=_token_specs() + [_const_spec((1, D_MODEL)),
                                   _const_spec((N_EXPERTS, D_MODEL)),
                                   _const_spec((N_EXPERTS, 1))],
        out_specs=[pl.BlockSpec((BLOCK_CAP, D_MODEL), lambda j: (j, 0)),
                   pl.BlockSpec((1, 2 * TOP_K, TOK_BLOCK), lambda j: (j, 0, 0)),
                   tbl_spec, tbl_spec],
        out_shape=[jax.ShapeDtypeStruct((_N_BLOCKS * BLOCK_CAP, D_MODEL), BF16),
                   jax.ShapeDtypeStruct((_N_BLOCKS, 2 * TOP_K, TOK_BLOCK), F32),
                   tbl, tbl],
        compiler_params=pltpu.CompilerParams(dimension_semantics=("arbitrary",),
                                             vmem_limit_bytes=V7X_VMEM_LIMIT),
        name="route_sort",
    )(x1c, x1l, modv, g_ffn, wr_t, br)


EXPERT_BUF_ROWS = 2048
SEGMENT_PIECES = (32, 16, 8, 4, 2, 1)
PASS_CHUNKS = EXPERT_BUF_ROWS // CHUNK


def _expert_kernel(np_ref, off_ref, wu_ref, bu_ref, wd_ref, bd_ref, xs_hbm, ys_hbm,
                   buf, wu_s, wd_s, gsem, ssem, pend):
    del xs_hbm
    e = pl.program_id(0)
    last = pl.num_programs(0) - 1
    slot = e % 2
    other = 1 - slot

    def for_chunks(ex, q_lo, slot_, gather, act):
        def block_body(j, q0):
            n = np_ref[j * N_EXPERTS + ex]
            off = off_ref[j * N_EXPERTS + ex]
            c_lo = jnp.clip(q_lo - q0, 0, n)
            m = jnp.clip(q_lo + PASS_CHUNKS - q0, 0, n) - c_lo
            row0 = j * BLOCK_CAP + (off + c_lo) * CHUNK
            brow0 = (q0 + c_lo - q_lo) * CHUNK
            for piece in SEGMENT_PIECES:
                @pl.when((m & piece) != 0)
                def _(piece=piece):
                    first = (m & ~(2 * piece - 1)) * CHUNK
                    hbm = ys_hbm.at[pl.ds(pl.multiple_of(row0 + first, CHUNK), piece * CHUNK), :]
                    vm = buf.at[slot_, pl.ds(pl.multiple_of(brow0 + first, CHUNK), piece * CHUNK), :]
                    if gather:
                        act(pltpu.make_async_copy(hbm, vm, gsem.at[slot_]))
                    else:
                        act(pltpu.make_async_copy(vm, hbm, ssem.at[slot_]))
            return q0 + n

        return lax.fori_loop(0, _N_BLOCKS, block_body, 0)

    def wait_chunks(sem, n, slot_):
        def wait_rows(rows):
            def body(i, carry):
                pltpu.make_async_copy(ys_hbm.at[pl.ds(0, rows), :], buf.at[slot_, pl.ds(0, rows), :], sem).wait()
                return carry
            return body

        lax.fori_loop(0, n // 8, wait_rows(8 * CHUNK), 0)
        lax.fori_loop(0, n % 8, wait_rows(CHUNK), 0)

    def start(copy):
        copy.start()

    def mlp_rows(r0, rows):
        x = buf[slot, pl.ds(r0, rows), :]
        gu = _dot(x, wu_s[...]) + bu_ref[0]
        gate = jnp.minimum(gu[:, :D_FF], SWIGLU_LIMIT)
        up = jnp.clip(gu[:, D_FF:], -SWIGLU_LIMIT, SWIGLU_LIMIT)
        glu = gate * _sigmoid(SWIGLU_ALPHA * gate)
        y = _dot(((up + 1.0) * glu).astype(BF16), wd_s[...]) + bd_ref[0]
        buf[slot, pl.ds(r0, rows), :] = y.astype(BF16)

    def compute(n):
        rows = n * CHUNK
        rem = rows % EXPERT_ROW_TILE
        half = EXPERT_ROW_TILE // 2
        n_full = rows // EXPERT_ROW_TILE + jnp.where(rem > half, 1, 0)

        def tile_body(t, carry):
            mlp_rows(pl.multiple_of(t * EXPERT_ROW_TILE, EXPERT_ROW_TILE), EXPERT_ROW_TILE)
            return carry

        lax.fori_loop(0, n_full, tile_body, 0)

        @pl.when((rem > 0) & (rem <= half))
        def _():
            mlp_rows(pl.multiple_of(rows - rem, half), half)

    @pl.when(e == 0)
    def _():
        buf[...] = jnp.zeros_like(buf)
        pend[0] = 0
        pend[1] = 0
        for_chunks(0, 0, 0, True, start)

    wait_chunks(ssem.at[other], pend[other], other)
    pend[other] = 0

    @pl.when(e < last)
    def _():
        for_chunks(e + 1, 0, other, True, start)

    wu_s[...] = wu_ref[0].astype(BF16)
    wd_s[...] = wd_ref[0].astype(BF16)

    total = lax.fori_loop(0, _N_BLOCKS, lambda j, acc: acc + np_ref[j * N_EXPERTS + e], 0)
    n0 = jnp.minimum(total, PASS_CHUNKS)
    wait_chunks(gsem.at[slot], n0, slot)
    compute(n0)
    for_chunks(e, 0, slot, False, start)
    pend[slot] = n0

    def pass_body(p, carry):
        wait_chunks(ssem.at[slot], pend[slot], slot)
        lo = p * PASS_CHUNKS
        n = jnp.minimum(total - lo, PASS_CHUNKS)
        for_chunks(e, lo, slot, True, start)
        wait_chunks(gsem.at[slot], n, slot)
        compute(n)
        for_chunks(e, lo, slot, False, start)
        pend[slot] = n
        return carry

    lax.fori_loop(1, (total + PASS_CHUNKS - 1) // PASS_CHUNKS, pass_body, 0)

    @pl.when(e == last)
    def _():
        wait_chunks(ssem.at[slot], pend[slot], slot)
        pend[slot] = 0


def _experts(np16, off16, w_up, b_up, w_down, b_down, xs):
    grid_spec = pltpu.PrefetchScalarGridSpec(
        num_scalar_prefetch=2,
        grid=(N_EXPERTS,),
        in_specs=[pl.BlockSpec((1, D_MODEL, 2 * D_FF), lambda e, *_: (e, 0, 0)),
                  pl.BlockSpec((1, 1, 2 * D_FF), lambda e, *_: (e, 0, 0)),
                  pl.BlockSpec((1, D_FF, D_MODEL), lambda e, *_: (e, 0, 0)),
                  pl.BlockSpec((1, 1, D_MODEL), lambda e, *_: (e, 0, 0)),
                  pl.BlockSpec(memory_space=pl.ANY)],
        out_specs=pl.BlockSpec(memory_space=pl.ANY),
        scratch_shapes=[pltpu.VMEM((2, EXPERT_BUF_ROWS, D_MODEL), BF16),
                        pltpu.VMEM((D_MODEL, 2 * D_FF), BF16),
                        pltpu.VMEM((D_FF, D_MODEL), BF16),
                        pltpu.SemaphoreType.DMA((2,)),
                        pltpu.SemaphoreType.DMA((2,)),
                        pltpu.SMEM((2,), jnp.int32)],
    )
    return pl.pallas_call(
        _expert_kernel,
        grid_spec=grid_spec,
        out_shape=jax.ShapeDtypeStruct(xs.shape, xs.dtype),
        input_output_aliases={6: 0},
        compiler_params=pltpu.CompilerParams(dimension_semantics=("arbitrary",),
                                             vmem_limit_bytes=V7X_VMEM_LIMIT),
        name="experts",
    )(np16, off16, w_up, b_up.reshape(N_EXPERTS, 1, 2 * D_FF), w_down,
      b_down.reshape(N_EXPERTS, 1, D_MODEL), xs)


def _combine_kernel(ys_ref, dk_ref, xc_ref, xl_ref, mod_ref, gfin_ref, yc_ref, yl_ref):
    j = pl.program_id(0)
    d = dk_ref[0]
    acc = jnp.zeros((TOK_BLOCK, D_MODEL), F32)
    for c in range(BLOCK_CAP // SORT_ROWS):
        cio = (lax.broadcasted_iota(jnp.int32, (TOK_BLOCK, SORT_ROWS), 1) + c * SORT_ROWS).astype(F32)
        w = jnp.zeros((TOK_BLOCK, SORT_ROWS), F32)
        for k in range(TOP_K):
            w = w + jnp.where(cio == d[:, k:k + 1], d[:, TOP_K + k:TOP_K + k + 1], 0.0)
        acc = acc + _dot(w.astype(BF16), ys_ref[c * SORT_ROWS:(c + 1) * SORT_ROWS, :])
    x1 = jnp.where(j < _N_CTX_BLOCKS, xc_ref[...], xl_ref[...])
    x2 = x1 + mod_ref[0, 5:6, :] * acc
    ms = jnp.mean(x2 * x2, axis=-1, keepdims=True)
    y = x2 * lax.rsqrt(ms + RMS_EPS) * gfin_ref[...]

    @pl.when(j < _N_CTX_BLOCKS)
    def _():
        yc_ref[...] = y

    @pl.when(j >= _N_CTX_BLOCKS)
    def _():
        yl_ref[...] = y


def _combine(ys, dk_t, x1c, x1l, modv, g_final):
    return pl.pallas_call(
        _combine_kernel,
        grid=(_N_BLOCKS,),
        in_specs=[pl.BlockSpec((BLOCK_CAP, D_MODEL), lambda j: (j, 0)),
                  pl.BlockSpec((1, TOK_BLOCK, 2 * TOP_K), lambda j: (j, 0, 0))] + _token_specs() + [
                  _const_spec((1, D_MODEL))],
        out_specs=[pl.BlockSpec((TOK_BLOCK, D_MODEL), lambda j: (jnp.minimum(j, _N_CTX_BLOCKS - 1), 0)),
                   pl.BlockSpec((TOK_BLOCK, D_MODEL), lambda j: (jnp.maximum(j - _N_CTX_BLOCKS, 0), 0))],
        out_shape=[jax.ShapeDtypeStruct(x1c.shape, F32), jax.ShapeDtypeStruct(x1l.shape, F32)],
        compiler_params=pltpu.CompilerParams(dimension_semantics=("arbitrary",),
                                             vmem_limit_bytes=V7X_VMEM_LIMIT),
        name="combine",
    )(ys, dk_t, x1c, x1l, modv, g_final)


def kernel(x_prompt, x_sample, cache_k, cache_v, c, c_ctx, w_ada, b_ada, g_mix, w_in, w_pool, pool_scale,
           w_pa, w_pb, rpb, w_out, g_ffn, w_router, b_router, w_up, b_up, w_down, b_down, g_final):
    assert w_ada.shape[0] == 1, "single trunk layer"
    batch, seq, d = x_prompt.shape
    dec_batch, dec_seq, _ = x_sample.shape
    assert (seq, dec_seq, d) == (SEQ, DEC_SEQ, D_MODEL)
    assert batch * seq == _N_CTX_BLOCKS * TOK_BLOCK and dec_batch * dec_seq == _N_LAT_BLOCKS * TOK_BLOCK

    cmat = jnp.concatenate([c_ctx[None, :], c, jnp.zeros((8 - 1 - dec_batch, d), F32)], axis=0)
    modv = _modulation(cmat, w_ada[0], b_ada[0]).reshape(8, N_MOD, d)

    weights = (g_mix[0][None, :], w_in[0].astype(BF16), w_pool[0].astype(BF16), pool_scale[0][None, :],
               w_pa[0].astype(BF16), w_pb[0].astype(BF16), w_out[0].astype(BF16))

    def by_pair(cache):
        z = cache[:, 0].reshape(dec_batch, N_PAIRS, 2, PAST_LEN, HEAD_DIM)
        return z.transpose(0, 1, 3, 2, 4).reshape(dec_batch, N_PAIRS, PAST_LEN, PAIR_W)

    x1c, new_k, new_v = _ctx_mixer(x_prompt.reshape(batch * seq, d), modv, weights)
    x1l = _lat_mixer(x_sample.reshape(dec_batch * dec_seq, d), modv, weights,
                     by_pair(cache_k), by_pair(cache_v), _bias_tables(rpb[0]))

    xs, dk, np16, off16 = _route_sort(x1c, x1l, modv, g_ffn[0][None, :],
                                      w_router[0].T.astype(BF16), b_router[0][:, None])
    np16_i = np16[:, :, 0].astype(jnp.int32).reshape(-1)
    off16_i = off16[:, :, 0].astype(jnp.int32).reshape(-1)
    ys = _experts(np16_i, off16_i, w_up[0], b_up[0], w_down[0], b_down[0], xs)
    yc, yl = _combine(ys, dk.transpose(0, 2, 1), x1c, x1l, modv, g_final[None, :])
    return (yc.reshape(batch, seq, d), yl.reshape(dec_batch, dec_seq, d), new_k, new_v)
```

```python
import functools

import jax
import jax.numpy as jnp
import numpy as np
from jax import lax
from jax.experimental import pallas as pl
from jax.experimental.pallas import tpu as pltpu

F32 = jnp.float32
BF16 = jnp.bfloat16

D_MODEL = 1024
SEQ = 256
DEC_SEQ = 1024
GRID_W = 64
ROWS = DEC_SEQ // GRID_W
N_HEADS = 8
HEAD_DIM = 64
N_PAIRS = N_HEADS // 2
PAIR_W = 2 * HEAD_DIM
PAST_LEN = 512
POOL_DIM = 512
POOL_WINDOWS = (2, 4, 8, 16)
POOL_GROUP_DIM = 128
ATTN_DIM = 512
WIN_R = 8
WIN_C = 16
N_EXPERTS = 32
TOP_K = 4
D_FF = 1024
SWIGLU_LIMIT = 7.0
SWIGLU_ALPHA = 1.702
N_MOD = 6
RMS_EPS = 1e-6
NEG_INF = -1e30
ATTN_SCALE = HEAD_DIM ** -0.5
IN_DIM = POOL_DIM + 3 * ATTN_DIM + 2 * D_MODEL

TOK_BLOCK = 512
CHUNK = 16
BLOCK_CAP = TOK_BLOCK * TOP_K + N_EXPERTS * CHUNK
ZERO_ROWS = 4 * CHUNK
BLOCK_STRIDE = BLOCK_CAP + ZERO_ROWS
EXPERT_ROW_TILE = 256
V7X_VMEM_LIMIT = 60 * 1024 * 1024

_N_CTX_BLOCKS = 8
_N_LAT_BLOCKS = 4
_N_BLOCKS = _N_CTX_BLOCKS + _N_LAT_BLOCKS

_NT = (((1,), (1,)), ((), ()))


def _dot(a, b):
    return jnp.dot(a, b, preferred_element_type=F32)


def _dot_nt(a, b):
    return lax.dot_general(a, b, _NT, preferred_element_type=F32)


def _sigmoid(x):
    return 1.0 / (1.0 + jnp.exp(-x))


def _norm_mod(x, gain, scale, shift):
    ms = jnp.mean(x * x, axis=-1, keepdims=True)
    return (x * lax.rsqrt(ms + RMS_EPS) * gain) * (1.0 + scale) + shift


def _const_spec(shape):
    zeros = (0,) * len(shape)
    return pl.BlockSpec(shape, lambda *_: zeros, pipeline_mode=pl.Buffered(1))


MOD_COLS = 1536


def _mod_kernel(c_ref, w_ref, b_ref, o_ref):
    c = c_ref[...]
    s = c * _sigmoid(c)
    s_hi = s.astype(BF16)
    s_lo = (s - s_hi.astype(F32)).astype(BF16)
    w = w_ref[...]
    w_hi = w.astype(BF16)
    w_lo = (w - w_hi.astype(F32)).astype(BF16)
    o_ref[...] = (_dot(s_hi, w_hi) + _dot(s_lo, w_hi) + _dot(s_hi, w_lo)) + b_ref[...]


def _modulation(cmat, w_ada, b_ada):
    n = w_ada.shape[1]
    return pl.pallas_call(
        _mod_kernel,
        grid=(n // MOD_COLS,),
        in_specs=[pl.BlockSpec((8, D_MODEL), lambda i: (0, 0)),
                  pl.BlockSpec((D_MODEL, MOD_COLS), lambda i: (0, i)),
                  pl.BlockSpec((1, MOD_COLS), lambda i: (0, i))],
        out_specs=pl.BlockSpec((8, MOD_COLS), lambda i: (0, i)),
        out_shape=jax.ShapeDtypeStruct((8, n), F32),
        name="modulation",
    )(cmat, w_ada, b_ada.reshape(1, n))


def _pool_mix(u, pos, seq):
    n = u.shape[0]

    def down(x, d):
        return jnp.where(pos >= d, pltpu.roll(x, d, 0), 0.0)

    def up(x, d):
        return jnp.where(pos < seq - d, pltpu.roll(x, n - d, 0), 0.0)

    return down, up


def _pool_group(u, pos, seq, w):
    down, up = _pool_mix(u, pos, seq)
    hw = w // 2
    back = u
    fwd = u
    d = 1
    while d < hw:
        back = back + down(back, d)
        fwd = fwd + up(fwd, d)
        d *= 2
    s = down(back, 1) + fwd
    posf = pos.astype(F32)
    cnt = jnp.minimum(posf + hw, float(seq)) - jnp.maximum(posf - hw, 0.0)
    return s / cnt - u


def _mixer_front(x_ref, mod_ref, gmix_ref, win_ref, hb_s, u_s, q_s, k_s, v_s):
    x = x_ref[...]
    shift, scale = mod_ref[0, 0:1, :], mod_ref[0, 1:2, :]
    hb = _norm_mod(x, gmix_ref[...], scale, shift).astype(BF16)
    hb_s[...] = hb
    u_s[...] = _dot(hb, win_ref[:, 0:POOL_DIM])
    for dst, base in ((q_s, POOL_DIM), (k_s, POOL_DIM + ATTN_DIM), (v_s, POOL_DIM + 2 * ATTN_DIM)):
        z = _dot(hb, win_ref[:, base:base + ATTN_DIM])
        for g in range(N_PAIRS):
            dst[g] = z[:, g * PAIR_W:(g + 1) * PAIR_W].astype(dst.dtype)


def _mixer_back(x_ref, mod_ref, win_ref, wpool_ref, ps_ref, wpa_ref, wpb_ref, wout_ref, x1_ref,
                hb_s, u_s, o_s, pg_s, seq):
    tile = x_ref.shape[0]
    gate = mod_ref[0, 2:3, :]
    pos = lax.broadcasted_iota(jnp.int32, (tile, 1), 0) % seq
    for g, w in enumerate(POOL_WINDOWS):
        cols = slice(g * POOL_GROUP_DIM, (g + 1) * POOL_GROUP_DIM)
        pg_s[:, cols] = _pool_group(u_s[:, cols], pos, seq, w).astype(BF16)
    for c in range(tile // TOK_BLOCK):
        rows = slice(c * TOK_BLOCK, (c + 1) * TOK_BLOCK)
        ys = []
        for g in range(len(POOL_WINDOWS)):
            cols = slice(g * POOL_GROUP_DIM, (g + 1) * POOL_GROUP_DIM)
            ys.append((_dot(pg_s[rows, cols], wpool_ref[g]) * ps_ref[:, cols]).astype(BF16))
        a = _dot(jnp.concatenate(ys, axis=1), wpa_ref[...])
        ob = _dot(jnp.concatenate([o_s[g, rows, :] for g in range(N_PAIRS)], axis=1), wpb_ref[...])
        gab = _dot(hb_s[rows, :], win_ref[:, POOL_DIM + 3 * ATTN_DIM:IN_DIM])
        merged = _sigmoid(gab[:, :D_MODEL]) * a + _sigmoid(gab[:, D_MODEL:]) * ob
        mix = _dot(merged.astype(BF16), wout_ref[...])
        x1_ref[rows, :] = x_ref[rows, :] + gate * mix


def _ctx_mixer_kernel(x_ref, mod_ref, gmix_ref, win_ref, wpool_ref, ps_ref, wpa_ref, wpb_ref, wout_ref,
                      x1_ref, ko_ref, vo_ref, hb_s, u_s, q_s, k_s, v_s, o_s, pg_s):
    _mixer_front(x_ref, mod_ref, gmix_ref, win_ref, hb_s, u_s, q_s, k_s, v_s)
    even = lax.broadcasted_iota(jnp.int32, (1, PAIR_W), 1) < HEAD_DIM
    tile = x_ref.shape[0]
    for s in range(tile // SEQ):
        rows = slice(s * SEQ, (s + 1) * SEQ)
        for g in range(N_PAIRS):
            q2, k2, v2 = q_s[g, rows, :], k_s[g, rows, :], v_s[g, rows, :]
            ko_ref[s, 0, 2 * g] = k2[:, :HEAD_DIM]
            ko_ref[s, 0, 2 * g + 1] = k2[:, HEAD_DIM:]
            vo_ref[s, 0, 2 * g] = v2[:, :HEAD_DIM]
            vo_ref[s, 0, 2 * g + 1] = v2[:, HEAD_DIM:]
            kb, vb = k2.astype(BF16), v2.astype(BF16)
            outs = []
            for par in range(2):
                qm = jnp.where(even if par == 0 else jnp.logical_not(even), q2, 0.0).astype(BF16)
                sc = _dot_nt(qm, kb) * ATTN_SCALE
                m = jnp.max(sc, axis=-1, keepdims=True)
                p = jnp.exp(sc - m)
                l = jnp.sum(p, axis=-1, keepdims=True)
                outs.append(_dot(p.astype(BF16), vb) / l)
            o_s[g, rows, :] = jnp.where(even, outs[0], outs[1]).astype(BF16)
    _mixer_back(x_ref, mod_ref, win_ref, wpool_ref, ps_ref, wpa_ref, wpb_ref, wout_ref, x1_ref,
                hb_s, u_s, o_s, pg_s, SEQ)


def _row_window(r):
    rs = min(max(r - WIN_R // 2, 0), ROWS - WIN_R)
    return rs, rs - r + WIN_R - 1


def _lat_mixer_kernel(x_ref, mod_ref, gmix_ref, win_ref, wpool_ref, ps_ref, wpa_ref, wpb_ref, wout_ref,
                      kc_ref, vc_ref, tb_ref, x1_ref, hb_s, u_s, q_s, k_s, v_s, o_s, pg_s):
    _mixer_front(x_ref, mod_ref, gmix_ref, win_ref, hb_s, u_s, q_s, k_s, v_s)
    even = lax.broadcasted_iota(jnp.int32, (1, PAIR_W), 1) < HEAD_DIM
    nk = WIN_R * GRID_W

    def pair_body(g, carry):
        q2 = q_s[g]
        kb, vb = k_s[g].astype(BF16), v_s[g].astype(BF16)
        kcb, vcb = kc_ref[0, g].astype(BF16), vc_ref[0, g].astype(BF16)
        outs = []
        for par in range(2):
            qm = jnp.where(even if par == 0 else jnp.logical_not(even), q2, 0.0).astype(BF16)
            s_ctx = _dot_nt(qm, kcb) * ATTN_SCALE
            slabs = []
            for r in range(ROWS):
                rs, rho = _row_window(r)
                bias = tb_ref[2 * g + par, rho % 2, :, (rho - rho % 2) * GRID_W:(rho - rho % 2) * GRID_W + nk]
                sl = _dot_nt(qm[r * GRID_W:(r + 1) * GRID_W, :], kb[rs * GRID_W:rs * GRID_W + nk, :])
                slabs.append(sl * ATTN_SCALE + bias)
            s_loc = jnp.concatenate(slabs, axis=0)
            m = jnp.maximum(jnp.max(s_loc, axis=-1, keepdims=True), jnp.max(s_ctx, axis=-1, keepdims=True))
            p_loc = jnp.exp(s_loc - m)
            p_ctx = jnp.exp(s_ctx - m)
            l = jnp.sum(p_loc, axis=-1, keepdims=True) + jnp.sum(p_ctx, axis=-1, keepdims=True)
            p_locb = p_loc.astype(BF16)
            o_rows = []
            for r in range(ROWS):
                rs, _ = _row_window(r)
                o_rows.append(_dot(p_locb[r * GRID_W:(r + 1) * GRID_W, :], vb[rs * GRID_W:rs * GRID_W + nk, :]))
            o = jnp.concatenate(o_rows, axis=0) + _dot(p_ctx.astype(BF16), vcb)
            outs.append(o / l)
        o_s[g] = jnp.where(even, outs[0], outs[1]).astype(BF16)
        return carry

    lax.fori_loop(0, N_PAIRS, pair_body, 0)
    _mixer_back(x_ref, mod_ref, win_ref, wpool_ref, ps_ref, wpa_ref, wpb_ref, wout_ref, x1_ref,
                hb_s, u_s, o_s, pg_s, DEC_SEQ)


def _mixer_scratch(tile, kv_dtype):
    return [pltpu.VMEM((tile, D_MODEL), BF16),
            pltpu.VMEM((tile, POOL_DIM), F32),
            pltpu.VMEM((N_PAIRS, tile, PAIR_W), BF16),
            pltpu.VMEM((N_PAIRS, tile, PAIR_W), kv_dtype),
            pltpu.VMEM((N_PAIRS, tile, PAIR_W), kv_dtype),
            pltpu.VMEM((N_PAIRS, tile, PAIR_W), BF16),
            pltpu.VMEM((tile, POOL_DIM), BF16)]


def _weight_specs():
    return [_const_spec((1, D_MODEL)),
            _const_spec((D_MODEL, IN_DIM)),
            _const_spec((len(POOL_WINDOWS), POOL_GROUP_DIM, POOL_GROUP_DIM)),
            _const_spec((1, POOL_DIM)),
            _const_spec((POOL_DIM, D_MODEL)),
            _const_spec((ATTN_DIM, D_MODEL)),
            _const_spec((D_MODEL, D_MODEL))]


def _ctx_mixer(x, modv, weights):
    n = x.shape[0]
    nseq = TOK_BLOCK // SEQ
    cache = jax.ShapeDtypeStruct((n // SEQ, 1, N_HEADS, SEQ, HEAD_DIM), F32)
    cache_spec = pl.BlockSpec((nseq, 1, N_HEADS, SEQ, HEAD_DIM), lambda i: (i, 0, 0, 0, 0))
    return pl.pallas_call(
        _ctx_mixer_kernel,
        grid=(n // TOK_BLOCK,),
        in_specs=[pl.BlockSpec((TOK_BLOCK, D_MODEL), lambda i: (i, 0)),
                  pl.BlockSpec((1, N_MOD, D_MODEL), lambda i: (0, 0, 0))] + _weight_specs(),
        out_specs=[pl.BlockSpec((TOK_BLOCK, D_MODEL), lambda i: (i, 0)), cache_spec, cache_spec],
        out_shape=[jax.ShapeDtypeStruct((n, D_MODEL), F32), cache, cache],
        scratch_shapes=_mixer_scratch(TOK_BLOCK, F32),
        compiler_params=pltpu.CompilerParams(dimension_semantics=("arbitrary",),
                                             vmem_limit_bytes=V7X_VMEM_LIMIT),
        name="ctx_mixer",
    )(x, modv, *weights)


def _lat_mixer(x, modv, weights, kc, vc, tb):
    n = x.shape[0]
    return pl.pallas_call(
        _lat_mixer_kernel,
        grid=(n // DEC_SEQ,),
        in_specs=[pl.BlockSpec((DEC_SEQ, D_MODEL), lambda i: (i, 0)),
                  pl.BlockSpec((1, N_MOD, D_MODEL), lambda i: (i + 1, 0, 0))] + _weight_specs() + [
                  pl.BlockSpec((1, N_PAIRS, PAST_LEN, PAIR_W), lambda i: (i, 0, 0, 0)),
                  pl.BlockSpec((1, N_PAIRS, PAST_LEN, PAIR_W), lambda i: (i, 0, 0, 0)),
                  _const_spec((N_HEADS, 2, GRID_W, ROWS * GRID_W))],
        out_specs=pl.BlockSpec((DEC_SEQ, D_MODEL), lambda i: (i, 0)),
        out_shape=jax.ShapeDtypeStruct((n, D_MODEL), F32),
        scratch_shapes=_mixer_scratch(DEC_SEQ, BF16),
        compiler_params=pltpu.CompilerParams(dimension_semantics=("arbitrary",),
                                             vmem_limit_bytes=V7X_VMEM_LIMIT),
        name="lat_mixer",
    )(x, modv, *weights, kc, vc, tb)


N_RPB_ROWS = 2 * WIN_R - 1
N_RPB_COLS = 2 * WIN_C - 1
TABLE_W = ROWS * GRID_W


def _bias_kernel(v_ref, keep_ref, o_ref):
    for h in range(N_HEADS):
        for par in range(2):
            x = jnp.broadcast_to(v_ref[h, par:par + 1, :], (GRID_W, TABLE_W))
            shifted = pltpu.roll(x, TABLE_W - (WIN_C - 1), 1, stride=1, stride_axis=0)
            o_ref[h, par] = jnp.where(keep_ref[par] > 0.0, shifted, NEG_INF)


def _bias_tables(rpb):
    col = np.arange(GRID_W)
    cs = np.clip(col - WIN_C // 2, 0, GRID_W - WIN_C)
    in_win = (col[None, :] >= cs[:, None]) & (col[None, :] < cs[:, None] + WIN_C)
    keep = np.tile(in_win.astype(np.float32), (2, 1, ROWS))
    keep[0, :, N_RPB_ROWS * GRID_W:] = 0.0
    keep[1, :, (N_RPB_ROWS - 1) * GRID_W:] = 0.0
    rp = jnp.pad(rpb.astype(F32), ((0, 0), (0, ROWS + 1 - N_RPB_ROWS), (0, GRID_W - N_RPB_COLS)))
    v = jnp.stack([rp[:, :ROWS].reshape(N_HEADS, TABLE_W), rp[:, 1:].reshape(N_HEADS, TABLE_W)], axis=1)
    return pl.pallas_call(
        _bias_kernel,
        out_shape=jax.ShapeDtypeStruct((N_HEADS, 2, GRID_W, TABLE_W), F32),
        name="bias_tables",
    )(v, jnp.asarray(keep))


SORT_ROWS = 512


def _sort_kernel(xc_ref, xl_ref, mod_ref, gffn_ref, wr_ref, br_ref,
                 xs_ref, dk_ref, np_ref, off_ref):
    j = pl.program_id(0)
    x = jnp.where(j < _N_CTX_BLOCKS, xc_ref[...], xl_ref[...])
    shift, scale = mod_ref[0, 3:4, :], mod_ref[0, 4:5, :]
    hb = _norm_mod(x, gffn_ref[...], scale, shift).astype(BF16)
    logits = _dot_nt(wr_ref[...], hb) + br_ref[...]
    eio = lax.broadcasted_iota(jnp.int32, logits.shape, 0)
    work = logits
    sels, vals = [], []
    for _ in range(TOP_K):
        m = jnp.max(work, axis=0, keepdims=True)
        idx = jnp.min(jnp.where(work == m, eio, N_EXPERTS), axis=0, keepdims=True)
        sel = eio == idx
        sels.append(sel)
        vals.append(m)
        work = jnp.where(sel, -jnp.inf, work)
    exps = [jnp.exp(v - vals[0]) for v in vals]
    den = exps[0] + exps[1] + exps[2] + exps[3]
    mask = jnp.zeros(logits.shape, F32)
    for sel in sels:
        mask = mask + jnp.where(sel, 1.0, 0.0)
    t_row = lax.broadcasted_iota(jnp.int32, (TOK_BLOCK, TOK_BLOCK), 0)
    t_col = lax.broadcasted_iota(jnp.int32, (TOK_BLOCK, TOK_BLOCK), 1)
    before = jnp.where(t_row < t_col, 1.0, 0.0).astype(BF16)
    rank = _dot(mask.astype(BF16), before)
    cnt = jnp.sum(mask, axis=1, keepdims=True)
    np16 = jnp.floor((cnt + (CHUNK - 1.0)) * (1.0 / CHUNK))
    e_row = lax.broadcasted_iota(jnp.int32, (N_EXPERTS, N_EXPERTS), 0)
    e_col = lax.broadcasted_iota(jnp.int32, (N_EXPERTS, N_EXPERTS), 1)
    lower = jnp.where(e_col < e_row, 1.0, 0.0).astype(BF16)
    np16_b = jnp.broadcast_to(np16, (N_EXPERTS, 128))
    off16 = _dot(lower, np16_b.astype(BF16))
    np_ref[0] = np16_b
    off_ref[0] = off16
    dest = off16[:, 0:1] * float(CHUNK) + rank
    dests = []
    for k in range(TOP_K):
        dk = jnp.sum(jnp.where(sels[k], dest, 0.0), axis=0, keepdims=True)
        dk_ref[0, k:k + 1, :] = dk
        dk_ref[0, TOP_K + k:TOP_K + k + 1, :] = exps[k] / den
        dests.append(dk.astype(jnp.int32))
    for c in range(BLOCK_CAP // SORT_ROWS):
        rio = lax.broadcasted_iota(jnp.int32, (SORT_ROWS, TOK_BLOCK), 0) + c * SORT_ROWS
        hit = (rio == dests[0]) | (rio == dests[1]) | (rio == dests[2]) | (rio == dests[3])
        onehot = jnp.where(hit, 1.0, 0.0).astype(BF16)
        xs_ref[c * SORT_ROWS:(c + 1) * SORT_ROWS, :] = _dot(onehot, hb).astype(BF16)
    xs_ref[BLOCK_CAP:, :] = jnp.zeros((ZERO_ROWS, D_MODEL), BF16)


def _block_mod_index(j):
    return jnp.where(j < _N_CTX_BLOCKS, 0, 1 + (j - _N_CTX_BLOCKS) // (DEC_SEQ // TOK_BLOCK))


def _token_specs():
    return [pl.BlockSpec((TOK_BLOCK, D_MODEL), lambda j: (jnp.minimum(j, _N_CTX_BLOCKS - 1), 0)),
            pl.BlockSpec((TOK_BLOCK, D_MODEL), lambda j: (jnp.maximum(j - _N_CTX_BLOCKS, 0), 0)),
            pl.BlockSpec((1, N_MOD, D_MODEL), lambda j: (_block_mod_index(j), 0, 0))]


def _route_sort(x1c, x1l, modv, g_ffn, wr_t, br):
    tbl = jax.ShapeDtypeStruct((_N_BLOCKS, N_EXPERTS, 128), F32)
    tbl_spec = pl.BlockSpec((1, N_EXPERTS, 128), lambda j: (j, 0, 0))
    return pl.pallas_call(
        _sort_kernel,
        grid=(_N_BLOCKS,),
        in_specs=_token_specs() + [_const_spec((1, D_MODEL)),
                                   _const_spec((N_EXPERTS, D_MODEL)),
                                   _const_spec((N_EXPERTS, 1))],
        out_specs=[pl.BlockSpec((BLOCK_STRIDE, D_MODEL), lambda j: (j, 0)),
                   pl.BlockSpec((1, 2 * TOP_K, TOK_BLOCK), lambda j: (j, 0, 0)),
                   tbl_spec, tbl_spec],
        out_shape=[jax.ShapeDtypeStruct((SORTED_ROWS, D_MODEL), BF16),
                   jax.ShapeDtypeStruct((_N_BLOCKS, 2 * TOP_K, TOK_BLOCK), F32),
                   tbl, tbl],
        compiler_params=pltpu.CompilerParams(dimension_semantics=("arbitrary",),
                                             vmem_limit_bytes=V7X_VMEM_LIMIT),
        name="route_sort",
    )(x1c, x1l, modv, g_ffn, wr_t, br)


EXPERT_BUF_ROWS = 2048
PASS_CHUNKS = EXPERT_BUF_ROWS // CHUNK
GROUP_CHUNKS = 8
SMALL_PIECES = ((4, 0), (2, 4), (1, 6))
DUMP_CHUNKS = 7
DUMP_ROWS = _N_BLOCKS * DUMP_CHUNKS * CHUNK
SORTED_ROWS = _N_BLOCKS * BLOCK_STRIDE + DUMP_ROWS


def _expert_kernel(np_ref, off_ref, wu_ref, bu_ref, wd_ref, bd_ref, xs_hbm, ys_hbm,
                   buf, wu_s, wd_s, gsem, ssem, pend):
    del xs_hbm
    e = pl.program_id(0)
    last = pl.num_programs(0) - 1
    slot = e % 2
    other = 1 - slot

    def copy(hbm_row, buf_row, rows, slot_, gather):
        hbm = ys_hbm.at[pl.ds(pl.multiple_of(hbm_row, CHUNK), rows), :]
        vm = buf.at[slot_, pl.ds(pl.multiple_of(buf_row, CHUNK), rows), :]
        if gather:
            return pltpu.make_async_copy(hbm, vm, gsem.at[slot_])
        return pltpu.make_async_copy(vm, hbm, ssem.at[slot_])

    def move_chunks(ex, q_lo, slot_, gather):
        q0 = 0
        segments = []
        for j in range(_N_BLOCKS):
            n = np_ref[j * N_EXPERTS + ex]
            off = off_ref[j * N_EXPERTS + ex]
            c_lo = jnp.clip(q_lo - q0, 0, n)
            m = jnp.clip(q_lo + PASS_CHUNKS - q0, 0, n) - c_lo
            row0 = j * BLOCK_STRIDE + (off + c_lo) * CHUNK
            brow0 = (q0 + c_lo - q_lo) * CHUNK
            groups = m // GROUP_CHUNKS
            for piece, dump_chunk in SMALL_PIECES:
                take = (m & piece) != 0
                first = (m & ~(2 * piece - 1)) * CHUNK
                dump = (j * DUMP_CHUNKS + dump_chunk) * CHUNK
                buf_row = jnp.where(take, brow0 + first, EXPERT_BUF_ROWS + dump)
                if gather:
                    hbm_row = jnp.where(take, row0 + first, j * BLOCK_STRIDE + BLOCK_CAP)
                else:
                    hbm_row = jnp.where(take, row0 + first, _N_BLOCKS * BLOCK_STRIDE + dump)
                copy(hbm_row, buf_row, piece * CHUNK, slot_, gather).start()
            segments.append((groups, row0, brow0))
            q0 = q0 + n
        n_groups = sum(g for g, _, _ in segments)

        @pl.when(n_groups > 0)
        def _():
            for groups, row0, brow0 in segments:
                def group_body(i, carry, row0=row0, brow0=brow0):
                    step = i * (GROUP_CHUNKS * CHUNK)
                    copy(row0 + step, brow0 + step, GROUP_CHUNKS * CHUNK, slot_, gather).start()
                    return carry

                lax.fori_loop(0, groups, group_body, 0)

        pend[(0 if gather else 2) + slot_] = n_groups
        return q0

    def wait_moved(slot_, gather):
        sem = gsem.at[slot_] if gather else ssem.at[slot_]
        fixed = _N_BLOCKS * DUMP_CHUNKS * CHUNK
        pltpu.make_async_copy(ys_hbm.at[pl.ds(0, fixed), :], buf.at[slot_, pl.ds(0, fixed), :], sem).wait()

        def group_body(i, carry):
            rows = GROUP_CHUNKS * CHUNK
            pltpu.make_async_copy(ys_hbm.at[pl.ds(0, rows), :], buf.at[slot_, pl.ds(0, rows), :], sem).wait()
            return carry

        lax.fori_loop(0, pend[(0 if gather else 2) + slot_], group_body, 0)

    def mlp_rows(r0, rows):
        x = buf[slot, pl.ds(r0, rows), :]
        gu = _dot(x, wu_s[...]) + bu_ref[0]
        gate = jnp.minimum(gu[:, :D_FF], SWIGLU_LIMIT)
        up = jnp.clip(gu[:, D_FF:], -SWIGLU_LIMIT, SWIGLU_LIMIT)
        glu = gate * _sigmoid(SWIGLU_ALPHA * gate)
        y = _dot(((up + 1.0) * glu).astype(BF16), wd_s[...]) + bd_ref[0]
        buf[slot, pl.ds(r0, rows), :] = y.astype(BF16)

    def compute(n):
        rows = n * CHUNK
        rem = rows % EXPERT_ROW_TILE
        half = EXPERT_ROW_TILE // 2
        n_full = rows // EXPERT_ROW_TILE + jnp.where(rem > half, 1, 0)

        def tile_body(t, carry):
            mlp_rows(pl.multiple_of(t * EXPERT_ROW_TILE, EXPERT_ROW_TILE), EXPERT_ROW_TILE)
            return carry

        lax.fori_loop(0, n_full, tile_body, 0)

        @pl.when((rem > 0) & (rem <= half))
        def _():
            mlp_rows(pl.multiple_of(rows - rem, half), half)

    @pl.when(e == 0)
    def _():
        buf[...] = jnp.zeros_like(buf)
        pend[3] = -1
        move_chunks(0, 0, 0, True)

    @pl.when(pend[2 + other] >= 0)
    def _():
        wait_moved(other, False)

    move_chunks((e + 1) % N_EXPERTS, 0, other, True)
    wu_s[...] = wu_ref[0].astype(BF16)
    wd_s[...] = wd_ref[0].astype(BF16)

    total = 0
    for j in range(_N_BLOCKS):
        total = total + np_ref[j * N_EXPERTS + e]
    wait_moved(slot, True)
    compute(jnp.minimum(total, PASS_CHUNKS))
    move_chunks(e, 0, slot, False)

    def pass_body(p, carry):
        wait_moved(slot, False)
        lo = p * PASS_CHUNKS
        move_chunks(e, lo, slot, True)
        wait_moved(slot, True)
        compute(jnp.minimum(total - lo, PASS_CHUNKS))
        move_chunks(e, lo, slot, False)
        return carry

    lax.fori_loop(1, (total + PASS_CHUNKS - 1) // PASS_CHUNKS, pass_body, 0)

    @pl.when(e == last)
    def _():
        wait_moved(other, True)
        wait_moved(slot, False)


def _experts(np16, off16, w_up, b_up, w_down, b_down, xs):
    grid_spec = pltpu.PrefetchScalarGridSpec(
        num_scalar_prefetch=2,
        grid=(N_EXPERTS,),
        in_specs=[pl.BlockSpec((1, D_MODEL, 2 * D_FF), lambda e, *_: (e, 0, 0)),
                  pl.BlockSpec((1, 1, 2 * D_FF), lambda e, *_: (e, 0, 0)),
                  pl.BlockSpec((1, D_FF, D_MODEL), lambda e, *_: (e, 0, 0)),
                  pl.BlockSpec((1, 1, D_MODEL), lambda e, *_: (e, 0, 0)),
                  pl.BlockSpec(memory_space=pl.ANY)],
        out_specs=pl.BlockSpec(memory_space=pl.ANY),
        scratch_shapes=[pltpu.VMEM((2, EXPERT_BUF_ROWS + DUMP_ROWS, D_MODEL), BF16),
                        pltpu.VMEM((D_MODEL, 2 * D_FF), BF16),
                        pltpu.VMEM((D_FF, D_MODEL), BF16),
                        pltpu.SemaphoreType.DMA((2,)),
                        pltpu.SemaphoreType.DMA((2,)),
                        pltpu.SMEM((4,), jnp.int32)],
    )
    return pl.pallas_call(
        _expert_kernel,
        grid_spec=grid_spec,
        out_shape=jax.ShapeDtypeStruct(xs.shape, xs.dtype),
        input_output_aliases={6: 0},
        compiler_params=pltpu.CompilerParams(dimension_semantics=("arbitrary",),
                                             vmem_limit_bytes=V7X_VMEM_LIMIT),
        name="experts",
    )(np16, off16, w_up, b_up.reshape(N_EXPERTS, 1, 2 * D_FF), w_down,
      b_down.reshape(N_EXPERTS, 1, D_MODEL), xs)


def _combine_kernel(ys_ref, dk_ref, xc_ref, xl_ref, mod_ref, gfin_ref, yc_ref, yl_ref):
    j = pl.program_id(0)
    d = dk_ref[0]
    acc = jnp.zeros((TOK_BLOCK, D_MODEL), F32)
    for c in range(BLOCK_CAP // SORT_ROWS):
        cio = (lax.broadcasted_iota(jnp.int32, (TOK_BLOCK, SORT_ROWS), 1) + c * SORT_ROWS).astype(F32)
        w = jnp.zeros((TOK_BLOCK, SORT_ROWS), F32)
        for k in range(TOP_K):
            w = w + jnp.where(cio == d[:, k:k + 1], d[:, TOP_K + k:TOP_K + k + 1], 0.0)
        acc = acc + _dot(w.astype(BF16), ys_ref[c * SORT_ROWS:(c + 1) * SORT_ROWS, :])
    x1 = jnp.where(j < _N_CTX_BLOCKS, xc_ref[...], xl_ref[...])
    x2 = x1 + mod_ref[0, 5:6, :] * acc
    ms = jnp.mean(x2 * x2, axis=-1, keepdims=True)
    y = x2 * lax.rsqrt(ms + RMS_EPS) * gfin_ref[...]

    @pl.when(j < _N_CTX_BLOCKS)
    def _():
        yc_ref[...] = y

    @pl.when(j >= _N_CTX_BLOCKS)
    def _():
        yl_ref[...] = y


def _combine(ys, dk_t, x1c, x1l, modv, g_final):
    return pl.pallas_call(
        _combine_kernel,
        grid=(_N_BLOCKS,),
        in_specs=[pl.BlockSpec((BLOCK_STRIDE, D_MODEL), lambda j: (j, 0)),
                  pl.BlockSpec((1, TOK_BLOCK, 2 * TOP_K), lambda j: (j, 0, 0))] + _token_specs() + [
                  _const_spec((1, D_MODEL))],
        out_specs=[pl.BlockSpec((TOK_BLOCK, D_MODEL), lambda j: (jnp.minimum(j, _N_CTX_BLOCKS - 1), 0)),
                   pl.BlockSpec((TOK_BLOCK, D_MODEL), lambda j: (jnp.maximum(j - _N_CTX_BLOCKS, 0), 0))],
        out_shape=[jax.ShapeDtypeStruct(x1c.shape, F32), jax.ShapeDtypeStruct(x1l.shape, F32)],
        compiler_params=pltpu.CompilerParams(dimension_semantics=("arbitrary",),
                                             vmem_limit_bytes=V7X_VMEM_LIMIT),
        name="combine",
    )(ys, dk_t, x1c, x1l, modv, g_final)


def kernel(x_prompt, x_sample, cache_k, cache_v, c, c_ctx, w_ada, b_ada, g_mix, w_in, w_pool, pool_scale,
           w_pa, w_pb, rpb, w_out, g_ffn, w_router, b_router, w_up, b_up, w_down, b_down, g_final):
    assert w_ada.shape[0] == 1, "single trunk layer"
    batch, seq, d = x_prompt.shape
    dec_batch, dec_seq, _ = x_sample.shape
    assert (seq, dec_seq, d) == (SEQ, DEC_SEQ, D_MODEL)
    assert batch * seq == _N_CTX_BLOCKS * TOK_BLOCK and dec_batch * dec_seq == _N_LAT_BLOCKS * TOK_BLOCK

    cmat = jnp.concatenate([c_ctx[None, :], c, jnp.zeros((8 - 1 - dec_batch, d), F32)], axis=0)
    modv = _modulation(cmat, w_ada[0], b_ada[0]).reshape(8, N_MOD, d)

    weights = (g_mix[0][None, :], w_in[0].astype(BF16), w_pool[0].astype(BF16), pool_scale[0][None, :],
               w_pa[0].astype(BF16), w_pb[0].astype(BF16), w_out[0].astype(BF16))

    def by_pair(cache):
        z = cache[:, 0].reshape(dec_batch, N_PAIRS, 2, PAST_LEN, HEAD_DIM)
        return z.transpose(0, 1, 3, 2, 4).reshape(dec_batch, N_PAIRS, PAST_LEN, PAIR_W)

    x1c, new_k, new_v = _ctx_mixer(x_prompt.reshape(batch * seq, d), modv, weights)
    x1l = _lat_mixer(x_sample.reshape(dec_batch * dec_seq, d), modv, weights,
                     by_pair(cache_k), by_pair(cache_v), _bias_tables(rpb[0]))

    xs, dk, np16, off16 = _route_sort(x1c, x1l, modv, g_ffn[0][None, :],
                                      w_router[0].T.astype(BF16), b_router[0][:, None])
    np16_i = np16[:, :, 0].astype(jnp.int32).reshape(-1)
    off16_i = off16[:, :, 0].astype(jnp.int32).reshape(-1)
    ys = _experts(np16_i, off16_i, w_up[0], b_up[0], w_down[0], b_down[0], xs)
    yc, yl = _combine(ys, dk.transpose(0, 2, 1), x1c, x1l, modv, g_final[None, :])
    return (yc.reshape(batch, seq, d), yl.reshape(dec_batch, dec_seq, d), new_k, new_v)
```

```python
import functools

import jax
import jax.numpy as jnp
import numpy as np
from jax import lax
from jax.experimental import pallas as pl
from jax.experimental.pallas import tpu as pltpu

F32 = jnp.float32
BF16 = jnp.bfloat16

D_MODEL = 1024
SEQ = 256
DEC_SEQ = 1024
GRID_W = 64
ROWS = DEC_SEQ // GRID_W
N_HEADS = 8
HEAD_DIM = 64
N_PAIRS = N_HEADS // 2
PAIR_W = 2 * HEAD_DIM
PAST_LEN = 512
POOL_DIM = 512
POOL_WINDOWS = (2, 4, 8, 16)
POOL_GROUP_DIM = 128
ATTN_DIM = 512
WIN_R = 8
WIN_C = 16
N_EXPERTS = 32
TOP_K = 4
D_FF = 1024
SWIGLU_LIMIT = 7.0
SWIGLU_ALPHA = 1.702
N_MOD = 6
RMS_EPS = 1e-6
NEG_INF = -1e30
ATTN_SCALE = HEAD_DIM ** -0.5
IN_DIM = POOL_DIM + 3 * ATTN_DIM + 2 * D_MODEL

TOK_BLOCK = 512
CHUNK = 16
BLOCK_CAP = TOK_BLOCK * TOP_K + N_EXPERTS * CHUNK
EXPERT_ROW_TILE = 256
V7X_VMEM_LIMIT = 60 * 1024 * 1024

_N_CTX_BLOCKS = 8
_N_LAT_BLOCKS = 4
_N_BLOCKS = _N_CTX_BLOCKS + _N_LAT_BLOCKS

_NT = (((1,), (1,)), ((), ()))


def _dot(a, b):
    return jnp.dot(a, b, preferred_element_type=F32)


def _dot_nt(a, b):
    return lax.dot_general(a, b, _NT, preferred_element_type=F32)


def _sigmoid(x):
    return 1.0 / (1.0 + jnp.exp(-x))


def _norm_mod(x, gain, scale, shift):
    ms = jnp.mean(x * x, axis=-1, keepdims=True)
    return (x * lax.rsqrt(ms + RMS_EPS) * gain) * (1.0 + scale) + shift


def _const_spec(shape):
    zeros = (0,) * len(shape)
    return pl.BlockSpec(shape, lambda *_: zeros, pipeline_mode=pl.Buffered(1))


MOD_COLS = 1536


def _mod_kernel(c_ref, w_ref, b_ref, o_ref):
    c = c_ref[...]
    s = c * _sigmoid(c)
    s_hi = s.astype(BF16)
    s_lo = (s - s_hi.astype(F32)).astype(BF16)
    w = w_ref[...]
    w_hi = w.astype(BF16)
    w_lo = (w - w_hi.astype(F32)).astype(BF16)
    o_ref[...] = (_dot(s_hi, w_hi) + _dot(s_lo, w_hi) + _dot(s_hi, w_lo)) + b_ref[...]


def _modulation(cmat, w_ada, b_ada):
    n = w_ada.shape[1]
    return pl.pallas_call(
        _mod_kernel,
        grid=(n // MOD_COLS,),
        in_specs=[pl.BlockSpec((8, D_MODEL), lambda i: (0, 0)),
                  pl.BlockSpec((D_MODEL, MOD_COLS), lambda i: (0, i)),
                  pl.BlockSpec((1, MOD_COLS), lambda i: (0, i))],
        out_specs=pl.BlockSpec((8, MOD_COLS), lambda i: (0, i)),
        out_shape=jax.ShapeDtypeStruct((8, n), F32),
        name="modulation",
    )(cmat, w_ada, b_ada.reshape(1, n))


def _pool_mix(u, pos, seq):
    n = u.shape[0]

    def down(x, d):
        return jnp.where(pos >= d, pltpu.roll(x, d, 0), 0.0)

    def up(x, d):
        return jnp.where(pos < seq - d, pltpu.roll(x, n - d, 0), 0.0)

    return down, up


def _pool_group(u, pos, seq, w):
    down, up = _pool_mix(u, pos, seq)
    hw = w // 2
    back = u
    fwd = u
    d = 1
    while d < hw:
        back = back + down(back, d)
        fwd = fwd + up(fwd, d)
        d *= 2
    s = down(back, 1) + fwd
    posf = pos.astype(F32)
    cnt = jnp.minimum(posf + hw, float(seq)) - jnp.maximum(posf - hw, 0.0)
    return s / cnt - u


def _mixer_front(x_ref, mod_ref, gmix_ref, win_ref, hb_s, u_s, q_s, k_s, v_s):
    x = x_ref[...]
    shift, scale = mod_ref[0, 0:1, :], mod_ref[0, 1:2, :]
    hb = _norm_mod(x, gmix_ref[...], scale, shift).astype(BF16)
    hb_s[...] = hb
    u_s[...] = _dot(hb, win_ref[:, 0:POOL_DIM])
    for dst, base in ((q_s, POOL_DIM), (k_s, POOL_DIM + ATTN_DIM), (v_s, POOL_DIM + 2 * ATTN_DIM)):
        z = _dot(hb, win_ref[:, base:base + ATTN_DIM])
        for g in range(N_PAIRS):
            dst[g] = z[:, g * PAIR_W:(g + 1) * PAIR_W].astype(dst.dtype)


def _mixer_back(x_ref, mod_ref, win_ref, wpool_ref, ps_ref, wpa_ref, wpb_ref, wout_ref, x1_ref,
                hb_s, u_s, o_s, pg_s, seq):
    tile = x_ref.shape[0]
    gate = mod_ref[0, 2:3, :]
    pos = lax.broadcasted_iota(jnp.int32, (tile, 1), 0) % seq
    for g, w in enumerate(POOL_WINDOWS):
        cols = slice(g * POOL_GROUP_DIM, (g + 1) * POOL_GROUP_DIM)
        pg_s[:, cols] = _pool_group(u_s[:, cols], pos, seq, w).astype(BF16)
    for c in range(tile // TOK_BLOCK):
        rows = slice(c * TOK_BLOCK, (c + 1) * TOK_BLOCK)
        ys = []
        for g in range(len(POOL_WINDOWS)):
            cols = slice(g * POOL_GROUP_DIM, (g + 1) * POOL_GROUP_DIM)
            ys.append((_dot(pg_s[rows, cols], wpool_ref[g]) * ps_ref[:, cols]).astype(BF16))
        a = _dot(jnp.concatenate(ys, axis=1), wpa_ref[...])
        ob = _dot(jnp.concatenate([o_s[g, rows, :] for g in range(N_PAIRS)], axis=1), wpb_ref[...])
        gab = _dot(hb_s[rows, :], win_ref[:, POOL_DIM + 3 * ATTN_DIM:IN_DIM])
        merged = _sigmoid(gab[:, :D_MODEL]) * a + _sigmoid(gab[:, D_MODEL:]) * ob
        mix = _dot(merged.astype(BF16), wout_ref[...])
        x1_ref[rows, :] = x_ref[rows, :] + gate * mix


def _ctx_mixer_kernel(x_ref, mod_ref, gmix_ref, win_ref, wpool_ref, ps_ref, wpa_ref, wpb_ref, wout_ref,
                      x1_ref, ko_ref, vo_ref, hb_s, u_s, q_s, k_s, v_s, o_s, pg_s):
    _mixer_front(x_ref, mod_ref, gmix_ref, win_ref, hb_s, u_s, q_s, k_s, v_s)
    even = lax.broadcasted_iota(jnp.int32, (1, PAIR_W), 1) < HEAD_DIM
    tile = x_ref.shape[0]
    for s in range(tile // SEQ):
        rows = slice(s * SEQ, (s + 1) * SEQ)
        for g in range(N_PAIRS):
            q2, k2, v2 = q_s[g, rows, :], k_s[g, rows, :], v_s[g, rows, :]
            ko_ref[s, 0, 2 * g] = k2[:, :HEAD_DIM]
            ko_ref[s, 0, 2 * g + 1] = k2[:, HEAD_DIM:]
            vo_ref[s, 0, 2 * g] = v2[:, :HEAD_DIM]
            vo_ref[s, 0, 2 * g + 1] = v2[:, HEAD_DIM:]
            kb, vb = k2.astype(BF16), v2.astype(BF16)
            outs = []
            for par in range(2):
                qm = jnp.where(even if par == 0 else jnp.logical_not(even), q2, 0.0).astype(BF16)
                sc = _dot_nt(qm, kb) * ATTN_SCALE
                m = jnp.max(sc, axis=-1, keepdims=True)
                p = jnp.exp(sc - m)
                l = jnp.sum(p, axis=-1, keepdims=True)
                outs.append(_dot(p.astype(BF16), vb) / l)
            o_s[g, rows, :] = jnp.where(even, outs[0], outs[1]).astype(BF16)
    _mixer_back(x_ref, mod_ref, win_ref, wpool_ref, ps_ref, wpa_ref, wpb_ref, wout_ref, x1_ref,
                hb_s, u_s, o_s, pg_s, SEQ)


def _row_window(r):
    rs = min(max(r - WIN_R // 2, 0), ROWS - WIN_R)
    return rs, rs - r + WIN_R - 1


def _lat_mixer_kernel(x_ref, mod_ref, gmix_ref, win_ref, wpool_ref, ps_ref, wpa_ref, wpb_ref, wout_ref,
                      kc_ref, vc_ref, tb_ref, x1_ref, hb_s, u_s, q_s, k_s, v_s, o_s, pg_s):
    _mixer_front(x_ref, mod_ref, gmix_ref, win_ref, hb_s, u_s, q_s, k_s, v_s)
    even = lax.broadcasted_iota(jnp.int32, (1, PAIR_W), 1) < HEAD_DIM
    nk = WIN_R * GRID_W

    def pair_body(g, carry):
        q2 = q_s[g]
        kb, vb = k_s[g].astype(BF16), v_s[g].astype(BF16)
        kcb, vcb = kc_ref[0, g].astype(BF16), vc_ref[0, g].astype(BF16)
        outs = []
        for par in range(2):
            qm = jnp.where(even if par == 0 else jnp.logical_not(even), q2, 0.0).astype(BF16)
            s_ctx = _dot_nt(qm, kcb) * ATTN_SCALE
            slabs = []
            for r in range(ROWS):
                rs, rho = _row_window(r)
                bias = tb_ref[2 * g + par, rho % 2, :, (rho - rho % 2) * GRID_W:(rho - rho % 2) * GRID_W + nk]
                sl = _dot_nt(qm[r * GRID_W:(r + 1) * GRID_W, :], kb[rs * GRID_W:rs * GRID_W + nk, :])
                slabs.append(sl * ATTN_SCALE + bias)
            s_loc = jnp.concatenate(slabs, axis=0)
            m = jnp.maximum(jnp.max(s_loc, axis=-1, keepdims=True), jnp.max(s_ctx, axis=-1, keepdims=True))
            p_loc = jnp.exp(s_loc - m)
            p_ctx = jnp.exp(s_ctx - m)
            l = jnp.sum(p_loc, axis=-1, keepdims=True) + jnp.sum(p_ctx, axis=-1, keepdims=True)
            p_locb = p_loc.astype(BF16)
            o_rows = []
            for r in range(ROWS):
                rs, _ = _row_window(r)
                o_rows.append(_dot(p_locb[r * GRID_W:(r + 1) * GRID_W, :], vb[rs * GRID_W:rs * GRID_W + nk, :]))
            o = jnp.concatenate(o_rows, axis=0) + _dot(p_ctx.astype(BF16), vcb)
            outs.append(o / l)
        o_s[g] = jnp.where(even, outs[0], outs[1]).astype(BF16)
        return carry

    lax.fori_loop(0, N_PAIRS, pair_body, 0)
    _mixer_back(x_ref, mod_ref, win_ref, wpool_ref, ps_ref, wpa_ref, wpb_ref, wout_ref, x1_ref,
                hb_s, u_s, o_s, pg_s, DEC_SEQ)


def _mixer_scratch(tile, kv_dtype):
    return [pltpu.VMEM((tile, D_MODEL), BF16),
            pltpu.VMEM((tile, POOL_DIM), F32),
            pltpu.VMEM((N_PAIRS, tile, PAIR_W), BF16),
            pltpu.VMEM((N_PAIRS, tile, PAIR_W), kv_dtype),
            pltpu.VMEM((N_PAIRS, tile, PAIR_W), kv_dtype),
            pltpu.VMEM((N_PAIRS, tile, PAIR_W), BF16),
            pltpu.VMEM((tile, POOL_DIM), BF16)]


def _weight_specs():
    return [_const_spec((1, D_MODEL)),
            _const_spec((D_MODEL, IN_DIM)),
            _const_spec((len(POOL_WINDOWS), POOL_GROUP_DIM, POOL_GROUP_DIM)),
            _const_spec((1, POOL_DIM)),
            _const_spec((POOL_DIM, D_MODEL)),
            _const_spec((ATTN_DIM, D_MODEL)),
            _const_spec((D_MODEL, D_MODEL))]


def _ctx_mixer(x, modv, weights):
    n = x.shape[0]
    nseq = TOK_BLOCK // SEQ
    cache = jax.ShapeDtypeStruct((n // SEQ, 1, N_HEADS, SEQ, HEAD_DIM), F32)
    cache_spec = pl.BlockSpec((nseq, 1, N_HEADS, SEQ, HEAD_DIM), lambda i: (i, 0, 0, 0, 0))
    return pl.pallas_call(
        _ctx_mixer_kernel,
        grid=(n // TOK_BLOCK,),
        in_specs=[pl.BlockSpec((TOK_BLOCK, D_MODEL), lambda i: (i, 0)),
                  pl.BlockSpec((1, N_MOD, D_MODEL), lambda i: (0, 0, 0))] + _weight_specs(),
        out_specs=[pl.BlockSpec((TOK_BLOCK, D_MODEL), lambda i: (i, 0)), cache_spec, cache_spec],
        out_shape=[jax.ShapeDtypeStruct((n, D_MODEL), F32), cache, cache],
        scratch_shapes=_mixer_scratch(TOK_BLOCK, F32),
        compiler_params=pltpu.CompilerParams(dimension_semantics=("arbitrary",),
                                             vmem_limit_bytes=V7X_VMEM_LIMIT),
        name="ctx_mixer",
    )(x, modv, *weights)


def _lat_mixer(x, modv, weights, kc, vc, tb):
    n = x.shape[0]
    return pl.pallas_call(
        _lat_mixer_kernel,
        grid=(n // DEC_SEQ,),
        in_specs=[pl.BlockSpec((DEC_SEQ, D_MODEL), lambda i: (i, 0)),
                  pl.BlockSpec((1, N_MOD, D_MODEL), lambda i: (i + 1, 0, 0))] + _weight_specs() + [
                  pl.BlockSpec((1, N_PAIRS, PAST_LEN, PAIR_W), lambda i: (i, 0, 0, 0)),
                  pl.BlockSpec((1, N_PAIRS, PAST_LEN, PAIR_W), lambda i: (i, 0, 0, 0)),
                  _const_spec((N_HEADS, 2, GRID_W, ROWS * GRID_W))],
        out_specs=pl.BlockSpec((DEC_SEQ, D_MODEL), lambda i: (i, 0)),
        out_shape=jax.ShapeDtypeStruct((n, D_MODEL), F32),
        scratch_shapes=_mixer_scratch(DEC_SEQ, BF16),
        compiler_params=pltpu.CompilerParams(dimension_semantics=("arbitrary",),
                                             vmem_limit_bytes=V7X_VMEM_LIMIT),
        name="lat_mixer",
    )(x, modv, *weights, kc, vc, tb)


N_RPB_ROWS = 2 * WIN_R - 1
N_RPB_COLS = 2 * WIN_C - 1
TABLE_W = ROWS * GRID_W


def _bias_kernel(v_ref, keep_ref, o_ref):
    for h in range(N_HEADS):
        for par in range(2):
            x = jnp.broadcast_to(v_ref[h, par:par + 1, :], (GRID_W, TABLE_W))
            shifted = pltpu.roll(x, TABLE_W - (WIN_C - 1), 1, stride=1, stride_axis=0)
            o_ref[h, par] = jnp.where(keep_ref[par] > 0.0, shifted, NEG_INF)


def _bias_tables(rpb):
    col = np.arange(GRID_W)
    cs = np.clip(col - WIN_C // 2, 0, GRID_W - WIN_C)
    in_win = (col[None, :] >= cs[:, None]) & (col[None, :] < cs[:, None] + WIN_C)
    keep = np.tile(in_win.astype(np.float32), (2, 1, ROWS))
    keep[0, :, N_RPB_ROWS * GRID_W:] = 0.0
    keep[1, :, (N_RPB_ROWS - 1) * GRID_W:] = 0.0
    rp = jnp.pad(rpb.astype(F32), ((0, 0), (0, ROWS + 1 - N_RPB_ROWS), (0, GRID_W - N_RPB_COLS)))
    v = jnp.stack([rp[:, :ROWS].reshape(N_HEADS, TABLE_W), rp[:, 1:].reshape(N_HEADS, TABLE_W)], axis=1)
    return pl.pallas_call(
        _bias_kernel,
        out_shape=jax.ShapeDtypeStruct((N_HEADS, 2, GRID_W, TABLE_W), F32),
        name="bias_tables",
    )(v, jnp.asarray(keep))


SORT_ROWS = 512


def _sort_kernel(xc_ref, xl_ref, mod_ref, gffn_ref, wr_ref, br_ref,
                 xs_ref, dk_ref, np_ref, off_ref):
    j = pl.program_id(0)
    x = jnp.where(j < _N_CTX_BLOCKS, xc_ref[...], xl_ref[...])
    shift, scale = mod_ref[0, 3:4, :], mod_ref[0, 4:5, :]
    hb = _norm_mod(x, gffn_ref[...], scale, shift).astype(BF16)
    logits = _dot_nt(wr_ref[...], hb) + br_ref[...]
    eio = lax.broadcasted_iota(jnp.int32, logits.shape, 0)
    work = logits
    sels, vals = [], []
    for _ in range(TOP_K):
        m = jnp.max(work, axis=0, keepdims=True)
        idx = jnp.min(jnp.where(work == m, eio, N_EXPERTS), axis=0, keepdims=True)
        sel = eio == idx
        sels.append(sel)
        vals.append(m)
        work = jnp.where(sel, -jnp.inf, work)
    exps = [jnp.exp(v - vals[0]) for v in vals]
    den = exps[0] + exps[1] + exps[2] + exps[3]
    mask = jnp.zeros(logits.shape, F32)
    for sel in sels:
        mask = mask + jnp.where(sel, 1.0, 0.0)
    t_row = lax.broadcasted_iota(jnp.int32, (TOK_BLOCK, TOK_BLOCK), 0)
    t_col = lax.broadcasted_iota(jnp.int32, (TOK_BLOCK, TOK_BLOCK), 1)
    before = jnp.where(t_row < t_col, 1.0, 0.0).astype(BF16)
    rank = _dot(mask.astype(BF16), before)
    cnt = jnp.sum(mask, axis=1, keepdims=True)
    np16 = jnp.floor((cnt + (CHUNK - 1.0)) * (1.0 / CHUNK))
    e_row = lax.broadcasted_iota(jnp.int32, (N_EXPERTS, N_EXPERTS), 0)
    e_col = lax.broadcasted_iota(jnp.int32, (N_EXPERTS, N_EXPERTS), 1)
    lower = jnp.where(e_col < e_row, 1.0, 0.0).astype(BF16)
    np16_b = jnp.broadcast_to(np16, (N_EXPERTS, 128))
    off16 = _dot(lower, np16_b.astype(BF16))
    np_ref[0] = np16_b
    off_ref[0] = off16
    dest = off16[:, 0:1] * float(CHUNK) + rank
    dests = []
    for k in range(TOP_K):
        dk = jnp.sum(jnp.where(sels[k], dest, 0.0), axis=0, keepdims=True)
        dk_ref[0, k:k + 1, :] = dk
        dk_ref[0, TOP_K + k:TOP_K + k + 1, :] = exps[k] / den
        dests.append(dk.astype(jnp.int32))
    for c in range(BLOCK_CAP // SORT_ROWS):
        rio = lax.broadcasted_iota(jnp.int32, (SORT_ROWS, TOK_BLOCK), 0) + c * SORT_ROWS
        hit = (rio == dests[0]) | (rio == dests[1]) | (rio == dests[2]) | (rio == dests[3])
        onehot = jnp.where(hit, 1.0, 0.0).astype(BF16)
        xs_ref[c * SORT_ROWS:(c + 1) * SORT_ROWS, :] = _dot(onehot, hb).astype(BF16)


def _block_mod_index(j):
    return jnp.where(j < _N_CTX_BLOCKS, 0, 1 + (j - _N_CTX_BLOCKS) // (DEC_SEQ // TOK_BLOCK))


def _token_specs():
    return [pl.BlockSpec((TOK_BLOCK, D_MODEL), lambda j: (jnp.minimum(j, _N_CTX_BLOCKS - 1), 0)),
            pl.BlockSpec((TOK_BLOCK, D_MODEL), lambda j: (jnp.maximum(j - _N_CTX_BLOCKS, 0), 0)),
            pl.BlockSpec((1, N_MOD, D_MODEL), lambda j: (_block_mod_index(j), 0, 0))]


def _route_sort(x1c, x1l, modv, g_ffn, wr_t, br):
    tbl = jax.ShapeDtypeStruct((_N_BLOCKS, N_EXPERTS, 128), F32)
    tbl_spec = pl.BlockSpec((1, N_EXPERTS, 128), lambda j: (j, 0, 0))
    return pl.pallas_call(
        _sort_kernel,
        grid=(_N_BLOCKS,),
        in_specs=_token_specs() + [_const_spec((1, D_MODEL)),
                                   _const_spec((N_EXPERTS, D_MODEL)),
                                   _const_spec((N_EXPERTS, 1))],
        out_specs=[pl.BlockSpec((BLOCK_CAP, D_MODEL), lambda j: (j, 0)),
                   pl.BlockSpec((1, 2 * TOP_K, TOK_BLOCK), lambda j: (j, 0, 0)),
                   tbl_spec, tbl_spec],
        out_shape=[jax.ShapeDtypeStruct((_N_BLOCKS * BLOCK_CAP, D_MODEL), BF16),
                   jax.ShapeDtypeStruct((_N_BLOCKS, 2 * TOP_K, TOK_BLOCK), F32),
                   tbl, tbl],
        compiler_params=pltpu.CompilerParams(dimension_semantics=("arbitrary",),
                                             vmem_limit_bytes=V7X_VMEM_LIMIT),
        name="route_sort",
    )(x1c, x1l, modv, g_ffn, wr_t, br)


EXPERT_BUF_ROWS = 2048
PASS_CHUNKS = EXPERT_BUF_ROWS // CHUNK
N_ROW_BUFS = 3


def _expert_kernel(np_ref, off_ref, wu_ref, bu_ref, wd_ref, bd_ref, xs_hbm, ys_hbm,
                   buf, wu_s, wd_s, gsem, ssem, pend):
    del xs_hbm
    e = pl.program_id(0)
    last = pl.num_programs(0) - 1
    slot = e % N_ROW_BUFS
    nxt = (e + 1) % N_ROW_BUFS

    def move_chunks(ex, q_lo, slot_, gather):
        def block_body(j, q0):
            n = np_ref[j * N_EXPERTS + ex]
            off = off_ref[j * N_EXPERTS + ex]
            c_lo = jnp.clip(q_lo - q0, 0, n)
            m = jnp.clip(q_lo + PASS_CHUNKS - q0, 0, n) - c_lo

            @pl.when(m > 0)
            def _():
                rows = pl.multiple_of(m * CHUNK, CHUNK)
                row0 = pl.multiple_of(j * BLOCK_CAP + (off + c_lo) * CHUNK, CHUNK)
                brow0 = pl.multiple_of((q0 + c_lo - q_lo) * CHUNK, CHUNK)
                hbm = ys_hbm.at[pl.ds(row0, rows), :]
                vm = buf.at[slot_, pl.ds(brow0, rows), :]
                if gather:
                    pltpu.make_async_copy(hbm, vm, gsem.at[slot_]).start()
                else:
                    pltpu.make_async_copy(vm, hbm, ssem.at[slot_]).start()

            return q0 + n

        lax.fori_loop(0, _N_BLOCKS, block_body, 0)

    def wait_chunks(sem, n, slot_):
        @pl.when(n > 0)
        def _():
            rows = pl.multiple_of(n * CHUNK, CHUNK)
            pltpu.make_async_copy(ys_hbm.at[pl.ds(0, rows), :], buf.at[slot_, pl.ds(0, rows), :], sem).wait()

    def mlp_rows(r0, rows):
        x = buf[slot, pl.ds(r0, rows), :]
        gu = _dot(x, wu_s[...]) + bu_ref[0]
        gate = jnp.minimum(gu[:, :D_FF], SWIGLU_LIMIT)
        up = jnp.clip(gu[:, D_FF:], -SWIGLU_LIMIT, SWIGLU_LIMIT)
        glu = gate * _sigmoid(SWIGLU_ALPHA * gate)
        y = _dot(((up + 1.0) * glu).astype(BF16), wd_s[...]) + bd_ref[0]
        buf[slot, pl.ds(r0, rows), :] = y.astype(BF16)

    def compute(n):
        rows = n * CHUNK
        rem = rows % EXPERT_ROW_TILE
        half = EXPERT_ROW_TILE // 2
        n_full = rows // EXPERT_ROW_TILE + jnp.where(rem > half, 1, 0)

        def tile_body(t, carry):
            mlp_rows(pl.multiple_of(t * EXPERT_ROW_TILE, EXPERT_ROW_TILE), EXPERT_ROW_TILE)
            return carry

        lax.fori_loop(0, n_full, tile_body, 0)

        @pl.when((rem > 0) & (rem <= half))
        def _():
            mlp_rows(pl.multiple_of(rows - rem, half), half)

    def chunks_of(ex):
        return lax.fori_loop(0, _N_BLOCKS, lambda j, acc: acc + np_ref[j * N_EXPERTS + ex], 0)

    @pl.when(e == 0)
    def _():
        buf[...] = jnp.zeros_like(buf)
        for b in range(N_ROW_BUFS):
            pend[b] = 0
        move_chunks(0, 0, 0, True)

    wait_chunks(ssem.at[nxt], pend[nxt], nxt)
    pend[nxt] = 0

    @pl.when(e < last)
    def _():
        move_chunks(e + 1, 0, nxt, True)

    wu_s[...] = wu_ref[0].astype(BF16)
    wd_s[...] = wd_ref[0].astype(BF16)

    total = chunks_of(e)
    n0 = jnp.minimum(total, PASS_CHUNKS)
    wait_chunks(gsem.at[slot], n0, slot)
    compute(n0)
    move_chunks(e, 0, slot, False)
    pend[slot] = n0

    def pass_body(p, carry):
        wait_chunks(ssem.at[slot], pend[slot], slot)
        lo = p * PASS_CHUNKS
        n = jnp.minimum(total - lo, PASS_CHUNKS)
        move_chunks(e, lo, slot, True)
        wait_chunks(gsem.at[slot], n, slot)
        compute(n)
        move_chunks(e, lo, slot, False)
        pend[slot] = n
        return carry

    lax.fori_loop(1, (total + PASS_CHUNKS - 1) // PASS_CHUNKS, pass_body, 0)

    @pl.when(e == last)
    def _():
        for b in range(N_ROW_BUFS):
            wait_chunks(ssem.at[b], pend[b], b)
            pend[b] = 0


def _experts(np16, off16, w_up, b_up, w_down, b_down, xs):
    grid_spec = pltpu.PrefetchScalarGridSpec(
        num_scalar_prefetch=2,
        grid=(N_EXPERTS,),
        in_specs=[pl.BlockSpec((1, D_MODEL, 2 * D_FF), lambda e, *_: (e, 0, 0)),
                  pl.BlockSpec((1, 1, 2 * D_FF), lambda e, *_: (e, 0, 0)),
                  pl.BlockSpec((1, D_FF, D_MODEL), lambda e, *_: (e, 0, 0)),
                  pl.BlockSpec((1, 1, D_MODEL), lambda e, *_: (e, 0, 0)),
                  pl.BlockSpec(memory_space=pl.ANY)],
        out_specs=pl.BlockSpec(memory_space=pl.ANY),
        scratch_shapes=[pltpu.VMEM((N_ROW_BUFS, EXPERT_BUF_ROWS, D_MODEL), BF16),
                        pltpu.VMEM((D_MODEL, 2 * D_FF), BF16),
                        pltpu.VMEM((D_FF, D_MODEL), BF16),
                        pltpu.SemaphoreType.DMA((N_ROW_BUFS,)),
                        pltpu.SemaphoreType.DMA((N_ROW_BUFS,)),
                        pltpu.SMEM((N_ROW_BUFS,), jnp.int32)],
    )
    return pl.pallas_call(
        _expert_kernel,
        grid_spec=grid_spec,
        out_shape=jax.ShapeDtypeStruct(xs.shape, xs.dtype),
        input_output_aliases={6: 0},
        compiler_params=pltpu.CompilerParams(dimension_semantics=("arbitrary",),
                                             vmem_limit_bytes=V7X_VMEM_LIMIT),
        name="experts",
    )(np16, off16, w_up, b_up.reshape(N_EXPERTS, 1, 2 * D_FF), w_down,
      b_down.reshape(N_EXPERTS, 1, D_MODEL), xs)


def _combine_kernel(ys_ref, dk_ref, xc_ref, xl_ref, mod_ref, gfin_ref, yc_ref, yl_ref):
    j = pl.program_id(0)
    d = dk_ref[0]
    acc = jnp.zeros((TOK_BLOCK, D_MODEL), F32)
    for c in range(BLOCK_CAP // SORT_ROWS):
        cio = (lax.broadcasted_iota(jnp.int32, (TOK_BLOCK, SORT_ROWS), 1) + c * SORT_ROWS).astype(F32)
        w = jnp.zeros((TOK_BLOCK, SORT_ROWS), F32)
        for k in range(TOP_K):
            w = w + jnp.where(cio == d[:, k:k + 1], d[:, TOP_K + k:TOP_K + k + 1], 0.0)
        acc = acc + _dot(w.astype(BF16), ys_ref[c * SORT_ROWS:(c + 1) * SORT_ROWS, :])
    x1 = jnp.where(j < _N_CTX_BLOCKS, xc_ref[...], xl_ref[...])
    x2 = x1 + mod_ref[0, 5:6, :] * acc
    ms = jnp.mean(x2 * x2, axis=-1, keepdims=True)
    y = x2 * lax.rsqrt(ms + RMS_EPS) * gfin_ref[...]

    @pl.when(j < _N_CTX_BLOCKS)
    def _():
        yc_ref[...] = y

    @pl.when(j >= _N_CTX_BLOCKS)
    def _():
        yl_ref[...] = y


def _combine(ys, dk_t, x1c, x1l, modv, g_final):
    return pl.pallas_call(
        _combine_kernel,
        grid=(_N_BLOCKS,),
        in_specs=[pl.BlockSpec((BLOCK_CAP, D_MODEL), lambda j: (j, 0)),
                  pl.BlockSpec((1, TOK_BLOCK, 2 * TOP_K), lambda j: (j, 0, 0))] + _token_specs() + [
                  _const_spec((1, D_MODEL))],
        out_specs=[pl.BlockSpec((TOK_BLOCK, D_MODEL), lambda j: (jnp.minimum(j, _N_CTX_BLOCKS - 1), 0)),
                   pl.BlockSpec((TOK_BLOCK, D_MODEL), lambda j: (jnp.maximum(j - _N_CTX_BLOCKS, 0), 0))],
        out_shape=[jax.ShapeDtypeStruct(x1c.shape, F32), jax.ShapeDtypeStruct(x1l.shape, F32)],
        compiler_params=pltpu.CompilerParams(dimension_semantics=("arbitrary",),
                                             vmem_limit_bytes=V7X_VMEM_LIMIT),
        name="combine",
    )(ys, dk_t, x1c, x1l, modv, g_final)


def kernel(x_prompt, x_sample, cache_k, cache_v, c, c_ctx, w_ada, b_ada, g_mix, w_in, w_pool, pool_scale,
           w_pa, w_pb, rpb, w_out, g_ffn, w_router, b_router, w_up, b_up, w_down, b_down, g_final):
    assert w_ada.shape[0] == 1, "single trunk layer"
    batch, seq, d = x_prompt.shape
    dec_batch, dec_seq, _ = x_sample.shape
    assert (seq, dec_seq, d) == (SEQ, DEC_SEQ, D_MODEL)
    assert batch * seq == _N_CTX_BLOCKS * TOK_BLOCK and dec_batch * dec_seq == _N_LAT_BLOCKS * TOK_BLOCK

    cmat = jnp.concatenate([c_ctx[None, :], c, jnp.zeros((8 - 1 - dec_batch, d), F32)], axis=0)
    modv = _modulation(cmat, w_ada[0], b_ada[0]).reshape(8, N_MOD, d)

    weights = (g_mix[0][None, :], w_in[0].astype(BF16), w_pool[0].astype(BF16), pool_scale[0][None, :],
               w_pa[0].astype(BF16), w_pb[0].astype(BF16), w_out[0].astype(BF16))

    def by_pair(cache):
        z = cache[:, 0].reshape(dec_batch, N_PAIRS, 2, PAST_LEN, HEAD_DIM)
        return z.transpose(0, 1, 3, 2, 4).reshape(dec_batch, N_PAIRS, PAST_LEN, PAIR_W)

    x1c, new_k, new_v = _ctx_mixer(x_prompt.reshape(batch * seq, d), modv, weights)
    x1l = _lat_mixer(x_sample.reshape(dec_batch * dec_seq, d), modv, weights,
                     by_pair(cache_k), by_pair(cache_v), _bias_tables(rpb[0]))

    xs, dk, np16, off16 = _route_sort(x1c, x1l, modv, g_ffn[0][None, :],
                                      w_router[0].T.astype(BF16), b_router[0][:, None])
    np16_i = np16[:, :, 0].astype(jnp.int32).reshape(-1)
    off16_i = off16[:, :, 0].astype(jnp.int32).reshape(-1)
    ys = _experts(np16_i, off16_i, w_up[0], b_up[0], w_down[0], b_down[0], xs)
    yc, yl = _combine(ys, dk.transpose(0, 2, 1), x1c, x1l, modv, g_final[None, :])
    return (yc.reshape(batch, seq, d), yl.reshape(dec_batch, dec_seq, d), new_k, new_v)
```

```python
import functools

import jax
import jax.numpy as jnp
import numpy as np
from jax import lax
from jax.experimental import pallas as pl
from jax.experimental.pallas import tpu as pltpu

F32 = jnp.float32
BF16 = jnp.bfloat16

D_MODEL = 1024
SEQ = 256
DEC_SEQ = 1024
GRID_W = 64
ROWS = DEC_SEQ // GRID_W
N_HEADS = 8
HEAD_DIM = 64
N_PAIRS = N_HEADS // 2
PAIR_W = 2 * HEAD_DIM
PAST_LEN = 512
POOL_DIM = 512
POOL_WINDOWS = (2, 4, 8, 16)
POOL_GROUP_DIM = 128
ATTN_DIM = 512
WIN_R = 8
WIN_C = 16
N_EXPERTS = 32
TOP_K = 4
D_FF = 1024
SWIGLU_LIMIT = 7.0
SWIGLU_ALPHA = 1.702
N_MOD = 6
RMS_EPS = 1e-6
NEG_INF = -1e30
ATTN_SCALE = HEAD_DIM ** -0.5
IN_DIM = POOL_DIM + 3 * ATTN_DIM + 2 * D_MODEL

TOK_BLOCK = 512
CHUNK = 16
BLOCK_CAP = TOK_BLOCK * TOP_K + N_EXPERTS * CHUNK
EXPERT_ROW_TILE = 256
V7X_VMEM_LIMIT = 60 * 1024 * 1024

_N_CTX_BLOCKS = 8
_N_LAT_BLOCKS = 4
_N_BLOCKS = _N_CTX_BLOCKS + _N_LAT_BLOCKS

_NT = (((1,), (1,)), ((), ()))


def _dot(a, b):
    return jnp.dot(a, b, preferred_element_type=F32)


def _dot_nt(a, b):
    return lax.dot_general(a, b, _NT, preferred_element_type=F32)


def _sigmoid(x):
    return 1.0 / (1.0 + jnp.exp(-x))


def _norm_mod(x, gain, scale, shift):
    ms = jnp.mean(x * x, axis=-1, keepdims=True)
    return (x * lax.rsqrt(ms + RMS_EPS) * gain) * (1.0 + scale) + shift


def _const_spec(shape):
    zeros = (0,) * len(shape)
    return pl.BlockSpec(shape, lambda *_: zeros, pipeline_mode=pl.Buffered(1))


MOD_COLS = 1536


def _mod_kernel(c_ref, w_ref, b_ref, o_ref):
    c = c_ref[...]
    s = c * _sigmoid(c)
    s_hi = s.astype(BF16)
    s_lo = (s - s_hi.astype(F32)).astype(BF16)
    r = _dot(jnp.concatenate([s_hi, s_lo], axis=0), w_ref[...].astype(BF16))
    o_ref[...] = r[:8] + r[8:] + b_ref[...]


def _modulation(cmat, w_ada, b_ada):
    n = w_ada.shape[1]
    return pl.pallas_call(
        _mod_kernel,
        grid=(n // MOD_COLS,),
        in_specs=[pl.BlockSpec((8, D_MODEL), lambda i: (0, 0)),
                  pl.BlockSpec((D_MODEL, MOD_COLS), lambda i: (0, i)),
                  pl.BlockSpec((1, MOD_COLS), lambda i: (0, i))],
        out_specs=pl.BlockSpec((8, MOD_COLS), lambda i: (0, i)),
        out_shape=jax.ShapeDtypeStruct((8, n), F32),
        name="modulation",
    )(cmat, w_ada, b_ada.reshape(1, n))


def _pool_mix(u, pos, seq):
    n = u.shape[0]

    def down(x, d):
        return jnp.where(pos >= d, pltpu.roll(x, d, 0), 0.0)

    def up(x, d):
        return jnp.where(pos < seq - d, pltpu.roll(x, n - d, 0), 0.0)

    return down, up


def _pool_group(u, pos, seq, w):
    down, up = _pool_mix(u, pos, seq)
    hw = w // 2
    back = u
    fwd = u
    d = 1
    while d < hw:
        back = back + down(back, d)
        fwd = fwd + up(fwd, d)
        d *= 2
    s = down(back, 1) + fwd
    posf = pos.astype(F32)
    cnt = jnp.minimum(posf + hw, float(seq)) - jnp.maximum(posf - hw, 0.0)
    return s / cnt - u


def _mixer_front(x_ref, mod_ref, gmix_ref, win_ref, hb_s, u_s, q_s, k_s, v_s):
    x = x_ref[...]
    shift, scale = mod_ref[0, 0:1, :], mod_ref[0, 1:2, :]
    hb = _norm_mod(x, gmix_ref[...], scale, shift).astype(BF16)
    hb_s[...] = hb
    u_s[...] = _dot(hb, win_ref[:, 0:POOL_DIM])
    for dst, base in ((q_s, POOL_DIM), (k_s, POOL_DIM + ATTN_DIM), (v_s, POOL_DIM + 2 * ATTN_DIM)):
        z = _dot(hb, win_ref[:, base:base + ATTN_DIM])
        for g in range(N_PAIRS):
            dst[g] = z[:, g * PAIR_W:(g + 1) * PAIR_W].astype(dst.dtype)


def _mixer_back(x_ref, mod_ref, win_ref, wpool_ref, ps_ref, wpa_ref, wpb_ref, wout_ref, x1_ref,
                hb_s, u_s, o_s, pg_s, seq):
    tile = x_ref.shape[0]
    gate = mod_ref[0, 2:3, :]
    pos = lax.broadcasted_iota(jnp.int32, (tile, 1), 0) % seq
    for g, w in enumerate(POOL_WINDOWS):
        cols = slice(g * POOL_GROUP_DIM, (g + 1) * POOL_GROUP_DIM)
        pg_s[:, cols] = _pool_group(u_s[:, cols], pos, seq, w).astype(BF16)
    for c in range(tile // TOK_BLOCK):
        rows = slice(c * TOK_BLOCK, (c + 1) * TOK_BLOCK)
        ys = []
        for g in range(len(POOL_WINDOWS)):
            cols = slice(g * POOL_GROUP_DIM, (g + 1) * POOL_GROUP_DIM)
            ys.append((_dot(pg_s[rows, cols], wpool_ref[g]) * ps_ref[:, cols]).astype(BF16))
        a = _dot(jnp.concatenate(ys, axis=1), wpa_ref[...])
        ob = _dot(jnp.concatenate([o_s[g, rows, :] for g in range(N_PAIRS)], axis=1), wpb_ref[...])
        gab = _dot(hb_s[rows, :], win_ref[:, POOL_DIM + 3 * ATTN_DIM:IN_DIM])
        merged = _sigmoid(gab[:, :D_MODEL]) * a + _sigmoid(gab[:, D_MODEL:]) * ob
        mix = _dot(merged.astype(BF16), wout_ref[...])
        x1_ref[rows, :] = x_ref[rows, :] + gate * mix


def _ctx_mixer_kernel(x_ref, mod_ref, gmix_ref, win_ref, wpool_ref, ps_ref, wpa_ref, wpb_ref, wout_ref,
                      x1_ref, ko_ref, vo_ref, hb_s, u_s, q_s, k_s, v_s, o_s, pg_s):
    _mixer_front(x_ref, mod_ref, gmix_ref, win_ref, hb_s, u_s, q_s, k_s, v_s)
    even = lax.broadcasted_iota(jnp.int32, (1, PAIR_W), 1) < HEAD_DIM
    tile = x_ref.shape[0]
    for s in range(tile // SEQ):
        rows = slice(s * SEQ, (s + 1) * SEQ)
        for g in range(N_PAIRS):
            q2, k2, v2 = q_s[g, rows, :], k_s[g, rows, :], v_s[g, rows, :]
            ko_ref[s, 0, 2 * g] = k2[:, :HEAD_DIM]
            ko_ref[s, 0, 2 * g + 1] = k2[:, HEAD_DIM:]
            vo_ref[s, 0, 2 * g] = v2[:, :HEAD_DIM]
            vo_ref[s, 0, 2 * g + 1] = v2[:, HEAD_DIM:]
            kb, vb = k2.astype(BF16), v2.astype(BF16)
            outs = []
            for par in range(2):
                qm = jnp.where(even if par == 0 else jnp.logical_not(even), q2, 0.0).astype(BF16)
                sc = _dot_nt(qm, kb) * ATTN_SCALE
                m = jnp.max(sc, axis=-1, keepdims=True)
                p = jnp.exp(sc - m)
                l = jnp.sum(p, axis=-1, keepdims=True)
                outs.append(_dot(p.astype(BF16), vb) / l)
            o_s[g, rows, :] = jnp.where(even, outs[0], outs[1]).astype(BF16)
    _mixer_back(x_ref, mod_ref, win_ref, wpool_ref, ps_ref, wpa_ref, wpb_ref, wout_ref, x1_ref,
                hb_s, u_s, o_s, pg_s, SEQ)


def _row_window(r):
    rs = min(max(r - WIN_R // 2, 0), ROWS - WIN_R)
    return rs, rs - r + WIN_R - 1


def _lat_mixer_kernel(x_ref, mod_ref, gmix_ref, win_ref, wpool_ref, ps_ref, wpa_ref, wpb_ref, wout_ref,
                      kc_ref, vc_ref, tb_ref, x1_ref, hb_s, u_s, q_s, k_s, v_s, o_s, pg_s):
    _mixer_front(x_ref, mod_ref, gmix_ref, win_ref, hb_s, u_s, q_s, k_s, v_s)
    even = lax.broadcasted_iota(jnp.int32, (1, PAIR_W), 1) < HEAD_DIM
    nk = WIN_R * GRID_W

    def pair_body(g, carry):
        q2 = q_s[g]
        kb, vb = k_s[g].astype(BF16), v_s[g].astype(BF16)
        kcb, vcb = kc_ref[0, g].astype(BF16), vc_ref[0, g].astype(BF16)
        outs = []
        for par in range(2):
            qm = jnp.where(even if par == 0 else jnp.logical_not(even), q2, 0.0).astype(BF16)
            s_ctx = _dot_nt(qm, kcb) * ATTN_SCALE
            slabs = []
            for r in range(ROWS):
                rs, rho = _row_window(r)
                bias = tb_ref[2 * g + par, rho % 2, :, (rho - rho % 2) * GRID_W:(rho - rho % 2) * GRID_W + nk]
                sl = _dot_nt(qm[r * GRID_W:(r + 1) * GRID_W, :], kb[rs * GRID_W:rs * GRID_W + nk, :])
                slabs.append(sl * ATTN_SCALE + bias)
            s_loc = jnp.concatenate(slabs, axis=0)
            m = jnp.maximum(jnp.max(s_loc, axis=-1, keepdims=True), jnp.max(s_ctx, axis=-1, keepdims=True))
            p_loc = jnp.exp(s_loc - m)
            p_ctx = jnp.exp(s_ctx - m)
            l = jnp.sum(p_loc, axis=-1, keepdims=True) + jnp.sum(p_ctx, axis=-1, keepdims=True)
            p_locb = p_loc.astype(BF16)
            o_rows = []
            for r in range(ROWS):
                rs, _ = _row_window(r)
                o_rows.append(_dot(p_locb[r * GRID_W:(r + 1) * GRID_W, :], vb[rs * GRID_W:rs * GRID_W + nk, :]))
            o = jnp.concatenate(o_rows, axis=0) + _dot(p_ctx.astype(BF16), vcb)
            outs.append(o / l)
        o_s[g] = jnp.where(even, outs[0], outs[1]).astype(BF16)
        return carry

    lax.fori_loop(0, N_PAIRS, pair_body, 0)
    _mixer_back(x_ref, mod_ref, win_ref, wpool_ref, ps_ref, wpa_ref, wpb_ref, wout_ref, x1_ref,
                hb_s, u_s, o_s, pg_s, DEC_SEQ)


def _mixer_scratch(tile, kv_dtype):
    return [pltpu.VMEM((tile, D_MODEL), BF16),
            pltpu.VMEM((tile, POOL_DIM), F32),
            pltpu.VMEM((N_PAIRS, tile, PAIR_W), BF16),
            pltpu.VMEM((N_PAIRS, tile, PAIR_W), kv_dtype),
            pltpu.VMEM((N_PAIRS, tile, PAIR_W), kv_dtype),
            pltpu.VMEM((N_PAIRS, tile, PAIR_W), BF16),
            pltpu.VMEM((tile, POOL_DIM), BF16)]


def _weight_specs():
    return [_const_spec((1, D_MODEL)),
            _const_spec((D_MODEL, IN_DIM)),
            _const_spec((len(POOL_WINDOWS), POOL_GROUP_DIM, POOL_GROUP_DIM)),
            _const_spec((1, POOL_DIM)),
            _const_spec((POOL_DIM, D_MODEL)),
            _const_spec((ATTN_DIM, D_MODEL)),
            _const_spec((D_MODEL, D_MODEL))]


def _ctx_mixer(x, modv, weights):
    n = x.shape[0]
    nseq = TOK_BLOCK // SEQ
    cache = jax.ShapeDtypeStruct((n // SEQ, 1, N_HEADS, SEQ, HEAD_DIM), F32)
    cache_spec = pl.BlockSpec((nseq, 1, N_HEADS, SEQ, HEAD_DIM), lambda i: (i, 0, 0, 0, 0))
    return pl.pallas_call(
        _ctx_mixer_kernel,
        grid=(n // TOK_BLOCK,),
        in_specs=[pl.BlockSpec((TOK_BLOCK, D_MODEL), lambda i: (i, 0)),
                  pl.BlockSpec((1, N_MOD, D_MODEL), lambda i: (0, 0, 0))] + _weight_specs(),
        out_specs=[pl.BlockSpec((TOK_BLOCK, D_MODEL), lambda i: (i, 0)), cache_spec, cache_spec],
        out_shape=[jax.ShapeDtypeStruct((n, D_MODEL), F32), cache, cache],
        scratch_shapes=_mixer_scratch(TOK_BLOCK, F32),
        compiler_params=pltpu.CompilerParams(dimension_semantics=("arbitrary",),
                                             vmem_limit_bytes=V7X_VMEM_LIMIT),
        name="ctx_mixer",
    )(x, modv, *weights)


def _lat_mixer(x, modv, weights, kc, vc, tb):
    n = x.shape[0]
    return pl.pallas_call(
        _lat_mixer_kernel,
        grid=(n // DEC_SEQ,),
        in_specs=[pl.BlockSpec((DEC_SEQ, D_MODEL), lambda i: (i, 0)),
                  pl.BlockSpec((1, N_MOD, D_MODEL), lambda i: (i + 1, 0, 0))] + _weight_specs() + [
                  pl.BlockSpec((1, N_PAIRS, PAST_LEN, PAIR_W), lambda i: (i, 0, 0, 0)),
                  pl.BlockSpec((1, N_PAIRS, PAST_LEN, PAIR_W), lambda i: (i, 0, 0, 0)),
                  _const_spec((N_HEADS, 2, GRID_W, ROWS * GRID_W))],
        out_specs=pl.BlockSpec((DEC_SEQ, D_MODEL), lambda i: (i, 0)),
        out_shape=jax.ShapeDtypeStruct((n, D_MODEL), F32),
        scratch_shapes=_mixer_scratch(DEC_SEQ, BF16),
        compiler_params=pltpu.CompilerParams(dimension_semantics=("arbitrary",),
                                             vmem_limit_bytes=V7X_VMEM_LIMIT),
        name="lat_mixer",
    )(x, modv, *weights, kc, vc, tb)


N_RPB_ROWS = 2 * WIN_R - 1
N_RPB_COLS = 2 * WIN_C - 1
TABLE_W = ROWS * GRID_W


def _bias_kernel(v_ref, keep_ref, o_ref):
    for h in range(N_HEADS):
        for par in range(2):
            x = jnp.broadcast_to(v_ref[h, par:par + 1, :], (GRID_W, TABLE_W))
            shifted = pltpu.roll(x, TABLE_W - (WIN_C - 1), 1, stride=1, stride_axis=0)
            o_ref[h, par] = jnp.where(keep_ref[par] > 0.0, shifted, NEG_INF)


def _bias_tables(rpb):
    col = np.arange(GRID_W)
    cs = np.clip(col - WIN_C // 2, 0, GRID_W - WIN_C)
    in_win = (col[None, :] >= cs[:, None]) & (col[None, :] < cs[:, None] + WIN_C)
    keep = np.tile(in_win.astype(np.float32), (2, 1, ROWS))
    keep[0, :, N_RPB_ROWS * GRID_W:] = 0.0
    keep[1, :, (N_RPB_ROWS - 1) * GRID_W:] = 0.0
    rp = jnp.pad(rpb.astype(F32), ((0, 0), (0, ROWS + 1 - N_RPB_ROWS), (0, GRID_W - N_RPB_COLS)))
    v = jnp.stack([rp[:, :ROWS].reshape(N_HEADS, TABLE_W), rp[:, 1:].reshape(N_HEADS, TABLE_W)], axis=1)
    return pl.pallas_call(
        _bias_kernel,
        out_shape=jax.ShapeDtypeStruct((N_HEADS, 2, GRID_W, TABLE_W), F32),
        name="bias_tables",
    )(v, jnp.asarray(keep))


SORT_ROWS = 512


def _sort_kernel(xc_ref, xl_ref, mod_ref, gffn_ref, wr_ref, br_ref,
                 xs_ref, dk_ref, np_ref, off_ref):
    j = pl.program_id(0)
    x = jnp.where(j < _N_CTX_BLOCKS, xc_ref[...], xl_ref[...])
    shift, scale = mod_ref[0, 3:4, :], mod_ref[0, 4:5, :]
    hb = _norm_mod(x, gffn_ref[...], scale, shift).astype(BF16)
    logits = _dot_nt(wr_ref[...], hb) + br_ref[...]
    eio = lax.broadcasted_iota(jnp.int32, logits.shape, 0)
    work = logits
    sels, vals = [], []
    for _ in range(TOP_K):
        m = jnp.max(work, axis=0, keepdims=True)
        idx = jnp.min(jnp.where(work == m, eio, N_EXPERTS), axis=0, keepdims=True)
        sel = eio == idx
        sels.append(sel)
        vals.append(m)
        work = jnp.where(sel, -jnp.inf, work)
    exps = [jnp.exp(v - vals[0]) for v in vals]
    den = exps[0] + exps[1] + exps[2] + exps[3]
    mask = jnp.zeros(logits.shape, F32)
    for sel in sels:
        mask = mask + jnp.where(sel, 1.0, 0.0)
    t_row = lax.broadcasted_iota(jnp.int32, (TOK_BLOCK, TOK_BLOCK), 0)
    t_col = lax.broadcasted_iota(jnp.int32, (TOK_BLOCK, TOK_BLOCK), 1)
    before = jnp.where(t_row < t_col, 1.0, 0.0).astype(BF16)
    rank = _dot(mask.astype(BF16), before)
    cnt = jnp.sum(mask, axis=1, keepdims=True)
    np16 = jnp.floor((cnt + (CHUNK - 1.0)) * (1.0 / CHUNK))
    e_row = lax.broadcasted_iota(jnp.int32, (N_EXPERTS, N_EXPERTS), 0)
    e_col = lax.broadcasted_iota(jnp.int32, (N_EXPERTS, N_EXPERTS), 1)
    lower = jnp.where(e_col < e_row, 1.0, 0.0).astype(BF16)
    np16_b = jnp.broadcast_to(np16, (N_EXPERTS, 128))
    off16 = _dot(lower, np16_b.astype(BF16))
    np_ref[0] = np16_b
    off_ref[0] = off16
    dest = off16[:, 0:1] * float(CHUNK) + rank
    dests = []
    for k in range(TOP_K):
        dk = jnp.sum(jnp.where(sels[k], dest, 0.0), axis=0, keepdims=True)
        dk_ref[0, k:k + 1, :] = dk
        dk_ref[0, TOP_K + k:TOP_K + k + 1, :] = exps[k] / den
        dests.append(dk.astype(jnp.int32))
    rio = lax.broadcasted_iota(jnp.int32, (SORT_ROWS, TOK_BLOCK), 0)
    for c in range(BLOCK_CAP // SORT_ROWS):
        onehot = jnp.zeros((SORT_ROWS, TOK_BLOCK), F32)
        for dk in dests:
            onehot = jnp.where(rio == dk - c * SORT_ROWS, 1.0, onehot)
        xs_ref[c * SORT_ROWS:(c + 1) * SORT_ROWS, :] = _dot(onehot.astype(BF16), hb).astype(BF16)


def _block_mod_index(j):
    return jnp.where(j < _N_CTX_BLOCKS, 0, 1 + (j - _N_CTX_BLOCKS) // (DEC_SEQ // TOK_BLOCK))


def _token_specs():
    return [pl.BlockSpec((TOK_BLOCK, D_MODEL), lambda j: (jnp.minimum(j, _N_CTX_BLOCKS - 1), 0)),
            pl.BlockSpec((TOK_BLOCK, D_MODEL), lambda j: (jnp.maximum(j - _N_CTX_BLOCKS, 0), 0)),
            pl.BlockSpec((1, N_MOD, D_MODEL), lambda j: (_block_mod_index(j), 0, 0))]


def _route_sort(x1c, x1l, modv, g_ffn, wr_t, br):
    tbl = jax.ShapeDtypeStruct((_N_BLOCKS, N_EXPERTS, 128), F32)
    tbl_spec = pl.BlockSpec((1, N_EXPERTS, 128), lambda j: (j, 0, 0))
    return pl.pallas_call(
        _sort_kernel,
        grid=(_N_BLOCKS,),
        in_specs=_token_specs() + [_const_spec((1, D_MODEL)),
                                   _const_spec((N_EXPERTS, D_MODEL)),
                                   _const_spec((N_EXPERTS, 1))],
        out_specs=[pl.BlockSpec((BLOCK_CAP, D_MODEL), lambda j: (j, 0)),
                   pl.BlockSpec((1, 2 * TOP_K, TOK_BLOCK), lambda j: (j, 0, 0)),
                   tbl_spec, tbl_spec],
        out_shape=[jax.ShapeDtypeStruct((_N_BLOCKS * BLOCK_CAP, D_MODEL), BF16),
                   jax.ShapeDtypeStruct((_N_BLOCKS, 2 * TOP_K, TOK_BLOCK), F32),
                   tbl, tbl],
        compiler_params=pltpu.CompilerParams(dimension_semantics=("arbitrary",),
                                             vmem_limit_bytes=V7X_VMEM_LIMIT),
        name="route_sort",
    )(x1c, x1l, modv, g_ffn, wr_t, br)


EXPERT_BUF_ROWS = 2048
PASS_CHUNKS = EXPERT_BUF_ROWS // CHUNK
N_ROW_BUFS = 3


def _expert_kernel(np_ref, off_ref, wu_ref, bu_ref, wd_ref, bd_ref, xs_hbm, ys_hbm,
                   buf, wu_s, wd_s, gsem, ssem, pend):
    del xs_hbm
    e = pl.program_id(0)
    last = pl.num_programs(0) - 1
    slot = e % N_ROW_BUFS
    nxt = (e + 1) % N_ROW_BUFS

    def move_chunks(ex, q_lo, slot_, gather):
        def block_body(j, q0):
            n = np_ref[j * N_EXPERTS + ex]
            off = off_ref[j * N_EXPERTS + ex]
            c_lo = jnp.clip(q_lo - q0, 0, n)
            m = jnp.clip(q_lo + PASS_CHUNKS - q0, 0, n) - c_lo

            @pl.when(m > 0)
            def _():
                rows = pl.multiple_of(m * CHUNK, CHUNK)
                row0 = pl.multiple_of(j * BLOCK_CAP + (off + c_lo) * CHUNK, CHUNK)
                brow0 = pl.multiple_of((q0 + c_lo - q_lo) * CHUNK, CHUNK)
                hbm = ys_hbm.at[pl.ds(row0, rows), :]
                vm = buf.at[slot_, pl.ds(brow0, rows), :]
                if gather:
                    pltpu.make_async_copy(hbm, vm, gsem.at[slot_]).start()
                else:
                    pltpu.make_async_copy(vm, hbm, ssem.at[slot_]).start()

            return q0 + n

        lax.fori_loop(0, _N_BLOCKS, block_body, 0)

    def wait_chunks(sem, n, slot_):
        @pl.when(n > 0)
        def _():
            rows = pl.multiple_of(n * CHUNK, CHUNK)
            pltpu.make_async_copy(ys_hbm.at[pl.ds(0, rows), :], buf.at[slot_, pl.ds(0, rows), :], sem).wait()

    def mlp_rows(r0, rows):
        x = buf[slot, pl.ds(r0, rows), :]
        gu = _dot(x, wu_s[...]) + bu_ref[0]
        gate = jnp.minimum(gu[:, :D_FF], SWIGLU_LIMIT)
        up = jnp.clip(gu[:, D_FF:], -SWIGLU_LIMIT, SWIGLU_LIMIT)
        glu = gate * _sigmoid(SWIGLU_ALPHA * gate)
        y = _dot(((up + 1.0) * glu).astype(BF16), wd_s[...]) + bd_ref[0]
        buf[slot, pl.ds(r0, rows), :] = y.astype(BF16)

    def compute(n):
        rows = n * CHUNK
        rem = rows % EXPERT_ROW_TILE
        half = EXPERT_ROW_TILE // 2
        n_full = rows // EXPERT_ROW_TILE + jnp.where(rem > half, 1, 0)

        def tile_body(t, carry):
            mlp_rows(pl.multiple_of(t * EXPERT_ROW_TILE, EXPERT_ROW_TILE), EXPERT_ROW_TILE)
            return carry

        lax.fori_loop(0, n_full, tile_body, 0)

        @pl.when((rem > 0) & (rem <= half))
        def _():
            mlp_rows(pl.multiple_of(rows - rem, half), half)

    def chunks_of(ex):
        return lax.fori_loop(0, _N_BLOCKS, lambda j, acc: acc + np_ref[j * N_EXPERTS + ex], 0)

    @pl.when(e == 0)
    def _():
        buf[...] = jnp.zeros_like(buf)
        for b in range(N_ROW_BUFS):
            pend[b] = 0
        move_chunks(0, 0, 0, True)

    wait_chunks(ssem.at[nxt], pend[nxt], nxt)
    pend[nxt] = 0

    @pl.when(e < last)
    def _():
        move_chunks(e + 1, 0, nxt, True)

    wu_s[...] = wu_ref[0].astype(BF16)
    wd_s[...] = wd_ref[0].astype(BF16)

    total = chunks_of(e)
    n0 = jnp.minimum(total, PASS_CHUNKS)
    wait_chunks(gsem.at[slot], n0, slot)
    compute(n0)
    move_chunks(e, 0, slot, False)
    pend[slot] = n0

    def pass_body(p, carry):
        wait_chunks(ssem.at[slot], pend[slot], slot)
        lo = p * PASS_CHUNKS
        n = jnp.minimum(total - lo, PASS_CHUNKS)
        move_chunks(e, lo, slot, True)
        wait_chunks(gsem.at[slot], n, slot)
        compute(n)
        move_chunks(e, lo, slot, False)
        pend[slot] = n
        return carry

    lax.fori_loop(1, (total + PASS_CHUNKS - 1) // PASS_CHUNKS, pass_body, 0)

    @pl.when(e == last)
    def _():
        for b in range(N_ROW_BUFS):
            wait_chunks(ssem.at[b], pend[b], b)
            pend[b] = 0


def _experts(np16, off16, w_up, b_up, w_down, b_down, xs):
    grid_spec = pltpu.PrefetchScalarGridSpec(
        num_scalar_prefetch=2,
        grid=(N_EXPERTS,),
        in_specs=[pl.BlockSpec((1, D_MODEL, 2 * D_FF), lambda e, *_: (e, 0, 0)),
                  pl.BlockSpec((1, 1, 2 * D_FF), lambda e, *_: (e, 0, 0)),
                  pl.BlockSpec((1, D_FF, D_MODEL), lambda e, *_: (e, 0, 0)),
                  pl.BlockSpec((1, 1, D_MODEL), lambda e, *_: (e, 0, 0)),
                  pl.BlockSpec(memory_space=pl.ANY)],
        out_specs=pl.BlockSpec(memory_space=pl.ANY),
        scratch_shapes=[pltpu.VMEM((N_ROW_BUFS, EXPERT_BUF_ROWS, D_MODEL), BF16),
                        pltpu.VMEM((D_MODEL, 2 * D_FF), BF16),
                        pltpu.VMEM((D_FF, D_MODEL), BF16),
                        pltpu.SemaphoreType.DMA((N_ROW_BUFS,)),
                        pltpu.SemaphoreType.DMA((N_ROW_BUFS,)),
                        pltpu.SMEM((N_ROW_BUFS,), jnp.int32)],
    )
    return pl.pallas_call(
        _expert_kernel,
        grid_spec=grid_spec,
        out_shape=jax.ShapeDtypeStruct(xs.shape, xs.dtype),
        input_output_aliases={6: 0},
        compiler_params=pltpu.CompilerParams(dimension_semantics=("arbitrary",),
                                             vmem_limit_bytes=V7X_VMEM_LIMIT),
        name="experts",
    )(np16, off16, w_up, b_up.reshape(N_EXPERTS, 1, 2 * D_FF), w_down,
      b_down.reshape(N_EXPERTS, 1, D_MODEL), xs)


def _combine_kernel(ys_ref, dk_ref, xc_ref, xl_ref, mod_ref, gfin_ref, yc_ref, yl_ref, y_s):
    j = pl.program_id(0)
    half = TOK_BLOCK // 2
    cio = lax.broadcasted_iota(jnp.int32, (half, SORT_ROWS), 1).astype(F32)
    for h in range(2):
        rows = slice(h * half, (h + 1) * half)
        d = dk_ref[0, rows, :]
        acc = jnp.zeros((half, D_MODEL), F32)
        for c in range(BLOCK_CAP // SORT_ROWS):
            w = jnp.zeros((half, SORT_ROWS), F32)
            for k in range(TOP_K):
                w = jnp.where(cio == d[:, k:k + 1] - float(c * SORT_ROWS), d[:, TOP_K + k:TOP_K + k + 1], w)
            acc = acc + _dot(w.astype(BF16), ys_ref[c * SORT_ROWS:(c + 1) * SORT_ROWS, :])
        x1 = jnp.where(j < _N_CTX_BLOCKS, xc_ref[rows, :], xl_ref[rows, :])
        x2 = x1 + mod_ref[0, 5:6, :] * acc
        ms = jnp.mean(x2 * x2, axis=-1, keepdims=True)
        y_s[rows, :] = x2 * lax.rsqrt(ms + RMS_EPS) * gfin_ref[...]

    @pl.when(j < _N_CTX_BLOCKS)
    def _():
        yc_ref[...] = y_s[...]

    @pl.when(j >= _N_CTX_BLOCKS)
    def _():
        yl_ref[...] = y_s[...]


def _combine(ys, dk_t, x1c, x1l, modv, g_final):
    return pl.pallas_call(
        _combine_kernel,
        grid=(_N_BLOCKS,),
        in_specs=[pl.BlockSpec((BLOCK_CAP, D_MODEL), lambda j: (j, 0)),
                  pl.BlockSpec((1, TOK_BLOCK, 2 * TOP_K), lambda j: (j, 0, 0))] + _token_specs() + [
                  _const_spec((1, D_MODEL))],
        out_specs=[pl.BlockSpec((TOK_BLOCK, D_MODEL), lambda j: (jnp.minimum(j, _N_CTX_BLOCKS - 1), 0)),
                   pl.BlockSpec((TOK_BLOCK, D_MODEL), lambda j: (jnp.maximum(j - _N_CTX_BLOCKS, 0), 0))],
        out_shape=[jax.ShapeDtypeStruct(x1c.shape, F32), jax.ShapeDtypeStruct(x1l.shape, F32)],
        scratch_shapes=[pltpu.VMEM((TOK_BLOCK, D_MODEL), F32)],
        compiler_params=pltpu.CompilerParams(dimension_semantics=("arbitrary",),
                                             vmem_limit_bytes=V7X_VMEM_LIMIT),
        name="combine",
    )(ys, dk_t, x1c, x1l, modv, g_final)


def kernel(x_prompt, x_sample, cache_k, cache_v, c, c_ctx, w_ada, b_ada, g_mix, w_in, w_pool, pool_scale,
           w_pa, w_pb, rpb, w_out, g_ffn, w_router, b_router, w_up, b_up, w_down, b_down, g_final):
    assert w_ada.shape[0] == 1, "single trunk layer"
    batch, seq, d = x_prompt.shape
    dec_batch, dec_seq, _ = x_sample.shape
    assert (seq, dec_seq, d) == (SEQ, DEC_SEQ, D_MODEL)
    assert batch * seq == _N_CTX_BLOCKS * TOK_BLOCK and dec_batch * dec_seq == _N_LAT_BLOCKS * TOK_BLOCK

    cmat = jnp.concatenate([c_ctx[None, :], c, jnp.zeros((8 - 1 - dec_batch, d), F32)], axis=0)
    modv = _modulation(cmat, w_ada[0], b_ada[0]).reshape(8, N_MOD, d)

    weights = (g_mix[0][None, :], w_in[0].astype(BF16), w_pool[0].astype(BF16), pool_scale[0][None, :],
               w_pa[0].astype(BF16), w_pb[0].astype(BF16), w_out[0].astype(BF16))

    def by_pair(cache):
        z = cache[:, 0].reshape(dec_batch, N_PAIRS, 2, PAST_LEN, HEAD_DIM)
        return z.transpose(0, 1, 3, 2, 4).reshape(dec_batch, N_PAIRS, PAST_LEN, PAIR_W)

    x1c, new_k, new_v = _ctx_mixer(x_prompt.reshape(batch * seq, d), modv, weights)
    x1l = _lat_mixer(x_sample.reshape(dec_batch * dec_seq, d), modv, weights,
                     by_pair(cache_k), by_pair(cache_v), _bias_tables(rpb[0]))

    xs, dk, np16, off16 = _route_sort(x1c, x1l, modv, g_ffn[0][None, :],
                                      w_router[0].T.astype(BF16), b_router[0][:, None])
    np16_i = np16[:, :, 0].astype(jnp.int32).reshape(-1)
    off16_i = off16[:, :, 0].astype(jnp.int32).reshape(-1)
    ys = _experts(np16_i, off16_i, w_up[0], b_up[0], w_down[0], b_down[0], xs)
    yc, yl = _combine(ys, dk.transpose(0, 2, 1), x1c, x1l, modv, g_final[None, :])
    return (yc.reshape(batch, seq, d), yl.reshape(dec_batch, dec_seq, d), new_k, new_v)
```

```python
import functools

import jax
import jax.numpy as jnp
import numpy as np
from jax import lax
from jax.experimental import pallas as pl
from jax.experimental.pallas import tpu as pltpu

F32 = jnp.float32
BF16 = jnp.bfloat16

D_MODEL = 1024
SEQ = 256
DEC_SEQ = 1024
GRID_W = 64
ROWS = DEC_SEQ // GRID_W
N_HEADS = 8
HEAD_DIM = 64
N_PAIRS = N_HEADS // 2
PAIR_W = 2 * HEAD_DIM
PAST_LEN = 512
POOL_DIM = 512
POOL_WINDOWS = (2, 4, 8, 16)
POOL_GROUP_DIM = 128
ATTN_DIM = 512
WIN_R = 8
WIN_C = 16
N_EXPERTS = 32
TOP_K = 4
D_FF = 1024
SWIGLU_LIMIT = 7.0
SWIGLU_ALPHA = 1.702
N_MOD = 6
RMS_EPS = 1e-6
NEG_INF = -1e30
ATTN_SCALE = HEAD_DIM ** -0.5
IN_DIM = POOL_DIM + 3 * ATTN_DIM + 2 * D_MODEL

TOK_BLOCK = 512
CHUNK = 16
BLOCK_CAP = TOK_BLOCK * TOP_K + N_EXPERTS * CHUNK
EXPERT_ROW_TILE = 256
V7X_VMEM_LIMIT = 60 * 1024 * 1024

_N_CTX_BLOCKS = 8
_N_LAT_BLOCKS = 4
_N_BLOCKS = _N_CTX_BLOCKS + _N_LAT_BLOCKS

_NT = (((1,), (1,)), ((), ()))


def _dot(a, b):
    return jnp.dot(a, b, preferred_element_type=F32)


def _dot_nt(a, b):
    return lax.dot_general(a, b, _NT, preferred_element_type=F32)


def _sigmoid(x):
    return 1.0 / (1.0 + jnp.exp(-x))


def _norm_mod(x, gain, scale, shift):
    ms = jnp.mean(x * x, axis=-1, keepdims=True)
    return (x * lax.rsqrt(ms + RMS_EPS) * gain) * (1.0 + scale) + shift


def _const_spec(shape):
    zeros = (0,) * len(shape)
    return pl.BlockSpec(shape, lambda *_: zeros, pipeline_mode=pl.Buffered(1))


MOD_COLS = 1536


def _mod_kernel(c_ref, w_ref, b_ref, o_ref):
    c = c_ref[...]
    s = c * _sigmoid(c)
    s_hi = s.astype(BF16)
    s_lo = (s - s_hi.astype(F32)).astype(BF16)
    r = _dot(jnp.concatenate([s_hi, s_lo], axis=0), w_ref[...].astype(BF16))
    o_ref[...] = r[:8] + r[8:] + b_ref[...]


def _modulation(cmat, w_ada, b_ada):
    n = w_ada.shape[1]
    return pl.pallas_call(
        _mod_kernel,
        grid=(n // MOD_COLS,),
        in_specs=[pl.BlockSpec((8, D_MODEL), lambda i: (0, 0)),
                  pl.BlockSpec((D_MODEL, MOD_COLS), lambda i: (0, i)),
                  pl.BlockSpec((1, MOD_COLS), lambda i: (0, i))],
        out_specs=pl.BlockSpec((8, MOD_COLS), lambda i: (0, i)),
        out_shape=jax.ShapeDtypeStruct((8, n), F32),
        name="modulation",
    )(cmat, w_ada, b_ada.reshape(1, n))


def _pool_mix(u, pos, seq):
    n = u.shape[0]

    def down(x, d):
        return jnp.where(pos >= d, pltpu.roll(x, d, 0), 0.0)

    def up(x, d):
        return jnp.where(pos < seq - d, pltpu.roll(x, n - d, 0), 0.0)

    return down, up


def _pool_group(u, pos, seq, w):
    down, up = _pool_mix(u, pos, seq)
    hw = w // 2
    back = u
    fwd = u
    d = 1
    while d < hw:
        back = back + down(back, d)
        fwd = fwd + up(fwd, d)
        d *= 2
    s = down(back, 1) + fwd
    posf = pos.astype(F32)
    cnt = jnp.minimum(posf + hw, float(seq)) - jnp.maximum(posf - hw, 0.0)
    return s / cnt - u


def _mixer_front(x_ref, mod_ref, gmix_ref, win_ref, hb_s, u_s, q_s, k_s, v_s):
    x = x_ref[...]
    shift, scale = mod_ref[0, 0:1, :], mod_ref[0, 1:2, :]
    hb = _norm_mod(x, gmix_ref[...], scale, shift).astype(BF16)
    hb_s[...] = hb
    u_s[...] = _dot(hb, win_ref[:, 0:POOL_DIM])
    for dst, base in ((q_s, POOL_DIM), (k_s, POOL_DIM + ATTN_DIM), (v_s, POOL_DIM + 2 * ATTN_DIM)):
        z = _dot(hb, win_ref[:, base:base + ATTN_DIM])
        if dst is q_s:
            z = z * ATTN_SCALE
        for g in range(N_PAIRS):
            dst[g] = z[:, g * PAIR_W:(g + 1) * PAIR_W].astype(dst.dtype)


def _mixer_back(x_ref, mod_ref, win_ref, wpool_ref, ps_ref, wpa_ref, wpb_ref, wout_ref, x1_ref,
                hb_s, u_s, o_s, pg_s, seq):
    tile = x_ref.shape[0]
    gate = mod_ref[0, 2:3, :]
    pos = lax.broadcasted_iota(jnp.int32, (tile, 1), 0) % seq
    for g, w in enumerate(POOL_WINDOWS):
        cols = slice(g * POOL_GROUP_DIM, (g + 1) * POOL_GROUP_DIM)
        pg_s[:, cols] = _pool_group(u_s[:, cols], pos, seq, w).astype(BF16)
    for c in range(tile // TOK_BLOCK):
        rows = slice(c * TOK_BLOCK, (c + 1) * TOK_BLOCK)
        ys = []
        for g in range(len(POOL_WINDOWS)):
            cols = slice(g * POOL_GROUP_DIM, (g + 1) * POOL_GROUP_DIM)
            ys.append((_dot(pg_s[rows, cols], wpool_ref[g]) * ps_ref[:, cols]).astype(BF16))
        a = _dot(jnp.concatenate(ys, axis=1), wpa_ref[...])
        ob = _dot(jnp.concatenate([o_s[g, rows, :] for g in range(N_PAIRS)], axis=1), wpb_ref[...])
        gab = _dot(hb_s[rows, :], win_ref[:, POOL_DIM + 3 * ATTN_DIM:IN_DIM])
        merged = _sigmoid(gab[:, :D_MODEL]) * a + _sigmoid(gab[:, D_MODEL:]) * ob
        mix = _dot(merged.astype(BF16), wout_ref[...])
        x1_ref[rows, :] = x_ref[rows, :] + gate * mix


def _ctx_mixer_kernel(x_ref, mod_ref, gmix_ref, win_ref, wpool_ref, ps_ref, wpa_ref, wpb_ref, wout_ref,
                      x1_ref, ko_ref, vo_ref, hb_s, u_s, q_s, k_s, v_s, o_s, pg_s):
    _mixer_front(x_ref, mod_ref, gmix_ref, win_ref, hb_s, u_s, q_s, k_s, v_s)
    even = lax.broadcasted_iota(jnp.int32, (1, PAIR_W), 1) < HEAD_DIM
    tile = x_ref.shape[0]
    for s in range(tile // SEQ):
        rows = slice(s * SEQ, (s + 1) * SEQ)
        for g in range(N_PAIRS):
            q2, k2, v2 = q_s[g, rows, :], k_s[g, rows, :], v_s[g, rows, :]
            ko_ref[s, 0, 2 * g] = k2[:, :HEAD_DIM]
            ko_ref[s, 0, 2 * g + 1] = k2[:, HEAD_DIM:]
            vo_ref[s, 0, 2 * g] = v2[:, :HEAD_DIM]
            vo_ref[s, 0, 2 * g + 1] = v2[:, HEAD_DIM:]
            kb, vb = k2.astype(BF16), v2.astype(BF16)
            outs = []
            for par in range(2):
                qm = jnp.where(even if par == 0 else jnp.logical_not(even), q2, 0.0).astype(BF16)
                sc = _dot_nt(qm, kb)
                m = jnp.max(sc, axis=-1, keepdims=True)
                p = jnp.exp(sc - m)
                l = jnp.sum(p, axis=-1, keepdims=True)
                outs.append(_dot(p.astype(BF16), vb) / l)
            o_s[g, rows, :] = jnp.where(even, outs[0], outs[1]).astype(BF16)
    _mixer_back(x_ref, mod_ref, win_ref, wpool_ref, ps_ref, wpa_ref, wpb_ref, wout_ref, x1_ref,
                hb_s, u_s, o_s, pg_s, SEQ)


def _row_window(r):
    rs = min(max(r - WIN_R // 2, 0), ROWS - WIN_R)
    return rs, rs - r + WIN_R - 1


def _lat_mixer_kernel(x_ref, mod_ref, gmix_ref, win_ref, wpool_ref, ps_ref, wpa_ref, wpb_ref, wout_ref,
                      kc_ref, vc_ref, tb_ref, x1_ref, hb_s, u_s, q_s, k_s, v_s, o_s, pg_s):
    _mixer_front(x_ref, mod_ref, gmix_ref, win_ref, hb_s, u_s, q_s, k_s, v_s)
    even = lax.broadcasted_iota(jnp.int32, (1, PAIR_W), 1) < HEAD_DIM
    nk = WIN_R * GRID_W

    def pair_body(g, carry):
        q2 = q_s[g]
        kb, vb = k_s[g].astype(BF16), v_s[g].astype(BF16)
        kcb, vcb = kc_ref[0, g].astype(BF16), vc_ref[0, g].astype(BF16)
        outs = []
        for par in range(2):
            qm = jnp.where(even if par == 0 else jnp.logical_not(even), q2, 0.0).astype(BF16)
            s_ctx = _dot_nt(qm, kcb)
            slabs = []
            for r in range(ROWS):
                rs, rho = _row_window(r)
                bias = tb_ref[2 * g + par, rho % 2, :, (rho - rho % 2) * GRID_W:(rho - rho % 2) * GRID_W + nk]
                sl = _dot_nt(qm[r * GRID_W:(r + 1) * GRID_W, :], kb[rs * GRID_W:rs * GRID_W + nk, :])
                slabs.append(sl + bias)
            s_loc = jnp.concatenate(slabs, axis=0)
            m = jnp.maximum(jnp.max(s_loc, axis=-1, keepdims=True), jnp.max(s_ctx, axis=-1, keepdims=True))
            p_loc = jnp.exp(s_loc - m)
            p_ctx = jnp.exp(s_ctx - m)
            l = jnp.sum(p_loc, axis=-1, keepdims=True) + jnp.sum(p_ctx, axis=-1, keepdims=True)
            p_locb = p_loc.astype(BF16)
            o_rows = []
            for r in range(ROWS):
                rs, _ = _row_window(r)
                o_rows.append(_dot(p_locb[r * GRID_W:(r + 1) * GRID_W, :], vb[rs * GRID_W:rs * GRID_W + nk, :]))
            o = jnp.concatenate(o_rows, axis=0) + _dot(p_ctx.astype(BF16), vcb)
            outs.append(o / l)
        o_s[g] = jnp.where(even, outs[0], outs[1]).astype(BF16)
        return carry

    lax.fori_loop(0, N_PAIRS, pair_body, 0)
    _mixer_back(x_ref, mod_ref, win_ref, wpool_ref, ps_ref, wpa_ref, wpb_ref, wout_ref, x1_ref,
                hb_s, u_s, o_s, pg_s, DEC_SEQ)


def _mixer_scratch(tile, kv_dtype):
    return [pltpu.VMEM((tile, D_MODEL), BF16),
            pltpu.VMEM((tile, POOL_DIM), F32),
            pltpu.VMEM((N_PAIRS, tile, PAIR_W), BF16),
            pltpu.VMEM((N_PAIRS, tile, PAIR_W), kv_dtype),
            pltpu.VMEM((N_PAIRS, tile, PAIR_W), kv_dtype),
            pltpu.VMEM((N_PAIRS, tile, PAIR_W), BF16),
            pltpu.VMEM((tile, POOL_DIM), BF16)]


def _weight_specs():
    return [_const_spec((1, D_MODEL)),
            _const_spec((D_MODEL, IN_DIM)),
            _const_spec((len(POOL_WINDOWS), POOL_GROUP_DIM, POOL_GROUP_DIM)),
            _const_spec((1, POOL_DIM)),
            _const_spec((POOL_DIM, D_MODEL)),
            _const_spec((ATTN_DIM, D_MODEL)),
            _const_spec((D_MODEL, D_MODEL))]


def _ctx_mixer(x, modv, weights):
    n = x.shape[0]
    nseq = TOK_BLOCK // SEQ
    cache = jax.ShapeDtypeStruct((n // SEQ, 1, N_HEADS, SEQ, HEAD_DIM), F32)
    cache_spec = pl.BlockSpec((nseq, 1, N_HEADS, SEQ, HEAD_DIM), lambda i: (i, 0, 0, 0, 0))
    return pl.pallas_call(
        _ctx_mixer_kernel,
        grid=(n // TOK_BLOCK,),
        in_specs=[pl.BlockSpec((TOK_BLOCK, D_MODEL), lambda i: (i, 0)),
                  pl.BlockSpec((1, N_MOD, D_MODEL), lambda i: (0, 0, 0))] + _weight_specs(),
        out_specs=[pl.BlockSpec((TOK_BLOCK, D_MODEL), lambda i: (i, 0)), cache_spec, cache_spec],
        out_shape=[jax.ShapeDtypeStruct((n, D_MODEL), F32), cache, cache],
        scratch_shapes=_mixer_scratch(TOK_BLOCK, F32),
        compiler_params=pltpu.CompilerParams(dimension_semantics=("arbitrary",),
                                             vmem_limit_bytes=V7X_VMEM_LIMIT),
        name="ctx_mixer",
    )(x, modv, *weights)


def _lat_mixer(x, modv, weights, kc, vc, tb):
    n = x.shape[0]
    return pl.pallas_call(
        _lat_mixer_kernel,
        grid=(n // DEC_SEQ,),
        in_specs=[pl.BlockSpec((DEC_SEQ, D_MODEL), lambda i: (i, 0)),
                  pl.BlockSpec((1, N_MOD, D_MODEL), lambda i: (i + 1, 0, 0))] + _weight_specs() + [
                  pl.BlockSpec((1, N_PAIRS, PAST_LEN, PAIR_W), lambda i: (i, 0, 0, 0)),
                  pl.BlockSpec((1, N_PAIRS, PAST_LEN, PAIR_W), lambda i: (i, 0, 0, 0)),
                  _const_spec((N_HEADS, 2, GRID_W, ROWS * GRID_W))],
        out_specs=pl.BlockSpec((DEC_SEQ, D_MODEL), lambda i: (i, 0)),
        out_shape=jax.ShapeDtypeStruct((n, D_MODEL), F32),
        scratch_shapes=_mixer_scratch(DEC_SEQ, BF16),
        compiler_params=pltpu.CompilerParams(dimension_semantics=("arbitrary",),
                                             vmem_limit_bytes=V7X_VMEM_LIMIT),
        name="lat_mixer",
    )(x, modv, *weights, kc, vc, tb)


N_RPB_ROWS = 2 * WIN_R - 1
N_RPB_COLS = 2 * WIN_C - 1
TABLE_W = ROWS * GRID_W


def _bias_kernel(v_ref, keep_ref, o_ref):
    for h in range(N_HEADS):
        for par in range(2):
            x = jnp.broadcast_to(v_ref[h, par:par + 1, :], (GRID_W, TABLE_W))
            shifted = pltpu.roll(x, TABLE_W - (WIN_C - 1), 1, stride=1, stride_axis=0)
            o_ref[h, par] = jnp.where(keep_ref[par] > 0.0, shifted, NEG_INF)


def _bias_tables(rpb):
    col = np.arange(GRID_W)
    cs = np.clip(col - WIN_C // 2, 0, GRID_W - WIN_C)
    in_win = (col[None, :] >= cs[:, None]) & (col[None, :] < cs[:, None] + WIN_C)
    keep = np.tile(in_win.astype(np.float32), (2, 1, ROWS))
    keep[0, :, N_RPB_ROWS * GRID_W:] = 0.0
    keep[1, :, (N_RPB_ROWS - 1) * GRID_W:] = 0.0
    rp = jnp.pad(rpb.astype(F32), ((0, 0), (0, ROWS + 1 - N_RPB_ROWS), (0, GRID_W - N_RPB_COLS)))
    v = jnp.stack([rp[:, :ROWS].reshape(N_HEADS, TABLE_W), rp[:, 1:].reshape(N_HEADS, TABLE_W)], axis=1)
    return pl.pallas_call(
        _bias_kernel,
        out_shape=jax.ShapeDtypeStruct((N_HEADS, 2, GRID_W, TABLE_W), F32),
        name="bias_tables",
    )(v, jnp.asarray(keep))


SORT_ROWS = 512


def _sort_kernel(xc_ref, xl_ref, mod_ref, gffn_ref, wr_ref, br_ref,
                 xs_ref, dk_ref, np_ref, off_ref):
    j = pl.program_id(0)
    x = jnp.where(j < _N_CTX_BLOCKS, xc_ref[...], xl_ref[...])
    shift, scale = mod_ref[0, 3:4, :], mod_ref[0, 4:5, :]
    hb = _norm_mod(x, gffn_ref[...], scale, shift).astype(BF16)
    logits = _dot_nt(wr_ref[...], hb) + br_ref[...]
    eio = lax.broadcasted_iota(jnp.int32, logits.shape, 0)
    work = logits
    sels, vals = [], []
    for _ in range(TOP_K):
        m = jnp.max(work, axis=0, keepdims=True)
        idx = jnp.min(jnp.where(work == m, eio, N_EXPERTS), axis=0, keepdims=True)
        sel = eio == idx
        sels.append(sel)
        vals.append(m)
        work = jnp.where(sel, -jnp.inf, work)
    exps = [jnp.exp(v - vals[0]) for v in vals]
    den = exps[0] + exps[1] + exps[2] + exps[3]
    mask = jnp.zeros(logits.shape, F32)
    for sel in sels:
        mask = mask + jnp.where(sel, 1.0, 0.0)
    t_row = lax.broadcasted_iota(jnp.int32, (TOK_BLOCK, TOK_BLOCK), 0)
    t_col = lax.broadcasted_iota(jnp.int32, (TOK_BLOCK, TOK_BLOCK), 1)
    before = jnp.where(t_row < t_col, 1.0, 0.0).astype(BF16)
    rank = _dot(mask.astype(BF16), before)
    cnt = jnp.sum(mask, axis=1, keepdims=True)
    np16 = jnp.floor((cnt + (CHUNK - 1.0)) * (1.0 / CHUNK))
    e_row = lax.broadcasted_iota(jnp.int32, (N_EXPERTS, N_EXPERTS), 0)
    e_col = lax.broadcasted_iota(jnp.int32, (N_EXPERTS, N_EXPERTS), 1)
    lower = jnp.where(e_col < e_row, 1.0, 0.0).astype(BF16)
    np16_b = jnp.broadcast_to(np16, (N_EXPERTS, 128))
    off16 = _dot(lower, np16_b.astype(BF16))
    np_ref[0] = np16_b
    off_ref[0] = off16
    dest = off16[:, 0:1] * float(CHUNK) + rank
    dests = []
    for k in range(TOP_K):
        dk = jnp.sum(jnp.where(sels[k], dest, 0.0), axis=0, keepdims=True)
        dk_ref[0, k:k + 1, :] = dk
        dk_ref[0, TOP_K + k:TOP_K + k + 1, :] = exps[k] / den
        dests.append(dk.astype(jnp.int32))
    rio = lax.broadcasted_iota(jnp.int32, (SORT_ROWS, TOK_BLOCK), 0)
    for c in range(BLOCK_CAP // SORT_ROWS):
        onehot = jnp.zeros((SORT_ROWS, TOK_BLOCK), F32)
        for dk in dests:
            onehot = jnp.where(rio == dk - c * SORT_ROWS, 1.0, onehot)
        xs_ref[c * SORT_ROWS:(c + 1) * SORT_ROWS, :] = _dot(onehot.astype(BF16), hb).astype(BF16)


def _block_mod_index(j):
    return jnp.where(j < _N_CTX_BLOCKS, 0, 1 + (j - _N_CTX_BLOCKS) // (DEC_SEQ // TOK_BLOCK))


def _token_specs():
    return [pl.BlockSpec((TOK_BLOCK, D_MODEL), lambda j: (jnp.minimum(j, _N_CTX_BLOCKS - 1), 0)),
            pl.BlockSpec((TOK_BLOCK, D_MODEL), lambda j: (jnp.maximum(j - _N_CTX_BLOCKS, 0), 0)),
            pl.BlockSpec((1, N_MOD, D_MODEL), lambda j: (_block_mod_index(j), 0, 0))]


def _route_sort(x1c, x1l, modv, g_ffn, wr_t, br):
    tbl = jax.ShapeDtypeStruct((_N_BLOCKS, N_EXPERTS, 128), F32)
    tbl_spec = pl.BlockSpec((1, N_EXPERTS, 128), lambda j: (j, 0, 0))
    return pl.pallas_call(
        _sort_kernel,
        grid=(_N_BLOCKS,),
        in_specs=_token_specs() + [_const_spec((1, D_MODEL)),
                                   _const_spec((N_EXPERTS, D_MODEL)),
                                   _const_spec((N_EXPERTS, 1))],
        out_specs=[pl.BlockSpec((BLOCK_CAP, D_MODEL), lambda j: (j, 0)),
                   pl.BlockSpec((1, 2 * TOP_K, TOK_BLOCK), lambda j: (j, 0, 0)),
                   tbl_spec, tbl_spec],
        out_shape=[jax.ShapeDtypeStruct((_N_BLOCKS * BLOCK_CAP, D_MODEL), BF16),
                   jax.ShapeDtypeStruct((_N_BLOCKS, 2 * TOP_K, TOK_BLOCK), F32),
                   tbl, tbl],
        compiler_params=pltpu.CompilerParams(dimension_semantics=("arbitrary",),
                                             vmem_limit_bytes=V7X_VMEM_LIMIT),
        name="route_sort",
    )(x1c, x1l, modv, g_ffn, wr_t, br)


EXPERT_BUF_ROWS = 2048
PASS_CHUNKS = EXPERT_BUF_ROWS // CHUNK
N_ROW_BUFS = 3


def _expert_kernel(np_ref, off_ref, wu_ref, bu_ref, wd_ref, bd_ref, xs_hbm, ys_hbm,
                   buf, wu_s, wd_s, gsem, ssem, pend):
    del xs_hbm
    e = pl.program_id(0)
    last = pl.num_programs(0) - 1
    slot = e % N_ROW_BUFS
    nxt = (e + 1) % N_ROW_BUFS

    def move_chunks(ex, q_lo, slot_, gather):
        def block_body(j, q0):
            n = np_ref[j * N_EXPERTS + ex]
            off = off_ref[j * N_EXPERTS + ex]
            c_lo = jnp.clip(q_lo - q0, 0, n)
            m = jnp.clip(q_lo + PASS_CHUNKS - q0, 0, n) - c_lo

            @pl.when(m > 0)
            def _():
                rows = pl.multiple_of(m * CHUNK, CHUNK)
                row0 = pl.multiple_of(j * BLOCK_CAP + (off + c_lo) * CHUNK, CHUNK)
                brow0 = pl.multiple_of((q0 + c_lo - q_lo) * CHUNK, CHUNK)
                hbm = ys_hbm.at[pl.ds(row0, rows), :]
                vm = buf.at[slot_, pl.ds(brow0, rows), :]
                if gather:
                    pltpu.make_async_copy(hbm, vm, gsem.at[slot_]).start()
                else:
                    pltpu.make_async_copy(vm, hbm, ssem.at[slot_]).start()

            return q0 + n

        lax.fori_loop(0, _N_BLOCKS, block_body, 0)

    def wait_chunks(sem, n, slot_):
        @pl.when(n > 0)
        def _():
            rows = pl.multiple_of(n * CHUNK, CHUNK)
            pltpu.make_async_copy(ys_hbm.at[pl.ds(0, rows), :], buf.at[slot_, pl.ds(0, rows), :], sem).wait()

    def mlp_rows(r0, rows):
        x = buf[slot, pl.ds(r0, rows), :]
        gu = _dot(x, wu_s[...]) + bu_ref[0]
        gate = jnp.minimum(gu[:, :D_FF], SWIGLU_LIMIT)
        up = jnp.clip(gu[:, D_FF:], -SWIGLU_LIMIT, SWIGLU_LIMIT)
        glu = gate * _sigmoid(SWIGLU_ALPHA * gate)
        y = _dot(((up + 1.0) * glu).astype(BF16), wd_s[...]) + bd_ref[0]
        buf[slot, pl.ds(r0, rows), :] = y.astype(BF16)

    def compute(n):
        rows = n * CHUNK
        rem = rows % EXPERT_ROW_TILE
        half = EXPERT_ROW_TILE // 2
        n_full = rows // EXPERT_ROW_TILE + jnp.where(rem > half, 1, 0)

        def tile_body(t, carry):
            mlp_rows(pl.multiple_of(t * EXPERT_ROW_TILE, EXPERT_ROW_TILE), EXPERT_ROW_TILE)
            return carry

        lax.fori_loop(0, n_full, tile_body, 0)

        @pl.when((rem > 0) & (rem <= half))
        def _():
            mlp_rows(pl.multiple_of(rows - rem, half), half)

    def chunks_of(ex):
        return lax.fori_loop(0, _N_BLOCKS, lambda j, acc: acc + np_ref[j * N_EXPERTS + ex], 0)

    @pl.when(e == 0)
    def _():
        buf[...] = jnp.zeros_like(buf)
        for b in range(N_ROW_BUFS):
            pend[b] = 0
        move_chunks(0, 0, 0, True)

    wait_chunks(ssem.at[nxt], pend[nxt], nxt)
    pend[nxt] = 0

    @pl.when(e < last)
    def _():
        move_chunks(e + 1, 0, nxt, True)

    wu_s[...] = wu_ref[0].astype(BF16)
    wd_s[...] = wd_ref[0].astype(BF16)

    total = chunks_of(e)
    n0 = jnp.minimum(total, PASS_CHUNKS)
    wait_chunks(gsem.at[slot], n0, slot)
    compute(n0)
    move_chunks(e, 0, slot, False)
    pend[slot] = n0

    def pass_body(p, carry):
        wait_chunks(ssem.at[slot], pend[slot], slot)
        lo = p * PASS_CHUNKS
        n = jnp.minimum(total - lo, PASS_CHUNKS)
        move_chunks(e, lo, slot, True)
        wait_chunks(gsem.at[slot], n, slot)
        compute(n)
        move_chunks(e, lo, slot, False)
        pend[slot] = n
        return carry

    lax.fori_loop(1, (total + PASS_CHUNKS - 1) // PASS_CHUNKS, pass_body, 0)

    @pl.when(e == last)
    def _():
        for b in range(N_ROW_BUFS):
            wait_chunks(ssem.at[b], pend[b], b)
            pend[b] = 0


def _experts(np16, off16, w_up, b_up, w_down, b_down, xs):
    grid_spec = pltpu.PrefetchScalarGridSpec(
        num_scalar_prefetch=2,
        grid=(N_EXPERTS,),
        in_specs=[pl.BlockSpec((1, D_MODEL, 2 * D_FF), lambda e, *_: (e, 0, 0)),
                  pl.BlockSpec((1, 1, 2 * D_FF), lambda e, *_: (e, 0, 0)),
                  pl.BlockSpec((1, D_FF, D_MODEL), lambda e, *_: (e, 0, 0)),
                  pl.BlockSpec((1, 1, D_MODEL), lambda e, *_: (e, 0, 0)),
                  pl.BlockSpec(memory_space=pl.ANY)],
        out_specs=pl.BlockSpec(memory_space=pl.ANY),
        scratch_shapes=[pltpu.VMEM((N_ROW_BUFS, EXPERT_BUF_ROWS, D_MODEL), BF16),
                        pltpu.VMEM((D_MODEL, 2 * D_FF), BF16),
                        pltpu.VMEM((D_FF, D_MODEL), BF16),
                        pltpu.SemaphoreType.DMA((N_ROW_BUFS,)),
                        pltpu.SemaphoreType.DMA((N_ROW_BUFS,)),
                        pltpu.SMEM((N_ROW_BUFS,), jnp.int32)],
    )
    return pl.pallas_call(
        _expert_kernel,
        grid_spec=grid_spec,
        out_shape=jax.ShapeDtypeStruct(xs.shape, xs.dtype),
        input_output_aliases={6: 0},
        compiler_params=pltpu.CompilerParams(dimension_semantics=("arbitrary",),
                                             vmem_limit_bytes=V7X_VMEM_LIMIT),
        name="experts",
    )(np16, off16, w_up, b_up.reshape(N_EXPERTS, 1, 2 * D_FF), w_down,
      b_down.reshape(N_EXPERTS, 1, D_MODEL), xs)


def _combine_kernel(ys_ref, dk_ref, xc_ref, xl_ref, mod_ref, gfin_ref, yc_ref, yl_ref, y_s):
    j = pl.program_id(0)
    half = TOK_BLOCK // 2
    cio = lax.broadcasted_iota(jnp.int32, (half, SORT_ROWS), 1).astype(F32)
    halves = [slice(h * half, (h + 1) * half) for h in range(2)]
    accs = [jnp.zeros((half, D_MODEL), F32) for _ in halves]
    for c in range(BLOCK_CAP // SORT_ROWS):
        ys = ys_ref[c * SORT_ROWS:(c + 1) * SORT_ROWS, :]
        for h, rows in enumerate(halves):
            d = dk_ref[0, rows, :]
            w = jnp.zeros((half, SORT_ROWS), F32)
            for k in range(TOP_K):
                w = jnp.where(cio == d[:, k:k + 1] - float(c * SORT_ROWS), d[:, TOP_K + k:TOP_K + k + 1], w)
            accs[h] = accs[h] + _dot(w.astype(BF16), ys)
    for h, rows in enumerate(halves):
        acc = accs[h]
        x1 = jnp.where(j < _N_CTX_BLOCKS, xc_ref[rows, :], xl_ref[rows, :])
        x2 = x1 + mod_ref[0, 5:6, :] * acc
        ms = jnp.mean(x2 * x2, axis=-1, keepdims=True)
        y_s[rows, :] = x2 * lax.rsqrt(ms + RMS_EPS) * gfin_ref[...]

    @pl.when(j < _N_CTX_BLOCKS)
    def _():
        yc_ref[...] = y_s[...]

    @pl.when(j >= _N_CTX_BLOCKS)
    def _():
        yl_ref[...] = y_s[...]


def _combine(ys, dk_t, x1c, x1l, modv, g_final):
    return pl.pallas_call(
        _combine_kernel,
        grid=(_N_BLOCKS,),
        in_specs=[pl.BlockSpec((BLOCK_CAP, D_MODEL), lambda j: (j, 0)),
                  pl.BlockSpec((1, TOK_BLOCK, 2 * TOP_K), lambda j: (j, 0, 0))] + _token_specs() + [
                  _const_spec((1, D_MODEL))],
        out_specs=[pl.BlockSpec((TOK_BLOCK, D_MODEL), lambda j: (jnp.minimum(j, _N_CTX_BLOCKS - 1), 0)),
                   pl.BlockSpec((TOK_BLOCK, D_MODEL), lambda j: (jnp.maximum(j - _N_CTX_BLOCKS, 0), 0))],
        out_shape=[jax.ShapeDtypeStruct(x1c.shape, F32), jax.ShapeDtypeStruct(x1l.shape, F32)],
        scratch_shapes=[pltpu.VMEM((TOK_BLOCK, D_MODEL), F32)],
        compiler_params=pltpu.CompilerParams(dimension_semantics=("arbitrary",),
                                             vmem_limit_bytes=V7X_VMEM_LIMIT),
        name="combine",
    )(ys, dk_t, x1c, x1l, modv, g_final)


def kernel(x_prompt, x_sample, cache_k, cache_v, c, c_ctx, w_ada, b_ada, g_mix, w_in, w_pool, pool_scale,
           w_pa, w_pb, rpb, w_out, g_ffn, w_router, b_router, w_up, b_up, w_down, b_down, g_final):
    assert w_ada.shape[0] == 1, "single trunk layer"
    batch, seq, d = x_prompt.shape
    dec_batch, dec_seq, _ = x_sample.shape
    assert (seq, dec_seq, d) == (SEQ, DEC_SEQ, D_MODEL)
    assert batch * seq == _N_CTX_BLOCKS * TOK_BLOCK and dec_batch * dec_seq == _N_LAT_BLOCKS * TOK_BLOCK

    cmat = jnp.concatenate([c_ctx[None, :], c, jnp.zeros((8 - 1 - dec_batch, d), F32)], axis=0)
    modv = _modulation(cmat, w_ada[0], b_ada[0]).reshape(8, N_MOD, d)

    weights = (g_mix[0][None, :], w_in[0].astype(BF16), w_pool[0].astype(BF16), pool_scale[0][None, :],
               w_pa[0].astype(BF16), w_pb[0].astype(BF16), w_out[0].astype(BF16))

    def by_pair(cache):
        z = cache[:, 0].reshape(dec_batch, N_PAIRS, 2, PAST_LEN, HEAD_DIM)
        return z.transpose(0, 1, 3, 2, 4).reshape(dec_batch, N_PAIRS, PAST_LEN, PAIR_W)

    x1c, new_k, new_v = _ctx_mixer(x_prompt.reshape(batch * seq, d), modv, weights)
    x1l = _lat_mixer(x_sample.reshape(dec_batch * dec_seq, d), modv, weights,
                     by_pair(cache_k), by_pair(cache_v), _bias_tables(rpb[0]))

    xs, dk, np16, off16 = _route_sort(x1c, x1l, modv, g_ffn[0][None, :],
                                      w_router[0].T.astype(BF16), b_router[0][:, None])
    np16_i = np16[:, :, 0].astype(jnp.int32).reshape(-1)
    off16_i = off16[:, :, 0].astype(jnp.int32).reshape(-1)
    ys = _experts(np16_i, off16_i, w_up[0], b_up[0], w_down[0], b_down[0], xs)
    yc, yl = _combine(ys, dk.transpose(0, 2, 1), x1c, x1l, modv, g_final[None, :])
    return (yc.reshape(batch, seq, d), yl.reshape(dec_batch, dec_seq, d), new_k, new_v)
```

```python
import functools

import jax
import jax.numpy as jnp
import numpy as np
from jax import lax
from jax.experimental import pallas as pl
from jax.experimental.pallas import tpu as pltpu

F32 = jnp.float32
BF16 = jnp.bfloat16

D_MODEL = 1024
SEQ = 256
DEC_SEQ = 1024
GRID_W = 64
ROWS = DEC_SEQ // GRID_W
N_HEADS = 8
HEAD_DIM = 64
N_PAIRS = N_HEADS // 2
PAIR_W = 2 * HEAD_DIM
PAST_LEN = 512
POOL_DIM = 512
POOL_WINDOWS = (2, 4, 8, 16)
POOL_GROUP_DIM = 128
ATTN_DIM = 512
WIN_R = 8
WIN_C = 16
N_EXPERTS = 32
TOP_K = 4
D_FF = 1024
SWIGLU_LIMIT = 7.0
SWIGLU_ALPHA = 1.702
N_MOD = 6
RMS_EPS = 1e-6
NEG_INF = -1e30
ATTN_SCALE = HEAD_DIM ** -0.5
IN_DIM = POOL_DIM + 3 * ATTN_DIM + 2 * D_MODEL

TOK_BLOCK = 512
CHUNK = 16
BLOCK_CAP = TOK_BLOCK * TOP_K + N_EXPERTS * CHUNK
EXPERT_ROW_TILE = 256
V7X_VMEM_LIMIT = 60 * 1024 * 1024

_N_CTX_BLOCKS = 8
_N_LAT_BLOCKS = 4
_N_BLOCKS = _N_CTX_BLOCKS + _N_LAT_BLOCKS

_NT = (((1,), (1,)), ((), ()))


def _dot(a, b):
    return jnp.dot(a, b, preferred_element_type=F32)


def _dot_nt(a, b):
    return lax.dot_general(a, b, _NT, preferred_element_type=F32)


def _sigmoid(x):
    return 1.0 / (1.0 + jnp.exp(-x))


def _norm_mod(x, gain, scale, shift):
    ms = jnp.mean(x * x, axis=-1, keepdims=True)
    return (x * lax.rsqrt(ms + RMS_EPS) * gain) * (1.0 + scale) + shift


def _const_spec(shape):
    zeros = (0,) * len(shape)
    return pl.BlockSpec(shape, lambda *_: zeros, pipeline_mode=pl.Buffered(1))


MOD_COLS = 1536


def _mod_kernel(c_ref, w_ref, b_ref, o_ref):
    c = c_ref[...]
    s = c * _sigmoid(c)
    s_hi = s.astype(BF16)
    s_lo = (s - s_hi.astype(F32)).astype(BF16)
    r = _dot(jnp.concatenate([s_hi, s_lo], axis=0), w_ref[...].astype(BF16))
    o_ref[...] = r[:8] + r[8:] + b_ref[...]


def _modulation(cmat, w_ada, b_ada):
    n = w_ada.shape[1]
    return pl.pallas_call(
        _mod_kernel,
        grid=(n // MOD_COLS,),
        in_specs=[pl.BlockSpec((8, D_MODEL), lambda i: (0, 0)),
                  pl.BlockSpec((D_MODEL, MOD_COLS), lambda i: (0, i)),
                  pl.BlockSpec((1, MOD_COLS), lambda i: (0, i))],
        out_specs=pl.BlockSpec((8, MOD_COLS), lambda i: (0, i)),
        out_shape=jax.ShapeDtypeStruct((8, n), F32),
        name="modulation",
    )(cmat, w_ada, b_ada.reshape(1, n))


def _pool_mix(u, pos, seq):
    n = u.shape[0]

    def down(x, d):
        return jnp.where(pos >= d, pltpu.roll(x, d, 0), 0.0)

    def up(x, d):
        return jnp.where(pos < seq - d, pltpu.roll(x, n - d, 0), 0.0)

    return down, up


def _pool_group(u, pos, seq, w):
    down, up = _pool_mix(u, pos, seq)
    hw = w // 2
    back = u
    fwd = u
    d = 1
    while d < hw:
        back = back + down(back, d)
        fwd = fwd + up(fwd, d)
        d *= 2
    s = down(back, 1) + fwd
    posf = pos.astype(F32)
    cnt = jnp.minimum(posf + hw, float(seq)) - jnp.maximum(posf - hw, 0.0)
    return s / cnt - u


def _mixer_front(x_ref, mod_ref, gmix_ref, win_ref, hb_s, u_s, q_s, k_s, v_s):
    x = x_ref[...]
    shift, scale = mod_ref[0, 0:1, :], mod_ref[0, 1:2, :]
    hb = _norm_mod(x, gmix_ref[...], scale, shift).astype(BF16)
    hb_s[...] = hb
    u_s[...] = _dot(hb, win_ref[:, 0:POOL_DIM])
    for dst, base in ((q_s, POOL_DIM), (k_s, POOL_DIM + ATTN_DIM), (v_s, POOL_DIM + 2 * ATTN_DIM)):
        z = _dot(hb, win_ref[:, base:base + ATTN_DIM])
        if dst is q_s:
            z = z * ATTN_SCALE
        for g in range(N_PAIRS):
            dst[g] = z[:, g * PAIR_W:(g + 1) * PAIR_W].astype(dst.dtype)


def _mixer_back(x_ref, mod_ref, win_ref, wpool_ref, ps_ref, wpa_ref, wpb_ref, wout_ref, x1_ref,
                hb_s, u_s, o_s, pg_s, seq):
    tile = x_ref.shape[0]
    gate = mod_ref[0, 2:3, :]
    pos = lax.broadcasted_iota(jnp.int32, (tile, 1), 0) % seq
    for g, w in enumerate(POOL_WINDOWS):
        cols = slice(g * POOL_GROUP_DIM, (g + 1) * POOL_GROUP_DIM)
        pg_s[:, cols] = _pool_group(u_s[:, cols], pos, seq, w).astype(BF16)
    for c in range(tile // TOK_BLOCK):
        rows = slice(c * TOK_BLOCK, (c + 1) * TOK_BLOCK)
        ys = []
        for g in range(len(POOL_WINDOWS)):
            cols = slice(g * POOL_GROUP_DIM, (g + 1) * POOL_GROUP_DIM)
            ys.append((_dot(pg_s[rows, cols], wpool_ref[g]) * ps_ref[:, cols]).astype(BF16))
        a = _dot(jnp.concatenate(ys, axis=1), wpa_ref[...])
        ob = _dot(jnp.concatenate([o_s[g, rows, :] for g in range(N_PAIRS)], axis=1), wpb_ref[...])
        gab = _dot(hb_s[rows, :], win_ref[:, POOL_DIM + 3 * ATTN_DIM:IN_DIM])
        merged = _sigmoid(gab[:, :D_MODEL]) * a + _sigmoid(gab[:, D_MODEL:]) * ob
        mix = _dot(merged.astype(BF16), wout_ref[...])
        x1_ref[rows, :] = x_ref[rows, :] + gate * mix


def _ctx_mixer_kernel(x_ref, mod_ref, gmix_ref, win_ref, wpool_ref, ps_ref, wpa_ref, wpb_ref, wout_ref,
                      x1_ref, ko_ref, vo_ref, hb_s, u_s, q_s, k_s, v_s, o_s, pg_s):
    _mixer_front(x_ref, mod_ref, gmix_ref, win_ref, hb_s, u_s, q_s, k_s, v_s)
    even = lax.broadcasted_iota(jnp.int32, (1, PAIR_W), 1) < HEAD_DIM
    tile = x_ref.shape[0]
    for s in range(tile // SEQ):
        rows = slice(s * SEQ, (s + 1) * SEQ)
        for g in range(N_PAIRS):
            q2, k2, v2 = q_s[g, rows, :], k_s[g, rows, :], v_s[g, rows, :]
            ko_ref[s, 0, 2 * g] = k2[:, :HEAD_DIM]
            ko_ref[s, 0, 2 * g + 1] = k2[:, HEAD_DIM:]
            vo_ref[s, 0, 2 * g] = v2[:, :HEAD_DIM]
            vo_ref[s, 0, 2 * g + 1] = v2[:, HEAD_DIM:]
            kb, vb = k2.astype(BF16), v2.astype(BF16)
            outs = []
            for par in range(2):
                qm = jnp.where(even if par == 0 else jnp.logical_not(even), q2, 0.0).astype(BF16)
                sc = _dot_nt(qm, kb)
                m = jnp.max(sc, axis=-1, keepdims=True)
                p = jnp.exp(sc - m)
                l = jnp.sum(p, axis=-1, keepdims=True)
                outs.append(_dot(p.astype(BF16), vb) / l)
            o_s[g, rows, :] = jnp.where(even, outs[0], outs[1]).astype(BF16)
    _mixer_back(x_ref, mod_ref, win_ref, wpool_ref, ps_ref, wpa_ref, wpb_ref, wout_ref, x1_ref,
                hb_s, u_s, o_s, pg_s, SEQ)


def _row_window(r):
    rs = min(max(r - WIN_R // 2, 0), ROWS - WIN_R)
    return rs, rs - r + WIN_R - 1


def _lat_mixer_kernel(x_ref, mod_ref, gmix_ref, win_ref, wpool_ref, ps_ref, wpa_ref, wpb_ref, wout_ref,
                      kc_ref, vc_ref, tb_ref, x1_ref, hb_s, u_s, q_s, k_s, v_s, o_s, pg_s):
    _mixer_front(x_ref, mod_ref, gmix_ref, win_ref, hb_s, u_s, q_s, k_s, v_s)
    even = lax.broadcasted_iota(jnp.int32, (1, PAIR_W), 1) < HEAD_DIM
    nk = WIN_R * GRID_W

    def pair_body(g, carry):
        q2 = q_s[g]
        kb, vb = k_s[g].astype(BF16), v_s[g].astype(BF16)
        kcb, vcb = kc_ref[0, g].astype(BF16), vc_ref[0, g].astype(BF16)
        outs = []
        for par in range(2):
            qm = jnp.where(even if par == 0 else jnp.logical_not(even), q2, 0.0).astype(BF16)
            s_ctx = _dot_nt(qm, kcb)
            slabs = []
            for r in range(ROWS):
                rs, rho = _row_window(r)
                bias = tb_ref[2 * g + par, rho % 2, :, (rho - rho % 2) * GRID_W:(rho - rho % 2) * GRID_W + nk]
                sl = _dot_nt(qm[r * GRID_W:(r + 1) * GRID_W, :], kb[rs * GRID_W:rs * GRID_W + nk, :])
                slabs.append(sl + bias)
            s_loc = jnp.concatenate(slabs, axis=0)
            m = jnp.maximum(jnp.max(s_loc, axis=-1, keepdims=True), jnp.max(s_ctx, axis=-1, keepdims=True))
            p_loc = jnp.exp(s_loc - m)
            p_ctx = jnp.exp(s_ctx - m)
            l = jnp.sum(p_loc, axis=-1, keepdims=True) + jnp.sum(p_ctx, axis=-1, keepdims=True)
            p_locb = p_loc.astype(BF16)
            o_rows = []
            for r in range(ROWS):
                rs, _ = _row_window(r)
                o_rows.append(_dot(p_locb[r * GRID_W:(r + 1) * GRID_W, :], vb[rs * GRID_W:rs * GRID_W + nk, :]))
            o = jnp.concatenate(o_rows, axis=0) + _dot(p_ctx.astype(BF16), vcb)
            outs.append(o / l)
        o_s[g] = jnp.where(even, outs[0], outs[1]).astype(BF16)
        return carry

    lax.fori_loop(0, N_PAIRS, pair_body, 0)
    _mixer_back(x_ref, mod_ref, win_ref, wpool_ref, ps_ref, wpa_ref, wpb_ref, wout_ref, x1_ref,
                hb_s, u_s, o_s, pg_s, DEC_SEQ)


def _mixer_scratch(tile, kv_dtype):
    return [pltpu.VMEM((tile, D_MODEL), BF16),
            pltpu.VMEM((tile, POOL_DIM), F32),
            pltpu.VMEM((N_PAIRS, tile, PAIR_W), BF16),
            pltpu.VMEM((N_PAIRS, tile, PAIR_W), kv_dtype),
            pltpu.VMEM((N_PAIRS, tile, PAIR_W), kv_dtype),
            pltpu.VMEM((N_PAIRS, tile, PAIR_W), BF16),
            pltpu.VMEM((tile, POOL_DIM), BF16)]


def _weight_specs():
    return [_const_spec((1, D_MODEL)),
            _const_spec((D_MODEL, IN_DIM)),
            _const_spec((len(POOL_WINDOWS), POOL_GROUP_DIM, POOL_GROUP_DIM)),
            _const_spec((1, POOL_DIM)),
            _const_spec((POOL_DIM, D_MODEL)),
            _const_spec((ATTN_DIM, D_MODEL)),
            _const_spec((D_MODEL, D_MODEL))]


def _ctx_mixer(x, modv, weights):
    n = x.shape[0]
    nseq = TOK_BLOCK // SEQ
    cache = jax.ShapeDtypeStruct((n // SEQ, 1, N_HEADS, SEQ, HEAD_DIM), F32)
    cache_spec = pl.BlockSpec((nseq, 1, N_HEADS, SEQ, HEAD_DIM), lambda i: (i, 0, 0, 0, 0))
    return pl.pallas_call(
        _ctx_mixer_kernel,
        grid=(n // TOK_BLOCK,),
        in_specs=[pl.BlockSpec((TOK_BLOCK, D_MODEL), lambda i: (i, 0)),
                  pl.BlockSpec((1, N_MOD, D_MODEL), lambda i: (0, 0, 0))] + _weight_specs(),
        out_specs=[pl.BlockSpec((TOK_BLOCK, D_MODEL), lambda i: (i, 0)), cache_spec, cache_spec],
        out_shape=[jax.ShapeDtypeStruct((n, D_MODEL), F32), cache, cache],
        scratch_shapes=_mixer_scratch(TOK_BLOCK, F32),
        compiler_params=pltpu.CompilerParams(dimension_semantics=("arbitrary",),
                                             vmem_limit_bytes=V7X_VMEM_LIMIT),
        name="ctx_mixer",
    )(x, modv, *weights)


def _lat_mixer(x, modv, weights, kc, vc, tb):
    n = x.shape[0]
    return pl.pallas_call(
        _lat_mixer_kernel,
        grid=(n // DEC_SEQ,),
        in_specs=[pl.BlockSpec((DEC_SEQ, D_MODEL), lambda i: (i, 0)),
                  pl.BlockSpec((1, N_MOD, D_MODEL), lambda i: (i + 1, 0, 0))] + _weight_specs() + [
                  pl.BlockSpec((1, N_PAIRS, PAST_LEN, PAIR_W), lambda i: (i, 0, 0, 0)),
                  pl.BlockSpec((1, N_PAIRS, PAST_LEN, PAIR_W), lambda i: (i, 0, 0, 0)),
                  _const_spec((N_HEADS, 2, GRID_W, ROWS * GRID_W))],
        out_specs=pl.BlockSpec((DEC_SEQ, D_MODEL), lambda i: (i, 0)),
        out_shape=jax.ShapeDtypeStruct((n, D_MODEL), F32),
        scratch_shapes=_mixer_scratch(DEC_SEQ, BF16),
        compiler_params=pltpu.CompilerParams(dimension_semantics=("arbitrary",),
                                             vmem_limit_bytes=V7X_VMEM_LIMIT),
        name="lat_mixer",
    )(x, modv, *weights, kc, vc, tb)


N_RPB_ROWS = 2 * WIN_R - 1
N_RPB_COLS = 2 * WIN_C - 1
TABLE_W = ROWS * GRID_W


def _bias_kernel(v_ref, keep_ref, o_ref):
    for h in range(N_HEADS):
        for par in range(2):
            x = jnp.broadcast_to(v_ref[h, par:par + 1, :], (GRID_W, TABLE_W))
            shifted = pltpu.roll(x, TABLE_W - (WIN_C - 1), 1, stride=1, stride_axis=0)
            o_ref[h, par] = jnp.where(keep_ref[par] > 0.0, shifted, NEG_INF)


def _bias_tables(rpb):
    col = np.arange(GRID_W)
    cs = np.clip(col - WIN_C // 2, 0, GRID_W - WIN_C)
    in_win = (col[None, :] >= cs[:, None]) & (col[None, :] < cs[:, None] + WIN_C)
    keep = np.tile(in_win.astype(np.float32), (2, 1, ROWS))
    keep[0, :, N_RPB_ROWS * GRID_W:] = 0.0
    keep[1, :, (N_RPB_ROWS - 1) * GRID_W:] = 0.0
    rp = jnp.pad(rpb.astype(F32), ((0, 0), (0, ROWS + 1 - N_RPB_ROWS), (0, GRID_W - N_RPB_COLS)))
    v = jnp.stack([rp[:, :ROWS].reshape(N_HEADS, TABLE_W), rp[:, 1:].reshape(N_HEADS, TABLE_W)], axis=1)
    return pl.pallas_call(
        _bias_kernel,
        out_shape=jax.ShapeDtypeStruct((N_HEADS, 2, GRID_W, TABLE_W), F32),
        name="bias_tables",
    )(v, jnp.asarray(keep))


SORT_ROWS = 512


def _sort_kernel(xc_ref, xl_ref, mod_ref, gffn_ref, wr_ref, br_ref,
                 xs_ref, dk_ref, np_ref, off_ref):
    j = pl.program_id(0)
    x = jnp.where(j < _N_CTX_BLOCKS, xc_ref[...], xl_ref[...])
    shift, scale = mod_ref[0, 3:4, :], mod_ref[0, 4:5, :]
    hb = _norm_mod(x, gffn_ref[...], scale, shift).astype(BF16)
    logits = _dot_nt(wr_ref[...], hb) + br_ref[...]
    eio = lax.broadcasted_iota(jnp.int32, logits.shape, 0)
    work = logits
    sels, vals = [], []
    for _ in range(TOP_K):
        m = jnp.max(work, axis=0, keepdims=True)
        idx = jnp.min(jnp.where(work == m, eio, N_EXPERTS), axis=0, keepdims=True)
        sel = eio == idx
        sels.append(sel)
        vals.append(m)
        work = jnp.where(sel, -jnp.inf, work)
    exps = [jnp.exp(v - vals[0]) for v in vals]
    den = exps[0] + exps[1] + exps[2] + exps[3]
    mask = jnp.zeros(logits.shape, F32)
    for sel in sels:
        mask = mask + jnp.where(sel, 1.0, 0.0)
    t_row = lax.broadcasted_iota(jnp.int32, (TOK_BLOCK, TOK_BLOCK), 0)
    t_col = lax.broadcasted_iota(jnp.int32, (TOK_BLOCK, TOK_BLOCK), 1)
    before = jnp.where(t_row < t_col, 1.0, 0.0).astype(BF16)
    rank = _dot(mask.astype(BF16), before)
    cnt = jnp.sum(mask, axis=1, keepdims=True)
    np16 = jnp.floor((cnt + (CHUNK - 1.0)) * (1.0 / CHUNK))
    e_row = lax.broadcasted_iota(jnp.int32, (N_EXPERTS, N_EXPERTS), 0)
    e_col = lax.broadcasted_iota(jnp.int32, (N_EXPERTS, N_EXPERTS), 1)
    lower = jnp.where(e_col < e_row, 1.0, 0.0).astype(BF16)
    np16_b = jnp.broadcast_to(np16, (N_EXPERTS, 128))
    off16 = _dot(lower, np16_b.astype(BF16))
    np_ref[0] = np16_b
    off_ref[0] = off16
    dest = off16[:, 0:1] * float(CHUNK) + rank
    dests = []
    for k in range(TOP_K):
        dk = jnp.sum(jnp.where(sels[k], dest, 0.0), axis=0, keepdims=True)
        dk_ref[0, k:k + 1, :] = dk
        dk_ref[0, TOP_K + k:TOP_K + k + 1, :] = exps[k] / den
        dests.append(dk.astype(jnp.int32))
    rio = lax.broadcasted_iota(jnp.int32, (SORT_ROWS, TOK_BLOCK), 0)
    for c in range(BLOCK_CAP // SORT_ROWS):
        onehot = jnp.zeros((SORT_ROWS, TOK_BLOCK), F32)
        for dk in dests:
            onehot = jnp.where(rio == dk - c * SORT_ROWS, 1.0, onehot)
        xs_ref[c * SORT_ROWS:(c + 1) * SORT_ROWS, :] = _dot(onehot.astype(BF16), hb).astype(BF16)


def _block_mod_index(j):
    return jnp.where(j < _N_CTX_BLOCKS, 0, 1 + (j - _N_CTX_BLOCKS) // (DEC_SEQ // TOK_BLOCK))


def _token_specs():
    return [pl.BlockSpec((TOK_BLOCK, D_MODEL), lambda j: (jnp.minimum(j, _N_CTX_BLOCKS - 1), 0)),
            pl.BlockSpec((TOK_BLOCK, D_MODEL), lambda j: (jnp.maximum(j - _N_CTX_BLOCKS, 0), 0)),
            pl.BlockSpec((1, N_MOD, D_MODEL), lambda j: (_block_mod_index(j), 0, 0))]


def _route_sort(x1c, x1l, modv, g_ffn, wr_t, br):
    tbl = jax.ShapeDtypeStruct((_N_BLOCKS, N_EXPERTS, 128), F32)
    tbl_spec = pl.BlockSpec((1, N_EXPERTS, 128), lambda j: (j, 0, 0))
    return pl.pallas_call(
        _sort_kernel,
        grid=(_N_BLOCKS,),
        in_specs=_token_specs() + [_const_spec((1, D_MODEL)),
                                   _const_spec((N_EXPERTS, D_MODEL)),
                                   _const_spec((N_EXPERTS, 1))],
        out_specs=[pl.BlockSpec((BLOCK_CAP, D_MODEL), lambda j: (j, 0)),
                   pl.BlockSpec((1, 2 * TOP_K, TOK_BLOCK), lambda j: (j, 0, 0)),
                   tbl_spec, tbl_spec],
        out_shape=[jax.ShapeDtypeStruct((_N_BLOCKS * BLOCK_CAP, D_MODEL), BF16),
                   jax.ShapeDtypeStruct((_N_BLOCKS, 2 * TOP_K, TOK_BLOCK), F32),
                   tbl, tbl],
        compiler_params=pltpu.CompilerParams(dimension_semantics=("arbitrary",),
                                             vmem_limit_bytes=V7X_VMEM_LIMIT),
        name="route_sort",
    )(x1c, x1l, modv, g_ffn, wr_t, br)


EXPERT_BUF_ROWS = 2048
PASS_CHUNKS = EXPERT_BUF_ROWS // CHUNK
N_ROW_BUFS = 3


def _expert_kernel(np_ref, off_ref, wu_ref, bu_ref, wd_ref, bd_ref, xs_hbm, ys_hbm,
                   buf, wu_s, wd_s, gsem, ssem, pend):
    del xs_hbm
    e = pl.program_id(0)
    last = pl.num_programs(0) - 1
    slot = e % N_ROW_BUFS
    nxt = (e + 1) % N_ROW_BUFS

    def move_chunks(ex, q_lo, slot_, gather):
        def block_body(j, q0):
            n = np_ref[j * N_EXPERTS + ex]
            off = off_ref[j * N_EXPERTS + ex]
            c_lo = jnp.clip(q_lo - q0, 0, n)
            m = jnp.clip(q_lo + PASS_CHUNKS - q0, 0, n) - c_lo

            @pl.when(m > 0)
            def _():
                rows = pl.multiple_of(m * CHUNK, CHUNK)
                row0 = pl.multiple_of(j * BLOCK_CAP + (off + c_lo) * CHUNK, CHUNK)
                brow0 = pl.multiple_of((q0 + c_lo - q_lo) * CHUNK, CHUNK)
                hbm = ys_hbm.at[pl.ds(row0, rows), :]
                vm = buf.at[slot_, pl.ds(brow0, rows), :]
                if gather:
                    pltpu.make_async_copy(hbm, vm, gsem.at[slot_]).start()
                else:
                    pltpu.make_async_copy(vm, hbm, ssem.at[slot_]).start()

            return q0 + n

        lax.fori_loop(0, _N_BLOCKS, block_body, 0)

    def wait_chunks(sem, n, slot_):
        @pl.when(n > 0)
        def _():
            rows = pl.multiple_of(n * CHUNK, CHUNK)
            pltpu.make_async_copy(ys_hbm.at[pl.ds(0, rows), :], buf.at[slot_, pl.ds(0, rows), :], sem).wait()

    def mlp_rows(r0, rows):
        x = buf[slot, pl.ds(r0, rows), :]
        gu = _dot(x, wu_s[...]) + bu_ref[0]
        gate = jnp.minimum(gu[:, :D_FF], SWIGLU_LIMIT)
        up = jnp.clip(gu[:, D_FF:], -SWIGLU_LIMIT, SWIGLU_LIMIT)
        glu = gate * _sigmoid(SWIGLU_ALPHA * gate)
        y = _dot(((up + 1.0) * glu).astype(BF16), wd_s[...]) + bd_ref[0]
        buf[slot, pl.ds(r0, rows), :] = y.astype(BF16)

    def compute(n):
        rows = n * CHUNK
        rem = rows % EXPERT_ROW_TILE
        half = EXPERT_ROW_TILE // 2
        n_full = rows // EXPERT_ROW_TILE + jnp.where(rem > half, 1, 0)
        tail = (rem > 0) & (rem <= half)
        merge = tail & (n_full > 0)

        def tile_body(t, carry):
            mlp_rows(pl.multiple_of(t * EXPERT_ROW_TILE, EXPERT_ROW_TILE), EXPERT_ROW_TILE)
            return carry

        lax.fori_loop(0, n_full - jnp.where(merge, 1, 0), tile_body, 0)

        @pl.when(merge)
        def _():
            mlp_rows(pl.multiple_of((n_full - 1) * EXPERT_ROW_TILE, EXPERT_ROW_TILE), EXPERT_ROW_TILE + half)

        @pl.when(tail & jnp.logical_not(merge))
        def _():
            mlp_rows(0, half)

    def chunks_of(ex):
        return lax.fori_loop(0, _N_BLOCKS, lambda j, acc: acc + np_ref[j * N_EXPERTS + ex], 0)

    @pl.when(e == 0)
    def _():
        buf[...] = jnp.zeros_like(buf)
        for b in range(N_ROW_BUFS):
            pend[b] = 0
        move_chunks(0, 0, 0, True)

    wait_chunks(ssem.at[nxt], pend[nxt], nxt)
    pend[nxt] = 0

    @pl.when(e < last)
    def _():
        move_chunks(e + 1, 0, nxt, True)

    wu_s[...] = wu_ref[0].astype(BF16)
    wd_s[...] = wd_ref[0].astype(BF16)

    total = chunks_of(e)
    n0 = jnp.minimum(total, PASS_CHUNKS)
    wait_chunks(gsem.at[slot], n0, slot)
    compute(n0)
    move_chunks(e, 0, slot, False)
    pend[slot] = n0

    def pass_body(p, carry):
        wait_chunks(ssem.at[slot], pend[slot], slot)
        lo = p * PASS_CHUNKS
        n = jnp.minimum(total - lo, PASS_CHUNKS)
        move_chunks(e, lo, slot, True)
        wait_chunks(gsem.at[slot], n, slot)
        compute(n)
        move_chunks(e, lo, slot, False)
        pend[slot] = n
        return carry

    lax.fori_loop(1, (total + PASS_CHUNKS - 1) // PASS_CHUNKS, pass_body, 0)

    @pl.when(e == last)
    def _():
        for b in range(N_ROW_BUFS):
            wait_chunks(ssem.at[b], pend[b], b)
            pend[b] = 0


def _experts(np16, off16, w_up, b_up, w_down, b_down, xs):
    grid_spec = pltpu.PrefetchScalarGridSpec(
        num_scalar_prefetch=2,
        grid=(N_EXPERTS,),
        in_specs=[pl.BlockSpec((1, D_MODEL, 2 * D_FF), lambda e, *_: (e, 0, 0)),
                  pl.BlockSpec((1, 1, 2 * D_FF), lambda e, *_: (e, 0, 0)),
                  pl.BlockSpec((1, D_FF, D_MODEL), lambda e, *_: (e, 0, 0)),
                  pl.BlockSpec((1, 1, D_MODEL), lambda e, *_: (e, 0, 0)),
                  pl.BlockSpec(memory_space=pl.ANY)],
        out_specs=pl.BlockSpec(memory_space=pl.ANY),
        scratch_shapes=[pltpu.VMEM((N_ROW_BUFS, EXPERT_BUF_ROWS, D_MODEL), BF16),
                        pltpu.VMEM((D_MODEL, 2 * D_FF), BF16),
                        pltpu.VMEM((D_FF, D_MODEL), BF16),
                        pltpu.SemaphoreType.DMA((N_ROW_BUFS,)),
                        pltpu.SemaphoreType.DMA((N_ROW_BUFS,)),
                        pltpu.SMEM((N_ROW_BUFS,), jnp.int32)],
    )
    return pl.pallas_call(
        _expert_kernel,
        grid_spec=grid_spec,
        out_shape=jax.ShapeDtypeStruct(xs.shape, xs.dtype),
        input_output_aliases={6: 0},
        compiler_params=pltpu.CompilerParams(dimension_semantics=("arbitrary",),
                                             vmem_limit_bytes=V7X_VMEM_LIMIT),
        name="experts",
    )(np16, off16, w_up, b_up.reshape(N_EXPERTS, 1, 2 * D_FF), w_down,
      b_down.reshape(N_EXPERTS, 1, D_MODEL), xs)


def _combine_kernel(ys_ref, dk_ref, xc_ref, xl_ref, mod_ref, gfin_ref, yc_ref, yl_ref, y_s):
    j = pl.program_id(0)
    half = TOK_BLOCK // 2
    cio = lax.broadcasted_iota(jnp.int32, (half, SORT_ROWS), 1).astype(F32)
    halves = [slice(h * half, (h + 1) * half) for h in range(2)]
    accs = [jnp.zeros((half, D_MODEL), F32) for _ in halves]
    for c in range(BLOCK_CAP // SORT_ROWS):
        ys = ys_ref[c * SORT_ROWS:(c + 1) * SORT_ROWS, :]
        for h, rows in enumerate(halves):
            d = dk_ref[0, rows, :]
            w = jnp.zeros((half, SORT_ROWS), F32)
            for k in range(TOP_K):
                w = jnp.where(cio == d[:, k:k + 1] - float(c * SORT_ROWS), d[:, TOP_K + k:TOP_K + k + 1], w)
            accs[h] = accs[h] + _dot(w.astype(BF16), ys)
    for h, rows in enumerate(halves):
        acc = accs[h]
        x1 = jnp.where(j < _N_CTX_BLOCKS, xc_ref[rows, :], xl_ref[rows, :])
        x2 = x1 + mod_ref[0, 5:6, :] * acc
        ms = jnp.mean(x2 * x2, axis=-1, keepdims=True)
        y_s[rows, :] = x2 * lax.rsqrt(ms + RMS_EPS) * gfin_ref[...]

    @pl.when(j < _N_CTX_BLOCKS)
    def _():
        yc_ref[...] = y_s[...]

    @pl.when(j >= _N_CTX_BLOCKS)
    def _():
        yl_ref[...] = y_s[...]


def _combine(ys, dk_t, x1c, x1l, modv, g_final):
    return pl.pallas_call(
        _combine_kernel,
        grid=(_N_BLOCKS,),
        in_specs=[pl.BlockSpec((BLOCK_CAP, D_MODEL), lambda j: (j, 0)),
                  pl.BlockSpec((1, TOK_BLOCK, 2 * TOP_K), lambda j: (j, 0, 0))] + _token_specs() + [
                  _const_spec((1, D_MODEL))],
        out_specs=[pl.BlockSpec((TOK_BLOCK, D_MODEL), lambda j: (jnp.minimum(j, _N_CTX_BLOCKS - 1), 0)),
                   pl.BlockSpec((TOK_BLOCK, D_MODEL), lambda j: (jnp.maximum(j - _N_CTX_BLOCKS, 0), 0))],
        out_shape=[jax.ShapeDtypeStruct(x1c.shape, F32), jax.ShapeDtypeStruct(x1l.shape, F32)],
        scratch_shapes=[pltpu.VMEM((TOK_BLOCK, D_MODEL), F32)],
        compiler_params=pltpu.CompilerParams(dimension_semantics=("arbitrary",),
                                             vmem_limit_bytes=V7X_VMEM_LIMIT),
        name="combine",
    )(ys, dk_t, x1c, x1l, modv, g_final)


def kernel(x_prompt, x_sample, cache_k, cache_v, c, c_ctx, w_ada, b_ada, g_mix, w_in, w_pool, pool_scale,
           w_pa, w_pb, rpb, w_out, g_ffn, w_router, b_router, w_up, b_up, w_down, b_down, g_final):
    assert w_ada.shape[0] == 1, "single trunk layer"
    batch, seq, d = x_prompt.shape
    dec_batch, dec_seq, _ = x_sample.shape
    assert (seq, dec_seq, d) == (SEQ, DEC_SEQ, D_MODEL)
    assert batch * seq == _N_CTX_BLOCKS * TOK_BLOCK and dec_batch * dec_seq == _N_LAT_BLOCKS * TOK_BLOCK

    cmat = jnp.concatenate([c_ctx[None, :], c, jnp.zeros((8 - 1 - dec_batch, d), F32)], axis=0)
    modv = _modulation(cmat, w_ada[0], b_ada[0]).reshape(8, N_MOD, d)

    weights = (g_mix[0][None, :], w_in[0].astype(BF16), w_pool[0].astype(BF16), pool_scale[0][None, :],
               w_pa[0].astype(BF16), w_pb[0].astype(BF16), w_out[0].astype(BF16))

    def by_pair(cache):
        z = cache[:, 0].reshape(dec_batch, N_PAIRS, 2, PAST_LEN, HEAD_DIM)
        return z.transpose(0, 1, 3, 2, 4).reshape(dec_batch, N_PAIRS, PAST_LEN, PAIR_W)

    x1c, new_k, new_v = _ctx_mixer(x_prompt.reshape(batch * seq, d), modv, weights)
    x1l = _lat_mixer(x_sample.reshape(dec_batch * dec_seq, d), modv, weights,
                     by_pair(cache_k), by_pair(cache_v), _bias_tables(rpb[0]))

    xs, dk, np16, off16 = _route_sort(x1c, x1l, modv, g_ffn[0][None, :],
                                      w_router[0].T.astype(BF16), b_router[0][:, None])
    np16_i = np16[:, :, 0].astype(jnp.int32).reshape(-1)
    off16_i = off16[:, :, 0].astype(jnp.int32).reshape(-1)
    ys = _experts(np16_i, off16_i, w_up[0], b_up[0], w_down[0], b_down[0], xs)
    yc, yl = _combine(ys, dk.transpose(0, 2, 1), x1c, x1l, modv, g_final[None, :])
    return (yc.reshape(batch, seq, d), yl.reshape(dec_batch, dec_seq, d), new_k, new_v)
```

```python
import functools

import jax
import jax.numpy as jnp
import numpy as np
from jax import lax
from jax.experimental import pallas as pl
from jax.experimental.pallas import tpu as pltpu

F32 = jnp.float32
BF16 = jnp.bfloat16

D_MODEL = 1024
SEQ = 256
DEC_SEQ = 1024
GRID_W = 64
ROWS = DEC_SEQ // GRID_W
N_HEADS = 8
HEAD_DIM = 64
N_PAIRS = N_HEADS // 2
PAIR_W = 2 * HEAD_DIM
PAST_LEN = 512
POOL_DIM = 512
POOL_WINDOWS = (2, 4, 8, 16)
POOL_GROUP_DIM = 128
ATTN_DIM = 512
WIN_R = 8
WIN_C = 16
N_EXPERTS = 32
TOP_K = 4
D_FF = 1024
SWIGLU_LIMIT = 7.0
SWIGLU_ALPHA = 1.702
N_MOD = 6
RMS_EPS = 1e-6
NEG_INF = -1e30
ATTN_SCALE = HEAD_DIM ** -0.5
IN_DIM = POOL_DIM + 3 * ATTN_DIM + 2 * D_MODEL

TOK_BLOCK = 512
CHUNK = 16
BLOCK_CAP = TOK_BLOCK * TOP_K + N_EXPERTS * CHUNK
ROW_STEP = 128
BIG_TILE = 512
LAST_TILE_MAX = 1024
V7X_VMEM_LIMIT = 60 * 1024 * 1024

_N_CTX_BLOCKS = 8
_N_LAT_BLOCKS = 4
_N_BLOCKS = _N_CTX_BLOCKS + _N_LAT_BLOCKS

_NT = (((1,), (1,)), ((), ()))


def _dot(a, b):
    return jnp.dot(a, b, preferred_element_type=F32)


def _dot_nt(a, b):
    return lax.dot_general(a, b, _NT, preferred_element_type=F32)


def _sigmoid(x):
    return 1.0 / (1.0 + jnp.exp(-x))


def _norm_mod(x, gain, scale, shift):
    ms = jnp.mean(x * x, axis=-1, keepdims=True)
    return (x * lax.rsqrt(ms + RMS_EPS) * gain) * (1.0 + scale) + shift


def _const_spec(shape):
    zeros = (0,) * len(shape)
    return pl.BlockSpec(shape, lambda *_: zeros, pipeline_mode=pl.Buffered(1))


MOD_COLS = 1536


def _mod_kernel(c_ref, w_ref, b_ref, o_ref):
    c = c_ref[...]
    s = c * _sigmoid(c)
    s_hi = s.astype(BF16)
    s_lo = (s - s_hi.astype(F32)).astype(BF16)
    r = _dot(jnp.concatenate([s_hi, s_lo], axis=0), w_ref[...].astype(BF16))
    o_ref[...] = r[:8] + r[8:] + b_ref[...]


def _modulation(cmat, w_ada, b_ada):
    n = w_ada.shape[1]
    return pl.pallas_call(
        _mod_kernel,
        grid=(n // MOD_COLS,),
        in_specs=[pl.BlockSpec((8, D_MODEL), lambda i: (0, 0)),
                  pl.BlockSpec((D_MODEL, MOD_COLS), lambda i: (0, i)),
                  pl.BlockSpec((1, MOD_COLS), lambda i: (0, i))],
        out_specs=pl.BlockSpec((8, MOD_COLS), lambda i: (0, i)),
        out_shape=jax.ShapeDtypeStruct((8, n), F32),
        name="modulation",
    )(cmat, w_ada, b_ada.reshape(1, n))


def _pool_mix(u, pos, seq):
    n = u.shape[0]

    def down(x, d):
        return jnp.where(pos >= d, pltpu.roll(x, d, 0), 0.0)

    def up(x, d):
        return jnp.where(pos < seq - d, pltpu.roll(x, n - d, 0), 0.0)

    return down, up


def _pool_group(u, pos, seq, w):
    down, up = _pool_mix(u, pos, seq)
    hw = w // 2
    back = u
    fwd = u
    d = 1
    while d < hw:
        back = back + down(back, d)
        fwd = fwd + up(fwd, d)
        d *= 2
    s = down(back, 1) + fwd
    posf = pos.astype(F32)
    cnt = jnp.minimum(posf + hw, float(seq)) - jnp.maximum(posf - hw, 0.0)
    return s / cnt - u


def _mixer_front(x_ref, mod_ref, gmix_ref, win_ref, hb_s, u_s, q_s, k_s, v_s):
    x = x_ref[...]
    shift, scale = mod_ref[0, 0:1, :], mod_ref[0, 1:2, :]
    hb = _norm_mod(x, gmix_ref[...], scale, shift).astype(BF16)
    hb_s[...] = hb
    u_s[...] = _dot(hb, win_ref[:, 0:POOL_DIM])
    for dst, base in ((q_s, POOL_DIM), (k_s, POOL_DIM + ATTN_DIM), (v_s, POOL_DIM + 2 * ATTN_DIM)):
        z = _dot(hb, win_ref[:, base:base + ATTN_DIM])
        if dst is q_s:
            z = z * ATTN_SCALE
        for g in range(N_PAIRS):
            dst[g] = z[:, g * PAIR_W:(g + 1) * PAIR_W].astype(dst.dtype)


def _mixer_back(x_ref, mod_ref, win_ref, wpool_ref, ps_ref, wpa_ref, wpb_ref, wout_ref, x1_ref,
                hb_s, u_s, o_s, pg_s, seq):
    tile = x_ref.shape[0]
    gate = mod_ref[0, 2:3, :]
    pos = lax.broadcasted_iota(jnp.int32, (tile, 1), 0) % seq
    for g, w in enumerate(POOL_WINDOWS):
        cols = slice(g * POOL_GROUP_DIM, (g + 1) * POOL_GROUP_DIM)
        pg_s[:, cols] = _pool_group(u_s[:, cols], pos, seq, w).astype(BF16)
    for c in range(tile // TOK_BLOCK):
        rows = slice(c * TOK_BLOCK, (c + 1) * TOK_BLOCK)
        ys = []
        for g in range(len(POOL_WINDOWS)):
            cols = slice(g * POOL_GROUP_DIM, (g + 1) * POOL_GROUP_DIM)
            ys.append((_dot(pg_s[rows, cols], wpool_ref[g]) * ps_ref[:, cols]).astype(BF16))
        a = _dot(jnp.concatenate(ys, axis=1), wpa_ref[...])
        ob = _dot(jnp.concatenate([o_s[g, rows, :] for g in range(N_PAIRS)], axis=1), wpb_ref[...])
        gab = _dot(hb_s[rows, :], win_ref[:, POOL_DIM + 3 * ATTN_DIM:IN_DIM])
        merged = _sigmoid(gab[:, :D_MODEL]) * a + _sigmoid(gab[:, D_MODEL:]) * ob
        mix = _dot(merged.astype(BF16), wout_ref[...])
        x1_ref[rows, :] = x_ref[rows, :] + gate * mix


def _ctx_mixer_kernel(x_ref, mod_ref, gmix_ref, win_ref, wpool_ref, ps_ref, wpa_ref, wpb_ref, wout_ref,
                      x1_ref, ko_ref, vo_ref, hb_s, u_s, q_s, k_s, v_s, o_s, pg_s):
    _mixer_front(x_ref, mod_ref, gmix_ref, win_ref, hb_s, u_s, q_s, k_s, v_s)
    even = lax.broadcasted_iota(jnp.int32, (1, PAIR_W), 1) < HEAD_DIM
    tile = x_ref.shape[0]
    for s in range(tile // SEQ):
        rows = slice(s * SEQ, (s + 1) * SEQ)
        for g in range(N_PAIRS):
            q2, k2, v2 = q_s[g, rows, :], k_s[g, rows, :], v_s[g, rows, :]
            ko_ref[s, 0, 2 * g] = k2[:, :HEAD_DIM]
            ko_ref[s, 0, 2 * g + 1] = k2[:, HEAD_DIM:]
            vo_ref[s, 0, 2 * g] = v2[:, :HEAD_DIM]
            vo_ref[s, 0, 2 * g + 1] = v2[:, HEAD_DIM:]
            kb, vb = k2.astype(BF16), v2.astype(BF16)
            outs = []
            for par in range(2):
                qm = jnp.where(even if par == 0 else jnp.logical_not(even), q2, 0.0).astype(BF16)
                sc = _dot_nt(qm, kb)
                m = jnp.max(sc, axis=-1, keepdims=True)
                p = jnp.exp(sc - m)
                l = jnp.sum(p, axis=-1, keepdims=True)
                outs.append(_dot(p.astype(BF16), vb) / l)
            o_s[g, rows, :] = jnp.where(even, outs[0], outs[1]).astype(BF16)
    _mixer_back(x_ref, mod_ref, win_ref, wpool_ref, ps_ref, wpa_ref, wpb_ref, wout_ref, x1_ref,
                hb_s, u_s, o_s, pg_s, SEQ)


def _row_window(r):
    rs = min(max(r - WIN_R // 2, 0), ROWS - WIN_R)
    return rs, rs - r + WIN_R - 1


def _lat_mixer_kernel(x_ref, mod_ref, gmix_ref, win_ref, wpool_ref, ps_ref, wpa_ref, wpb_ref, wout_ref,
                      kc_ref, vc_ref, tb_ref, x1_ref, hb_s, u_s, q_s, k_s, v_s, o_s, pg_s):
    _mixer_front(x_ref, mod_ref, gmix_ref, win_ref, hb_s, u_s, q_s, k_s, v_s)
    even = lax.broadcasted_iota(jnp.int32, (1, PAIR_W), 1) < HEAD_DIM
    nk = WIN_R * GRID_W

    def pair_body(g, carry):
        q2 = q_s[g]
        kb, vb = k_s[g].astype(BF16), v_s[g].astype(BF16)
        kcb, vcb = kc_ref[0, g].astype(BF16), vc_ref[0, g].astype(BF16)
        outs = []
        for par in range(2):
            qm = jnp.where(even if par == 0 else jnp.logical_not(even), q2, 0.0).astype(BF16)
            s_ctx = _dot_nt(qm, kcb)
            slabs = []
            for r in range(ROWS):
                rs, rho = _row_window(r)
                bias = tb_ref[2 * g + par, rho % 2, :, (rho - rho % 2) * GRID_W:(rho - rho % 2) * GRID_W + nk]
                sl = _dot_nt(qm[r * GRID_W:(r + 1) * GRID_W, :], kb[rs * GRID_W:rs * GRID_W + nk, :])
                slabs.append(sl + bias)
            s_loc = jnp.concatenate(slabs, axis=0)
            m = jnp.maximum(jnp.max(s_loc, axis=-1, keepdims=True), jnp.max(s_ctx, axis=-1, keepdims=True))
            p_loc = jnp.exp(s_loc - m)
            p_ctx = jnp.exp(s_ctx - m)
            l = jnp.sum(p_loc, axis=-1, keepdims=True) + jnp.sum(p_ctx, axis=-1, keepdims=True)
            p_locb = p_loc.astype(BF16)
            o_rows = []
            for r in range(ROWS):
                rs, _ = _row_window(r)
                o_rows.append(_dot(p_locb[r * GRID_W:(r + 1) * GRID_W, :], vb[rs * GRID_W:rs * GRID_W + nk, :]))
            o = jnp.concatenate(o_rows, axis=0) + _dot(p_ctx.astype(BF16), vcb)
            outs.append(o / l)
        o_s[g] = jnp.where(even, outs[0], outs[1]).astype(BF16)
        return carry

    lax.fori_loop(0, N_PAIRS, pair_body, 0)
    _mixer_back(x_ref, mod_ref, win_ref, wpool_ref, ps_ref, wpa_ref, wpb_ref, wout_ref, x1_ref,
                hb_s, u_s, o_s, pg_s, DEC_SEQ)


def _mixer_scratch(tile, kv_dtype):
    return [pltpu.VMEM((tile, D_MODEL), BF16),
            pltpu.VMEM((tile, POOL_DIM), F32),
            pltpu.VMEM((N_PAIRS, tile, PAIR_W), BF16),
            pltpu.VMEM((N_PAIRS, tile, PAIR_W), kv_dtype),
            pltpu.VMEM((N_PAIRS, tile, PAIR_W), kv_dtype),
            pltpu.VMEM((N_PAIRS, tile, PAIR_W), BF16),
            pltpu.VMEM((tile, POOL_DIM), BF16)]


def _weight_specs():
    return [_const_spec((1, D_MODEL)),
            _const_spec((D_MODEL, IN_DIM)),
            _const_spec((len(POOL_WINDOWS), POOL_GROUP_DIM, POOL_GROUP_DIM)),
            _const_spec((1, POOL_DIM)),
            _const_spec((POOL_DIM, D_MODEL)),
            _const_spec((ATTN_DIM, D_MODEL)),
            _const_spec((D_MODEL, D_MODEL))]


def _ctx_mixer(x, modv, weights):
    n = x.shape[0]
    nseq = TOK_BLOCK // SEQ
    cache = jax.ShapeDtypeStruct((n // SEQ, 1, N_HEADS, SEQ, HEAD_DIM), F32)
    cache_spec = pl.BlockSpec((nseq, 1, N_HEADS, SEQ, HEAD_DIM), lambda i: (i, 0, 0, 0, 0))
    return pl.pallas_call(
        _ctx_mixer_kernel,
        grid=(n // TOK_BLOCK,),
        in_specs=[pl.BlockSpec((TOK_BLOCK, D_MODEL), lambda i: (i, 0)),
                  pl.BlockSpec((1, N_MOD, D_MODEL), lambda i: (0, 0, 0))] + _weight_specs(),
        out_specs=[pl.BlockSpec((TOK_BLOCK, D_MODEL), lambda i: (i, 0)), cache_spec, cache_spec],
        out_shape=[jax.ShapeDtypeStruct((n, D_MODEL), F32), cache, cache],
        scratch_shapes=_mixer_scratch(TOK_BLOCK, F32),
        compiler_params=pltpu.CompilerParams(dimension_semantics=("arbitrary",),
                                             vmem_limit_bytes=V7X_VMEM_LIMIT),
        name="ctx_mixer",
    )(x, modv, *weights)


def _lat_mixer(x, modv, weights, kc, vc, tb):
    n = x.shape[0]
    return pl.pallas_call(
        _lat_mixer_kernel,
        grid=(n // DEC_SEQ,),
        in_specs=[pl.BlockSpec((DEC_SEQ, D_MODEL), lambda i: (i, 0)),
                  pl.BlockSpec((1, N_MOD, D_MODEL), lambda i: (i + 1, 0, 0))] + _weight_specs() + [
                  pl.BlockSpec((1, N_PAIRS, PAST_LEN, PAIR_W), lambda i: (i, 0, 0, 0)),
                  pl.BlockSpec((1, N_PAIRS, PAST_LEN, PAIR_W), lambda i: (i, 0, 0, 0)),
                  _const_spec((N_HEADS, 2, GRID_W, ROWS * GRID_W))],
        out_specs=pl.BlockSpec((DEC_SEQ, D_MODEL), lambda i: (i, 0)),
        out_shape=jax.ShapeDtypeStruct((n, D_MODEL), F32),
        scratch_shapes=_mixer_scratch(DEC_SEQ, BF16),
        compiler_params=pltpu.CompilerParams(dimension_semantics=("arbitrary",),
                                             vmem_limit_bytes=V7X_VMEM_LIMIT),
        name="lat_mixer",
    )(x, modv, *weights, kc, vc, tb)


N_RPB_ROWS = 2 * WIN_R - 1
N_RPB_COLS = 2 * WIN_C - 1
TABLE_W = ROWS * GRID_W


def _bias_kernel(v_ref, keep_ref, o_ref):
    for h in range(N_HEADS):
        for par in range(2):
            x = jnp.broadcast_to(v_ref[h, par:par + 1, :], (GRID_W, TABLE_W))
            shifted = pltpu.roll(x, TABLE_W - (WIN_C - 1), 1, stride=1, stride_axis=0)
            o_ref[h, par] = jnp.where(keep_ref[par] > 0.0, shifted, NEG_INF)


def _bias_tables(rpb):
    col = np.arange(GRID_W)
    cs = np.clip(col - WIN_C // 2, 0, GRID_W - WIN_C)
    in_win = (col[None, :] >= cs[:, None]) & (col[None, :] < cs[:, None] + WIN_C)
    keep = np.tile(in_win.astype(np.float32), (2, 1, ROWS))
    keep[0, :, N_RPB_ROWS * GRID_W:] = 0.0
    keep[1, :, (N_RPB_ROWS - 1) * GRID_W:] = 0.0
    rp = jnp.pad(rpb.astype(F32), ((0, 0), (0, ROWS + 1 - N_RPB_ROWS), (0, GRID_W - N_RPB_COLS)))
    v = jnp.stack([rp[:, :ROWS].reshape(N_HEADS, TABLE_W), rp[:, 1:].reshape(N_HEADS, TABLE_W)], axis=1)
    return pl.pallas_call(
        _bias_kernel,
        out_shape=jax.ShapeDtypeStruct((N_HEADS, 2, GRID_W, TABLE_W), F32),
        name="bias_tables",
    )(v, jnp.asarray(keep))


SORT_ROWS = 512


def _sort_kernel(xc_ref, xl_ref, mod_ref, gffn_ref, wr_ref, br_ref,
                 xs_ref, dk_ref, np_ref, off_ref):
    j = pl.program_id(0)
    x = jnp.where(j < _N_CTX_BLOCKS, xc_ref[...], xl_ref[...])
    shift, scale = mod_ref[0, 3:4, :], mod_ref[0, 4:5, :]
    hb = _norm_mod(x, gffn_ref[...], scale, shift).astype(BF16)
    logits = _dot_nt(wr_ref[...], hb) + br_ref[...]
    eio = lax.broadcasted_iota(jnp.int32, logits.shape, 0)
    work = logits
    sels, vals = [], []
    for _ in range(TOP_K):
        m = jnp.max(work, axis=0, keepdims=True)
        idx = jnp.min(jnp.where(work == m, eio, N_EXPERTS), axis=0, keepdims=True)
        sel = eio == idx
        sels.append(sel)
        vals.append(m)
        work = jnp.where(sel, -jnp.inf, work)
    exps = [jnp.exp(v - vals[0]) for v in vals]
    den = exps[0] + exps[1] + exps[2] + exps[3]
    mask = jnp.zeros(logits.shape, F32)
    for sel in sels:
        mask = mask + jnp.where(sel, 1.0, 0.0)
    t_row = lax.broadcasted_iota(jnp.int32, (TOK_BLOCK, TOK_BLOCK), 0)
    t_col = lax.broadcasted_iota(jnp.int32, (TOK_BLOCK, TOK_BLOCK), 1)
    before = jnp.where(t_row < t_col, 1.0, 0.0).astype(BF16)
    rank = _dot(mask.astype(BF16), before)
    cnt = jnp.sum(mask, axis=1, keepdims=True)
    np16 = jnp.floor((cnt + (CHUNK - 1.0)) * (1.0 / CHUNK))
    e_row = lax.broadcasted_iota(jnp.int32, (N_EXPERTS, N_EXPERTS), 0)
    e_col = lax.broadcasted_iota(jnp.int32, (N_EXPERTS, N_EXPERTS), 1)
    lower = jnp.where(e_col < e_row, 1.0, 0.0).astype(BF16)
    np16_b = jnp.broadcast_to(np16, (N_EXPERTS, 128))
    off16 = _dot(lower, np16_b.astype(BF16))
    np_ref[0] = np16_b
    off_ref[0] = off16
    dest = off16[:, 0:1] * float(CHUNK) + rank
    dests = []
    for k in range(TOP_K):
        dk = jnp.sum(jnp.where(sels[k], dest, 0.0), axis=0, keepdims=True)
        dk_ref[0, k:k + 1, :] = dk
        dk_ref[0, TOP_K + k:TOP_K + k + 1, :] = exps[k] / den
        dests.append(dk.astype(jnp.int32))
    rio = lax.broadcasted_iota(jnp.int32, (SORT_ROWS, TOK_BLOCK), 0)
    for c in range(BLOCK_CAP // SORT_ROWS):
        onehot = jnp.zeros((SORT_ROWS, TOK_BLOCK), F32)
        for dk in dests:
            onehot = jnp.where(rio == dk - c * SORT_ROWS, 1.0, onehot)
        xs_ref[c * SORT_ROWS:(c + 1) * SORT_ROWS, :] = _dot(onehot.astype(BF16), hb).astype(BF16)


def _block_mod_index(j):
    return jnp.where(j < _N_CTX_BLOCKS, 0, 1 + (j - _N_CTX_BLOCKS) // (DEC_SEQ // TOK_BLOCK))


def _token_specs():
    return [pl.BlockSpec((TOK_BLOCK, D_MODEL), lambda j: (jnp.minimum(j, _N_CTX_BLOCKS - 1), 0)),
            pl.BlockSpec((TOK_BLOCK, D_MODEL), lambda j: (jnp.maximum(j - _N_CTX_BLOCKS, 0), 0)),
            pl.BlockSpec((1, N_MOD, D_MODEL), lambda j: (_block_mod_index(j), 0, 0))]


def _route_sort(x1c, x1l, modv, g_ffn, wr_t, br):
    tbl = jax.ShapeDtypeStruct((_N_BLOCKS, N_EXPERTS, 128), F32)
    tbl_spec = pl.BlockSpec((1, N_EXPERTS, 128), lambda j: (j, 0, 0))
    return pl.pallas_call(
        _sort_kernel,
        grid=(_N_BLOCKS,),
        in_specs=_token_specs() + [_const_spec((1, D_MODEL)),
                                   _const_spec((N_EXPERTS, D_MODEL)),
                                   _const_spec((N_EXPERTS, 1))],
        out_specs=[pl.BlockSpec((BLOCK_CAP, D_MODEL), lambda j: (j, 0)),
                   pl.BlockSpec((1, 2 * TOP_K, TOK_BLOCK), lambda j: (j, 0, 0)),
                   tbl_spec, tbl_spec],
        out_shape=[jax.ShapeDtypeStruct((_N_BLOCKS * BLOCK_CAP, D_MODEL), BF16),
                   jax.ShapeDtypeStruct((_N_BLOCKS, 2 * TOP_K, TOK_BLOCK), F32),
                   tbl, tbl],
        compiler_params=pltpu.CompilerParams(dimension_semantics=("arbitrary",),
                                             vmem_limit_bytes=V7X_VMEM_LIMIT),
        name="route_sort",
    )(x1c, x1l, modv, g_ffn, wr_t, br)


EXPERT_BUF_ROWS = 2048
PASS_CHUNKS = EXPERT_BUF_ROWS // CHUNK
N_ROW_BUFS = 3


def _expert_kernel(np_ref, off_ref, wu_ref, bu_ref, wd_ref, bd_ref, xs_hbm, ys_hbm,
                   buf, wu_s, wd_s, gsem, ssem, pend):
    del xs_hbm
    e = pl.program_id(0)
    last = pl.num_programs(0) - 1
    slot = e % N_ROW_BUFS
    nxt = (e + 1) % N_ROW_BUFS

    def move_chunks(ex, q_lo, slot_, gather):
        def block_body(j, q0):
            n = np_ref[j * N_EXPERTS + ex]
            off = off_ref[j * N_EXPERTS + ex]
            c_lo = jnp.clip(q_lo - q0, 0, n)
            m = jnp.clip(q_lo + PASS_CHUNKS - q0, 0, n) - c_lo

            @pl.when(m > 0)
            def _():
                rows = pl.multiple_of(m * CHUNK, CHUNK)
                row0 = pl.multiple_of(j * BLOCK_CAP + (off + c_lo) * CHUNK, CHUNK)
                brow0 = pl.multiple_of((q0 + c_lo - q_lo) * CHUNK, CHUNK)
                hbm = ys_hbm.at[pl.ds(row0, rows), :]
                vm = buf.at[slot_, pl.ds(brow0, rows), :]
                if gather:
                    pltpu.make_async_copy(hbm, vm, gsem.at[slot_]).start()
                else:
                    pltpu.make_async_copy(vm, hbm, ssem.at[slot_]).start()

            return q0 + n

        lax.fori_loop(0, _N_BLOCKS, block_body, 0)

    def wait_chunks(sem, n, slot_):
        @pl.when(n > 0)
        def _():
            rows = pl.multiple_of(n * CHUNK, CHUNK)
            pltpu.make_async_copy(ys_hbm.at[pl.ds(0, rows), :], buf.at[slot_, pl.ds(0, rows), :], sem).wait()

    def mlp_rows(r0, rows):
        x = buf[slot, pl.ds(r0, rows), :]
        gu = _dot(x, wu_s[...]) + bu_ref[0]
        gate = jnp.minimum(gu[:, :D_FF], SWIGLU_LIMIT)
        up = jnp.clip(gu[:, D_FF:], -SWIGLU_LIMIT, SWIGLU_LIMIT)
        glu = gate * _sigmoid(SWIGLU_ALPHA * gate)
        y = _dot(((up + 1.0) * glu).astype(BF16), wd_s[...]) + bd_ref[0]
        buf[slot, pl.ds(r0, rows), :] = y.astype(BF16)

    def compute(n):
        rows = (n * CHUNK + ROW_STEP - 1) // ROW_STEP * ROW_STEP
        n_big = (jnp.maximum(rows - LAST_TILE_MAX, 0) + BIG_TILE - 1) // BIG_TILE

        def tile_body(t, carry):
            mlp_rows(pl.multiple_of(t * BIG_TILE, BIG_TILE), BIG_TILE)
            return carry

        lax.fori_loop(0, n_big, tile_body, 0)
        r0 = pl.multiple_of(n_big * BIG_TILE, BIG_TILE)
        for size in range(ROW_STEP, LAST_TILE_MAX + ROW_STEP, ROW_STEP):
            pl.when(rows - r0 == size)(functools.partial(mlp_rows, r0, size))

    def chunks_of(ex):
        return lax.fori_loop(0, _N_BLOCKS, lambda j, acc: acc + np_ref[j * N_EXPERTS + ex], 0)

    @pl.when(e == 0)
    def _():
        buf[...] = jnp.zeros_like(buf)
        for b in range(N_ROW_BUFS):
            pend[b] = 0
        move_chunks(0, 0, 0, True)

    wait_chunks(ssem.at[nxt], pend[nxt], nxt)
    pend[nxt] = 0

    @pl.when(e < last)
    def _():
        move_chunks(e + 1, 0, nxt, True)

    wu_s[...] = wu_ref[0].astype(BF16)
    wd_s[...] = wd_ref[0].astype(BF16)

    total = chunks_of(e)
    n0 = jnp.minimum(total, PASS_CHUNKS)
    wait_chunks(gsem.at[slot], n0, slot)
    compute(n0)
    move_chunks(e, 0, slot, False)
    pend[slot] = n0

    def pass_body(p, carry):
        wait_chunks(ssem.at[slot], pend[slot], slot)
        lo = p * PASS_CHUNKS
        n = jnp.minimum(total - lo, PASS_CHUNKS)
        move_chunks(e, lo, slot, True)
        wait_chunks(gsem.at[slot], n, slot)
        compute(n)
        move_chunks(e, lo, slot, False)
        pend[slot] = n
        return carry

    lax.fori_loop(1, (total + PASS_CHUNKS - 1) // PASS_CHUNKS, pass_body, 0)

    @pl.when(e == last)
    def _():
        for b in range(N_ROW_BUFS):
            wait_chunks(ssem.at[b], pend[b], b)
            pend[b] = 0


def _experts(np16, off16, w_up, b_up, w_down, b_down, xs):
    grid_spec = pltpu.PrefetchScalarGridSpec(
        num_scalar_prefetch=2,
        grid=(N_EXPERTS,),
        in_specs=[pl.BlockSpec((1, D_MODEL, 2 * D_FF), lambda e, *_: (e, 0, 0)),
                  pl.BlockSpec((1, 1, 2 * D_FF), lambda e, *_: (e, 0, 0)),
                  pl.BlockSpec((1, D_FF, D_MODEL), lambda e, *_: (e, 0, 0)),
                  pl.BlockSpec((1, 1, D_MODEL), lambda e, *_: (e, 0, 0)),
                  pl.BlockSpec(memory_space=pl.ANY)],
        out_specs=pl.BlockSpec(memory_space=pl.ANY),
        scratch_shapes=[pltpu.VMEM((N_ROW_BUFS, EXPERT_BUF_ROWS, D_MODEL), BF16),
                        pltpu.VMEM((D_MODEL, 2 * D_FF), BF16),
                        pltpu.VMEM((D_FF, D_MODEL), BF16),
                        pltpu.SemaphoreType.DMA((N_ROW_BUFS,)),
                        pltpu.SemaphoreType.DMA((N_ROW_BUFS,)),
                        pltpu.SMEM((N_ROW_BUFS,), jnp.int32)],
    )
    return pl.pallas_call(
        _expert_kernel,
        grid_spec=grid_spec,
        out_shape=jax.ShapeDtypeStruct(xs.shape, xs.dtype),
        input_output_aliases={6: 0},
        compiler_params=pltpu.CompilerParams(dimension_semantics=("arbitrary",),
                                             vmem_limit_bytes=V7X_VMEM_LIMIT),
        name="experts",
    )(np16, off16, w_up, b_up.reshape(N_EXPERTS, 1, 2 * D_FF), w_down,
      b_down.reshape(N_EXPERTS, 1, D_MODEL), xs)


def _combine_kernel(ys_ref, dk_ref, xc_ref, xl_ref, mod_ref, gfin_ref, yc_ref, yl_ref, y_s):
    j = pl.program_id(0)
    half = TOK_BLOCK // 2
    cio = lax.broadcasted_iota(jnp.int32, (half, SORT_ROWS), 1).astype(F32)
    halves = [slice(h * half, (h + 1) * half) for h in range(2)]
    accs = [jnp.zeros((half, D_MODEL), F32) for _ in halves]
    for c in range(BLOCK_CAP // SORT_ROWS):
        ys = ys_ref[c * SORT_ROWS:(c + 1) * SORT_ROWS, :]
        for h, rows in enumerate(halves):
            d = dk_ref[0, rows, :]
            w = jnp.zeros((half, SORT_ROWS), F32)
            for k in range(TOP_K):
                w = jnp.where(cio == d[:, k:k + 1] - float(c * SORT_ROWS), d[:, TOP_K + k:TOP_K + k + 1], w)
            accs[h] = accs[h] + _dot(w.astype(BF16), ys)
    for h, rows in enumerate(halves):
        acc = accs[h]
        x1 = jnp.where(j < _N_CTX_BLOCKS, xc_ref[rows, :], xl_ref[rows, :])
        x2 = x1 + mod_ref[0, 5:6, :] * acc
        ms = jnp.mean(x2 * x2, axis=-1, keepdims=True)
        y_s[rows, :] = x2 * lax.rsqrt(ms + RMS_EPS) * gfin_ref[...]

    @pl.when(j < _N_CTX_BLOCKS)
    def _():
        yc_ref[...] = y_s[...]

    @pl.when(j >= _N_CTX_BLOCKS)
    def _():
        yl_ref[...] = y_s[...]


def _combine(ys, dk_t, x1c, x1l, modv, g_final):
    return pl.pallas_call(
        _combine_kernel,
        grid=(_N_BLOCKS,),
        in_specs=[pl.BlockSpec((BLOCK_CAP, D_MODEL), lambda j: (j, 0)),
                  pl.BlockSpec((1, TOK_BLOCK, 2 * TOP_K), lambda j: (j, 0, 0))] + _token_specs() + [
                  _const_spec((1, D_MODEL))],
        out_specs=[pl.BlockSpec((TOK_BLOCK, D_MODEL), lambda j: (jnp.minimum(j, _N_CTX_BLOCKS - 1), 0)),
                   pl.BlockSpec((TOK_BLOCK, D_MODEL), lambda j: (jnp.maximum(j - _N_CTX_BLOCKS, 0), 0))],
        out_shape=[jax.ShapeDtypeStruct(x1c.shape, F32), jax.ShapeDtypeStruct(x1l.shape, F32)],
        scratch_shapes=[pltpu.VMEM((TOK_BLOCK, D_MODEL), F32)],
        compiler_params=pltpu.CompilerParams(dimension_semantics=("arbitrary",),
                                             vmem_limit_bytes=V7X_VMEM_LIMIT),
        name="combine",
    )(ys, dk_t, x1c, x1l, modv, g_final)


def kernel(x_prompt, x_sample, cache_k, cache_v, c, c_ctx, w_ada, b_ada, g_mix, w_in, w_pool, pool_scale,
           w_pa, w_pb, rpb, w_out, g_ffn, w_router, b_router, w_up, b_up, w_down, b_down, g_final):
    assert w_ada.shape[0] == 1, "single trunk layer"
    batch, seq, d = x_prompt.shape
    dec_batch, dec_seq, _ = x_sample.shape
    assert (seq, dec_seq, d) == (SEQ, DEC_SEQ, D_MODEL)
    assert batch * seq == _N_CTX_BLOCKS * TOK_BLOCK and dec_batch * dec_seq == _N_LAT_BLOCKS * TOK_BLOCK

    cmat = jnp.concatenate([c_ctx[None, :], c, jnp.zeros((8 - 1 - dec_batch, d), F32)], axis=0)
    modv = _modulation(cmat, w_ada[0], b_ada[0]).reshape(8, N_MOD, d)

    weights = (g_mix[0][None, :], w_in[0].astype(BF16), w_pool[0].astype(BF16), pool_scale[0][None, :],
               w_pa[0].astype(BF16), w_pb[0].astype(BF16), w_out[0].astype(BF16))

    def by_pair(cache):
        z = cache[:, 0].reshape(dec_batch, N_PAIRS, 2, PAST_LEN, HEAD_DIM)
        return z.transpose(0, 1, 3, 2, 4).reshape(dec_batch, N_PAIRS, PAST_LEN, PAIR_W)

    x1c, new_k, new_v = _ctx_mixer(x_prompt.reshape(batch * seq, d), modv, weights)
    x1l = _lat_mixer(x_sample.reshape(dec_batch * dec_seq, d), modv, weights,
                     by_pair(cache_k), by_pair(cache_v), _bias_tables(rpb[0]))

    xs, dk, np16, off16 = _route_sort(x1c, x1l, modv, g_ffn[0][None, :],
                                      w_router[0].T.astype(BF16), b_router[0][:, None])
    np16_i = np16[:, :, 0].astype(jnp.int32).reshape(-1)
    off16_i = off16[:, :, 0].astype(jnp.int32).reshape(-1)
    ys = _experts(np16_i, off16_i, w_up[0], b_up[0], w_down[0], b_down[0], xs)
    yc, yl = _combine(ys, dk.transpose(0, 2, 1), x1c, x1l, modv, g_final[None, :])
    return (yc.reshape(batch, seq, d), yl.reshape(dec_batch, dec_seq, d), new_k, new_v)
```

```python
import functools

import jax
import jax.numpy as jnp
import numpy as np
from jax import lax
from jax.experimental import pallas as pl
from jax.experimental.pallas import tpu as pltpu

F32 = jnp.float32
BF16 = jnp.bfloat16

D_MODEL = 1024
SEQ = 256
DEC_SEQ = 1024
GRID_W = 64
ROWS = DEC_SEQ // GRID_W
N_HEADS = 8
HEAD_DIM = 64
N_PAIRS = N_HEADS // 2
PAIR_W = 2 * HEAD_DIM
PAST_LEN = 512
POOL_DIM = 512
POOL_WINDOWS = (2, 4, 8, 16)
POOL_GROUP_DIM = 128
ATTN_DIM = 512
WIN_R = 8
WIN_C = 16
N_EXPERTS = 32
TOP_K = 4
D_FF = 1024
SWIGLU_LIMIT = 7.0
SWIGLU_ALPHA = 1.702
N_MOD = 6
RMS_EPS = 1e-6
NEG_INF = -1e30
ATTN_SCALE = HEAD_DIM ** -0.5
IN_DIM = POOL_DIM + 3 * ATTN_DIM + 2 * D_MODEL

TOK_BLOCK = 512
CHUNK = 16
BLOCK_CAP = TOK_BLOCK * TOP_K + N_EXPERTS * CHUNK
EXPERT_ROW_TILE = 256
V7X_VMEM_LIMIT = 60 * 1024 * 1024

_N_CTX_BLOCKS = 8
_N_LAT_BLOCKS = 4
_N_BLOCKS = _N_CTX_BLOCKS + _N_LAT_BLOCKS

_NT = (((1,), (1,)), ((), ()))


def _dot(a, b):
    return jnp.dot(a, b, preferred_element_type=F32)


def _dot_nt(a, b):
    return lax.dot_general(a, b, _NT, preferred_element_type=F32)


def _sigmoid(x):
    return 1.0 / (1.0 + jnp.exp(-x))


def _norm_mod(x, gain, scale, shift):
    ms = jnp.mean(x * x, axis=-1, keepdims=True)
    return (x * lax.rsqrt(ms + RMS_EPS) * gain) * (1.0 + scale) + shift


def _const_spec(shape):
    zeros = (0,) * len(shape)
    return pl.BlockSpec(shape, lambda *_: zeros, pipeline_mode=pl.Buffered(1))


MOD_COLS = 1536


def _mod_kernel(c_ref, w_ref, b_ref, o_ref):
    c = c_ref[...]
    s = c * _sigmoid(c)
    s_hi = s.astype(BF16)
    s_lo = (s - s_hi.astype(F32)).astype(BF16)
    r = _dot(jnp.concatenate([s_hi, s_lo], axis=0), w_ref[...].astype(BF16))
    o_ref[...] = r[:8] + r[8:] + b_ref[...]


def _modulation(cmat, w_ada, b_ada):
    n = w_ada.shape[1]
    return pl.pallas_call(
        _mod_kernel,
        grid=(n // MOD_COLS,),
        in_specs=[pl.BlockSpec((8, D_MODEL), lambda i: (0, 0)),
                  pl.BlockSpec((D_MODEL, MOD_COLS), lambda i: (0, i)),
                  pl.BlockSpec((1, MOD_COLS), lambda i: (0, i))],
        out_specs=pl.BlockSpec((8, MOD_COLS), lambda i: (0, i)),
        out_shape=jax.ShapeDtypeStruct((8, n), F32),
        name="modulation",
    )(cmat, w_ada, b_ada.reshape(1, n))


def _pool_mix(u, pos, seq):
    n = u.shape[0]

    def down(x, d):
        return jnp.where(pos >= d, pltpu.roll(x, d, 0), 0.0)

    def up(x, d):
        return jnp.where(pos < seq - d, pltpu.roll(x, n - d, 0), 0.0)

    return down, up


def _pool_group(u, pos, seq, w):
    down, up = _pool_mix(u, pos, seq)
    hw = w // 2
    back = u
    fwd = u
    d = 1
    while d < hw:
        back = back + down(back, d)
        fwd = fwd + up(fwd, d)
        d *= 2
    s = down(back, 1) + fwd
    posf = pos.astype(F32)
    cnt = jnp.minimum(posf + hw, float(seq)) - jnp.maximum(posf - hw, 0.0)
    return s / cnt - u


MIX_ROWS = 256


def _mixer_front(x_ref, mod_ref, gmix_ref, win_ref, hb_s, u_s, q_s, k_s, v_s):
    shift, scale = mod_ref[0, 0:1, :], mod_ref[0, 1:2, :]
    for c in range(x_ref.shape[0] // MIX_ROWS):
        rows = slice(c * MIX_ROWS, (c + 1) * MIX_ROWS)
        hb = _norm_mod(x_ref[rows, :], gmix_ref[...], scale, shift).astype(BF16)
        hb_s[rows, :] = hb
        u_s[rows, :] = _dot(hb, win_ref[:, 0:POOL_DIM])
        for dst, base in ((q_s, POOL_DIM), (k_s, POOL_DIM + ATTN_DIM), (v_s, POOL_DIM + 2 * ATTN_DIM)):
            z = _dot(hb, win_ref[:, base:base + ATTN_DIM])
            if dst is q_s:
                z = z * ATTN_SCALE
            for g in range(N_PAIRS):
                dst[g, rows, :] = z[:, g * PAIR_W:(g + 1) * PAIR_W].astype(dst.dtype)


def _mixer_back(x_ref, mod_ref, win_ref, wpool_ref, ps_ref, wpa_ref, wpb_ref, wout_ref, x1_ref,
                hb_s, u_s, o_s, pg_s, seq):
    tile = x_ref.shape[0]
    gate = mod_ref[0, 2:3, :]
    pos = lax.broadcasted_iota(jnp.int32, (tile, 1), 0) % seq
    for g, w in enumerate(POOL_WINDOWS):
        cols = slice(g * POOL_GROUP_DIM, (g + 1) * POOL_GROUP_DIM)
        pg_s[:, cols] = _pool_group(u_s[:, cols], pos, seq, w).astype(BF16)
    for c in range(tile // MIX_ROWS):
        rows = slice(c * MIX_ROWS, (c + 1) * MIX_ROWS)
        ys = []
        for g in range(len(POOL_WINDOWS)):
            cols = slice(g * POOL_GROUP_DIM, (g + 1) * POOL_GROUP_DIM)
            ys.append((_dot(pg_s[rows, cols], wpool_ref[g]) * ps_ref[:, cols]).astype(BF16))
        a = _dot(jnp.concatenate(ys, axis=1), wpa_ref[...])
        ob = _dot(jnp.concatenate([o_s[g, rows, :] for g in range(N_PAIRS)], axis=1), wpb_ref[...])
        gab = _dot(hb_s[rows, :], win_ref[:, POOL_DIM + 3 * ATTN_DIM:IN_DIM])
        merged = _sigmoid(gab[:, :D_MODEL]) * a + _sigmoid(gab[:, D_MODEL:]) * ob
        mix = _dot(merged.astype(BF16), wout_ref[...])
        x1_ref[rows, :] = x_ref[rows, :] + gate * mix


def _ctx_mixer_kernel(x_ref, mod_ref, gmix_ref, win_ref, wpool_ref, ps_ref, wpa_ref, wpb_ref, wout_ref,
                      x1_ref, ko_ref, vo_ref, hb_s, u_s, q_s, k_s, v_s, o_s, pg_s):
    _mixer_front(x_ref, mod_ref, gmix_ref, win_ref, hb_s, u_s, q_s, k_s, v_s)
    even = lax.broadcasted_iota(jnp.int32, (1, PAIR_W), 1) < HEAD_DIM
    tile = x_ref.shape[0]
    for s in range(tile // SEQ):
        rows = slice(s * SEQ, (s + 1) * SEQ)
        for g in range(N_PAIRS):
            q2, k2, v2 = q_s[g, rows, :], k_s[g, rows, :], v_s[g, rows, :]
            ko_ref[s, 0, 2 * g] = k2[:, :HEAD_DIM]
            ko_ref[s, 0, 2 * g + 1] = k2[:, HEAD_DIM:]
            vo_ref[s, 0, 2 * g] = v2[:, :HEAD_DIM]
            vo_ref[s, 0, 2 * g + 1] = v2[:, HEAD_DIM:]
            kb, vb = k2.astype(BF16), v2.astype(BF16)
            outs = []
            for par in range(2):
                qm = jnp.where(even if par == 0 else jnp.logical_not(even), q2, 0.0).astype(BF16)
                sc = _dot_nt(qm, kb)
                m = jnp.max(sc, axis=-1, keepdims=True)
                p = jnp.exp(sc - m)
                l = jnp.sum(p, axis=-1, keepdims=True)
                outs.append(_dot(p.astype(BF16), vb) / l)
            o_s[g, rows, :] = jnp.where(even, outs[0], outs[1]).astype(BF16)
    _mixer_back(x_ref, mod_ref, win_ref, wpool_ref, ps_ref, wpa_ref, wpb_ref, wout_ref, x1_ref,
                hb_s, u_s, o_s, pg_s, SEQ)


def _row_window(r):
    rs = min(max(r - WIN_R // 2, 0), ROWS - WIN_R)
    return rs, rs - r + WIN_R - 1


def _lat_mixer_kernel(x_ref, mod_ref, gmix_ref, win_ref, wpool_ref, ps_ref, wpa_ref, wpb_ref, wout_ref,
                      kc_ref, vc_ref, tb_ref, x1_ref, hb_s, u_s, q_s, k_s, v_s, o_s, pg_s):
    _mixer_front(x_ref, mod_ref, gmix_ref, win_ref, hb_s, u_s, q_s, k_s, v_s)
    even = lax.broadcasted_iota(jnp.int32, (1, PAIR_W), 1) < HEAD_DIM
    nk = WIN_R * GRID_W

    def pair_body(g, carry):
        q2 = q_s[g]
        kb, vb = k_s[g].astype(BF16), v_s[g].astype(BF16)
        kcb, vcb = kc_ref[0, g].astype(BF16), vc_ref[0, g].astype(BF16)
        outs = []
        for par in range(2):
            qm = jnp.where(even if par == 0 else jnp.logical_not(even), q2, 0.0).astype(BF16)
            s_ctx = _dot_nt(qm, kcb)
            slabs = []
            for r in range(ROWS):
                rs, rho = _row_window(r)
                bias = tb_ref[2 * g + par, rho % 2, :, (rho - rho % 2) * GRID_W:(rho - rho % 2) * GRID_W + nk]
                sl = _dot_nt(qm[r * GRID_W:(r + 1) * GRID_W, :], kb[rs * GRID_W:rs * GRID_W + nk, :])
                slabs.append(sl + bias)
            s_loc = jnp.concatenate(slabs, axis=0)
            m = jnp.maximum(jnp.max(s_loc, axis=-1, keepdims=True), jnp.max(s_ctx, axis=-1, keepdims=True))
            p_loc = jnp.exp(s_loc - m)
            p_ctx = jnp.exp(s_ctx - m)
            l = jnp.sum(p_loc, axis=-1, keepdims=True) + jnp.sum(p_ctx, axis=-1, keepdims=True)
            p_locb = p_loc.astype(BF16)
            o_rows = []
            for r in range(ROWS):
                rs, _ = _row_window(r)
                o_rows.append(_dot(p_locb[r * GRID_W:(r + 1) * GRID_W, :], vb[rs * GRID_W:rs * GRID_W + nk, :]))
            o = jnp.concatenate(o_rows, axis=0) + _dot(p_ctx.astype(BF16), vcb)
            outs.append(o / l)
        o_s[g] = jnp.where(even, outs[0], outs[1]).astype(BF16)
        return carry

    lax.fori_loop(0, N_PAIRS, pair_body, 0)
    _mixer_back(x_ref, mod_ref, win_ref, wpool_ref, ps_ref, wpa_ref, wpb_ref, wout_ref, x1_ref,
                hb_s, u_s, o_s, pg_s, DEC_SEQ)


def _mixer_scratch(tile, kv_dtype):
    return [pltpu.VMEM((tile, D_MODEL), BF16),
            pltpu.VMEM((tile, POOL_DIM), F32),
            pltpu.VMEM((N_PAIRS, tile, PAIR_W), BF16),
            pltpu.VMEM((N_PAIRS, tile, PAIR_W), kv_dtype),
            pltpu.VMEM((N_PAIRS, tile, PAIR_W), kv_dtype),
            pltpu.VMEM((N_PAIRS, tile, PAIR_W), BF16),
            pltpu.VMEM((tile, POOL_DIM), BF16)]


def _weight_specs():
    return [_const_spec((1, D_MODEL)),
            _const_spec((D_MODEL, IN_DIM)),
            _const_spec((len(POOL_WINDOWS), POOL_GROUP_DIM, POOL_GROUP_DIM)),
            _const_spec((1, POOL_DIM)),
            _const_spec((POOL_DIM, D_MODEL)),
            _const_spec((ATTN_DIM, D_MODEL)),
            _const_spec((D_MODEL, D_MODEL))]


def _ctx_mixer(x, modv, weights):
    n = x.shape[0]
    nseq = TOK_BLOCK // SEQ
    cache = jax.ShapeDtypeStruct((n // SEQ, 1, N_HEADS, SEQ, HEAD_DIM), F32)
    cache_spec = pl.BlockSpec((nseq, 1, N_HEADS, SEQ, HEAD_DIM), lambda i: (i, 0, 0, 0, 0))
    return pl.pallas_call(
        _ctx_mixer_kernel,
        grid=(n // TOK_BLOCK,),
        in_specs=[pl.BlockSpec((TOK_BLOCK, D_MODEL), lambda i: (i, 0)),
                  pl.BlockSpec((1, N_MOD, D_MODEL), lambda i: (0, 0, 0))] + _weight_specs(),
        out_specs=[pl.BlockSpec((TOK_BLOCK, D_MODEL), lambda i: (i, 0)), cache_spec, cache_spec],
        out_shape=[jax.ShapeDtypeStruct((n, D_MODEL), F32), cache, cache],
        scratch_shapes=_mixer_scratch(TOK_BLOCK, F32),
        compiler_params=pltpu.CompilerParams(dimension_semantics=("arbitrary",),
                                             vmem_limit_bytes=V7X_VMEM_LIMIT),
        name="ctx_mixer",
    )(x, modv, *weights)


def _lat_mixer(x, modv, weights, kc, vc, tb):
    n = x.shape[0]
    return pl.pallas_call(
        _lat_mixer_kernel,
        grid=(n // DEC_SEQ,),
        in_specs=[pl.BlockSpec((DEC_SEQ, D_MODEL), lambda i: (i, 0)),
                  pl.BlockSpec((1, N_MOD, D_MODEL), lambda i: (i + 1, 0, 0))] + _weight_specs() + [
                  pl.BlockSpec((1, N_PAIRS, PAST_LEN, PAIR_W), lambda i: (i, 0, 0, 0)),
                  pl.BlockSpec((1, N_PAIRS, PAST_LEN, PAIR_W), lambda i: (i, 0, 0, 0)),
                  _const_spec((N_HEADS, 2, GRID_W, ROWS * GRID_W))],
        out_specs=pl.BlockSpec((DEC_SEQ, D_MODEL), lambda i: (i, 0)),
        out_shape=jax.ShapeDtypeStruct((n, D_MODEL), F32),
        scratch_shapes=_mixer_scratch(DEC_SEQ, BF16),
        compiler_params=pltpu.CompilerParams(dimension_semantics=("arbitrary",),
                                             vmem_limit_bytes=V7X_VMEM_LIMIT),
        name="lat_mixer",
    )(x, modv, *weights, kc, vc, tb)


N_RPB_ROWS = 2 * WIN_R - 1
N_RPB_COLS = 2 * WIN_C - 1
TABLE_W = ROWS * GRID_W


def _bias_kernel(v_ref, keep_ref, o_ref):
    for h in range(N_HEADS):
        for par in range(2):
            x = jnp.broadcast_to(v_ref[h, par:par + 1, :], (GRID_W, TABLE_W))
            shifted = pltpu.roll(x, TABLE_W - (WIN_C - 1), 1, stride=1, stride_axis=0)
            o_ref[h, par] = jnp.where(keep_ref[par] > 0.0, shifted, NEG_INF)


def _bias_tables(rpb):
    col = np.arange(GRID_W)
    cs = np.clip(col - WIN_C // 2, 0, GRID_W - WIN_C)
    in_win = (col[None, :] >= cs[:, None]) & (col[None, :] < cs[:, None] + WIN_C)
    keep = np.tile(in_win.astype(np.float32), (2, 1, ROWS))
    keep[0, :, N_RPB_ROWS * GRID_W:] = 0.0
    keep[1, :, (N_RPB_ROWS - 1) * GRID_W:] = 0.0
    rp = jnp.pad(rpb.astype(F32), ((0, 0), (0, ROWS + 1 - N_RPB_ROWS), (0, GRID_W - N_RPB_COLS)))
    v = jnp.stack([rp[:, :ROWS].reshape(N_HEADS, TABLE_W), rp[:, 1:].reshape(N_HEADS, TABLE_W)], axis=1)
    return pl.pallas_call(
        _bias_kernel,
        out_shape=jax.ShapeDtypeStruct((N_HEADS, 2, GRID_W, TABLE_W), F32),
        name="bias_tables",
    )(v, jnp.asarray(keep))


SORT_ROWS = 512


def _sort_kernel(xc_ref, xl_ref, mod_ref, gffn_ref, wr_ref, br_ref,
                 xs_ref, dk_ref, np_ref, off_ref):
    j = pl.program_id(0)
    x = jnp.where(j < _N_CTX_BLOCKS, xc_ref[...], xl_ref[...])
    shift, scale = mod_ref[0, 3:4, :], mod_ref[0, 4:5, :]
    hb = _norm_mod(x, gffn_ref[...], scale, shift).astype(BF16)
    logits = _dot_nt(wr_ref[...], hb) + br_ref[...]
    eio = lax.broadcasted_iota(jnp.int32, logits.shape, 0)
    work = logits
    sels, vals = [], []
    for _ in range(TOP_K):
        m = jnp.max(work, axis=0, keepdims=True)
        idx = jnp.min(jnp.where(work == m, eio, N_EXPERTS), axis=0, keepdims=True)
        sel = eio == idx
        sels.append(sel)
        vals.append(m)
        work = jnp.where(sel, -jnp.inf, work)
    exps = [jnp.exp(v - vals[0]) for v in vals]
    den = exps[0] + exps[1] + exps[2] + exps[3]
    mask = jnp.zeros(logits.shape, F32)
    for sel in sels:
        mask = mask + jnp.where(sel, 1.0, 0.0)
    t_row = lax.broadcasted_iota(jnp.int32, (TOK_BLOCK, TOK_BLOCK), 0)
    t_col = lax.broadcasted_iota(jnp.int32, (TOK_BLOCK, TOK_BLOCK), 1)
    before = jnp.where(t_row < t_col, 1.0, 0.0).astype(BF16)
    rank = _dot(mask.astype(BF16), before)
    cnt = jnp.sum(mask, axis=1, keepdims=True)
    np16 = jnp.floor((cnt + (CHUNK - 1.0)) * (1.0 / CHUNK))
    e_row = lax.broadcasted_iota(jnp.int32, (N_EXPERTS, N_EXPERTS), 0)
    e_col = lax.broadcasted_iota(jnp.int32, (N_EXPERTS, N_EXPERTS), 1)
    lower = jnp.where(e_col < e_row, 1.0, 0.0).astype(BF16)
    np16_b = jnp.broadcast_to(np16, (N_EXPERTS, 128))
    off16 = _dot(lower, np16_b.astype(BF16))
    np_ref[0] = np16_b
    off_ref[0] = off16
    dest = off16[:, 0:1] * float(CHUNK) + rank
    dests = []
    for k in range(TOP_K):
        dk = jnp.sum(jnp.where(sels[k], dest, 0.0), axis=0, keepdims=True)
        dk_ref[0, k:k + 1, :] = dk
        dk_ref[0, TOP_K + k:TOP_K + k + 1, :] = exps[k] / den
        dests.append(dk.astype(jnp.int32))
    rio = lax.broadcasted_iota(jnp.int32, (SORT_ROWS, TOK_BLOCK), 0)
    for c in range(BLOCK_CAP // SORT_ROWS):
        onehot = jnp.zeros((SORT_ROWS, TOK_BLOCK), F32)
        for dk in dests:
            onehot = jnp.where(rio == dk - c * SORT_ROWS, 1.0, onehot)
        xs_ref[c * SORT_ROWS:(c + 1) * SORT_ROWS, :] = _dot(onehot.astype(BF16), hb).astype(BF16)


def _block_mod_index(j):
    return jnp.where(j < _N_CTX_BLOCKS, 0, 1 + (j - _N_CTX_BLOCKS) // (DEC_SEQ // TOK_BLOCK))


def _token_specs():
    return [pl.BlockSpec((TOK_BLOCK, D_MODEL), lambda j: (jnp.minimum(j, _N_CTX_BLOCKS - 1), 0)),
            pl.BlockSpec((TOK_BLOCK, D_MODEL), lambda j: (jnp.maximum(j - _N_CTX_BLOCKS, 0), 0)),
            pl.BlockSpec((1, N_MOD, D_MODEL), lambda j: (_block_mod_index(j), 0, 0))]


def _route_sort(x1c, x1l, modv, g_ffn, wr_t, br):
    tbl = jax.ShapeDtypeStruct((_N_BLOCKS, N_EXPERTS, 128), F32)
    tbl_spec = pl.BlockSpec((1, N_EXPERTS, 128), lambda j: (j, 0, 0))
    return pl.pallas_call(
        _sort_kernel,
        grid=(_N_BLOCKS,),
        in_specs=_token_specs() + [_const_spec((1, D_MODEL)),
                                   _const_spec((N_EXPERTS, D_MODEL)),
                                   _const_spec((N_EXPERTS, 1))],
        out_specs=[pl.BlockSpec((BLOCK_CAP, D_MODEL), lambda j: (j, 0)),
                   pl.BlockSpec((1, 2 * TOP_K, TOK_BLOCK), lambda j: (j, 0, 0)),
                   tbl_spec, tbl_spec],
        out_shape=[jax.ShapeDtypeStruct((_N_BLOCKS * BLOCK_CAP, D_MODEL), BF16),
                   jax.ShapeDtypeStruct((_N_BLOCKS, 2 * TOP_K, TOK_BLOCK), F32),
                   tbl, tbl],
        compiler_params=pltpu.CompilerParams(dimension_semantics=("arbitrary",),
                                             vmem_limit_bytes=V7X_VMEM_LIMIT),
        name="route_sort",
    )(x1c, x1l, modv, g_ffn, wr_t, br)


EXPERT_BUF_ROWS = 2048
PASS_CHUNKS = EXPERT_BUF_ROWS // CHUNK
N_ROW_BUFS = 3


def _expert_kernel(np_ref, off_ref, wu_ref, bu_ref, wd_ref, bd_ref, xs_hbm, ys_hbm,
                   buf, wu_s, wd_s, gsem, ssem, pend):
    del xs_hbm
    e = pl.program_id(0)
    last = pl.num_programs(0) - 1
    slot = e % N_ROW_BUFS
    nxt = (e + 1) % N_ROW_BUFS

    def move_chunks(ex, q_lo, slot_, gather):
        def block_body(j, q0):
            n = np_ref[j * N_EXPERTS + ex]
            off = off_ref[j * N_EXPERTS + ex]
            c_lo = jnp.clip(q_lo - q0, 0, n)
            m = jnp.clip(q_lo + PASS_CHUNKS - q0, 0, n) - c_lo

            @pl.when(m > 0)
            def _():
                rows = pl.multiple_of(m * CHUNK, CHUNK)
                row0 = pl.multiple_of(j * BLOCK_CAP + (off + c_lo) * CHUNK, CHUNK)
                brow0 = pl.multiple_of((q0 + c_lo - q_lo) * CHUNK, CHUNK)
                hbm = ys_hbm.at[pl.ds(row0, rows), :]
                vm = buf.at[slot_, pl.ds(brow0, rows), :]
                if gather:
                    pltpu.make_async_copy(hbm, vm, gsem.at[slot_]).start()
                else:
                    pltpu.make_async_copy(vm, hbm, ssem.at[slot_]).start()

            return q0 + n

        lax.fori_loop(0, _N_BLOCKS, block_body, 0)

    def wait_chunks(sem, n, slot_):
        @pl.when(n > 0)
        def _():
            rows = pl.multiple_of(n * CHUNK, CHUNK)
            pltpu.make_async_copy(ys_hbm.at[pl.ds(0, rows), :], buf.at[slot_, pl.ds(0, rows), :], sem).wait()

    def mlp_rows(r0, rows):
        x = buf[slot, pl.ds(r0, rows), :]
        gu = _dot(x, wu_s[...]) + bu_ref[0]
        gate = jnp.minimum(gu[:, :D_FF], SWIGLU_LIMIT)
        up = jnp.clip(gu[:, D_FF:], -SWIGLU_LIMIT, SWIGLU_LIMIT)
        glu = gate * _sigmoid(SWIGLU_ALPHA * gate)
        y = _dot(((up + 1.0) * glu).astype(BF16), wd_s[...]) + bd_ref[0]
        buf[slot, pl.ds(r0, rows), :] = y.astype(BF16)

    def compute(n):
        rows = n * CHUNK
        rem = rows % EXPERT_ROW_TILE
        half = EXPERT_ROW_TILE // 2
        n_full = rows // EXPERT_ROW_TILE + jnp.where(rem > half, 1, 0)
        tail = (rem > 0) & (rem <= half)
        merge = tail & (n_full > 0)

        def tile_body(t, carry):
            mlp_rows(pl.multiple_of(t * EXPERT_ROW_TILE, EXPERT_ROW_TILE), EXPERT_ROW_TILE)
            return carry

        lax.fori_loop(0, n_full - jnp.where(merge, 1, 0), tile_body, 0)

        @pl.when(merge)
        def _():
            mlp_rows(pl.multiple_of((n_full - 1) * EXPERT_ROW_TILE, EXPERT_ROW_TILE), EXPERT_ROW_TILE + half)

        @pl.when(tail & jnp.logical_not(merge))
        def _():
            mlp_rows(0, half)

    def chunks_of(ex):
        return lax.fori_loop(0, _N_BLOCKS, lambda j, acc: acc + np_ref[j * N_EXPERTS + ex], 0)

    @pl.when(e == 0)
    def _():
        buf[...] = jnp.zeros_like(buf)
        for b in range(N_ROW_BUFS):
            pend[b] = 0
        move_chunks(0, 0, 0, True)

    wait_chunks(ssem.at[nxt], pend[nxt], nxt)
    pend[nxt] = 0

    @pl.when(e < last)
    def _():
        move_chunks(e + 1, 0, nxt, True)

    wu_s[...] = wu_ref[0].astype(BF16)
    wd_s[...] = wd_ref[0].astype(BF16)

    total = chunks_of(e)
    n0 = jnp.minimum(total, PASS_CHUNKS)
    wait_chunks(gsem.at[slot], n0, slot)
    compute(n0)
    move_chunks(e, 0, slot, False)
    pend[slot] = n0

    def pass_body(p, carry):
        wait_chunks(ssem.at[slot], pend[slot], slot)
        lo = p * PASS_CHUNKS
        n = jnp.minimum(total - lo, PASS_CHUNKS)
        move_chunks(e, lo, slot, True)
        wait_chunks(gsem.at[slot], n, slot)
        compute(n)
        move_chunks(e, lo, slot, False)
        pend[slot] = n
        return carry

    lax.fori_loop(1, (total + PASS_CHUNKS - 1) // PASS_CHUNKS, pass_body, 0)

    @pl.when(e == last)
    def _():
        for b in range(N_ROW_BUFS):
            wait_chunks(ssem.at[b], pend[b], b)
            pend[b] = 0


def _experts(np16, off16, w_up, b_up, w_down, b_down, xs):
    grid_spec = pltpu.PrefetchScalarGridSpec(
        num_scalar_prefetch=2,
        grid=(N_EXPERTS,),
        in_specs=[pl.BlockSpec((1, D_MODEL, 2 * D_FF), lambda e, *_: (e, 0, 0)),
                  pl.BlockSpec((1, 1, 2 * D_FF), lambda e, *_: (e, 0, 0)),
                  pl.BlockSpec((1, D_FF, D_MODEL), lambda e, *_: (e, 0, 0)),
                  pl.BlockSpec((1, 1, D_MODEL), lambda e, *_: (e, 0, 0)),
                  pl.BlockSpec(memory_space=pl.ANY)],
        out_specs=pl.BlockSpec(memory_space=pl.ANY),
        scratch_shapes=[pltpu.VMEM((N_ROW_BUFS, EXPERT_BUF_ROWS, D_MODEL), BF16),
                        pltpu.VMEM((D_MODEL, 2 * D_FF), BF16),
                        pltpu.VMEM((D_FF, D_MODEL), BF16),
                        pltpu.SemaphoreType.DMA((N_ROW_BUFS,)),
                        pltpu.SemaphoreType.DMA((N_ROW_BUFS,)),
                        pltpu.SMEM((N_ROW_BUFS,), jnp.int32)],
    )
    return pl.pallas_call(
        _expert_kernel,
        grid_spec=grid_spec,
        out_shape=jax.ShapeDtypeStruct(xs.shape, xs.dtype),
        input_output_aliases={6: 0},
        compiler_params=pltpu.CompilerParams(dimension_semantics=("arbitrary",),
                                             vmem_limit_bytes=V7X_VMEM_LIMIT),
        name="experts",
    )(np16, off16, w_up, b_up.reshape(N_EXPERTS, 1, 2 * D_FF), w_down,
      b_down.reshape(N_EXPERTS, 1, D_MODEL), xs)


def _combine_kernel(ys_ref, dk_ref, xc_ref, xl_ref, mod_ref, gfin_ref, yc_ref, yl_ref, y_s):
    j = pl.program_id(0)
    half = TOK_BLOCK // 2
    cio = lax.broadcasted_iota(jnp.int32, (half, SORT_ROWS), 1).astype(F32)
    halves = [slice(h * half, (h + 1) * half) for h in range(2)]
    accs = [jnp.zeros((half, D_MODEL), F32) for _ in halves]
    for c in range(BLOCK_CAP // SORT_ROWS):
        ys = ys_ref[c * SORT_ROWS:(c + 1) * SORT_ROWS, :]
        for h, rows in enumerate(halves):
            d = dk_ref[0, rows, :]
            w = jnp.zeros((half, SORT_ROWS), F32)
            for k in range(TOP_K):
                w = jnp.where(cio == d[:, k:k + 1] - float(c * SORT_ROWS), d[:, TOP_K + k:TOP_K + k + 1], w)
            accs[h] = accs[h] + _dot(w.astype(BF16), ys)
    for h, rows in enumerate(halves):
        acc = accs[h]
        x1 = jnp.where(j < _N_CTX_BLOCKS, xc_ref[rows, :], xl_ref[rows, :])
        x2 = x1 + mod_ref[0, 5:6, :] * acc
        ms = jnp.mean(x2 * x2, axis=-1, keepdims=True)
        y_s[rows, :] = x2 * lax.rsqrt(ms + RMS_EPS) * gfin_ref[...]

    @pl.when(j < _N_CTX_BLOCKS)
    def _():
        yc_ref[...] = y_s[...]

    @pl.when(j >= _N_CTX_BLOCKS)
    def _():
        yl_ref[...] = y_s[...]


def _combine(ys, dk_t, x1c, x1l, modv, g_final):
    return pl.pallas_call(
        _combine_kernel,
        grid=(_N_BLOCKS,),
        in_specs=[pl.BlockSpec((BLOCK_CAP, D_MODEL), lambda j: (j, 0)),
                  pl.BlockSpec((1, TOK_BLOCK, 2 * TOP_K), lambda j: (j, 0, 0))] + _token_specs() + [
                  _const_spec((1, D_MODEL))],
        out_specs=[pl.BlockSpec((TOK_BLOCK, D_MODEL), lambda j: (jnp.minimum(j, _N_CTX_BLOCKS - 1), 0)),
                   pl.BlockSpec((TOK_BLOCK, D_MODEL), lambda j: (jnp.maximum(j - _N_CTX_BLOCKS, 0), 0))],
        out_shape=[jax.ShapeDtypeStruct(x1c.shape, F32), jax.ShapeDtypeStruct(x1l.shape, F32)],
        scratch_shapes=[pltpu.VMEM((TOK_BLOCK, D_MODEL), F32)],
        compiler_params=pltpu.CompilerParams(dimension_semantics=("arbitrary",),
                                             vmem_limit_bytes=V7X_VMEM_LIMIT),
        name="combine",
    )(ys, dk_t, x1c, x1l, modv, g_final)


def kernel(x_prompt, x_sample, cache_k, cache_v, c, c_ctx, w_ada, b_ada, g_mix, w_in, w_pool, pool_scale,
           w_pa, w_pb, rpb, w_out, g_ffn, w_router, b_router, w_up, b_up, w_down, b_down, g_final):
    assert w_ada.shape[0] == 1, "single trunk layer"
    batch, seq, d = x_prompt.shape
    dec_batch, dec_seq, _ = x_sample.shape
    assert (seq, dec_seq, d) == (SEQ, DEC_SEQ, D_MODEL)
    assert batch * seq == _N_CTX_BLOCKS * TOK_BLOCK and dec_batch * dec_seq == _N_LAT_BLOCKS * TOK_BLOCK

    cmat = jnp.concatenate([c_ctx[None, :], c, jnp.zeros((8 - 1 - dec_batch, d), F32)], axis=0)
    modv = _modulation(cmat, w_ada[0], b_ada[0]).reshape(8, N_MOD, d)

    weights = (g_mix[0][None, :], w_in[0].astype(BF16), w_pool[0].astype(BF16), pool_scale[0][None, :],
               w_pa[0].astype(BF16), w_pb[0].astype(BF16), w_out[0].astype(BF16))

    def by_pair(cache):
        z = cache[:, 0].reshape(dec_batch, N_PAIRS, 2, PAST_LEN, HEAD_DIM)
        return z.transpose(0, 1, 3, 2, 4).reshape(dec_batch, N_PAIRS, PAST_LEN, PAIR_W)

    x1c, new_k, new_v = _ctx_mixer(x_prompt.reshape(batch * seq, d), modv, weights)
    x1l = _lat_mixer(x_sample.reshape(dec_batch * dec_seq, d), modv, weights,
                     by_pair(cache_k), by_pair(cache_v), _bias_tables(rpb[0]))

    xs, dk, np16, off16 = _route_sort(x1c, x1l, modv, g_ffn[0][None, :],
                                      w_router[0].T.astype(BF16), b_router[0][:, None])
    np16_i = np16[:, :, 0].astype(jnp.int32).reshape(-1)
    off16_i = off16[:, :, 0].astype(jnp.int32).reshape(-1)
    ys = _experts(np16_i, off16_i, w_up[0], b_up[0], w_down[0], b_down[0], xs)
    yc, yl = _combine(ys, dk.transpose(0, 2, 1), x1c, x1l, modv, g_final[None, :])
    return (yc.reshape(batch, seq, d), yl.reshape(dec_batch, dec_seq, d), new_k, new_v)
```

```python
import functools

import jax
import jax.numpy as jnp
import numpy as np
from jax import lax
from jax.experimental import pallas as pl
from jax.experimental.pallas import tpu as pltpu

F32 = jnp.float32
BF16 = jnp.bfloat16

D_MODEL = 1024
SEQ = 256
DEC_SEQ = 1024
GRID_W = 64
ROWS = DEC_SEQ // GRID_W
N_HEADS = 8
HEAD_DIM = 64
N_PAIRS = N_HEADS // 2
PAIR_W = 2 * HEAD_DIM
PAST_LEN = 512
POOL_DIM = 512
POOL_WINDOWS = (2, 4, 8, 16)
POOL_GROUP_DIM = 128
ATTN_DIM = 512
WIN_R = 8
WIN_C = 16
N_EXPERTS = 32
TOP_K = 4
D_FF = 1024
SWIGLU_LIMIT = 7.0
SWIGLU_ALPHA = 1.702
N_MOD = 6
RMS_EPS = 1e-6
NEG_INF = -1e30
ATTN_SCALE = HEAD_DIM ** -0.5
IN_DIM = POOL_DIM + 3 * ATTN_DIM + 2 * D_MODEL

TOK_BLOCK = 512
CHUNK = 16
BLOCK_CAP = TOK_BLOCK * TOP_K + N_EXPERTS * CHUNK
ROW_STEP = 128
EXPERT_ROW_TILE = 384
LAST_ROWS_MAX = 512
V7X_VMEM_LIMIT = 60 * 1024 * 1024

_N_CTX_BLOCKS = 8
_N_LAT_BLOCKS = 4
_N_BLOCKS = _N_CTX_BLOCKS + _N_LAT_BLOCKS

_NT = (((1,), (1,)), ((), ()))


def _dot(a, b):
    return jnp.dot(a, b, preferred_element_type=F32)


def _dot_nt(a, b):
    return lax.dot_general(a, b, _NT, preferred_element_type=F32)


def _sigmoid(x):
    return 1.0 / (1.0 + jnp.exp(-x))


def _norm_mod(x, gain, scale, shift):
    ms = jnp.mean(x * x, axis=-1, keepdims=True)
    return (x * lax.rsqrt(ms + RMS_EPS) * gain) * (1.0 + scale) + shift


def _const_spec(shape):
    zeros = (0,) * len(shape)
    return pl.BlockSpec(shape, lambda *_: zeros, pipeline_mode=pl.Buffered(1))


MOD_COLS = 1536


def _mod_kernel(c_ref, w_ref, b_ref, o_ref):
    c = c_ref[...]
    s = c * _sigmoid(c)
    s_hi = s.astype(BF16)
    s_lo = (s - s_hi.astype(F32)).astype(BF16)
    r = _dot(jnp.concatenate([s_hi, s_lo], axis=0), w_ref[...].astype(BF16))
    o_ref[...] = r[:8] + r[8:] + b_ref[...]


def _modulation(cmat, w_ada, b_ada):
    n = w_ada.shape[1]
    return pl.pallas_call(
        _mod_kernel,
        grid=(n // MOD_COLS,),
        in_specs=[pl.BlockSpec((8, D_MODEL), lambda i: (0, 0)),
                  pl.BlockSpec((D_MODEL, MOD_COLS), lambda i: (0, i)),
                  pl.BlockSpec((1, MOD_COLS), lambda i: (0, i))],
        out_specs=pl.BlockSpec((8, MOD_COLS), lambda i: (0, i)),
        out_shape=jax.ShapeDtypeStruct((8, n), F32),
        name="modulation",
    )(cmat, w_ada, b_ada.reshape(1, n))


def _pool_mix(u, pos, seq):
    n = u.shape[0]

    def down(x, d):
        return jnp.where(pos >= d, pltpu.roll(x, d, 0), 0.0)

    def up(x, d):
        return jnp.where(pos < seq - d, pltpu.roll(x, n - d, 0), 0.0)

    return down, up


def _pool_group(u, pos, seq, w):
    down, up = _pool_mix(u, pos, seq)
    hw = w // 2
    back = u
    fwd = u
    d = 1
    while d < hw:
        back = back + down(back, d)
        fwd = fwd + up(fwd, d)
        d *= 2
    s = down(back, 1) + fwd
    posf = pos.astype(F32)
    cnt = jnp.minimum(posf + hw, float(seq)) - jnp.maximum(posf - hw, 0.0)
    return s / cnt - u


MIX_ROWS = 256


def _mixer_front(x_ref, mod_ref, gmix_ref, win_ref, hb_s, u_s, q_s, k_s, v_s):
    shift, scale = mod_ref[0, 0:1, :], mod_ref[0, 1:2, :]
    for c in range(x_ref.shape[0] // MIX_ROWS):
        rows = slice(c * MIX_ROWS, (c + 1) * MIX_ROWS)
        hb = _norm_mod(x_ref[rows, :], gmix_ref[...], scale, shift).astype(BF16)
        hb_s[rows, :] = hb
        u_s[rows, :] = _dot(hb, win_ref[:, 0:POOL_DIM])
        for dst, base in ((q_s, POOL_DIM), (k_s, POOL_DIM + ATTN_DIM), (v_s, POOL_DIM + 2 * ATTN_DIM)):
            z = _dot(hb, win_ref[:, base:base + ATTN_DIM])
            if dst is q_s:
                z = z * ATTN_SCALE
            for g in range(N_PAIRS):
                dst[g, rows, :] = z[:, g * PAIR_W:(g + 1) * PAIR_W].astype(dst.dtype)


def _mixer_back(x_ref, mod_ref, win_ref, wpool_ref, ps_ref, wpa_ref, wpb_ref, wout_ref, x1_ref,
                hb_s, u_s, o_s, pg_s, seq):
    tile = x_ref.shape[0]
    gate = mod_ref[0, 2:3, :]
    pos = lax.broadcasted_iota(jnp.int32, (tile, 1), 0) % seq
    for g, w in enumerate(POOL_WINDOWS):
        cols = slice(g * POOL_GROUP_DIM, (g + 1) * POOL_GROUP_DIM)
        pg_s[:, cols] = _pool_group(u_s[:, cols], pos, seq, w).astype(BF16)
    for c in range(tile // MIX_ROWS):
        rows = slice(c * MIX_ROWS, (c + 1) * MIX_ROWS)
        ys = []
        for g in range(len(POOL_WINDOWS)):
            cols = slice(g * POOL_GROUP_DIM, (g + 1) * POOL_GROUP_DIM)
            ys.append((_dot(pg_s[rows, cols], wpool_ref[g]) * ps_ref[:, cols]).astype(BF16))
        a = _dot(jnp.concatenate(ys, axis=1), wpa_ref[...])
        ob = _dot(jnp.concatenate([o_s[g, rows, :] for g in range(N_PAIRS)], axis=1), wpb_ref[...])
        gab = _dot(hb_s[rows, :], win_ref[:, POOL_DIM + 3 * ATTN_DIM:IN_DIM])
        merged = _sigmoid(gab[:, :D_MODEL]) * a + _sigmoid(gab[:, D_MODEL:]) * ob
        mix = _dot(merged.astype(BF16), wout_ref[...])
        x1_ref[rows, :] = x_ref[rows, :] + gate * mix


def _ctx_mixer_kernel(x_ref, mod_ref, gmix_ref, win_ref, wpool_ref, ps_ref, wpa_ref, wpb_ref, wout_ref,
                      x1_ref, ko_ref, vo_ref, hb_s, u_s, q_s, k_s, v_s, o_s, pg_s):
    _mixer_front(x_ref, mod_ref, gmix_ref, win_ref, hb_s, u_s, q_s, k_s, v_s)
    even = lax.broadcasted_iota(jnp.int32, (1, PAIR_W), 1) < HEAD_DIM
    tile = x_ref.shape[0]
    for s in range(tile // SEQ):
        rows = slice(s * SEQ, (s + 1) * SEQ)
        for g in range(N_PAIRS):
            q2, k2, v2 = q_s[g, rows, :], k_s[g, rows, :], v_s[g, rows, :]
            ko_ref[s, 0, 2 * g] = k2[:, :HEAD_DIM]
            ko_ref[s, 0, 2 * g + 1] = k2[:, HEAD_DIM:]
            vo_ref[s, 0, 2 * g] = v2[:, :HEAD_DIM]
            vo_ref[s, 0, 2 * g + 1] = v2[:, HEAD_DIM:]
            kb, vb = k2.astype(BF16), v2.astype(BF16)
            outs = []
            for par in range(2):
                qm = jnp.where(even if par == 0 else jnp.logical_not(even), q2, 0.0).astype(BF16)
                sc = _dot_nt(qm, kb)
                m = jnp.max(sc, axis=-1, keepdims=True)
                p = jnp.exp(sc - m)
                l = jnp.sum(p, axis=-1, keepdims=True)
                outs.append(_dot(p.astype(BF16), vb) / l)
            o_s[g, rows, :] = jnp.where(even, outs[0], outs[1]).astype(BF16)
    _mixer_back(x_ref, mod_ref, win_ref, wpool_ref, ps_ref, wpa_ref, wpb_ref, wout_ref, x1_ref,
                hb_s, u_s, o_s, pg_s, SEQ)


def _row_window(r):
    rs = min(max(r - WIN_R // 2, 0), ROWS - WIN_R)
    return rs, rs - r + WIN_R - 1


def _lat_mixer_kernel(x_ref, mod_ref, gmix_ref, win_ref, wpool_ref, ps_ref, wpa_ref, wpb_ref, wout_ref,
                      kc_ref, vc_ref, tb_ref, x1_ref, hb_s, u_s, q_s, k_s, v_s, o_s, pg_s):
    _mixer_front(x_ref, mod_ref, gmix_ref, win_ref, hb_s, u_s, q_s, k_s, v_s)
    even = lax.broadcasted_iota(jnp.int32, (1, PAIR_W), 1) < HEAD_DIM
    nk = WIN_R * GRID_W

    def pair_body(g, carry):
        q2 = q_s[g]
        kb, vb = k_s[g].astype(BF16), v_s[g].astype(BF16)
        kcb, vcb = kc_ref[0, g].astype(BF16), vc_ref[0, g].astype(BF16)
        outs = []
        for par in range(2):
            qm = jnp.where(even if par == 0 else jnp.logical_not(even), q2, 0.0).astype(BF16)
            s_ctx = _dot_nt(qm, kcb)
            slabs = []
            for r in range(ROWS):
                rs, rho = _row_window(r)
                bias = tb_ref[2 * g + par, rho % 2, :, (rho - rho % 2) * GRID_W:(rho - rho % 2) * GRID_W + nk]
                sl = _dot_nt(qm[r * GRID_W:(r + 1) * GRID_W, :], kb[rs * GRID_W:rs * GRID_W + nk, :])
                slabs.append(sl + bias)
            s_loc = jnp.concatenate(slabs, axis=0)
            m = jnp.maximum(jnp.max(s_loc, axis=-1, keepdims=True), jnp.max(s_ctx, axis=-1, keepdims=True))
            p_loc = jnp.exp(s_loc - m)
            p_ctx = jnp.exp(s_ctx - m)
            l = jnp.sum(p_loc, axis=-1, keepdims=True) + jnp.sum(p_ctx, axis=-1, keepdims=True)
            p_locb = p_loc.astype(BF16)
            o_rows = []
            for r in range(ROWS):
                rs, _ = _row_window(r)
                o_rows.append(_dot(p_locb[r * GRID_W:(r + 1) * GRID_W, :], vb[rs * GRID_W:rs * GRID_W + nk, :]))
            o = jnp.concatenate(o_rows, axis=0) + _dot(p_ctx.astype(BF16), vcb)
            outs.append(o / l)
        o_s[g] = jnp.where(even, outs[0], outs[1]).astype(BF16)
        return carry

    lax.fori_loop(0, N_PAIRS, pair_body, 0)
    _mixer_back(x_ref, mod_ref, win_ref, wpool_ref, ps_ref, wpa_ref, wpb_ref, wout_ref, x1_ref,
                hb_s, u_s, o_s, pg_s, DEC_SEQ)


def _mixer_scratch(tile, kv_dtype):
    return [pltpu.VMEM((tile, D_MODEL), BF16),
            pltpu.VMEM((tile, POOL_DIM), F32),
            pltpu.VMEM((N_PAIRS, tile, PAIR_W), BF16),
            pltpu.VMEM((N_PAIRS, tile, PAIR_W), kv_dtype),
            pltpu.VMEM((N_PAIRS, tile, PAIR_W), kv_dtype),
            pltpu.VMEM((N_PAIRS, tile, PAIR_W), BF16),
            pltpu.VMEM((tile, POOL_DIM), BF16)]


def _weight_specs():
    return [_const_spec((1, D_MODEL)),
            _const_spec((D_MODEL, IN_DIM)),
            _const_spec((len(POOL_WINDOWS), POOL_GROUP_DIM, POOL_GROUP_DIM)),
            _const_spec((1, POOL_DIM)),
            _const_spec((POOL_DIM, D_MODEL)),
            _const_spec((ATTN_DIM, D_MODEL)),
            _const_spec((D_MODEL, D_MODEL))]


def _ctx_mixer(x, modv, weights):
    n = x.shape[0]
    nseq = TOK_BLOCK // SEQ
    cache = jax.ShapeDtypeStruct((n // SEQ, 1, N_HEADS, SEQ, HEAD_DIM), F32)
    cache_spec = pl.BlockSpec((nseq, 1, N_HEADS, SEQ, HEAD_DIM), lambda i: (i, 0, 0, 0, 0))
    return pl.pallas_call(
        _ctx_mixer_kernel,
        grid=(n // TOK_BLOCK,),
        in_specs=[pl.BlockSpec((TOK_BLOCK, D_MODEL), lambda i: (i, 0)),
                  pl.BlockSpec((1, N_MOD, D_MODEL), lambda i: (0, 0, 0))] + _weight_specs(),
        out_specs=[pl.BlockSpec((TOK_BLOCK, D_MODEL), lambda i: (i, 0)), cache_spec, cache_spec],
        out_shape=[jax.ShapeDtypeStruct((n, D_MODEL), F32), cache, cache],
        scratch_shapes=_mixer_scratch(TOK_BLOCK, F32),
        compiler_params=pltpu.CompilerParams(dimension_semantics=("arbitrary",),
                                             vmem_limit_bytes=V7X_VMEM_LIMIT),
        name="ctx_mixer",
    )(x, modv, *weights)


def _lat_mixer(x, modv, weights, kc, vc, tb):
    n = x.shape[0]
    return pl.pallas_call(
        _lat_mixer_kernel,
        grid=(n // DEC_SEQ,),
        in_specs=[pl.BlockSpec((DEC_SEQ, D_MODEL), lambda i: (i, 0)),
                  pl.BlockSpec((1, N_MOD, D_MODEL), lambda i: (i + 1, 0, 0))] + _weight_specs() + [
                  pl.BlockSpec((1, N_PAIRS, PAST_LEN, PAIR_W), lambda i: (i, 0, 0, 0)),
                  pl.BlockSpec((1, N_PAIRS, PAST_LEN, PAIR_W), lambda i: (i, 0, 0, 0)),
                  _const_spec((N_HEADS, 2, GRID_W, ROWS * GRID_W))],
        out_specs=pl.BlockSpec((DEC_SEQ, D_MODEL), lambda i: (i, 0)),
        out_shape=jax.ShapeDtypeStruct((n, D_MODEL), F32),
        scratch_shapes=_mixer_scratch(DEC_SEQ, BF16),
        compiler_params=pltpu.CompilerParams(dimension_semantics=("arbitrary",),
                                             vmem_limit_bytes=V7X_VMEM_LIMIT),
        name="lat_mixer",
    )(x, modv, *weights, kc, vc, tb)


N_RPB_ROWS = 2 * WIN_R - 1
N_RPB_COLS = 2 * WIN_C - 1
TABLE_W = ROWS * GRID_W


def _bias_kernel(v_ref, keep_ref, o_ref):
    for h in range(N_HEADS):
        for par in range(2):
            x = jnp.broadcast_to(v_ref[h, par:par + 1, :], (GRID_W, TABLE_W))
            shifted = pltpu.roll(x, TABLE_W - (WIN_C - 1), 1, stride=1, stride_axis=0)
            o_ref[h, par] = jnp.where(keep_ref[par] > 0.0, shifted, NEG_INF)


def _bias_tables(rpb):
    col = np.arange(GRID_W)
    cs = np.clip(col - WIN_C // 2, 0, GRID_W - WIN_C)
    in_win = (col[None, :] >= cs[:, None]) & (col[None, :] < cs[:, None] + WIN_C)
    keep = np.tile(in_win.astype(np.float32), (2, 1, ROWS))
    keep[0, :, N_RPB_ROWS * GRID_W:] = 0.0
    keep[1, :, (N_RPB_ROWS - 1) * GRID_W:] = 0.0
    rp = jnp.pad(rpb.astype(F32), ((0, 0), (0, ROWS + 1 - N_RPB_ROWS), (0, GRID_W - N_RPB_COLS)))
    v = jnp.stack([rp[:, :ROWS].reshape(N_HEADS, TABLE_W), rp[:, 1:].reshape(N_HEADS, TABLE_W)], axis=1)
    return pl.pallas_call(
        _bias_kernel,
        out_shape=jax.ShapeDtypeStruct((N_HEADS, 2, GRID_W, TABLE_W), F32),
        name="bias_tables",
    )(v, jnp.asarray(keep))


SORT_ROWS = 512


def _sort_kernel(xc_ref, xl_ref, mod_ref, gffn_ref, wr_ref, br_ref,
                 xs_ref, dk_ref, np_ref, off_ref):
    j = pl.program_id(0)
    x = jnp.where(j < _N_CTX_BLOCKS, xc_ref[...], xl_ref[...])
    shift, scale = mod_ref[0, 3:4, :], mod_ref[0, 4:5, :]
    hb = _norm_mod(x, gffn_ref[...], scale, shift).astype(BF16)
    logits = _dot_nt(wr_ref[...], hb) + br_ref[...]
    eio = lax.broadcasted_iota(jnp.int32, logits.shape, 0)
    work = logits
    sels, vals = [], []
    for _ in range(TOP_K):
        m = jnp.max(work, axis=0, keepdims=True)
        idx = jnp.min(jnp.where(work == m, eio, N_EXPERTS), axis=0, keepdims=True)
        sel = eio == idx
        sels.append(sel)
        vals.append(m)
        work = jnp.where(sel, -jnp.inf, work)
    exps = [jnp.exp(v - vals[0]) for v in vals]
    den = exps[0] + exps[1] + exps[2] + exps[3]
    mask = jnp.zeros(logits.shape, F32)
    for sel in sels:
        mask = mask + jnp.where(sel, 1.0, 0.0)
    t_row = lax.broadcasted_iota(jnp.int32, (TOK_BLOCK, TOK_BLOCK), 0)
    t_col = lax.broadcasted_iota(jnp.int32, (TOK_BLOCK, TOK_BLOCK), 1)
    before = jnp.where(t_row < t_col, 1.0, 0.0).astype(BF16)
    rank = _dot(mask.astype(BF16), before)
    cnt = jnp.sum(mask, axis=1, keepdims=True)
    np16 = jnp.floor((cnt + (CHUNK - 1.0)) * (1.0 / CHUNK))
    e_row = lax.broadcasted_iota(jnp.int32, (N_EXPERTS, N_EXPERTS), 0)
    e_col = lax.broadcasted_iota(jnp.int32, (N_EXPERTS, N_EXPERTS), 1)
    lower = jnp.where(e_col < e_row, 1.0, 0.0).astype(BF16)
    np16_b = jnp.broadcast_to(np16, (N_EXPERTS, 128))
    off16 = _dot(lower, np16_b.astype(BF16))
    np_ref[0] = np16_b
    off_ref[0] = off16
    dest = off16[:, 0:1] * float(CHUNK) + rank
    dests = []
    for k in range(TOP_K):
        dk = jnp.sum(jnp.where(sels[k], dest, 0.0), axis=0, keepdims=True)
        dk_ref[0, k:k + 1, :] = dk
        dk_ref[0, TOP_K + k:TOP_K + k + 1, :] = exps[k] / den
        dests.append(dk.astype(jnp.int32))
    rio = lax.broadcasted_iota(jnp.int32, (SORT_ROWS, TOK_BLOCK), 0)
    for c in range(BLOCK_CAP // SORT_ROWS):
        onehot = jnp.zeros((SORT_ROWS, TOK_BLOCK), F32)
        for dk in dests:
            onehot = jnp.where(rio == dk - c * SORT_ROWS, 1.0, onehot)
        xs_ref[c * SORT_ROWS:(c + 1) * SORT_ROWS, :] = _dot(onehot.astype(BF16), hb).astype(BF16)


def _block_mod_index(j):
    return jnp.where(j < _N_CTX_BLOCKS, 0, 1 + (j - _N_CTX_BLOCKS) // (DEC_SEQ // TOK_BLOCK))


def _token_specs():
    return [pl.BlockSpec((TOK_BLOCK, D_MODEL), lambda j: (jnp.minimum(j, _N_CTX_BLOCKS - 1), 0)),
            pl.BlockSpec((TOK_BLOCK, D_MODEL), lambda j: (jnp.maximum(j - _N_CTX_BLOCKS, 0), 0)),
            pl.BlockSpec((1, N_MOD, D_MODEL), lambda j: (_block_mod_index(j), 0, 0))]


def _route_sort(x1c, x1l, modv, g_ffn, wr_t, br):
    tbl = jax.ShapeDtypeStruct((_N_BLOCKS, N_EXPERTS, 128), F32)
    tbl_spec = pl.BlockSpec((1, N_EXPERTS, 128), lambda j: (j, 0, 0))
    return pl.pallas_call(
        _sort_kernel,
        grid=(_N_BLOCKS,),
        in_specs=_token_specs() + [_const_spec((1, D_MODEL)),
                                   _const_spec((N_EXPERTS, D_MODEL)),
                                   _const_spec((N_EXPERTS, 1))],
        out_specs=[pl.BlockSpec((BLOCK_CAP, D_MODEL), lambda j: (j, 0)),
                   pl.BlockSpec((1, 2 * TOP_K, TOK_BLOCK), lambda j: (j, 0, 0)),
                   tbl_spec, tbl_spec],
        out_shape=[jax.ShapeDtypeStruct((_N_BLOCKS * BLOCK_CAP, D_MODEL), BF16),
                   jax.ShapeDtypeStruct((_N_BLOCKS, 2 * TOP_K, TOK_BLOCK), F32),
                   tbl, tbl],
        compiler_params=pltpu.CompilerParams(dimension_semantics=("arbitrary",),
                                             vmem_limit_bytes=V7X_VMEM_LIMIT),
        name="route_sort",
    )(x1c, x1l, modv, g_ffn, wr_t, br)


EXPERT_BUF_ROWS = 2048
PASS_CHUNKS = EXPERT_BUF_ROWS // CHUNK
N_ROW_BUFS = 3


def _expert_kernel(np_ref, off_ref, wu_ref, bu_ref, wd_ref, bd_ref, xs_hbm, ys_hbm,
                   buf, wu_s, wd_s, gsem, ssem, pend):
    del xs_hbm
    e = pl.program_id(0)
    last = pl.num_programs(0) - 1
    slot = e % N_ROW_BUFS
    nxt = (e + 1) % N_ROW_BUFS

    def move_chunks(ex, q_lo, slot_, gather):
        def block_body(j, q0):
            n = np_ref[j * N_EXPERTS + ex]
            off = off_ref[j * N_EXPERTS + ex]
            c_lo = jnp.clip(q_lo - q0, 0, n)
            m = jnp.clip(q_lo + PASS_CHUNKS - q0, 0, n) - c_lo

            @pl.when(m > 0)
            def _():
                rows = pl.multiple_of(m * CHUNK, CHUNK)
                row0 = pl.multiple_of(j * BLOCK_CAP + (off + c_lo) * CHUNK, CHUNK)
                brow0 = pl.multiple_of((q0 + c_lo - q_lo) * CHUNK, CHUNK)
                hbm = ys_hbm.at[pl.ds(row0, rows), :]
                vm = buf.at[slot_, pl.ds(brow0, rows), :]
                if gather:
                    pltpu.make_async_copy(hbm, vm, gsem.at[slot_]).start()
                else:
                    pltpu.make_async_copy(vm, hbm, ssem.at[slot_]).start()

            return q0 + n

        lax.fori_loop(0, _N_BLOCKS, block_body, 0)

    def wait_chunks(sem, n, slot_):
        @pl.when(n > 0)
        def _():
            rows = pl.multiple_of(n * CHUNK, CHUNK)
            pltpu.make_async_copy(ys_hbm.at[pl.ds(0, rows), :], buf.at[slot_, pl.ds(0, rows), :], sem).wait()

    def mlp_rows(r0, rows):
        x = buf[slot, pl.ds(r0, rows), :]
        gu = _dot(x, wu_s[...]) + bu_ref[0]
        gate = jnp.minimum(gu[:, :D_FF], SWIGLU_LIMIT)
        up = jnp.clip(gu[:, D_FF:], -SWIGLU_LIMIT, SWIGLU_LIMIT)
        glu = gate * _sigmoid(SWIGLU_ALPHA * gate)
        y = _dot(((up + 1.0) * glu).astype(BF16), wd_s[...]) + bd_ref[0]
        buf[slot, pl.ds(r0, rows), :] = y.astype(BF16)

    def compute(n):
        rows = (n * CHUNK + ROW_STEP - 1) // ROW_STEP * ROW_STEP
        n_main = (jnp.maximum(rows - LAST_ROWS_MAX, 0) + EXPERT_ROW_TILE - 1) // EXPERT_ROW_TILE

        def tile_body(t, carry):
            mlp_rows(pl.multiple_of(t * EXPERT_ROW_TILE, ROW_STEP), EXPERT_ROW_TILE)
            return carry

        lax.fori_loop(0, n_main, tile_body, 0)
        r0 = pl.multiple_of(n_main * EXPERT_ROW_TILE, ROW_STEP)
        for size in range(ROW_STEP, LAST_ROWS_MAX + ROW_STEP, ROW_STEP):
            pl.when(rows - r0 == size)(functools.partial(mlp_rows, r0, size))

    def chunks_of(ex):
        return lax.fori_loop(0, _N_BLOCKS, lambda j, acc: acc + np_ref[j * N_EXPERTS + ex], 0)

    @pl.when(e == 0)
    def _():
        buf[...] = jnp.zeros_like(buf)
        for b in range(N_ROW_BUFS):
            pend[b] = 0
        move_chunks(0, 0, 0, True)

    wait_chunks(ssem.at[nxt], pend[nxt], nxt)
    pend[nxt] = 0

    @pl.when(e < last)
    def _():
        move_chunks(e + 1, 0, nxt, True)

    wu_s[...] = wu_ref[0].astype(BF16)
    wd_s[...] = wd_ref[0].astype(BF16)

    total = chunks_of(e)
    n0 = jnp.minimum(total, PASS_CHUNKS)
    wait_chunks(gsem.at[slot], n0, slot)
    compute(n0)
    move_chunks(e, 0, slot, False)
    pend[slot] = n0

    def pass_body(p, carry):
        wait_chunks(ssem.at[slot], pend[slot], slot)
        lo = p * PASS_CHUNKS
        n = jnp.minimum(total - lo, PASS_CHUNKS)
        move_chunks(e, lo, slot, True)
        wait_chunks(gsem.at[slot], n, slot)
        compute(n)
        move_chunks(e, lo, slot, False)
        pend[slot] = n
        return carry

    lax.fori_loop(1, (total + PASS_CHUNKS - 1) // PASS_CHUNKS, pass_body, 0)

    @pl.when(e == last)
    def _():
        for b in range(N_ROW_BUFS):
            wait_chunks(ssem.at[b], pend[b], b)
            pend[b] = 0


def _experts(np16, off16, w_up, b_up, w_down, b_down, xs):
    grid_spec = pltpu.PrefetchScalarGridSpec(
        num_scalar_prefetch=2,
        grid=(N_EXPERTS,),
        in_specs=[pl.BlockSpec((1, D_MODEL, 2 * D_FF), lambda e, *_: (e, 0, 0)),
                  pl.BlockSpec((1, 1, 2 * D_FF), lambda e, *_: (e, 0, 0)),
                  pl.BlockSpec((1, D_FF, D_MODEL), lambda e, *_: (e, 0, 0)),
                  pl.BlockSpec((1, 1, D_MODEL), lambda e, *_: (e, 0, 0)),
                  pl.BlockSpec(memory_space=pl.ANY)],
        out_specs=pl.BlockSpec(memory_space=pl.ANY),
        scratch_shapes=[pltpu.VMEM((N_ROW_BUFS, EXPERT_BUF_ROWS, D_MODEL), BF16),
                        pltpu.VMEM((D_MODEL, 2 * D_FF), BF16),
                        pltpu.VMEM((D_FF, D_MODEL), BF16),
                        pltpu.SemaphoreType.DMA((N_ROW_BUFS,)),
                        pltpu.SemaphoreType.DMA((N_ROW_BUFS,)),
                        pltpu.SMEM((N_ROW_BUFS,), jnp.int32)],
    )
    return pl.pallas_call(
        _expert_kernel,
        grid_spec=grid_spec,
        out_shape=jax.ShapeDtypeStruct(xs.shape, xs.dtype),
        input_output_aliases={6: 0},
        compiler_params=pltpu.CompilerParams(dimension_semantics=("arbitrary",),
                                             vmem_limit_bytes=V7X_VMEM_LIMIT),
        name="experts",
    )(np16, off16, w_up, b_up.reshape(N_EXPERTS, 1, 2 * D_FF), w_down,
      b_down.reshape(N_EXPERTS, 1, D_MODEL), xs)


def _combine_kernel(ys_ref, dk_ref, xc_ref, xl_ref, mod_ref, gfin_ref, yc_ref, yl_ref, y_s):
    j = pl.program_id(0)
    half = TOK_BLOCK // 2
    cio = lax.broadcasted_iota(jnp.int32, (half, SORT_ROWS), 1).astype(F32)
    halves = [slice(h * half, (h + 1) * half) for h in range(2)]
    accs = [jnp.zeros((half, D_MODEL), F32) for _ in halves]
    for c in range(BLOCK_CAP // SORT_ROWS):
        ys = ys_ref[c * SORT_ROWS:(c + 1) * SORT_ROWS, :]
        for h, rows in enumerate(halves):
            d = dk_ref[0, rows, :]
            w = jnp.zeros((half, SORT_ROWS), F32)
            for k in range(TOP_K):
                w = jnp.where(cio == d[:, k:k + 1] - float(c * SORT_ROWS), d[:, TOP_K + k:TOP_K + k + 1], w)
            accs[h] = accs[h] + _dot(w.astype(BF16), ys)
    for h, rows in enumerate(halves):
        acc = accs[h]
        x1 = jnp.where(j < _N_CTX_BLOCKS, xc_ref[rows, :], xl_ref[rows, :])
        x2 = x1 + mod_ref[0, 5:6, :] * acc
        ms = jnp.mean(x2 * x2, axis=-1, keepdims=True)
        y_s[rows, :] = x2 * lax.rsqrt(ms + RMS_EPS) * gfin_ref[...]

    @pl.when(j < _N_CTX_BLOCKS)
    def _():
        yc_ref[...] = y_s[...]

    @pl.when(j >= _N_CTX_BLOCKS)
    def _():
        yl_ref[...] = y_s[...]


def _combine(ys, dk_t, x1c, x1l, modv, g_final):
    return pl.pallas_call(
        _combine_kernel,
        grid=(_N_BLOCKS,),
        in_specs=[pl.BlockSpec((BLOCK_CAP, D_MODEL), lambda j: (j, 0)),
                  pl.BlockSpec((1, TOK_BLOCK, 2 * TOP_K), lambda j: (j, 0, 0))] + _token_specs() + [
                  _const_spec((1, D_MODEL))],
        out_specs=[pl.BlockSpec((TOK_BLOCK, D_MODEL), lambda j: (jnp.minimum(j, _N_CTX_BLOCKS - 1), 0)),
                   pl.BlockSpec((TOK_BLOCK, D_MODEL), lambda j: (jnp.maximum(j - _N_CTX_BLOCKS, 0), 0))],
        out_shape=[jax.ShapeDtypeStruct(x1c.shape, F32), jax.ShapeDtypeStruct(x1l.shape, F32)],
        scratch_shapes=[pltpu.VMEM((TOK_BLOCK, D_MODEL), F32)],
        compiler_params=pltpu.CompilerParams(dimension_semantics=("arbitrary",),
                                             vmem_limit_bytes=V7X_VMEM_LIMIT),
        name="combine",
    )(ys, dk_t, x1c, x1l, modv, g_final)


def kernel(x_prompt, x_sample, cache_k, cache_v, c, c_ctx, w_ada, b_ada, g_mix, w_in, w_pool, pool_scale,
           w_pa, w_pb, rpb, w_out, g_ffn, w_router, b_router, w_up, b_up, w_down, b_down, g_final):
    assert w_ada.shape[0] == 1, "single trunk layer"
    batch, seq, d = x_prompt.shape
    dec_batch, dec_seq, _ = x_sample.shape
    assert (seq, dec_seq, d) == (SEQ, DEC_SEQ, D_MODEL)
    assert batch * seq == _N_CTX_BLOCKS * TOK_BLOCK and dec_batch * dec_seq == _N_LAT_BLOCKS * TOK_BLOCK

    cmat = jnp.concatenate([c_ctx[None, :], c, jnp.zeros((8 - 1 - dec_batch, d), F32)], axis=0)
    modv = _modulation(cmat, w_ada[0], b_ada[0]).reshape(8, N_MOD, d)

    weights = (g_mix[0][None, :], w_in[0].astype(BF16), w_pool[0].astype(BF16), pool_scale[0][None, :],
               w_pa[0].astype(BF16), w_pb[0].astype(BF16), w_out[0].astype(BF16))

    def by_pair(cache):
        z = cache[:, 0].reshape(dec_batch, N_PAIRS, 2, PAST_LEN, HEAD_DIM)
        return z.transpose(0, 1, 3, 2, 4).reshape(dec_batch, N_PAIRS, PAST_LEN, PAIR_W)

    x1c, new_k, new_v = _ctx_mixer(x_prompt.reshape(batch * seq, d), modv, weights)
    x1l = _lat_mixer(x_sample.reshape(dec_batch * dec_seq, d), modv, weights,
                     by_pair(cache_k), by_pair(cache_v), _bias_tables(rpb[0]))

    xs, dk, np16, off16 = _route_sort(x1c, x1l, modv, g_ffn[0][None, :],
                                      w_router[0].T.astype(BF16), b_router[0][:, None])
    np16_i = np16[:, :, 0].astype(jnp.int32).reshape(-1)
    off16_i = off16[:, :, 0].astype(jnp.int32).reshape(-1)
    ys = _experts(np16_i, off16_i, w_up[0], b_up[0], w_down[0], b_down[0], xs)
    yc, yl = _combine(ys, dk.transpose(0, 2, 1), x1c, x1l, modv, g_final[None, :])
    return (yc.reshape(batch, seq, d), yl.reshape(dec_batch, dec_seq, d), new_k, new_v)
```

```python
import functools

import jax
import jax.numpy as jnp
import numpy as np
from jax import lax
from jax.experimental import pallas as pl
from jax.experimental.pallas import tpu as pltpu

F32 = jnp.float32
BF16 = jnp.bfloat16

D_MODEL = 1024
SEQ = 256
DEC_SEQ = 1024
GRID_W = 64
ROWS = DEC_SEQ // GRID_W
N_HEADS = 8
HEAD_DIM = 64
N_PAIRS = N_HEADS // 2
PAIR_W = 2 * HEAD_DIM
PAST_LEN = 512
POOL_DIM = 512
POOL_WINDOWS = (2, 4, 8, 16)
POOL_GROUP_DIM = 128
ATTN_DIM = 512
WIN_R = 8
WIN_C = 16
N_EXPERTS = 32
TOP_K = 4
D_FF = 1024
SWIGLU_LIMIT = 7.0
SWIGLU_ALPHA = 1.702
N_MOD = 6
RMS_EPS = 1e-6
NEG_INF = -1e30
ATTN_SCALE = HEAD_DIM ** -0.5
IN_DIM = POOL_DIM + 3 * ATTN_DIM + 2 * D_MODEL

TOK_BLOCK = 512
CHUNK = 16
BLOCK_CAP = TOK_BLOCK * TOP_K + N_EXPERTS * CHUNK
ROW_STEP = 128
EXPERT_ROW_TILE = 384
LAST_ROWS_MAX = 512
V7X_VMEM_LIMIT = 60 * 1024 * 1024

_N_CTX_BLOCKS = 8
_N_LAT_BLOCKS = 4
_N_BLOCKS = _N_CTX_BLOCKS + _N_LAT_BLOCKS

_NT = (((1,), (1,)), ((), ()))


def _dot(a, b):
    return jnp.dot(a, b, preferred_element_type=F32)


def _dot_nt(a, b):
    return lax.dot_general(a, b, _NT, preferred_element_type=F32)


def _sigmoid(x):
    return 1.0 / (1.0 + jnp.exp(-x))


def _norm_mod(x, gain, scale, shift):
    ms = jnp.mean(x * x, axis=-1, keepdims=True)
    return (x * lax.rsqrt(ms + RMS_EPS) * gain) * (1.0 + scale) + shift


def _const_spec(shape):
    zeros = (0,) * len(shape)
    return pl.BlockSpec(shape, lambda *_: zeros, pipeline_mode=pl.Buffered(1))


MOD_COLS = 768


def _mod_kernel(c_ref, w_ref, b_ref, o_ref):
    c = c_ref[...]
    s = c * _sigmoid(c)
    s_hi = s.astype(BF16)
    s_lo = (s - s_hi.astype(F32)).astype(BF16)
    r = _dot(jnp.concatenate([s_hi, s_lo], axis=0), w_ref[...].astype(BF16))
    o_ref[...] = r[:8] + r[8:] + b_ref[...]


def _modulation(cmat, w_ada, b_ada):
    n = w_ada.shape[1]
    return pl.pallas_call(
        _mod_kernel,
        grid=(n // MOD_COLS,),
        in_specs=[pl.BlockSpec((8, D_MODEL), lambda i: (0, 0)),
                  pl.BlockSpec((D_MODEL, MOD_COLS), lambda i: (0, i)),
                  pl.BlockSpec((1, MOD_COLS), lambda i: (0, i))],
        out_specs=pl.BlockSpec((8, MOD_COLS), lambda i: (0, i)),
        out_shape=jax.ShapeDtypeStruct((8, n), F32),
        name="modulation",
    )(cmat, w_ada, b_ada.reshape(1, n))


def _pool_mix(u, pos, seq):
    n = u.shape[0]

    def down(x, d):
        return jnp.where(pos >= d, pltpu.roll(x, d, 0), 0.0)

    def up(x, d):
        return jnp.where(pos < seq - d, pltpu.roll(x, n - d, 0), 0.0)

    return down, up


def _pool_group(u, pos, seq, w):
    down, up = _pool_mix(u, pos, seq)
    hw = w // 2
    back = u
    fwd = u
    d = 1
    while d < hw:
        back = back + down(back, d)
        fwd = fwd + up(fwd, d)
        d *= 2
    s = down(back, 1) + fwd
    posf = pos.astype(F32)
    cnt = jnp.minimum(posf + hw, float(seq)) - jnp.maximum(posf - hw, 0.0)
    return s / cnt - u


def _pass_rows(tile):
    return tile if tile <= TOK_BLOCK else tile // 4


def _mixer_front(x_ref, mod_ref, gmix_ref, win_ref, hb_s, u_s, q_s, k_s, v_s):
    shift, scale = mod_ref[0, 0:1, :], mod_ref[0, 1:2, :]
    step = _pass_rows(x_ref.shape[0])
    for c in range(x_ref.shape[0] // step):
        rows = slice(c * step, (c + 1) * step)
        hb = _norm_mod(x_ref[rows, :], gmix_ref[...], scale, shift).astype(BF16)
        hb_s[rows, :] = hb
        u_s[rows, :] = _dot(hb, win_ref[:, 0:POOL_DIM])
        for dst, base in ((q_s, POOL_DIM), (k_s, POOL_DIM + ATTN_DIM), (v_s, POOL_DIM + 2 * ATTN_DIM)):
            z = _dot(hb, win_ref[:, base:base + ATTN_DIM])
            if dst is q_s:
                z = z * ATTN_SCALE
            for g in range(N_PAIRS):
                dst[g, rows, :] = z[:, g * PAIR_W:(g + 1) * PAIR_W].astype(dst.dtype)


def _mixer_back(x_ref, mod_ref, win_ref, wpool_ref, ps_ref, wpa_ref, wpb_ref, wout_ref, x1_ref,
                hb_s, u_s, o_s, pg_s, seq):
    tile = x_ref.shape[0]
    gate = mod_ref[0, 2:3, :]
    pos = lax.broadcasted_iota(jnp.int32, (tile, 1), 0) % seq
    for g, w in enumerate(POOL_WINDOWS):
        cols = slice(g * POOL_GROUP_DIM, (g + 1) * POOL_GROUP_DIM)
        pg_s[:, cols] = _pool_group(u_s[:, cols], pos, seq, w).astype(BF16)
    step = _pass_rows(tile)
    for c in range(tile // step):
        rows = slice(c * step, (c + 1) * step)
        ys = []
        for g in range(len(POOL_WINDOWS)):
            cols = slice(g * POOL_GROUP_DIM, (g + 1) * POOL_GROUP_DIM)
            ys.append((_dot(pg_s[rows, cols], wpool_ref[g]) * ps_ref[:, cols]).astype(BF16))
        a = _dot(jnp.concatenate(ys, axis=1), wpa_ref[...])
        ob = _dot(jnp.concatenate([o_s[g, rows, :] for g in range(N_PAIRS)], axis=1), wpb_ref[...])
        gab = _dot(hb_s[rows, :], win_ref[:, POOL_DIM + 3 * ATTN_DIM:IN_DIM])
        merged = _sigmoid(gab[:, :D_MODEL]) * a + _sigmoid(gab[:, D_MODEL:]) * ob
        mix = _dot(merged.astype(BF16), wout_ref[...])
        x1_ref[rows, :] = x_ref[rows, :] + gate * mix


def _ctx_mixer_kernel(x_ref, mod_ref, gmix_ref, win_ref, wpool_ref, ps_ref, wpa_ref, wpb_ref, wout_ref,
                      x1_ref, ko_ref, vo_ref, hb_s, u_s, q_s, k_s, v_s, o_s, pg_s):
    _mixer_front(x_ref, mod_ref, gmix_ref, win_ref, hb_s, u_s, q_s, k_s, v_s)
    even = lax.broadcasted_iota(jnp.int32, (1, PAIR_W), 1) < HEAD_DIM
    tile = x_ref.shape[0]
    for s in range(tile // SEQ):
        rows = slice(s * SEQ, (s + 1) * SEQ)
        for g in range(N_PAIRS):
            q2, k2, v2 = q_s[g, rows, :], k_s[g, rows, :], v_s[g, rows, :]
            ko_ref[s, 0, 2 * g] = k2[:, :HEAD_DIM]
            ko_ref[s, 0, 2 * g + 1] = k2[:, HEAD_DIM:]
            vo_ref[s, 0, 2 * g] = v2[:, :HEAD_DIM]
            vo_ref[s, 0, 2 * g + 1] = v2[:, HEAD_DIM:]
            kb, vb = k2.astype(BF16), v2.astype(BF16)
            outs = []
            for par in range(2):
                qm = jnp.where(even if par == 0 else jnp.logical_not(even), q2, 0.0).astype(BF16)
                sc = _dot_nt(qm, kb)
                m = jnp.max(sc, axis=-1, keepdims=True)
                p = jnp.exp(sc - m)
                l = jnp.sum(p, axis=-1, keepdims=True)
                outs.append(_dot(p.astype(BF16), vb) / l)
            o_s[g, rows, :] = jnp.where(even, outs[0], outs[1]).astype(BF16)
    _mixer_back(x_ref, mod_ref, win_ref, wpool_ref, ps_ref, wpa_ref, wpb_ref, wout_ref, x1_ref,
                hb_s, u_s, o_s, pg_s, SEQ)


def _row_window(r):
    rs = min(max(r - WIN_R // 2, 0), ROWS - WIN_R)
    return rs, rs - r + WIN_R - 1


def _lat_mixer_kernel(x_ref, mod_ref, gmix_ref, win_ref, wpool_ref, ps_ref, wpa_ref, wpb_ref, wout_ref,
                      kc_ref, vc_ref, tb_ref, x1_ref, hb_s, u_s, q_s, k_s, v_s, o_s, pg_s):
    _mixer_front(x_ref, mod_ref, gmix_ref, win_ref, hb_s, u_s, q_s, k_s, v_s)
    even = lax.broadcasted_iota(jnp.int32, (1, PAIR_W), 1) < HEAD_DIM
    nk = WIN_R * GRID_W

    def pair_body(g, carry):
        q2 = q_s[g]
        kb, vb = k_s[g].astype(BF16), v_s[g].astype(BF16)
        kcb, vcb = kc_ref[0, g].astype(BF16), vc_ref[0, g].astype(BF16)
        outs = []
        for par in range(2):
            qm = jnp.where(even if par == 0 else jnp.logical_not(even), q2, 0.0).astype(BF16)
            s_ctx = _dot_nt(qm, kcb)
            slabs = []
            for r in range(ROWS):
                rs, rho = _row_window(r)
                bias = tb_ref[2 * g + par, rho % 2, :, (rho - rho % 2) * GRID_W:(rho - rho % 2) * GRID_W + nk]
                sl = _dot_nt(qm[r * GRID_W:(r + 1) * GRID_W, :], kb[rs * GRID_W:rs * GRID_W + nk, :])
                slabs.append(sl + bias)
            s_loc = jnp.concatenate(slabs, axis=0)
            m = jnp.maximum(jnp.max(s_loc, axis=-1, keepdims=True), jnp.max(s_ctx, axis=-1, keepdims=True))
            p_loc = jnp.exp(s_loc - m)
            p_ctx = jnp.exp(s_ctx - m)
            l = jnp.sum(p_loc, axis=-1, keepdims=True) + jnp.sum(p_ctx, axis=-1, keepdims=True)
            p_locb = p_loc.astype(BF16)
            o_rows = []
            for r in range(ROWS):
                rs, _ = _row_window(r)
                o_rows.append(_dot(p_locb[r * GRID_W:(r + 1) * GRID_W, :], vb[rs * GRID_W:rs * GRID_W + nk, :]))
            o = jnp.concatenate(o_rows, axis=0) + _dot(p_ctx.astype(BF16), vcb)
            outs.append(o / l)
        o_s[g] = jnp.where(even, outs[0], outs[1]).astype(BF16)
        return carry

    lax.fori_loop(0, N_PAIRS, pair_body, 0)
    _mixer_back(x_ref, mod_ref, win_ref, wpool_ref, ps_ref, wpa_ref, wpb_ref, wout_ref, x1_ref,
                hb_s, u_s, o_s, pg_s, DEC_SEQ)


def _mixer_scratch(tile, kv_dtype):
    return [pltpu.VMEM((tile, D_MODEL), BF16),
            pltpu.VMEM((tile, POOL_DIM), F32),
            pltpu.VMEM((N_PAIRS, tile, PAIR_W), BF16),
            pltpu.VMEM((N_PAIRS, tile, PAIR_W), kv_dtype),
            pltpu.VMEM((N_PAIRS, tile, PAIR_W), kv_dtype),
            pltpu.VMEM((N_PAIRS, tile, PAIR_W), BF16),
            pltpu.VMEM((tile, POOL_DIM), BF16)]


def _weight_specs():
    return [_const_spec((1, D_MODEL)),
            _const_spec((D_MODEL, IN_DIM)),
            _const_spec((len(POOL_WINDOWS), POOL_GROUP_DIM, POOL_GROUP_DIM)),
            _const_spec((1, POOL_DIM)),
            _const_spec((POOL_DIM, D_MODEL)),
            _const_spec((ATTN_DIM, D_MODEL)),
            _const_spec((D_MODEL, D_MODEL))]


def _ctx_mixer(x, modv, weights):
    n = x.shape[0]
    nseq = TOK_BLOCK // SEQ
    cache = jax.ShapeDtypeStruct((n // SEQ, 1, N_HEADS, SEQ, HEAD_DIM), F32)
    cache_spec = pl.BlockSpec((nseq, 1, N_HEADS, SEQ, HEAD_DIM), lambda i: (i, 0, 0, 0, 0))
    return pl.pallas_call(
        _ctx_mixer_kernel,
        grid=(n // TOK_BLOCK,),
        in_specs=[pl.BlockSpec((TOK_BLOCK, D_MODEL), lambda i: (i, 0)),
                  pl.BlockSpec((1, N_MOD, D_MODEL), lambda i: (0, 0, 0))] + _weight_specs(),
        out_specs=[pl.BlockSpec((TOK_BLOCK, D_MODEL), lambda i: (i, 0)), cache_spec, cache_spec],
        out_shape=[jax.ShapeDtypeStruct((n, D_MODEL), F32), cache, cache],
        scratch_shapes=_mixer_scratch(TOK_BLOCK, F32),
        compiler_params=pltpu.CompilerParams(dimension_semantics=("arbitrary",),
                                             vmem_limit_bytes=V7X_VMEM_LIMIT),
        name="ctx_mixer",
    )(x, modv, *weights)


def _lat_mixer(x, modv, weights, kc, vc, tb):
    n = x.shape[0]
    return pl.pallas_call(
        _lat_mixer_kernel,
        grid=(n // DEC_SEQ,),
        in_specs=[pl.BlockSpec((DEC_SEQ, D_MODEL), lambda i: (i, 0)),
                  pl.BlockSpec((1, N_MOD, D_MODEL), lambda i: (i + 1, 0, 0))] + _weight_specs() + [
                  pl.BlockSpec((1, N_PAIRS, PAST_LEN, PAIR_W), lambda i: (i, 0, 0, 0)),
                  pl.BlockSpec((1, N_PAIRS, PAST_LEN, PAIR_W), lambda i: (i, 0, 0, 0)),
                  _const_spec((N_HEADS, 2, GRID_W, ROWS * GRID_W))],
        out_specs=pl.BlockSpec((DEC_SEQ, D_MODEL), lambda i: (i, 0)),
        out_shape=jax.ShapeDtypeStruct((n, D_MODEL), F32),
        scratch_shapes=_mixer_scratch(DEC_SEQ, BF16),
        compiler_params=pltpu.CompilerParams(dimension_semantics=("arbitrary",),
                                             vmem_limit_bytes=V7X_VMEM_LIMIT),
        name="lat_mixer",
    )(x, modv, *weights, kc, vc, tb)


N_RPB_ROWS = 2 * WIN_R - 1
N_RPB_COLS = 2 * WIN_C - 1
TABLE_W = ROWS * GRID_W


def _bias_kernel(v_ref, keep_ref, o_ref):
    for h in range(N_HEADS):
        for par in range(2):
            x = jnp.broadcast_to(v_ref[h, par:par + 1, :], (GRID_W, TABLE_W))
            shifted = pltpu.roll(x, TABLE_W - (WIN_C - 1), 1, stride=1, stride_axis=0)
            o_ref[h, par] = jnp.where(keep_ref[par] > 0.0, shifted, NEG_INF)


def _bias_tables(rpb):
    col = np.arange(GRID_W)
    cs = np.clip(col - WIN_C // 2, 0, GRID_W - WIN_C)
    in_win = (col[None, :] >= cs[:, None]) & (col[None, :] < cs[:, None] + WIN_C)
    keep = np.tile(in_win.astype(np.float32), (2, 1, ROWS))
    keep[0, :, N_RPB_ROWS * GRID_W:] = 0.0
    keep[1, :, (N_RPB_ROWS - 1) * GRID_W:] = 0.0
    rp = jnp.pad(rpb.astype(F32), ((0, 0), (0, ROWS + 1 - N_RPB_ROWS), (0, GRID_W - N_RPB_COLS)))
    v = jnp.stack([rp[:, :ROWS].reshape(N_HEADS, TABLE_W), rp[:, 1:].reshape(N_HEADS, TABLE_W)], axis=1)
    return pl.pallas_call(
        _bias_kernel,
        out_shape=jax.ShapeDtypeStruct((N_HEADS, 2, GRID_W, TABLE_W), F32),
        name="bias_tables",
    )(v, jnp.asarray(keep))


SORT_ROWS = 512


def _sort_kernel(xc_ref, xl_ref, mod_ref, gffn_ref, wr_ref, br_ref,
                 xs_ref, dk_ref, np_ref, off_ref):
    j = pl.program_id(0)
    x = jnp.where(j < _N_CTX_BLOCKS, xc_ref[...], xl_ref[...])
    shift, scale = mod_ref[0, 3:4, :], mod_ref[0, 4:5, :]
    hb = _norm_mod(x, gffn_ref[...], scale, shift).astype(BF16)
    logits = _dot_nt(wr_ref[...], hb) + br_ref[...]
    eio = lax.broadcasted_iota(jnp.int32, logits.shape, 0)
    work = logits
    sels, vals = [], []
    for _ in range(TOP_K):
        m = jnp.max(work, axis=0, keepdims=True)
        idx = jnp.min(jnp.where(work == m, eio, N_EXPERTS), axis=0, keepdims=True)
        sel = eio == idx
        sels.append(sel)
        vals.append(m)
        work = jnp.where(sel, -jnp.inf, work)
    exps = [jnp.exp(v - vals[0]) for v in vals]
    den = exps[0] + exps[1] + exps[2] + exps[3]
    mask = jnp.zeros(logits.shape, F32)
    for sel in sels:
        mask = mask + jnp.where(sel, 1.0, 0.0)
    t_row = lax.broadcasted_iota(jnp.int32, (TOK_BLOCK, TOK_BLOCK), 0)
    t_col = lax.broadcasted_iota(jnp.int32, (TOK_BLOCK, TOK_BLOCK), 1)
    before = jnp.where(t_row < t_col, 1.0, 0.0).astype(BF16)
    rank = _dot(mask.astype(BF16), before)
    cnt = jnp.sum(mask, axis=1, keepdims=True)
    np16 = jnp.floor((cnt + (CHUNK - 1.0)) * (1.0 / CHUNK))
    e_row = lax.broadcasted_iota(jnp.int32, (N_EXPERTS, N_EXPERTS), 0)
    e_col = lax.broadcasted_iota(jnp.int32, (N_EXPERTS, N_EXPERTS), 1)
    lower = jnp.where(e_col < e_row, 1.0, 0.0).astype(BF16)
    np16_b = jnp.broadcast_to(np16, (N_EXPERTS, 128))
    off16 = _dot(lower, np16_b.astype(BF16))
    np_ref[0] = np16_b
    off_ref[0] = off16
    dest = off16[:, 0:1] * float(CHUNK) + rank
    dests = []
    for k in range(TOP_K):
        dk = jnp.sum(jnp.where(sels[k], dest, 0.0), axis=0, keepdims=True)
        dk_ref[0, k:k + 1, :] = dk
        dk_ref[0, TOP_K + k:TOP_K + k + 1, :] = exps[k] / den
        dests.append(dk.astype(jnp.int32))
    rio = lax.broadcasted_iota(jnp.int32, (SORT_ROWS, TOK_BLOCK), 0)
    for c in range(BLOCK_CAP // SORT_ROWS):
        onehot = jnp.zeros((SORT_ROWS, TOK_BLOCK), F32)
        for dk in dests:
            onehot = jnp.where(rio == dk - c * SORT_ROWS, 1.0, onehot)
        xs_ref[c * SORT_ROWS:(c + 1) * SORT_ROWS, :] = _dot(onehot.astype(BF16), hb).astype(BF16)


def _block_mod_index(j):
    return jnp.where(j < _N_CTX_BLOCKS, 0, 1 + (j - _N_CTX_BLOCKS) // (DEC_SEQ // TOK_BLOCK))


def _token_specs():
    return [pl.BlockSpec((TOK_BLOCK, D_MODEL), lambda j: (jnp.minimum(j, _N_CTX_BLOCKS - 1), 0)),
            pl.BlockSpec((TOK_BLOCK, D_MODEL), lambda j: (jnp.maximum(j - _N_CTX_BLOCKS, 0), 0)),
            pl.BlockSpec((1, N_MOD, D_MODEL), lambda j: (_block_mod_index(j), 0, 0))]


def _route_sort(x1c, x1l, modv, g_ffn, wr_t, br):
    tbl = jax.ShapeDtypeStruct((_N_BLOCKS, N_EXPERTS, 128), F32)
    tbl_spec = pl.BlockSpec((1, N_EXPERTS, 128), lambda j: (j, 0, 0))
    return pl.pallas_call(
        _sort_kernel,
        grid=(_N_BLOCKS,),
        in_specs=_token_specs() + [_const_spec((1, D_MODEL)),
                                   _const_spec((N_EXPERTS, D_MODEL)),
                                   _const_spec((N_EXPERTS, 1))],
        out_specs=[pl.BlockSpec((BLOCK_CAP, D_MODEL), lambda j: (j, 0)),
                   pl.BlockSpec((1, 2 * TOP_K, TOK_BLOCK), lambda j: (j, 0, 0)),
                   tbl_spec, tbl_spec],
        out_shape=[jax.ShapeDtypeStruct((_N_BLOCKS * BLOCK_CAP, D_MODEL), BF16),
                   jax.ShapeDtypeStruct((_N_BLOCKS, 2 * TOP_K, TOK_BLOCK), F32),
                   tbl, tbl],
        compiler_params=pltpu.CompilerParams(dimension_semantics=("arbitrary",),
                                             vmem_limit_bytes=V7X_VMEM_LIMIT),
        name="route_sort",
    )(x1c, x1l, modv, g_ffn, wr_t, br)


EXPERT_BUF_ROWS = 2048
PASS_CHUNKS = EXPERT_BUF_ROWS // CHUNK
N_ROW_BUFS = 3


def _expert_kernel(np_ref, off_ref, wu_ref, bu_ref, wd_ref, bd_ref, xs_hbm, ys_hbm,
                   buf, wu_s, wd_s, gsem, ssem, pend):
    del xs_hbm
    e = pl.program_id(0)
    last = pl.num_programs(0) - 1
    slot = e % N_ROW_BUFS
    nxt = (e + 1) % N_ROW_BUFS

    def move_chunks(ex, q_lo, slot_, gather):
        def block_body(j, q0):
            n = np_ref[j * N_EXPERTS + ex]
            off = off_ref[j * N_EXPERTS + ex]
            c_lo = jnp.clip(q_lo - q0, 0, n)
            m = jnp.clip(q_lo + PASS_CHUNKS - q0, 0, n) - c_lo

            @pl.when(m > 0)
            def _():
                rows = pl.multiple_of(m * CHUNK, CHUNK)
                row0 = pl.multiple_of(j * BLOCK_CAP + (off + c_lo) * CHUNK, CHUNK)
                brow0 = pl.multiple_of((q0 + c_lo - q_lo) * CHUNK, CHUNK)
                hbm = ys_hbm.at[pl.ds(row0, rows), :]
                vm = buf.at[slot_, pl.ds(brow0, rows), :]
                if gather:
                    pltpu.make_async_copy(hbm, vm, gsem.at[slot_]).start()
                else:
                    pltpu.make_async_copy(vm, hbm, ssem.at[slot_]).start()

            return q0 + n

        lax.fori_loop(0, _N_BLOCKS, block_body, 0)

    def wait_chunks(sem, n, slot_):
        @pl.when(n > 0)
        def _():
            rows = pl.multiple_of(n * CHUNK, CHUNK)
            pltpu.make_async_copy(ys_hbm.at[pl.ds(0, rows), :], buf.at[slot_, pl.ds(0, rows), :], sem).wait()

    def mlp_rows(r0, rows):
        x = buf[slot, pl.ds(r0, rows), :]
        gu = _dot(x, wu_s[...]) + bu_ref[0]
        gate = jnp.minimum(gu[:, :D_FF], SWIGLU_LIMIT)
        up = jnp.clip(gu[:, D_FF:], -SWIGLU_LIMIT, SWIGLU_LIMIT)
        glu = gate * _sigmoid(SWIGLU_ALPHA * gate)
        y = _dot(((up + 1.0) * glu).astype(BF16), wd_s[...]) + bd_ref[0]
        buf[slot, pl.ds(r0, rows), :] = y.astype(BF16)

    def compute(n):
        rows = (n * CHUNK + ROW_STEP - 1) // ROW_STEP * ROW_STEP
        n_main = (jnp.maximum(rows - LAST_ROWS_MAX, 0) + EXPERT_ROW_TILE - 1) // EXPERT_ROW_TILE

        def tile_body(t, carry):
            mlp_rows(pl.multiple_of(t * EXPERT_ROW_TILE, ROW_STEP), EXPERT_ROW_TILE)
            return carry

        lax.fori_loop(0, n_main, tile_body, 0)
        r0 = pl.multiple_of(n_main * EXPERT_ROW_TILE, ROW_STEP)
        for size in range(ROW_STEP, LAST_ROWS_MAX + ROW_STEP, ROW_STEP):
            pl.when(rows - r0 == size)(functools.partial(mlp_rows, r0, size))

    def chunks_of(ex):
        return lax.fori_loop(0, _N_BLOCKS, lambda j, acc: acc + np_ref[j * N_EXPERTS + ex], 0)

    @pl.when(e == 0)
    def _():
        buf[...] = jnp.zeros_like(buf)
        for b in range(N_ROW_BUFS):
            pend[b] = 0
        move_chunks(0, 0, 0, True)

    wait_chunks(ssem.at[nxt], pend[nxt], nxt)
    pend[nxt] = 0

    @pl.when(e < last)
    def _():
        move_chunks(e + 1, 0, nxt, True)

    wu_s[...] = wu_ref[0].astype(BF16)
    wd_s[...] = wd_ref[0].astype(BF16)

    total = chunks_of(e)
    n0 = jnp.minimum(total, PASS_CHUNKS)
    wait_chunks(gsem.at[slot], n0, slot)
    compute(n0)
    move_chunks(e, 0, slot, False)
    pend[slot] = n0

    def pass_body(p, carry):
        wait_chunks(ssem.at[slot], pend[slot], slot)
        lo = p * PASS_CHUNKS
        n = jnp.minimum(total - lo, PASS_CHUNKS)
        move_chunks(e, lo, slot, True)
        wait_chunks(gsem.at[slot], n, slot)
        compute(n)
        move_chunks(e, lo, slot, False)
        pend[slot] = n
        return carry

    lax.fori_loop(1, (total + PASS_CHUNKS - 1) // PASS_CHUNKS, pass_body, 0)

    @pl.when(e == last)
    def _():
        for b in range(N_ROW_BUFS):
            wait_chunks(ssem.at[b], pend[b], b)
            pend[b] = 0


def _experts(np16, off16, w_up, b_up, w_down, b_down, xs):
    grid_spec = pltpu.PrefetchScalarGridSpec(
        num_scalar_prefetch=2,
        grid=(N_EXPERTS,),
        in_specs=[pl.BlockSpec((1, D_MODEL, 2 * D_FF), lambda e, *_: (e, 0, 0)),
                  pl.BlockSpec((1, 1, 2 * D_FF), lambda e, *_: (e, 0, 0)),
                  pl.BlockSpec((1, D_FF, D_MODEL), lambda e, *_: (e, 0, 0)),
                  pl.BlockSpec((1, 1, D_MODEL), lambda e, *_: (e, 0, 0)),
                  pl.BlockSpec(memory_space=pl.ANY)],
        out_specs=pl.BlockSpec(memory_space=pl.ANY),
        scratch_shapes=[pltpu.VMEM((N_ROW_BUFS, EXPERT_BUF_ROWS, D_MODEL), BF16),
                        pltpu.VMEM((D_MODEL, 2 * D_FF), BF16),
                        pltpu.VMEM((D_FF, D_MODEL), BF16),
                        pltpu.SemaphoreType.DMA((N_ROW_BUFS,)),
                        pltpu.SemaphoreType.DMA((N_ROW_BUFS,)),
                        pltpu.SMEM((N_ROW_BUFS,), jnp.int32)],
    )
    return pl.pallas_call(
        _expert_kernel,
        grid_spec=grid_spec,
        out_shape=jax.ShapeDtypeStruct(xs.shape, xs.dtype),
        input_output_aliases={6: 0},
        compiler_params=pltpu.CompilerParams(dimension_semantics=("arbitrary",),
                                             vmem_limit_bytes=V7X_VMEM_LIMIT),
        name="experts",
    )(np16, off16, w_up, b_up.reshape(N_EXPERTS, 1, 2 * D_FF), w_down,
      b_down.reshape(N_EXPERTS, 1, D_MODEL), xs)


def _combine_kernel(ys_ref, dk_ref, xc_ref, xl_ref, mod_ref, gfin_ref, yc_ref, yl_ref, y_s):
    j = pl.program_id(0)
    half = TOK_BLOCK // 2
    cio = lax.broadcasted_iota(jnp.int32, (half, SORT_ROWS), 1).astype(F32)
    halves = [slice(h * half, (h + 1) * half) for h in range(2)]
    accs = [jnp.zeros((half, D_MODEL), F32) for _ in halves]
    for c in range(BLOCK_CAP // SORT_ROWS):
        ys = ys_ref[c * SORT_ROWS:(c + 1) * SORT_ROWS, :]
        for h, rows in enumerate(halves):
            d = dk_ref[0, rows, :]
            w = jnp.zeros((half, SORT_ROWS), F32)
            for k in range(TOP_K):
                w = jnp.where(cio == d[:, k:k + 1] - float(c * SORT_ROWS), d[:, TOP_K + k:TOP_K + k + 1], w)
            accs[h] = accs[h] + _dot(w.astype(BF16), ys)
    for h, rows in enumerate(halves):
        acc = accs[h]
        x1 = jnp.where(j < _N_CTX_BLOCKS, xc_ref[rows, :], xl_ref[rows, :])
        x2 = x1 + mod_ref[0, 5:6, :] * acc
        ms = jnp.mean(x2 * x2, axis=-1, keepdims=True)
        y_s[rows, :] = x2 * lax.rsqrt(ms + RMS_EPS) * gfin_ref[...]

    @pl.when(j < _N_CTX_BLOCKS)
    def _():
        yc_ref[...] = y_s[...]

    @pl.when(j >= _N_CTX_BLOCKS)
    def _():
        yl_ref[...] = y_s[...]


def _combine(ys, dk_t, x1c, x1l, modv, g_final):
    return pl.pallas_call(
        _combine_kernel,
        grid=(_N_BLOCKS,),
        in_specs=[pl.BlockSpec((BLOCK_CAP, D_MODEL), lambda j: (j, 0)),
                  pl.BlockSpec((1, TOK_BLOCK, 2 * TOP_K), lambda j: (j, 0, 0))] + _token_specs() + [
                  _const_spec((1, D_MODEL))],
        out_specs=[pl.BlockSpec((TOK_BLOCK, D_MODEL), lambda j: (jnp.minimum(j, _N_CTX_BLOCKS - 1), 0)),
                   pl.BlockSpec((TOK_BLOCK, D_MODEL), lambda j: (jnp.maximum(j - _N_CTX_BLOCKS, 0), 0))],
        out_shape=[jax.ShapeDtypeStruct(x1c.shape, F32), jax.ShapeDtypeStruct(x1l.shape, F32)],
        scratch_shapes=[pltpu.VMEM((TOK_BLOCK, D_MODEL), F32)],
        compiler_params=pltpu.CompilerParams(dimension_semantics=("arbitrary",),
                                             vmem_limit_bytes=V7X_VMEM_LIMIT),
        name="combine",
    )(ys, dk_t, x1c, x1l, modv, g_final)


def kernel(x_prompt, x_sample, cache_k, cache_v, c, c_ctx, w_ada, b_ada, g_mix, w_in, w_pool, pool_scale,
           w_pa, w_pb, rpb, w_out, g_ffn, w_router, b_router, w_up, b_up, w_down, b_down, g_final):
    assert w_ada.shape[0] == 1, "single trunk layer"
    batch, seq, d = x_prompt.shape
    dec_batch, dec_seq, _ = x_sample.shape
    assert (seq, dec_seq, d) == (SEQ, DEC_SEQ, D_MODEL)
    assert batch * seq == _N_CTX_BLOCKS * TOK_BLOCK and dec_batch * dec_seq == _N_LAT_BLOCKS * TOK_BLOCK

    cmat = jnp.concatenate([c_ctx[None, :], c, jnp.zeros((8 - 1 - dec_batch, d), F32)], axis=0)
    modv = _modulation(cmat, w_ada[0], b_ada[0]).reshape(8, N_MOD, d)

    weights = (g_mix[0][None, :], w_in[0].astype(BF16), w_pool[0].astype(BF16), pool_scale[0][None, :],
               w_pa[0].astype(BF16), w_pb[0].astype(BF16), w_out[0].astype(BF16))

    def by_pair(cache):
        z = cache[:, 0].reshape(dec_batch, N_PAIRS, 2, PAST_LEN, HEAD_DIM)
        return z.transpose(0, 1, 3, 2, 4).reshape(dec_batch, N_PAIRS, PAST_LEN, PAIR_W)

    x1c, new_k, new_v = _ctx_mixer(x_prompt.reshape(batch * seq, d), modv, weights)
    x1l = _lat_mixer(x_sample.reshape(dec_batch * dec_seq, d), modv, weights,
                     by_pair(cache_k), by_pair(cache_v), _bias_tables(rpb[0]))

    xs, dk, np16, off16 = _route_sort(x1c, x1l, modv, g_ffn[0][None, :],
                                      w_router[0].T.astype(BF16), b_router[0][:, None])
    np16_i = np16[:, :, 0].astype(jnp.int32).reshape(-1)
    off16_i = off16[:, :, 0].astype(jnp.int32).reshape(-1)
    ys = _experts(np16_i, off16_i, w_up[0], b_up[0], w_down[0], b_down[0], xs)
    yc, yl = _combine(ys, dk.transpose(0, 2, 1), x1c, x1l, modv, g_final[None, :])
    return (yc.reshape(batch, seq, d), yl.reshape(dec_batch, dec_seq, d), new_k, new_v)
```

```python
import functools

import jax
import jax.numpy as jnp
import numpy as np
from jax import lax
from jax.experimental import pallas as pl
from jax.experimental.pallas import tpu as pltpu

F32 = jnp.float32
BF16 = jnp.bfloat16

D_MODEL = 1024
SEQ = 256
DEC_SEQ = 1024
GRID_W = 64
ROWS = DEC_SEQ // GRID_W
N_HEADS = 8
HEAD_DIM = 64
N_PAIRS = N_HEADS // 2
PAIR_W = 2 * HEAD_DIM
PAST_LEN = 512
POOL_DIM = 512
POOL_WINDOWS = (2, 4, 8, 16)
POOL_GROUP_DIM = 128
ATTN_DIM = 512
WIN_R = 8
WIN_C = 16
N_EXPERTS = 32
TOP_K = 4
D_FF = 1024
SWIGLU_LIMIT = 7.0
SWIGLU_ALPHA = 1.702
N_MOD = 6
RMS_EPS = 1e-6
NEG_INF = -1e30
ATTN_SCALE = HEAD_DIM ** -0.5
IN_DIM = POOL_DIM + 3 * ATTN_DIM + 2 * D_MODEL

TOK_BLOCK = 512
CHUNK = 16
BLOCK_CAP = TOK_BLOCK * TOP_K + N_EXPERTS * CHUNK
ROW_STEP = 128
EXPERT_ROW_TILE = 384
LAST_ROWS_MAX = 512
V7X_VMEM_LIMIT = 60 * 1024 * 1024

_N_CTX_BLOCKS = 8
_N_LAT_BLOCKS = 4
_N_BLOCKS = _N_CTX_BLOCKS + _N_LAT_BLOCKS

_NT = (((1,), (1,)), ((), ()))


def _dot(a, b):
    return jnp.dot(a, b, preferred_element_type=F32)


def _dot_nt(a, b):
    return lax.dot_general(a, b, _NT, preferred_element_type=F32)


def _sigmoid(x):
    return 1.0 / (1.0 + jnp.exp(-x))


def _norm_mod(x, gain, scale, shift):
    ms = jnp.mean(x * x, axis=-1, keepdims=True)
    return (x * lax.rsqrt(ms + RMS_EPS) * gain) * (1.0 + scale) + shift


def _const_spec(shape):
    zeros = (0,) * len(shape)
    return pl.BlockSpec(shape, lambda *_: zeros, pipeline_mode=pl.Buffered(1))


MOD_COLS = 768


def _mod_kernel(c_ref, w_ref, b_ref, o_ref):
    c = c_ref[...]
    s = c * _sigmoid(c)
    s_hi = s.astype(BF16)
    s_lo = (s - s_hi.astype(F32)).astype(BF16)
    r = _dot(jnp.concatenate([s_hi, s_lo], axis=0), w_ref[...].astype(BF16))
    o_ref[...] = r[:8] + r[8:] + b_ref[...]


def _modulation(cmat, w_ada, b_ada):
    n = w_ada.shape[1]
    return pl.pallas_call(
        _mod_kernel,
        grid=(n // MOD_COLS,),
        in_specs=[pl.BlockSpec((8, D_MODEL), lambda i: (0, 0)),
                  pl.BlockSpec((D_MODEL, MOD_COLS), lambda i: (0, i)),
                  pl.BlockSpec((1, MOD_COLS), lambda i: (0, i))],
        out_specs=pl.BlockSpec((8, MOD_COLS), lambda i: (0, i)),
        out_shape=jax.ShapeDtypeStruct((8, n), F32),
        name="modulation",
    )(cmat, w_ada, b_ada.reshape(1, n))


def _pool_mix(u, pos, seq):
    n = u.shape[0]

    def down(x, d):
        return jnp.where(pos >= d, pltpu.roll(x, d, 0), 0.0)

    def up(x, d):
        return jnp.where(pos < seq - d, pltpu.roll(x, n - d, 0), 0.0)

    return down, up


def _pool_group(u, pos, seq, w):
    down, up = _pool_mix(u, pos, seq)
    hw = w // 2
    back = u
    fwd = u
    d = 1
    while d < hw:
        back = back + down(back, d)
        fwd = fwd + up(fwd, d)
        d *= 2
    s = down(back, 1) + fwd
    posf = pos.astype(F32)
    cnt = jnp.minimum(posf + hw, float(seq)) - jnp.maximum(posf - hw, 0.0)
    return s / cnt - u


def _pass_rows(tile):
    return tile if tile <= TOK_BLOCK else tile // 4


def _mixer_front(x_ref, mod_ref, gmix_ref, win_ref, hb_s, u_s, q_s, k_s, v_s):
    shift, scale = mod_ref[0, 0:1, :], mod_ref[0, 1:2, :]
    step = _pass_rows(x_ref.shape[0])
    for c in range(x_ref.shape[0] // step):
        rows = slice(c * step, (c + 1) * step)
        hb = _norm_mod(x_ref[rows, :], gmix_ref[...], scale, shift).astype(BF16)
        hb_s[rows, :] = hb
        u_s[rows, :] = _dot(hb, win_ref[:, 0:POOL_DIM])
        for dst, base in ((q_s, POOL_DIM), (k_s, POOL_DIM + ATTN_DIM), (v_s, POOL_DIM + 2 * ATTN_DIM)):
            z = _dot(hb, win_ref[:, base:base + ATTN_DIM])
            if dst is q_s:
                z = z * ATTN_SCALE
            for g in range(N_PAIRS):
                dst[g, rows, :] = z[:, g * PAIR_W:(g + 1) * PAIR_W].astype(dst.dtype)


def _mixer_back(x_ref, mod_ref, win_ref, wpool_ref, ps_ref, wpa_ref, wpb_ref, wout_ref, x1_ref,
                hb_s, u_s, o_s, pg_s, seq):
    tile = x_ref.shape[0]
    gate = mod_ref[0, 2:3, :]
    pos = lax.broadcasted_iota(jnp.int32, (tile, 1), 0) % seq
    for g, w in enumerate(POOL_WINDOWS):
        cols = slice(g * POOL_GROUP_DIM, (g + 1) * POOL_GROUP_DIM)
        pg_s[:, cols] = _pool_group(u_s[:, cols], pos, seq, w).astype(BF16)
    step = _pass_rows(tile)
    for c in range(tile // step):
        rows = slice(c * step, (c + 1) * step)
        ys = []
        for g in range(len(POOL_WINDOWS)):
            cols = slice(g * POOL_GROUP_DIM, (g + 1) * POOL_GROUP_DIM)
            ys.append((_dot(pg_s[rows, cols], wpool_ref[g]) * ps_ref[:, cols]).astype(BF16))
        a = _dot(jnp.concatenate(ys, axis=1), wpa_ref[...])
        ob = _dot(jnp.concatenate([o_s[g, rows, :] for g in range(N_PAIRS)], axis=1), wpb_ref[...])
        gab = _dot(hb_s[rows, :], win_ref[:, POOL_DIM + 3 * ATTN_DIM:IN_DIM])
        merged = _sigmoid(gab[:, :D_MODEL]) * a + _sigmoid(gab[:, D_MODEL:]) * ob
        mix = _dot(merged.astype(BF16), wout_ref[...])
        x1_ref[rows, :] = x_ref[rows, :] + gate * mix


def _ctx_mixer_kernel(x_ref, mod_ref, gmix_ref, win_ref, wpool_ref, ps_ref, wpa_ref, wpb_ref, wout_ref,
                      x1_ref, ko_ref, vo_ref, hb_s, u_s, q_s, k_s, v_s, o_s, pg_s):
    _mixer_front(x_ref, mod_ref, gmix_ref, win_ref, hb_s, u_s, q_s, k_s, v_s)
    even = lax.broadcasted_iota(jnp.int32, (1, PAIR_W), 1) < HEAD_DIM
    tile = x_ref.shape[0]
    for s in range(tile // SEQ):
        rows = slice(s * SEQ, (s + 1) * SEQ)
        for g in range(N_PAIRS):
            q2, k2, v2 = q_s[g, rows, :], k_s[g, rows, :], v_s[g, rows, :]
            ko_ref[s, 0, 2 * g] = k2[:, :HEAD_DIM]
            ko_ref[s, 0, 2 * g + 1] = k2[:, HEAD_DIM:]
            vo_ref[s, 0, 2 * g] = v2[:, :HEAD_DIM]
            vo_ref[s, 0, 2 * g + 1] = v2[:, HEAD_DIM:]
            kb, vb = k2.astype(BF16), v2.astype(BF16)
            outs = []
            for par in range(2):
                qm = jnp.where(even if par == 0 else jnp.logical_not(even), q2, 0.0).astype(BF16)
                sc = _dot_nt(qm, kb)
                m = jnp.max(sc, axis=-1, keepdims=True)
                p = jnp.exp(sc - m)
                l = jnp.sum(p, axis=-1, keepdims=True)
                outs.append(_dot(p.astype(BF16), vb) / l)
            o_s[g, rows, :] = jnp.where(even, outs[0], outs[1]).astype(BF16)
    _mixer_back(x_ref, mod_ref, win_ref, wpool_ref, ps_ref, wpa_ref, wpb_ref, wout_ref, x1_ref,
                hb_s, u_s, o_s, pg_s, SEQ)


def _row_window(r):
    rs = min(max(r - WIN_R // 2, 0), ROWS - WIN_R)
    return rs, rs - r + WIN_R - 1


def _lat_mixer_kernel(x_ref, mod_ref, gmix_ref, win_ref, wpool_ref, ps_ref, wpa_ref, wpb_ref, wout_ref,
                      kc_ref, vc_ref, tb_ref, x1_ref, hb_s, u_s, q_s, k_s, v_s, o_s, pg_s):
    _mixer_front(x_ref, mod_ref, gmix_ref, win_ref, hb_s, u_s, q_s, k_s, v_s)
    even = lax.broadcasted_iota(jnp.int32, (1, PAIR_W), 1) < HEAD_DIM
    nk = WIN_R * GRID_W

    def pair_body(g, carry):
        q2 = q_s[g]
        kb, vb = k_s[g].astype(BF16), v_s[g].astype(BF16)
        kcb = jnp.concatenate([kc_ref[0, 0, 2 * g], kc_ref[0, 0, 2 * g + 1]], axis=1).astype(BF16)
        vcb = jnp.concatenate([vc_ref[0, 0, 2 * g], vc_ref[0, 0, 2 * g + 1]], axis=1).astype(BF16)
        outs = []
        for par in range(2):
            qm = jnp.where(even if par == 0 else jnp.logical_not(even), q2, 0.0).astype(BF16)
            s_ctx = _dot_nt(qm, kcb)
            slabs = []
            for r in range(ROWS):
                rs, rho = _row_window(r)
                bias = tb_ref[2 * g + par, rho % 2, :, (rho - rho % 2) * GRID_W:(rho - rho % 2) * GRID_W + nk]
                sl = _dot_nt(qm[r * GRID_W:(r + 1) * GRID_W, :], kb[rs * GRID_W:rs * GRID_W + nk, :])
                slabs.append(sl + bias)
            s_loc = jnp.concatenate(slabs, axis=0)
            m = jnp.maximum(jnp.max(s_loc, axis=-1, keepdims=True), jnp.max(s_ctx, axis=-1, keepdims=True))
            p_loc = jnp.exp(s_loc - m)
            p_ctx = jnp.exp(s_ctx - m)
            l = jnp.sum(p_loc, axis=-1, keepdims=True) + jnp.sum(p_ctx, axis=-1, keepdims=True)
            p_locb = p_loc.astype(BF16)
            o_rows = []
            for r in range(ROWS):
                rs, _ = _row_window(r)
                o_rows.append(_dot(p_locb[r * GRID_W:(r + 1) * GRID_W, :], vb[rs * GRID_W:rs * GRID_W + nk, :]))
            o = jnp.concatenate(o_rows, axis=0) + _dot(p_ctx.astype(BF16), vcb)
            outs.append(o / l)
        o_s[g] = jnp.where(even, outs[0], outs[1]).astype(BF16)
        return carry

    lax.fori_loop(0, N_PAIRS, pair_body, 0)
    _mixer_back(x_ref, mod_ref, win_ref, wpool_ref, ps_ref, wpa_ref, wpb_ref, wout_ref, x1_ref,
                hb_s, u_s, o_s, pg_s, DEC_SEQ)


def _mixer_scratch(tile, kv_dtype):
    return [pltpu.VMEM((tile, D_MODEL), BF16),
            pltpu.VMEM((tile, POOL_DIM), F32),
            pltpu.VMEM((N_PAIRS, tile, PAIR_W), BF16),
            pltpu.VMEM((N_PAIRS, tile, PAIR_W), kv_dtype),
            pltpu.VMEM((N_PAIRS, tile, PAIR_W), kv_dtype),
            pltpu.VMEM((N_PAIRS, tile, PAIR_W), BF16),
            pltpu.VMEM((tile, POOL_DIM), BF16)]


def _weight_specs():
    return [_const_spec((1, D_MODEL)),
            _const_spec((D_MODEL, IN_DIM)),
            _const_spec((len(POOL_WINDOWS), POOL_GROUP_DIM, POOL_GROUP_DIM)),
            _const_spec((1, POOL_DIM)),
            _const_spec((POOL_DIM, D_MODEL)),
            _const_spec((ATTN_DIM, D_MODEL)),
            _const_spec((D_MODEL, D_MODEL))]


def _ctx_mixer(x, modv, weights):
    n = x.shape[0]
    nseq = TOK_BLOCK // SEQ
    cache = jax.ShapeDtypeStruct((n // SEQ, 1, N_HEADS, SEQ, HEAD_DIM), F32)
    cache_spec = pl.BlockSpec((nseq, 1, N_HEADS, SEQ, HEAD_DIM), lambda i: (i, 0, 0, 0, 0))
    return pl.pallas_call(
        _ctx_mixer_kernel,
        grid=(n // TOK_BLOCK,),
        in_specs=[pl.BlockSpec((TOK_BLOCK, D_MODEL), lambda i: (i, 0)),
                  pl.BlockSpec((1, N_MOD, D_MODEL), lambda i: (0, 0, 0))] + _weight_specs(),
        out_specs=[pl.BlockSpec((TOK_BLOCK, D_MODEL), lambda i: (i, 0)), cache_spec, cache_spec],
        out_shape=[jax.ShapeDtypeStruct((n, D_MODEL), F32), cache, cache],
        scratch_shapes=_mixer_scratch(TOK_BLOCK, F32),
        compiler_params=pltpu.CompilerParams(dimension_semantics=("arbitrary",),
                                             vmem_limit_bytes=V7X_VMEM_LIMIT),
        name="ctx_mixer",
    )(x, modv, *weights)


def _lat_mixer(x, modv, weights, kc, vc, tb):
    n = x.shape[0]
    return pl.pallas_call(
        _lat_mixer_kernel,
        grid=(n // DEC_SEQ,),
        in_specs=[pl.BlockSpec((DEC_SEQ, D_MODEL), lambda i: (i, 0)),
                  pl.BlockSpec((1, N_MOD, D_MODEL), lambda i: (i + 1, 0, 0))] + _weight_specs() + [
                  pl.BlockSpec((1, 1, N_HEADS, PAST_LEN, HEAD_DIM), lambda i: (i, 0, 0, 0, 0)),
                  pl.BlockSpec((1, 1, N_HEADS, PAST_LEN, HEAD_DIM), lambda i: (i, 0, 0, 0, 0)),
                  _const_spec((N_HEADS, 2, GRID_W, ROWS * GRID_W))],
        out_specs=pl.BlockSpec((DEC_SEQ, D_MODEL), lambda i: (i, 0)),
        out_shape=jax.ShapeDtypeStruct((n, D_MODEL), F32),
        scratch_shapes=_mixer_scratch(DEC_SEQ, BF16),
        compiler_params=pltpu.CompilerParams(dimension_semantics=("arbitrary",),
                                             vmem_limit_bytes=V7X_VMEM_LIMIT),
        name="lat_mixer",
    )(x, modv, *weights, kc, vc, tb)


N_RPB_ROWS = 2 * WIN_R - 1
N_RPB_COLS = 2 * WIN_C - 1
TABLE_W = ROWS * GRID_W


def _bias_kernel(v_ref, keep_ref, o_ref):
    for h in range(N_HEADS):
        for par in range(2):
            x = jnp.broadcast_to(v_ref[h, par:par + 1, :], (GRID_W, TABLE_W))
            shifted = pltpu.roll(x, TABLE_W - (WIN_C - 1), 1, stride=1, stride_axis=0)
            o_ref[h, par] = jnp.where(keep_ref[par] > 0.0, shifted, NEG_INF)


def _bias_tables(rpb):
    col = np.arange(GRID_W)
    cs = np.clip(col - WIN_C // 2, 0, GRID_W - WIN_C)
    in_win = (col[None, :] >= cs[:, None]) & (col[None, :] < cs[:, None] + WIN_C)
    keep = np.tile(in_win.astype(np.float32), (2, 1, ROWS))
    keep[0, :, N_RPB_ROWS * GRID_W:] = 0.0
    keep[1, :, (N_RPB_ROWS - 1) * GRID_W:] = 0.0
    rp = jnp.pad(rpb.astype(F32), ((0, 0), (0, ROWS + 1 - N_RPB_ROWS), (0, GRID_W - N_RPB_COLS)))
    v = jnp.stack([rp[:, :ROWS].reshape(N_HEADS, TABLE_W), rp[:, 1:].reshape(N_HEADS, TABLE_W)], axis=1)
    return pl.pallas_call(
        _bias_kernel,
        out_shape=jax.ShapeDtypeStruct((N_HEADS, 2, GRID_W, TABLE_W), F32),
        name="bias_tables",
    )(v, jnp.asarray(keep))


SORT_ROWS = 512


def _sort_kernel(xc_ref, xl_ref, mod_ref, gffn_ref, wr_ref, br_ref,
                 xs_ref, dk_ref, np_ref, off_ref):
    j = pl.program_id(0)
    x = jnp.where(j < _N_CTX_BLOCKS, xc_ref[...], xl_ref[...])
    shift, scale = mod_ref[0, 3:4, :], mod_ref[0, 4:5, :]
    hb = _norm_mod(x, gffn_ref[...], scale, shift).astype(BF16)
    logits = _dot_nt(wr_ref[...], hb) + br_ref[...]
    eio = lax.broadcasted_iota(jnp.int32, logits.shape, 0)
    work = logits
    sels, vals = [], []
    for _ in range(TOP_K):
        m = jnp.max(work, axis=0, keepdims=True)
        idx = jnp.min(jnp.where(work == m, eio, N_EXPERTS), axis=0, keepdims=True)
        sel = eio == idx
        sels.append(sel)
        vals.append(m)
        work = jnp.where(sel, -jnp.inf, work)
    exps = [jnp.exp(v - vals[0]) for v in vals]
    den = exps[0] + exps[1] + exps[2] + exps[3]
    mask = jnp.zeros(logits.shape, F32)
    for sel in sels:
        mask = mask + jnp.where(sel, 1.0, 0.0)
    t_row = lax.broadcasted_iota(jnp.int32, (TOK_BLOCK, TOK_BLOCK), 0)
    t_col = lax.broadcasted_iota(jnp.int32, (TOK_BLOCK, TOK_BLOCK), 1)
    before = jnp.where(t_row < t_col, 1.0, 0.0).astype(BF16)
    rank = _dot(mask.astype(BF16), before)
    cnt = jnp.sum(mask, axis=1, keepdims=True)
    np16 = jnp.floor((cnt + (CHUNK - 1.0)) * (1.0 / CHUNK))
    e_row = lax.broadcasted_iota(jnp.int32, (N_EXPERTS, N_EXPERTS), 0)
    e_col = lax.broadcasted_iota(jnp.int32, (N_EXPERTS, N_EXPERTS), 1)
    lower = jnp.where(e_col < e_row, 1.0, 0.0).astype(BF16)
    np16_b = jnp.broadcast_to(np16, (N_EXPERTS, 128))
    off16 = _dot(lower, np16_b.astype(BF16))
    np_ref[0] = np16_b
    off_ref[0] = off16
    dest = off16[:, 0:1] * float(CHUNK) + rank
    dests = []
    for k in range(TOP_K):
        dk = jnp.sum(jnp.where(sels[k], dest, 0.0), axis=0, keepdims=True)
        dk_ref[0, k:k + 1, :] = dk
        dk_ref[0, TOP_K + k:TOP_K + k + 1, :] = exps[k] / den
        dests.append(dk.astype(jnp.int32))
    rio = lax.broadcasted_iota(jnp.int32, (SORT_ROWS, TOK_BLOCK), 0)
    for c in range(BLOCK_CAP // SORT_ROWS):
        onehot = jnp.zeros((SORT_ROWS, TOK_BLOCK), F32)
        for dk in dests:
            onehot = jnp.where(rio == dk - c * SORT_ROWS, 1.0, onehot)
        xs_ref[c * SORT_ROWS:(c + 1) * SORT_ROWS, :] = _dot(onehot.astype(BF16), hb).astype(BF16)


def _block_mod_index(j):
    return jnp.where(j < _N_CTX_BLOCKS, 0, 1 + (j - _N_CTX_BLOCKS) // (DEC_SEQ // TOK_BLOCK))


def _token_specs():
    return [pl.BlockSpec((TOK_BLOCK, D_MODEL), lambda j: (jnp.minimum(j, _N_CTX_BLOCKS - 1), 0)),
            pl.BlockSpec((TOK_BLOCK, D_MODEL), lambda j: (jnp.maximum(j - _N_CTX_BLOCKS, 0), 0)),
            pl.BlockSpec((1, N_MOD, D_MODEL), lambda j: (_block_mod_index(j), 0, 0))]


def _route_sort(x1c, x1l, modv, g_ffn, wr_t, br):
    tbl = jax.ShapeDtypeStruct((_N_BLOCKS, N_EXPERTS, 128), F32)
    tbl_spec = pl.BlockSpec((1, N_EXPERTS, 128), lambda j: (j, 0, 0))
    return pl.pallas_call(
        _sort_kernel,
        grid=(_N_BLOCKS,),
        in_specs=_token_specs() + [_const_spec((1, D_MODEL)),
                                   _const_spec((N_EXPERTS, D_MODEL)),
                                   _const_spec((N_EXPERTS, 1))],
        out_specs=[pl.BlockSpec((BLOCK_CAP, D_MODEL), lambda j: (j, 0)),
                   pl.BlockSpec((1, 2 * TOP_K, TOK_BLOCK), lambda j: (j, 0, 0)),
                   tbl_spec, tbl_spec],
        out_shape=[jax.ShapeDtypeStruct((_N_BLOCKS * BLOCK_CAP, D_MODEL), BF16),
                   jax.ShapeDtypeStruct((_N_BLOCKS, 2 * TOP_K, TOK_BLOCK), F32),
                   tbl, tbl],
        compiler_params=pltpu.CompilerParams(dimension_semantics=("arbitrary",),
                                             vmem_limit_bytes=V7X_VMEM_LIMIT),
        name="route_sort",
    )(x1c, x1l, modv, g_ffn, wr_t, br)


EXPERT_BUF_ROWS = 2048
PASS_CHUNKS = EXPERT_BUF_ROWS // CHUNK
N_ROW_BUFS = 3


def _expert_kernel(np_ref, off_ref, wu_ref, bu_ref, wd_ref, bd_ref, xs_hbm, ys_hbm,
                   buf, wu_s, wd_s, gsem, ssem, pend):
    del xs_hbm
    e = pl.program_id(0)
    last = pl.num_programs(0) - 1
    slot = e % N_ROW_BUFS
    nxt = (e + 1) % N_ROW_BUFS

    def move_chunks(ex, q_lo, slot_, gather):
        def block_body(j, q0):
            n = np_ref[j * N_EXPERTS + ex]
            off = off_ref[j * N_EXPERTS + ex]
            c_lo = jnp.clip(q_lo - q0, 0, n)
            m = jnp.clip(q_lo + PASS_CHUNKS - q0, 0, n) - c_lo

            @pl.when(m > 0)
            def _():
                rows = pl.multiple_of(m * CHUNK, CHUNK)
                row0 = pl.multiple_of(j * BLOCK_CAP + (off + c_lo) * CHUNK, CHUNK)
                brow0 = pl.multiple_of((q0 + c_lo - q_lo) * CHUNK, CHUNK)
                hbm = ys_hbm.at[pl.ds(row0, rows), :]
                vm = buf.at[slot_, pl.ds(brow0, rows), :]
                if gather:
                    pltpu.make_async_copy(hbm, vm, gsem.at[slot_]).start()
                else:
                    pltpu.make_async_copy(vm, hbm, ssem.at[slot_]).start()

            return q0 + n

        lax.fori_loop(0, _N_BLOCKS, block_body, 0)

    def wait_chunks(sem, n, slot_):
        @pl.when(n > 0)
        def _():
            rows = pl.multiple_of(n * CHUNK, CHUNK)
            pltpu.make_async_copy(ys_hbm.at[pl.ds(0, rows), :], buf.at[slot_, pl.ds(0, rows), :], sem).wait()

    def mlp_rows(r0, rows):
        x = buf[slot, pl.ds(r0, rows), :]
        gu = _dot(x, wu_s[...]) + bu_ref[0]
        gate = jnp.minimum(gu[:, :D_FF], SWIGLU_LIMIT)
        up = jnp.clip(gu[:, D_FF:], -SWIGLU_LIMIT, SWIGLU_LIMIT)
        glu = gate * _sigmoid(SWIGLU_ALPHA * gate)
        y = _dot(((up + 1.0) * glu).astype(BF16), wd_s[...]) + bd_ref[0]
        buf[slot, pl.ds(r0, rows), :] = y.astype(BF16)

    def compute(n):
        rows = (n * CHUNK + ROW_STEP - 1) // ROW_STEP * ROW_STEP
        n_main = (jnp.maximum(rows - LAST_ROWS_MAX, 0) + EXPERT_ROW_TILE - 1) // EXPERT_ROW_TILE

        def tile_body(t, carry):
            mlp_rows(pl.multiple_of(t * EXPERT_ROW_TILE, ROW_STEP), EXPERT_ROW_TILE)
            return carry

        lax.fori_loop(0, n_main, tile_body, 0)
        r0 = pl.multiple_of(n_main * EXPERT_ROW_TILE, ROW_STEP)
        for size in range(ROW_STEP, LAST_ROWS_MAX + ROW_STEP, ROW_STEP):
            pl.when(rows - r0 == size)(functools.partial(mlp_rows, r0, size))

    def chunks_of(ex):
        return lax.fori_loop(0, _N_BLOCKS, lambda j, acc: acc + np_ref[j * N_EXPERTS + ex], 0)

    @pl.when(e == 0)
    def _():
        buf[...] = jnp.zeros_like(buf)
        for b in range(N_ROW_BUFS):
            pend[b] = 0
        move_chunks(0, 0, 0, True)

    wait_chunks(ssem.at[nxt], pend[nxt], nxt)
    pend[nxt] = 0

    @pl.when(e < last)
    def _():
        move_chunks(e + 1, 0, nxt, True)

    wu_s[...] = wu_ref[0].astype(BF16)
    wd_s[...] = wd_ref[0].astype(BF16)

    total = chunks_of(e)
    n0 = jnp.minimum(total, PASS_CHUNKS)
    wait_chunks(gsem.at[slot], n0, slot)
    compute(n0)
    move_chunks(e, 0, slot, False)
    pend[slot] = n0

    def pass_body(p, carry):
        wait_chunks(ssem.at[slot], pend[slot], slot)
        lo = p * PASS_CHUNKS
        n = jnp.minimum(total - lo, PASS_CHUNKS)
        move_chunks(e, lo, slot, True)
        wait_chunks(gsem.at[slot], n, slot)
        compute(n)
        move_chunks(e, lo, slot, False)
        pend[slot] = n
        return carry

    lax.fori_loop(1, (total + PASS_CHUNKS - 1) // PASS_CHUNKS, pass_body, 0)

    @pl.when(e == last)
    def _():
        for b in range(N_ROW_BUFS):
            wait_chunks(ssem.at[b], pend[b], b)
            pend[b] = 0


def _experts(np16, off16, w_up, b_up, w_down, b_down, xs):
    grid_spec = pltpu.PrefetchScalarGridSpec(
        num_scalar_prefetch=2,
        grid=(N_EXPERTS,),
        in_specs=[pl.BlockSpec((1, D_MODEL, 2 * D_FF), lambda e, *_: (e, 0, 0)),
                  pl.BlockSpec((1, 1, 2 * D_FF), lambda e, *_: (e, 0, 0)),
                  pl.BlockSpec((1, D_FF, D_MODEL), lambda e, *_: (e, 0, 0)),
                  pl.BlockSpec((1, 1, D_MODEL), lambda e, *_: (e, 0, 0)),
                  pl.BlockSpec(memory_space=pl.ANY)],
        out_specs=pl.BlockSpec(memory_space=pl.ANY),
        scratch_shapes=[pltpu.VMEM((N_ROW_BUFS, EXPERT_BUF_ROWS, D_MODEL), BF16),
                        pltpu.VMEM((D_MODEL, 2 * D_FF), BF16),
                        pltpu.VMEM((D_FF, D_MODEL), BF16),
                        pltpu.SemaphoreType.DMA((N_ROW_BUFS,)),
                        pltpu.SemaphoreType.DMA((N_ROW_BUFS,)),
                        pltpu.SMEM((N_ROW_BUFS,), jnp.int32)],
    )
    return pl.pallas_call(
        _expert_kernel,
        grid_spec=grid_spec,
        out_shape=jax.ShapeDtypeStruct(xs.shape, xs.dtype),
        input_output_aliases={6: 0},
        compiler_params=pltpu.CompilerParams(dimension_semantics=("arbitrary",),
                                             vmem_limit_bytes=V7X_VMEM_LIMIT),
        name="experts",
    )(np16, off16, w_up, b_up.reshape(N_EXPERTS, 1, 2 * D_FF), w_down,
      b_down.reshape(N_EXPERTS, 1, D_MODEL), xs)


def _combine_kernel(ys_ref, dk_ref, xc_ref, xl_ref, mod_ref, gfin_ref, yc_ref, yl_ref, y_s):
    j = pl.program_id(0)
    half = TOK_BLOCK // 2
    cio = lax.broadcasted_iota(jnp.int32, (half, SORT_ROWS), 1).astype(F32)
    halves = [slice(h * half, (h + 1) * half) for h in range(2)]
    accs = [jnp.zeros((half, D_MODEL), F32) for _ in halves]
    for c in range(BLOCK_CAP // SORT_ROWS):
        ys = ys_ref[c * SORT_ROWS:(c + 1) * SORT_ROWS, :]
        for h, rows in enumerate(halves):
            d = dk_ref[0, rows, :]
            w = jnp.zeros((half, SORT_ROWS), F32)
            for k in range(TOP_K):
                w = jnp.where(cio == d[:, k:k + 1] - float(c * SORT_ROWS), d[:, TOP_K + k:TOP_K + k + 1], w)
            accs[h] = accs[h] + _dot(w.astype(BF16), ys)
    for h, rows in enumerate(halves):
        acc = accs[h]
        x1 = jnp.where(j < _N_CTX_BLOCKS, xc_ref[rows, :], xl_ref[rows, :])
        x2 = x1 + mod_ref[0, 5:6, :] * acc
        ms = jnp.mean(x2 * x2, axis=-1, keepdims=True)
        y_s[rows, :] = x2 * lax.rsqrt(ms + RMS_EPS) * gfin_ref[...]

    @pl.when(j < _N_CTX_BLOCKS)
    def _():
        yc_ref[...] = y_s[...]

    @pl.when(j >= _N_CTX_BLOCKS)
    def _():
        yl_ref[...] = y_s[...]


def _combine(ys, dk_t, x1c, x1l, modv, g_final):
    return pl.pallas_call(
        _combine_kernel,
        grid=(_N_BLOCKS,),
        in_specs=[pl.BlockSpec((BLOCK_CAP, D_MODEL), lambda j: (j, 0)),
                  pl.BlockSpec((1, TOK_BLOCK, 2 * TOP_K), lambda j: (j, 0, 0))] + _token_specs() + [
                  _const_spec((1, D_MODEL))],
        out_specs=[pl.BlockSpec((TOK_BLOCK, D_MODEL), lambda j: (jnp.minimum(j, _N_CTX_BLOCKS - 1), 0)),
                   pl.BlockSpec((TOK_BLOCK, D_MODEL), lambda j: (jnp.maximum(j - _N_CTX_BLOCKS, 0), 0))],
        out_shape=[jax.ShapeDtypeStruct(x1c.shape, F32), jax.ShapeDtypeStruct(x1l.shape, F32)],
        scratch_shapes=[pltpu.VMEM((TOK_BLOCK, D_MODEL), F32)],
        compiler_params=pltpu.CompilerParams(dimension_semantics=("arbitrary",),
                                             vmem_limit_bytes=V7X_VMEM_LIMIT),
        name="combine",
    )(ys, dk_t, x1c, x1l, modv, g_final)


def kernel(x_prompt, x_sample, cache_k, cache_v, c, c_ctx, w_ada, b_ada, g_mix, w_in, w_pool, pool_scale,
           w_pa, w_pb, rpb, w_out, g_ffn, w_router, b_router, w_up, b_up, w_down, b_down, g_final):
    assert w_ada.shape[0] == 1, "single trunk layer"
    batch, seq, d = x_prompt.shape
    dec_batch, dec_seq, _ = x_sample.shape
    assert (seq, dec_seq, d) == (SEQ, DEC_SEQ, D_MODEL)
    assert batch * seq == _N_CTX_BLOCKS * TOK_BLOCK and dec_batch * dec_seq == _N_LAT_BLOCKS * TOK_BLOCK

    cmat = jnp.concatenate([c_ctx[None, :], c, jnp.zeros((8 - 1 - dec_batch, d), F32)], axis=0)
    modv = _modulation(cmat, w_ada[0], b_ada[0]).reshape(8, N_MOD, d)

    weights = (g_mix[0][None, :], w_in[0].astype(BF16), w_pool[0].astype(BF16), pool_scale[0][None, :],
               w_pa[0].astype(BF16), w_pb[0].astype(BF16), w_out[0].astype(BF16))

    x1c, new_k, new_v = _ctx_mixer(x_prompt.reshape(batch * seq, d), modv, weights)
    x1l = _lat_mixer(x_sample.reshape(dec_batch * dec_seq, d), modv, weights,
                     cache_k, cache_v, _bias_tables(rpb[0]))

    xs, dk, np16, off16 = _route_sort(x1c, x1l, modv, g_ffn[0][None, :],
                                      w_router[0].T.astype(BF16), b_router[0][:, None])
    np16_i = np16[:, :, 0].astype(jnp.int32).reshape(-1)
    off16_i = off16[:, :, 0].astype(jnp.int32).reshape(-1)
    ys = _experts(np16_i, off16_i, w_up[0], b_up[0], w_down[0], b_down[0], xs)
    yc, yl = _combine(ys, dk.transpose(0, 2, 1), x1c, x1l, modv, g_final[None, :])
    return (yc.reshape(batch, seq, d), yl.reshape(dec_batch, dec_seq, d), new_k, new_v)
```

```python
import functools

import jax
import jax.numpy as jnp
import numpy as np
from jax import lax
from jax.experimental import pallas as pl
from jax.experimental.pallas import tpu as pltpu

F32 = jnp.float32
BF16 = jnp.bfloat16

D_MODEL = 1024
SEQ = 256
DEC_SEQ = 1024
GRID_W = 64
ROWS = DEC_SEQ // GRID_W
N_HEADS = 8
HEAD_DIM = 64
N_PAIRS = N_HEADS // 2
PAIR_W = 2 * HEAD_DIM
PAST_LEN = 512
POOL_DIM = 512
POOL_WINDOWS = (2, 4, 8, 16)
POOL_GROUP_DIM = 128
ATTN_DIM = 512
WIN_R = 8
WIN_C = 16
N_EXPERTS = 32
TOP_K = 4
D_FF = 1024
SWIGLU_LIMIT = 7.0
SWIGLU_ALPHA = 1.702
N_MOD = 6
RMS_EPS = 1e-6
NEG_INF = -1e30
ATTN_SCALE = HEAD_DIM ** -0.5
IN_DIM = POOL_DIM + 3 * ATTN_DIM + 2 * D_MODEL

TOK_BLOCK = 512
CHUNK = 16
BLOCK_CAP = TOK_BLOCK * TOP_K + N_EXPERTS * CHUNK
ROW_STEP = 64
EXPERT_ROW_TILE = 384
LAST_ROWS_MAX = 512
V7X_VMEM_LIMIT = 60 * 1024 * 1024

_N_CTX_BLOCKS = 8
_N_LAT_BLOCKS = 4
_N_BLOCKS = _N_CTX_BLOCKS + _N_LAT_BLOCKS

_NT = (((1,), (1,)), ((), ()))


def _dot(a, b):
    return jnp.dot(a, b, preferred_element_type=F32)


def _dot_nt(a, b):
    return lax.dot_general(a, b, _NT, preferred_element_type=F32)


def _sigmoid(x):
    return 1.0 / (1.0 + jnp.exp(-x))


def _norm_mod(x, gain, scale, shift):
    ms = jnp.mean(x * x, axis=-1, keepdims=True)
    return (x * lax.rsqrt(ms + RMS_EPS) * gain) * (1.0 + scale) + shift


def _const_spec(shape):
    zeros = (0,) * len(shape)
    return pl.BlockSpec(shape, lambda *_: zeros, pipeline_mode=pl.Buffered(1))


MOD_COLS = 768


def _mod_kernel(c_ref, w_ref, b_ref, o_ref):
    c = c_ref[...]
    s = c * _sigmoid(c)
    s_hi = s.astype(BF16)
    s_lo = (s - s_hi.astype(F32)).astype(BF16)
    r = _dot(jnp.concatenate([s_hi, s_lo], axis=0), w_ref[...].astype(BF16))
    o_ref[...] = r[:8] + r[8:] + b_ref[...]


def _modulation(cmat, w_ada, b_ada):
    n = w_ada.shape[1]
    return pl.pallas_call(
        _mod_kernel,
        grid=(n // MOD_COLS,),
        in_specs=[pl.BlockSpec((8, D_MODEL), lambda i: (0, 0)),
                  pl.BlockSpec((D_MODEL, MOD_COLS), lambda i: (0, i)),
                  pl.BlockSpec((1, MOD_COLS), lambda i: (0, i))],
        out_specs=pl.BlockSpec((8, MOD_COLS), lambda i: (0, i)),
        out_shape=jax.ShapeDtypeStruct((8, n), F32),
        name="modulation",
    )(cmat, w_ada, b_ada.reshape(1, n))


def _pool_mix(u, pos, seq):
    n = u.shape[0]

    def down(x, d):
        return jnp.where(pos >= d, pltpu.roll(x, d, 0), 0.0)

    def up(x, d):
        return jnp.where(pos < seq - d, pltpu.roll(x, n - d, 0), 0.0)

    return down, up


def _pool_group(u, pos, seq, w):
    down, up = _pool_mix(u, pos, seq)
    hw = w // 2
    back = u
    fwd = u
    d = 1
    while d < hw:
        back = back + down(back, d)
        fwd = fwd + up(fwd, d)
        d *= 2
    s = down(back, 1) + fwd
    posf = pos.astype(F32)
    cnt = jnp.minimum(posf + hw, float(seq)) - jnp.maximum(posf - hw, 0.0)
    return s / cnt - u


def _pass_rows(tile):
    return tile if tile <= TOK_BLOCK else tile // 4


def _mixer_front(x_ref, mod_ref, gmix_ref, win_ref, hb_s, u_s, q_s, k_s, v_s):
    shift, scale = mod_ref[0, 0:1, :], mod_ref[0, 1:2, :]
    step = _pass_rows(x_ref.shape[0])
    for c in range(x_ref.shape[0] // step):
        rows = slice(c * step, (c + 1) * step)
        hb = _norm_mod(x_ref[rows, :], gmix_ref[...], scale, shift).astype(BF16)
        hb_s[rows, :] = hb
        u_s[rows, :] = _dot(hb, win_ref[:, 0:POOL_DIM])
        for dst, base in ((q_s, POOL_DIM), (k_s, POOL_DIM + ATTN_DIM), (v_s, POOL_DIM + 2 * ATTN_DIM)):
            z = _dot(hb, win_ref[:, base:base + ATTN_DIM])
            if dst is q_s:
                z = z * ATTN_SCALE
            for g in range(N_PAIRS):
                dst[g, rows, :] = z[:, g * PAIR_W:(g + 1) * PAIR_W].astype(dst.dtype)


def _mixer_back(x_ref, mod_ref, win_ref, wpool_ref, ps_ref, wpa_ref, wpb_ref, wout_ref, x1_ref,
                hb_s, u_s, o_s, pg_s, seq):
    tile = x_ref.shape[0]
    gate = mod_ref[0, 2:3, :]
    pos = lax.broadcasted_iota(jnp.int32, (tile, 1), 0) % seq
    for g, w in enumerate(POOL_WINDOWS):
        cols = slice(g * POOL_GROUP_DIM, (g + 1) * POOL_GROUP_DIM)
        pg_s[:, cols] = _pool_group(u_s[:, cols], pos, seq, w).astype(BF16)
    step = _pass_rows(tile)
    for c in range(tile // step):
        rows = slice(c * step, (c + 1) * step)
        ys = []
        for g in range(len(POOL_WINDOWS)):
            cols = slice(g * POOL_GROUP_DIM, (g + 1) * POOL_GROUP_DIM)
            ys.append((_dot(pg_s[rows, cols], wpool_ref[g]) * ps_ref[:, cols]).astype(BF16))
        a = _dot(jnp.concatenate(ys, axis=1), wpa_ref[...])
        ob = _dot(jnp.concatenate([o_s[g, rows, :] for g in range(N_PAIRS)], axis=1), wpb_ref[...])
        gab = _dot(hb_s[rows, :], win_ref[:, POOL_DIM + 3 * ATTN_DIM:IN_DIM])
        merged = _sigmoid(gab[:, :D_MODEL]) * a + _sigmoid(gab[:, D_MODEL:]) * ob
        mix = _dot(merged.astype(BF16), wout_ref[...])
        x1_ref[rows, :] = x_ref[rows, :] + gate * mix


def _ctx_mixer_kernel(x_ref, mod_ref, gmix_ref, win_ref, wpool_ref, ps_ref, wpa_ref, wpb_ref, wout_ref,
                      x1_ref, ko_ref, vo_ref, hb_s, u_s, q_s, k_s, v_s, o_s, pg_s):
    _mixer_front(x_ref, mod_ref, gmix_ref, win_ref, hb_s, u_s, q_s, k_s, v_s)
    even = lax.broadcasted_iota(jnp.int32, (1, PAIR_W), 1) < HEAD_DIM
    tile = x_ref.shape[0]
    for s in range(tile // SEQ):
        rows = slice(s * SEQ, (s + 1) * SEQ)
        for g in range(N_PAIRS):
            q2, k2, v2 = q_s[g, rows, :], k_s[g, rows, :], v_s[g, rows, :]
            ko_ref[s, 0, 2 * g] = k2[:, :HEAD_DIM]
            ko_ref[s, 0, 2 * g + 1] = k2[:, HEAD_DIM:]
            vo_ref[s, 0, 2 * g] = v2[:, :HEAD_DIM]
            vo_ref[s, 0, 2 * g + 1] = v2[:, HEAD_DIM:]
            kb, vb = k2.astype(BF16), v2.astype(BF16)
            outs = []
            for par in range(2):
                qm = jnp.where(even if par == 0 else jnp.logical_not(even), q2, 0.0).astype(BF16)
                sc = _dot_nt(qm, kb)
                m = jnp.max(sc, axis=-1, keepdims=True)
                p = jnp.exp(sc - m)
                l = jnp.sum(p, axis=-1, keepdims=True)
                outs.append(_dot(p.astype(BF16), vb) / l)
            o_s[g, rows, :] = jnp.where(even, outs[0], outs[1]).astype(BF16)
    _mixer_back(x_ref, mod_ref, win_ref, wpool_ref, ps_ref, wpa_ref, wpb_ref, wout_ref, x1_ref,
                hb_s, u_s, o_s, pg_s, SEQ)


def _row_window(r):
    rs = min(max(r - WIN_R // 2, 0), ROWS - WIN_R)
    return rs, rs - r + WIN_R - 1


def _lat_mixer_kernel(x_ref, mod_ref, gmix_ref, win_ref, wpool_ref, ps_ref, wpa_ref, wpb_ref, wout_ref,
                      kc_ref, vc_ref, tb_ref, x1_ref, hb_s, u_s, q_s, k_s, v_s, o_s, pg_s):
    _mixer_front(x_ref, mod_ref, gmix_ref, win_ref, hb_s, u_s, q_s, k_s, v_s)
    even = lax.broadcasted_iota(jnp.int32, (1, PAIR_W), 1) < HEAD_DIM
    nk = WIN_R * GRID_W

    def pair_body(g, carry):
        q2 = q_s[g]
        kb, vb = k_s[g].astype(BF16), v_s[g].astype(BF16)
        kcb, vcb = kc_ref[0, g].astype(BF16), vc_ref[0, g].astype(BF16)
        outs = []
        for par in range(2):
            qm = jnp.where(even if par == 0 else jnp.logical_not(even), q2, 0.0).astype(BF16)
            s_ctx = _dot_nt(qm, kcb)
            slabs = []
            for r in range(ROWS):
                rs, rho = _row_window(r)
                bias = tb_ref[2 * g + par, rho % 2, :, (rho - rho % 2) * GRID_W:(rho - rho % 2) * GRID_W + nk]
                sl = _dot_nt(qm[r * GRID_W:(r + 1) * GRID_W, :], kb[rs * GRID_W:rs * GRID_W + nk, :])
                slabs.append(sl + bias)
            s_loc = jnp.concatenate(slabs, axis=0)
            m = jnp.maximum(jnp.max(s_loc, axis=-1, keepdims=True), jnp.max(s_ctx, axis=-1, keepdims=True))
            p_loc = jnp.exp(s_loc - m)
            p_ctx = jnp.exp(s_ctx - m)
            l = jnp.sum(p_loc, axis=-1, keepdims=True) + jnp.sum(p_ctx, axis=-1, keepdims=True)
            p_locb = p_loc.astype(BF16)
            o_rows = []
            for r in range(ROWS):
                rs, _ = _row_window(r)
                o_rows.append(_dot(p_locb[r * GRID_W:(r + 1) * GRID_W, :], vb[rs * GRID_W:rs * GRID_W + nk, :]))
            o = jnp.concatenate(o_rows, axis=0) + _dot(p_ctx.astype(BF16), vcb)
            outs.append(o / l)
        o_s[g] = jnp.where(even, outs[0], outs[1]).astype(BF16)
        return carry

    lax.fori_loop(0, N_PAIRS, pair_body, 0)
    _mixer_back(x_ref, mod_ref, win_ref, wpool_ref, ps_ref, wpa_ref, wpb_ref, wout_ref, x1_ref,
                hb_s, u_s, o_s, pg_s, DEC_SEQ)


def _mixer_scratch(tile, kv_dtype):
    return [pltpu.VMEM((tile, D_MODEL), BF16),
            pltpu.VMEM((tile, POOL_DIM), F32),
            pltpu.VMEM((N_PAIRS, tile, PAIR_W), BF16),
            pltpu.VMEM((N_PAIRS, tile, PAIR_W), kv_dtype),
            pltpu.VMEM((N_PAIRS, tile, PAIR_W), kv_dtype),
            pltpu.VMEM((N_PAIRS, tile, PAIR_W), BF16),
            pltpu.VMEM((tile, POOL_DIM), BF16)]


def _weight_specs():
    return [_const_spec((1, D_MODEL)),
            _const_spec((D_MODEL, IN_DIM)),
            _const_spec((len(POOL_WINDOWS), POOL_GROUP_DIM, POOL_GROUP_DIM)),
            _const_spec((1, POOL_DIM)),
            _const_spec((POOL_DIM, D_MODEL)),
            _const_spec((ATTN_DIM, D_MODEL)),
            _const_spec((D_MODEL, D_MODEL))]


def _ctx_mixer(x, modv, weights):
    n = x.shape[0]
    nseq = TOK_BLOCK // SEQ
    cache = jax.ShapeDtypeStruct((n // SEQ, 1, N_HEADS, SEQ, HEAD_DIM), F32)
    cache_spec = pl.BlockSpec((nseq, 1, N_HEADS, SEQ, HEAD_DIM), lambda i: (i, 0, 0, 0, 0))
    return pl.pallas_call(
        _ctx_mixer_kernel,
        grid=(n // TOK_BLOCK,),
        in_specs=[pl.BlockSpec((TOK_BLOCK, D_MODEL), lambda i: (i, 0)),
                  pl.BlockSpec((1, N_MOD, D_MODEL), lambda i: (0, 0, 0))] + _weight_specs(),
        out_specs=[pl.BlockSpec((TOK_BLOCK, D_MODEL), lambda i: (i, 0)), cache_spec, cache_spec],
        out_shape=[jax.ShapeDtypeStruct((n, D_MODEL), F32), cache, cache],
        scratch_shapes=_mixer_scratch(TOK_BLOCK, F32),
        compiler_params=pltpu.CompilerParams(dimension_semantics=("arbitrary",),
                                             vmem_limit_bytes=V7X_VMEM_LIMIT),
        name="ctx_mixer",
    )(x, modv, *weights)


def _lat_mixer(x, modv, weights, kc, vc, tb):
    n = x.shape[0]
    return pl.pallas_call(
        _lat_mixer_kernel,
        grid=(n // DEC_SEQ,),
        in_specs=[pl.BlockSpec((DEC_SEQ, D_MODEL), lambda i: (i, 0)),
                  pl.BlockSpec((1, N_MOD, D_MODEL), lambda i: (i + 1, 0, 0))] + _weight_specs() + [
                  pl.BlockSpec((1, N_PAIRS, PAST_LEN, PAIR_W), lambda i: (i, 0, 0, 0)),
                  pl.BlockSpec((1, N_PAIRS, PAST_LEN, PAIR_W), lambda i: (i, 0, 0, 0)),
                  _const_spec((N_HEADS, 2, GRID_W, ROWS * GRID_W))],
        out_specs=pl.BlockSpec((DEC_SEQ, D_MODEL), lambda i: (i, 0)),
        out_shape=jax.ShapeDtypeStruct((n, D_MODEL), F32),
        scratch_shapes=_mixer_scratch(DEC_SEQ, BF16),
        compiler_params=pltpu.CompilerParams(dimension_semantics=("arbitrary",),
                                             vmem_limit_bytes=V7X_VMEM_LIMIT),
        name="lat_mixer",
    )(x, modv, *weights, kc, vc, tb)


N_RPB_ROWS = 2 * WIN_R - 1
N_RPB_COLS = 2 * WIN_C - 1
TABLE_W = ROWS * GRID_W


def _bias_kernel(v_ref, keep_ref, o_ref):
    for h in range(N_HEADS):
        for par in range(2):
            x = jnp.broadcast_to(v_ref[h, par:par + 1, :], (GRID_W, TABLE_W))
            shifted = pltpu.roll(x, TABLE_W - (WIN_C - 1), 1, stride=1, stride_axis=0)
            o_ref[h, par] = jnp.where(keep_ref[par] > 0.0, shifted, NEG_INF)


def _bias_tables(rpb):
    col = np.arange(GRID_W)
    cs = np.clip(col - WIN_C // 2, 0, GRID_W - WIN_C)
    in_win = (col[None, :] >= cs[:, None]) & (col[None, :] < cs[:, None] + WIN_C)
    keep = np.tile(in_win.astype(np.float32), (2, 1, ROWS))
    keep[0, :, N_RPB_ROWS * GRID_W:] = 0.0
    keep[1, :, (N_RPB_ROWS - 1) * GRID_W:] = 0.0
    rp = jnp.pad(rpb.astype(F32), ((0, 0), (0, ROWS + 1 - N_RPB_ROWS), (0, GRID_W - N_RPB_COLS)))
    v = jnp.stack([rp[:, :ROWS].reshape(N_HEADS, TABLE_W), rp[:, 1:].reshape(N_HEADS, TABLE_W)], axis=1)
    return pl.pallas_call(
        _bias_kernel,
        out_shape=jax.ShapeDtypeStruct((N_HEADS, 2, GRID_W, TABLE_W), F32),
        name="bias_tables",
    )(v, jnp.asarray(keep))


SORT_ROWS = 512


def _sort_kernel(xc_ref, xl_ref, mod_ref, gffn_ref, wr_ref, br_ref,
                 xs_ref, dk_ref, np_ref, off_ref):
    j = pl.program_id(0)
    x = jnp.where(j < _N_CTX_BLOCKS, xc_ref[...], xl_ref[...])
    shift, scale = mod_ref[0, 3:4, :], mod_ref[0, 4:5, :]
    hb = _norm_mod(x, gffn_ref[...], scale, shift).astype(BF16)
    logits = _dot_nt(wr_ref[...], hb) + br_ref[...]
    eio = lax.broadcasted_iota(jnp.int32, logits.shape, 0)
    work = logits
    sels, vals = [], []
    for _ in range(TOP_K):
        m = jnp.max(work, axis=0, keepdims=True)
        idx = jnp.min(jnp.where(work == m, eio, N_EXPERTS), axis=0, keepdims=True)
        sel = eio == idx
        sels.append(sel)
        vals.append(m)
        work = jnp.where(sel, -jnp.inf, work)
    exps = [jnp.exp(v - vals[0]) for v in vals]
    den = exps[0] + exps[1] + exps[2] + exps[3]
    mask = jnp.zeros(logits.shape, F32)
    for sel in sels:
        mask = mask + jnp.where(sel, 1.0, 0.0)
    t_row = lax.broadcasted_iota(jnp.int32, (TOK_BLOCK, TOK_BLOCK), 0)
    t_col = lax.broadcasted_iota(jnp.int32, (TOK_BLOCK, TOK_BLOCK), 1)
    before = jnp.where(t_row < t_col, 1.0, 0.0).astype(BF16)
    rank = _dot(mask.astype(BF16), before)
    cnt = jnp.sum(mask, axis=1, keepdims=True)
    np16 = jnp.floor((cnt + (CHUNK - 1.0)) * (1.0 / CHUNK))
    e_row = lax.broadcasted_iota(jnp.int32, (N_EXPERTS, N_EXPERTS), 0)
    e_col = lax.broadcasted_iota(jnp.int32, (N_EXPERTS, N_EXPERTS), 1)
    lower = jnp.where(e_col < e_row, 1.0, 0.0).astype(BF16)
    np16_b = jnp.broadcast_to(np16, (N_EXPERTS, 128))
    off16 = _dot(lower, np16_b.astype(BF16))
    np_ref[0] = np16_b
    off_ref[0] = off16
    dest = off16[:, 0:1] * float(CHUNK) + rank
    dests = []
    for k in range(TOP_K):
        dk = jnp.sum(jnp.where(sels[k], dest, 0.0), axis=0, keepdims=True)
        dk_ref[0, k:k + 1, :] = dk
        dk_ref[0, TOP_K + k:TOP_K + k + 1, :] = exps[k] / den
        dests.append(dk.astype(jnp.int32))
    rio = lax.broadcasted_iota(jnp.int32, (SORT_ROWS, TOK_BLOCK), 0)
    for c in range(BLOCK_CAP // SORT_ROWS):
        onehot = jnp.zeros((SORT_ROWS, TOK_BLOCK), F32)
        for dk in dests:
            onehot = jnp.where(rio == dk - c * SORT_ROWS, 1.0, onehot)
        xs_ref[c * SORT_ROWS:(c + 1) * SORT_ROWS, :] = _dot(onehot.astype(BF16), hb).astype(BF16)


def _block_mod_index(j):
    return jnp.where(j < _N_CTX_BLOCKS, 0, 1 + (j - _N_CTX_BLOCKS) // (DEC_SEQ // TOK_BLOCK))


def _token_specs():
    return [pl.BlockSpec((TOK_BLOCK, D_MODEL), lambda j: (jnp.minimum(j, _N_CTX_BLOCKS - 1), 0)),
            pl.BlockSpec((TOK_BLOCK, D_MODEL), lambda j: (jnp.maximum(j - _N_CTX_BLOCKS, 0), 0)),
            pl.BlockSpec((1, N_MOD, D_MODEL), lambda j: (_block_mod_index(j), 0, 0))]


def _route_sort(x1c, x1l, modv, g_ffn, wr_t, br):
    tbl = jax.ShapeDtypeStruct((_N_BLOCKS, N_EXPERTS, 128), F32)
    tbl_spec = pl.BlockSpec((1, N_EXPERTS, 128), lambda j: (j, 0, 0))
    return pl.pallas_call(
        _sort_kernel,
        grid=(_N_BLOCKS,),
        in_specs=_token_specs() + [_const_spec((1, D_MODEL)),
                                   _const_spec((N_EXPERTS, D_MODEL)),
                                   _const_spec((N_EXPERTS, 1))],
        out_specs=[pl.BlockSpec((BLOCK_CAP, D_MODEL), lambda j: (j, 0)),
                   pl.BlockSpec((1, 2 * TOP_K, TOK_BLOCK), lambda j: (j, 0, 0)),
                   tbl_spec, tbl_spec],
        out_shape=[jax.ShapeDtypeStruct((_N_BLOCKS * BLOCK_CAP, D_MODEL), BF16),
                   jax.ShapeDtypeStruct((_N_BLOCKS, 2 * TOP_K, TOK_BLOCK), F32),
                   tbl, tbl],
        compiler_params=pltpu.CompilerParams(dimension_semantics=("arbitrary",),
                                             vmem_limit_bytes=V7X_VMEM_LIMIT),
        name="route_sort",
    )(x1c, x1l, modv, g_ffn, wr_t, br)


EXPERT_BUF_ROWS = 2048
PASS_CHUNKS = EXPERT_BUF_ROWS // CHUNK
N_ROW_BUFS = 3


def _expert_kernel(np_ref, off_ref, wu_ref, bu_ref, wd_ref, bd_ref, xs_hbm, ys_hbm,
                   buf, wu_s, wd_s, gsem, ssem, pend):
    del xs_hbm
    e = pl.program_id(0)
    last = pl.num_programs(0) - 1
    slot = e % N_ROW_BUFS
    nxt = (e + 1) % N_ROW_BUFS

    def move_chunks(ex, q_lo, slot_, gather):
        def block_body(j, q0):
            n = np_ref[j * N_EXPERTS + ex]
            off = off_ref[j * N_EXPERTS + ex]
            c_lo = jnp.clip(q_lo - q0, 0, n)
            m = jnp.clip(q_lo + PASS_CHUNKS - q0, 0, n) - c_lo

            @pl.when(m > 0)
            def _():
                rows = pl.multiple_of(m * CHUNK, CHUNK)
                row0 = pl.multiple_of(j * BLOCK_CAP + (off + c_lo) * CHUNK, CHUNK)
                brow0 = pl.multiple_of((q0 + c_lo - q_lo) * CHUNK, CHUNK)
                hbm = ys_hbm.at[pl.ds(row0, rows), :]
                vm = buf.at[slot_, pl.ds(brow0, rows), :]
                if gather:
                    pltpu.make_async_copy(hbm, vm, gsem.at[slot_]).start()
                else:
                    pltpu.make_async_copy(vm, hbm, ssem.at[slot_]).start()

            return q0 + n

        lax.fori_loop(0, _N_BLOCKS, block_body, 0)

    def wait_chunks(sem, n, slot_):
        @pl.when(n > 0)
        def _():
            rows = pl.multiple_of(n * CHUNK, CHUNK)
            pltpu.make_async_copy(ys_hbm.at[pl.ds(0, rows), :], buf.at[slot_, pl.ds(0, rows), :], sem).wait()

    def mlp_rows(r0, rows):
        x = buf[slot, pl.ds(r0, rows), :]
        gu = _dot(x, wu_s[...]) + bu_ref[0]
        gate = jnp.minimum(gu[:, :D_FF], SWIGLU_LIMIT)
        up = jnp.clip(gu[:, D_FF:], -SWIGLU_LIMIT, SWIGLU_LIMIT)
        glu = gate * _sigmoid(SWIGLU_ALPHA * gate)
        y = _dot(((up + 1.0) * glu).astype(BF16), wd_s[...]) + bd_ref[0]
        buf[slot, pl.ds(r0, rows), :] = y.astype(BF16)

    def compute(n):
        rows = (n * CHUNK + ROW_STEP - 1) // ROW_STEP * ROW_STEP
        n_main = (jnp.maximum(rows - LAST_ROWS_MAX, 0) + EXPERT_ROW_TILE - 1) // EXPERT_ROW_TILE

        def tile_body(t, carry):
            mlp_rows(pl.multiple_of(t * EXPERT_ROW_TILE, ROW_STEP), EXPERT_ROW_TILE)
            return carry

        lax.fori_loop(0, n_main, tile_body, 0)
        r0 = pl.multiple_of(n_main * EXPERT_ROW_TILE, ROW_STEP)
        for size in range(ROW_STEP, LAST_ROWS_MAX + ROW_STEP, ROW_STEP):
            pl.when(rows - r0 == size)(functools.partial(mlp_rows, r0, size))

    def chunks_of(ex):
        return lax.fori_loop(0, _N_BLOCKS, lambda j, acc: acc + np_ref[j * N_EXPERTS + ex], 0)

    @pl.when(e == 0)
    def _():
        buf[...] = jnp.zeros_like(buf)
        for b in range(N_ROW_BUFS):
            pend[b] = 0
        move_chunks(0, 0, 0, True)

    wait_chunks(ssem.at[nxt], pend[nxt], nxt)
    pend[nxt] = 0

    @pl.when(e < last)
    def _():
        move_chunks(e + 1, 0, nxt, True)

    wu_s[...] = wu_ref[0].astype(BF16)
    wd_s[...] = wd_ref[0].astype(BF16)

    total = chunks_of(e)
    n0 = jnp.minimum(total, PASS_CHUNKS)
    wait_chunks(gsem.at[slot], n0, slot)
    compute(n0)
    move_chunks(e, 0, slot, False)
    pend[slot] = n0

    def pass_body(p, carry):
        wait_chunks(ssem.at[slot], pend[slot], slot)
        lo = p * PASS_CHUNKS
        n = jnp.minimum(total - lo, PASS_CHUNKS)
        move_chunks(e, lo, slot, True)
        wait_chunks(gsem.at[slot], n, slot)
        compute(n)
        move_chunks(e, lo, slot, False)
        pend[slot] = n
        return carry

    lax.fori_loop(1, (total + PASS_CHUNKS - 1) // PASS_CHUNKS, pass_body, 0)

    @pl.when(e == last)
    def _():
        for b in range(N_ROW_BUFS):
            wait_chunks(ssem.at[b], pend[b], b)
            pend[b] = 0


def _experts(np16, off16, w_up, b_up, w_down, b_down, xs):
    grid_spec = pltpu.PrefetchScalarGridSpec(
        num_scalar_prefetch=2,
        grid=(N_EXPERTS,),
        in_specs=[pl.BlockSpec((1, D_MODEL, 2 * D_FF), lambda e, *_: (e, 0, 0)),
                  pl.BlockSpec((1, 1, 2 * D_FF), lambda e, *_: (e, 0, 0)),
                  pl.BlockSpec((1, D_FF, D_MODEL), lambda e, *_: (e, 0, 0)),
                  pl.BlockSpec((1, 1, D_MODEL), lambda e, *_: (e, 0, 0)),
                  pl.BlockSpec(memory_space=pl.ANY)],
        out_specs=pl.BlockSpec(memory_space=pl.ANY),
        scratch_shapes=[pltpu.VMEM((N_ROW_BUFS, EXPERT_BUF_ROWS, D_MODEL), BF16),
                        pltpu.VMEM((D_MODEL, 2 * D_FF), BF16),
                        pltpu.VMEM((D_FF, D_MODEL), BF16),
                        pltpu.SemaphoreType.DMA((N_ROW_BUFS,)),
                        pltpu.SemaphoreType.DMA((N_ROW_BUFS,)),
                        pltpu.SMEM((N_ROW_BUFS,), jnp.int32)],
    )
    return pl.pallas_call(
        _expert_kernel,
        grid_spec=grid_spec,
        out_shape=jax.ShapeDtypeStruct(xs.shape, xs.dtype),
        input_output_aliases={6: 0},
        compiler_params=pltpu.CompilerParams(dimension_semantics=("arbitrary",),
                                             vmem_limit_bytes=V7X_VMEM_LIMIT),
        name="experts",
    )(np16, off16, w_up, b_up.reshape(N_EXPERTS, 1, 2 * D_FF), w_down,
      b_down.reshape(N_EXPERTS, 1, D_MODEL), xs)


def _combine_kernel(ys_ref, dk_ref, xc_ref, xl_ref, mod_ref, gfin_ref, yc_ref, yl_ref, y_s):
    j = pl.program_id(0)
    half = TOK_BLOCK // 2
    cio = lax.broadcasted_iota(jnp.int32, (half, SORT_ROWS), 1).astype(F32)
    halves = [slice(h * half, (h + 1) * half) for h in range(2)]
    accs = [jnp.zeros((half, D_MODEL), F32) for _ in halves]
    for c in range(BLOCK_CAP // SORT_ROWS):
        ys = ys_ref[c * SORT_ROWS:(c + 1) * SORT_ROWS, :]
        for h, rows in enumerate(halves):
            d = dk_ref[0, rows, :]
            w = jnp.zeros((half, SORT_ROWS), F32)
            for k in range(TOP_K):
                w = jnp.where(cio == d[:, k:k + 1] - float(c * SORT_ROWS), d[:, TOP_K + k:TOP_K + k + 1], w)
            accs[h] = accs[h] + _dot(w.astype(BF16), ys)
    for h, rows in enumerate(halves):
        acc = accs[h]
        x1 = jnp.where(j < _N_CTX_BLOCKS, xc_ref[rows, :], xl_ref[rows, :])
        x2 = x1 + mod_ref[0, 5:6, :] * acc
        ms = jnp.mean(x2 * x2, axis=-1, keepdims=True)
        y_s[rows, :] = x2 * lax.rsqrt(ms + RMS_EPS) * gfin_ref[...]

    @pl.when(j < _N_CTX_BLOCKS)
    def _():
        yc_ref[...] = y_s[...]

    @pl.when(j >= _N_CTX_BLOCKS)
    def _():
        yl_ref[...] = y_s[...]


def _combine(ys, dk_t, x1c, x1l, modv, g_final):
    return pl.pallas_call(
        _combine_kernel,
        grid=(_N_BLOCKS,),
        in_specs=[pl.BlockSpec((BLOCK_CAP, D_MODEL), lambda j: (j, 0)),
                  pl.BlockSpec((1, TOK_BLOCK, 2 * TOP_K), lambda j: (j, 0, 0))] + _token_specs() + [
                  _const_spec((1, D_MODEL))],
        out_specs=[pl.BlockSpec((TOK_BLOCK, D_MODEL), lambda j: (jnp.minimum(j, _N_CTX_BLOCKS - 1), 0)),
                   pl.BlockSpec((TOK_BLOCK, D_MODEL), lambda j: (jnp.maximum(j - _N_CTX_BLOCKS, 0), 0))],
        out_shape=[jax.ShapeDtypeStruct(x1c.shape, F32), jax.ShapeDtypeStruct(x1l.shape, F32)],
        scratch_shapes=[pltpu.VMEM((TOK_BLOCK, D_MODEL), F32)],
        compiler_params=pltpu.CompilerParams(dimension_semantics=("arbitrary",),
                                             vmem_limit_bytes=V7X_VMEM_LIMIT),
        name="combine",
    )(ys, dk_t, x1c, x1l, modv, g_final)


def kernel(x_prompt, x_sample, cache_k, cache_v, c, c_ctx, w_ada, b_ada, g_mix, w_in, w_pool, pool_scale,
           w_pa, w_pb, rpb, w_out, g_ffn, w_router, b_router, w_up, b_up, w_down, b_down, g_final):
    assert w_ada.shape[0] == 1, "single trunk layer"
    batch, seq, d = x_prompt.shape
    dec_batch, dec_seq, _ = x_sample.shape
    assert (seq, dec_seq, d) == (SEQ, DEC_SEQ, D_MODEL)
    assert batch * seq == _N_CTX_BLOCKS * TOK_BLOCK and dec_batch * dec_seq == _N_LAT_BLOCKS * TOK_BLOCK

    cmat = jnp.concatenate([c_ctx[None, :], c, jnp.zeros((8 - 1 - dec_batch, d), F32)], axis=0)
    modv = _modulation(cmat, w_ada[0], b_ada[0]).reshape(8, N_MOD, d)

    weights = (g_mix[0][None, :], w_in[0].astype(BF16), w_pool[0].astype(BF16), pool_scale[0][None, :],
               w_pa[0].astype(BF16), w_pb[0].astype(BF16), w_out[0].astype(BF16))

    def by_pair(cache):
        z = cache[:, 0].reshape(dec_batch, N_PAIRS, 2, PAST_LEN, HEAD_DIM)
        return z.transpose(0, 1, 3, 2, 4).reshape(dec_batch, N_PAIRS, PAST_LEN, PAIR_W)

    x1c, new_k, new_v = _ctx_mixer(x_prompt.reshape(batch * seq, d), modv, weights)
    x1l = _lat_mixer(x_sample.reshape(dec_batch * dec_seq, d), modv, weights,
                     by_pair(cache_k), by_pair(cache_v), _bias_tables(rpb[0]))

    xs, dk, np16, off16 = _route_sort(x1c, x1l, modv, g_ffn[0][None, :],
                                      w_router[0].T.astype(BF16), b_router[0][:, None])
    np16_i = np16[:, :, 0].astype(jnp.int32).reshape(-1)
    off16_i = off16[:, :, 0].astype(jnp.int32).reshape(-1)
    ys = _experts(np16_i, off16_i, w_up[0], b_up[0], w_down[0], b_down[0], xs)
    yc, yl = _combine(ys, dk.transpose(0, 2, 1), x1c, x1l, modv, g_final[None, :])
    return (yc.reshape(batch, seq, d), yl.reshape(dec_batch, dec_seq, d), new_k, new_v)
```

```python
import functools

import jax
import jax.numpy as jnp
import numpy as np
from jax import lax
from jax.experimental import pallas as pl
from jax.experimental.pallas import tpu as pltpu

F32 = jnp.float32
BF16 = jnp.bfloat16

D_MODEL = 1024
SEQ = 256
DEC_SEQ = 1024
GRID_W = 64
ROWS = DEC_SEQ // GRID_W
N_HEADS = 8
HEAD_DIM = 64
N_PAIRS = N_HEADS // 2
PAIR_W = 2 * HEAD_DIM
PAST_LEN = 512
POOL_DIM = 512
POOL_WINDOWS = (2, 4, 8, 16)
POOL_GROUP_DIM = 128
ATTN_DIM = 512
WIN_R = 8
WIN_C = 16
N_EXPERTS = 32
TOP_K = 4
D_FF = 1024
SWIGLU_LIMIT = 7.0
SWIGLU_ALPHA = 1.702
N_MOD = 6
RMS_EPS = 1e-6
NEG_INF = -1e30
ATTN_SCALE = HEAD_DIM ** -0.5
IN_DIM = POOL_DIM + 3 * ATTN_DIM + 2 * D_MODEL

TOK_BLOCK = 512
CHUNK = 16
BLOCK_CAP = TOK_BLOCK * TOP_K + N_EXPERTS * CHUNK
ROW_STEP = 128
EXPERT_ROW_TILE = 384
LAST_ROWS_MAX = 512
V7X_VMEM_LIMIT = 60 * 1024 * 1024

_N_CTX_BLOCKS = 8
_N_LAT_BLOCKS = 4
_N_BLOCKS = _N_CTX_BLOCKS + _N_LAT_BLOCKS

_NT = (((1,), (1,)), ((), ()))


def _dot(a, b):
    return jnp.dot(a, b, preferred_element_type=F32)


def _dot_nt(a, b):
    return lax.dot_general(a, b, _NT, preferred_element_type=F32)


def _sigmoid(x):
    return 1.0 / (1.0 + jnp.exp(-x))


def _norm_mod(x, gain, scale, shift):
    ms = jnp.mean(x * x, axis=-1, keepdims=True)
    return (x * lax.rsqrt(ms + RMS_EPS) * gain) * (1.0 + scale) + shift


def _const_spec(shape):
    zeros = (0,) * len(shape)
    return pl.BlockSpec(shape, lambda *_: zeros, pipeline_mode=pl.Buffered(1))


MOD_COLS = 1536


def _mod_kernel(c_ref, w_ref, b_ref, o_ref):
    c = c_ref[...]
    s = c * _sigmoid(c)
    s_hi = s.astype(BF16)
    s_lo = (s - s_hi.astype(F32)).astype(BF16)
    r = _dot(jnp.concatenate([s_hi, s_lo], axis=0), w_ref[...].astype(BF16))
    o_ref[...] = r[:8] + r[8:] + b_ref[...]


def _modulation(cmat, w_ada, b_ada):
    n = w_ada.shape[1]
    return pl.pallas_call(
        _mod_kernel,
        grid=(n // MOD_COLS,),
        in_specs=[pl.BlockSpec((8, D_MODEL), lambda i: (0, 0)),
                  pl.BlockSpec((D_MODEL, MOD_COLS), lambda i: (0, i)),
                  pl.BlockSpec((1, MOD_COLS), lambda i: (0, i))],
        out_specs=pl.BlockSpec((8, MOD_COLS), lambda i: (0, i)),
        out_shape=jax.ShapeDtypeStruct((8, n), F32),
        name="modulation",
    )(cmat, w_ada, b_ada.reshape(1, n))


def _pool_mix(u, pos, seq):
    n = u.shape[0]

    def down(x, d):
        return jnp.where(pos >= d, pltpu.roll(x, d, 0), 0.0)

    def up(x, d):
        return jnp.where(pos < seq - d, pltpu.roll(x, n - d, 0), 0.0)

    return down, up


def _pool_group(u, pos, seq, w):
    down, up = _pool_mix(u, pos, seq)
    hw = w // 2
    back = u
    fwd = u
    d = 1
    while d < hw:
        back = back + down(back, d)
        fwd = fwd + up(fwd, d)
        d *= 2
    s = down(back, 1) + fwd
    posf = pos.astype(F32)
    cnt = jnp.minimum(posf + hw, float(seq)) - jnp.maximum(posf - hw, 0.0)
    return s / cnt - u


def _pass_rows(tile):
    return tile if tile <= TOK_BLOCK else tile // 2


def _mixer_front(x_ref, mod_ref, gmix_ref, win_ref, hb_s, u_s, q_s, k_s, v_s):
    shift, scale = mod_ref[0, 0:1, :], mod_ref[0, 1:2, :]
    step = _pass_rows(x_ref.shape[0])
    for c in range(x_ref.shape[0] // step):
        rows = slice(c * step, (c + 1) * step)
        hb = _norm_mod(x_ref[rows, :], gmix_ref[...], scale, shift).astype(BF16)
        hb_s[rows, :] = hb
        u_s[rows, :] = _dot(hb, win_ref[:, 0:POOL_DIM])
        for dst, base in ((q_s, POOL_DIM), (k_s, POOL_DIM + ATTN_DIM), (v_s, POOL_DIM + 2 * ATTN_DIM)):
            z = _dot(hb, win_ref[:, base:base + ATTN_DIM])
            if dst is q_s:
                z = z * ATTN_SCALE
            for g in range(N_PAIRS):
                dst[g, rows, :] = z[:, g * PAIR_W:(g + 1) * PAIR_W].astype(dst.dtype)


def _mixer_back(x_ref, mod_ref, win_ref, wpool_ref, ps_ref, wpa_ref, wpb_ref, wout_ref, x1_ref,
                hb_s, u_s, o_s, pg_s, seq):
    tile = x_ref.shape[0]
    gate = mod_ref[0, 2:3, :]
    pos = lax.broadcasted_iota(jnp.int32, (tile, 1), 0) % seq
    for g, w in enumerate(POOL_WINDOWS):
        cols = slice(g * POOL_GROUP_DIM, (g + 1) * POOL_GROUP_DIM)
        pg_s[:, cols] = _pool_group(u_s[:, cols], pos, seq, w).astype(BF16)
    step = _pass_rows(tile)
    for c in range(tile // step):
        rows = slice(c * step, (c + 1) * step)
        ys = []
        for g in range(len(POOL_WINDOWS)):
            cols = slice(g * POOL_GROUP_DIM, (g + 1) * POOL_GROUP_DIM)
            ys.append((_dot(pg_s[rows, cols], wpool_ref[g]) * ps_ref[:, cols]).astype(BF16))
        a = _dot(jnp.concatenate(ys, axis=1), wpa_ref[...])
        ob = _dot(jnp.concatenate([o_s[g, rows, :] for g in range(N_PAIRS)], axis=1), wpb_ref[...])
        gab = _dot(hb_s[rows, :], win_ref[:, POOL_DIM + 3 * ATTN_DIM:IN_DIM])
        merged = _sigmoid(gab[:, :D_MODEL]) * a + _sigmoid(gab[:, D_MODEL:]) * ob
        mix = _dot(merged.astype(BF16), wout_ref[...])
        x1_ref[rows, :] = x_ref[rows, :] + gate * mix


def _ctx_mixer_kernel(x_ref, mod_ref, gmix_ref, win_ref, wpool_ref, ps_ref, wpa_ref, wpb_ref, wout_ref,
                      x1_ref, ko_ref, vo_ref, hb_s, u_s, q_s, k_s, v_s, o_s, pg_s):
    _mixer_front(x_ref, mod_ref, gmix_ref, win_ref, hb_s, u_s, q_s, k_s, v_s)
    even = lax.broadcasted_iota(jnp.int32, (1, PAIR_W), 1) < HEAD_DIM
    tile = x_ref.shape[0]
    for s in range(tile // SEQ):
        rows = slice(s * SEQ, (s + 1) * SEQ)
        for g in range(N_PAIRS):
            q2, k2, v2 = q_s[g, rows, :], k_s[g, rows, :], v_s[g, rows, :]
            ko_ref[s, 0, 2 * g] = k2[:, :HEAD_DIM]
            ko_ref[s, 0, 2 * g + 1] = k2[:, HEAD_DIM:]
            vo_ref[s, 0, 2 * g] = v2[:, :HEAD_DIM]
            vo_ref[s, 0, 2 * g + 1] = v2[:, HEAD_DIM:]
            kb, vb = k2.astype(BF16), v2.astype(BF16)
            outs = []
            for par in range(2):
                qm = jnp.where(even if par == 0 else jnp.logical_not(even), q2, 0.0).astype(BF16)
                sc = _dot_nt(qm, kb)
                m = jnp.max(sc, axis=-1, keepdims=True)
                p = jnp.exp(sc - m)
                l = jnp.sum(p, axis=-1, keepdims=True)
                outs.append(_dot(p.astype(BF16), vb) / l)
            o_s[g, rows, :] = jnp.where(even, outs[0], outs[1]).astype(BF16)
    _mixer_back(x_ref, mod_ref, win_ref, wpool_ref, ps_ref, wpa_ref, wpb_ref, wout_ref, x1_ref,
                hb_s, u_s, o_s, pg_s, SEQ)


def _row_window(r):
    rs = min(max(r - WIN_R // 2, 0), ROWS - WIN_R)
    return rs, rs - r + WIN_R - 1


def _lat_mixer_kernel(x_ref, mod_ref, gmix_ref, win_ref, wpool_ref, ps_ref, wpa_ref, wpb_ref, wout_ref,
                      kc_ref, vc_ref, tb_ref, x1_ref, hb_s, u_s, q_s, k_s, v_s, o_s, pg_s):
    _mixer_front(x_ref, mod_ref, gmix_ref, win_ref, hb_s, u_s, q_s, k_s, v_s)
    even = lax.broadcasted_iota(jnp.int32, (1, PAIR_W), 1) < HEAD_DIM
    nk = WIN_R * GRID_W

    def pair_body(g, carry):
        q2 = q_s[g]
        kb, vb = k_s[g].astype(BF16), v_s[g].astype(BF16)
        kcb, vcb = kc_ref[0, g].astype(BF16), vc_ref[0, g].astype(BF16)
        outs = []
        for par in range(2):
            qm = jnp.where(even if par == 0 else jnp.logical_not(even), q2, 0.0).astype(BF16)
            s_ctx = _dot_nt(qm, kcb)
            slabs = []
            for r in range(ROWS):
                rs, rho = _row_window(r)
                bias = tb_ref[2 * g + par, rho % 2, :, (rho - rho % 2) * GRID_W:(rho - rho % 2) * GRID_W + nk]
                sl = _dot_nt(qm[r * GRID_W:(r + 1) * GRID_W, :], kb[rs * GRID_W:rs * GRID_W + nk, :])
                slabs.append(sl + bias)
            s_loc = jnp.concatenate(slabs, axis=0)
            m = jnp.maximum(jnp.max(s_loc, axis=-1, keepdims=True), jnp.max(s_ctx, axis=-1, keepdims=True))
            p_loc = jnp.exp(s_loc - m)
            p_ctx = jnp.exp(s_ctx - m)
            l = jnp.sum(p_loc, axis=-1, keepdims=True) + jnp.sum(p_ctx, axis=-1, keepdims=True)
            p_locb = p_loc.astype(BF16)
            o_rows = []
            for r in range(ROWS):
                rs, _ = _row_window(r)
                o_rows.append(_dot(p_locb[r * GRID_W:(r + 1) * GRID_W, :], vb[rs * GRID_W:rs * GRID_W + nk, :]))
            o = jnp.concatenate(o_rows, axis=0) + _dot(p_ctx.astype(BF16), vcb)
            outs.append(o / l)
        o_s[g] = jnp.where(even, outs[0], outs[1]).astype(BF16)
        return carry

    lax.fori_loop(0, N_PAIRS, pair_body, 0)
    _mixer_back(x_ref, mod_ref, win_ref, wpool_ref, ps_ref, wpa_ref, wpb_ref, wout_ref, x1_ref,
                hb_s, u_s, o_s, pg_s, DEC_SEQ)


def _mixer_scratch(tile, kv_dtype):
    return [pltpu.VMEM((tile, D_MODEL), BF16),
            pltpu.VMEM((tile, POOL_DIM), F32),
            pltpu.VMEM((N_PAIRS, tile, PAIR_W), BF16),
            pltpu.VMEM((N_PAIRS, tile, PAIR_W), kv_dtype),
            pltpu.VMEM((N_PAIRS, tile, PAIR_W), kv_dtype),
            pltpu.VMEM((N_PAIRS, tile, PAIR_W), BF16),
            pltpu.VMEM((tile, POOL_DIM), BF16)]


def _weight_specs():
    return [_const_spec((1, D_MODEL)),
            _const_spec((D_MODEL, IN_DIM)),
            _const_spec((len(POOL_WINDOWS), POOL_GROUP_DIM, POOL_GROUP_DIM)),
            _const_spec((1, POOL_DIM)),
            _const_spec((POOL_DIM, D_MODEL)),
            _const_spec((ATTN_DIM, D_MODEL)),
            _const_spec((D_MODEL, D_MODEL))]


def _ctx_mixer(x, modv, weights):
    n = x.shape[0]
    nseq = TOK_BLOCK // SEQ
    cache = jax.ShapeDtypeStruct((n // SEQ, 1, N_HEADS, SEQ, HEAD_DIM), F32)
    cache_spec = pl.BlockSpec((nseq, 1, N_HEADS, SEQ, HEAD_DIM), lambda i: (i, 0, 0, 0, 0))
    return pl.pallas_call(
        _ctx_mixer_kernel,
        grid=(n // TOK_BLOCK,),
        in_specs=[pl.BlockSpec((TOK_BLOCK, D_MODEL), lambda i: (i, 0)),
                  pl.BlockSpec((1, N_MOD, D_MODEL), lambda i: (0, 0, 0))] + _weight_specs(),
        out_specs=[pl.BlockSpec((TOK_BLOCK, D_MODEL), lambda i: (i, 0)), cache_spec, cache_spec],
        out_shape=[jax.ShapeDtypeStruct((n, D_MODEL), F32), cache, cache],
        scratch_shapes=_mixer_scratch(TOK_BLOCK, F32),
        compiler_params=pltpu.CompilerParams(dimension_semantics=("arbitrary",),
                                             vmem_limit_bytes=V7X_VMEM_LIMIT),
        name="ctx_mixer",
    )(x, modv, *weights)


def _lat_mixer(x, modv, weights, kc, vc, tb):
    n = x.shape[0]
    return pl.pallas_call(
        _lat_mixer_kernel,
        grid=(n // DEC_SEQ,),
        in_specs=[pl.BlockSpec((DEC_SEQ, D_MODEL), lambda i: (i, 0)),
                  pl.BlockSpec((1, N_MOD, D_MODEL), lambda i: (i + 1, 0, 0))] + _weight_specs() + [
                  pl.BlockSpec((1, N_PAIRS, PAST_LEN, PAIR_W), lambda i: (i, 0, 0, 0)),
                  pl.BlockSpec((1, N_PAIRS, PAST_LEN, PAIR_W), lambda i: (i, 0, 0, 0)),
                  _const_spec((N_HEADS, 2, GRID_W, ROWS * GRID_W))],
        out_specs=pl.BlockSpec((DEC_SEQ, D_MODEL), lambda i: (i, 0)),
        out_shape=jax.ShapeDtypeStruct((n, D_MODEL), F32),
        scratch_shapes=_mixer_scratch(DEC_SEQ, BF16),
        compiler_params=pltpu.CompilerParams(dimension_semantics=("arbitrary",),
                                             vmem_limit_bytes=V7X_VMEM_LIMIT),
        name="lat_mixer",
    )(x, modv, *weights, kc, vc, tb)


N_RPB_ROWS = 2 * WIN_R - 1
N_RPB_COLS = 2 * WIN_C - 1
TABLE_W = ROWS * GRID_W


def _bias_kernel(v_ref, keep_ref, o_ref):
    for h in range(N_HEADS):
        for par in range(2):
            x = jnp.broadcast_to(v_ref[h, par:par + 1, :], (GRID_W, TABLE_W))
            shifted = pltpu.roll(x, TABLE_W - (WIN_C - 1), 1, stride=1, stride_axis=0)
            o_ref[h, par] = jnp.where(keep_ref[par] > 0.0, shifted, NEG_INF)


def _bias_tables(rpb):
    col = np.arange(GRID_W)
    cs = np.clip(col - WIN_C // 2, 0, GRID_W - WIN_C)
    in_win = (col[None, :] >= cs[:, None]) & (col[None, :] < cs[:, None] + WIN_C)
    keep = np.tile(in_win.astype(np.float32), (2, 1, ROWS))
    keep[0, :, N_RPB_ROWS * GRID_W:] = 0.0
    keep[1, :, (N_RPB_ROWS - 1) * GRID_W:] = 0.0
    rp = jnp.pad(rpb.astype(F32), ((0, 0), (0, ROWS + 1 - N_RPB_ROWS), (0, GRID_W - N_RPB_COLS)))
    v = jnp.stack([rp[:, :ROWS].reshape(N_HEADS, TABLE_W), rp[:, 1:].reshape(N_HEADS, TABLE_W)], axis=1)
    return pl.pallas_call(
        _bias_kernel,
        out_shape=jax.ShapeDtypeStruct((N_HEADS, 2, GRID_W, TABLE_W), F32),
        name="bias_tables",
    )(v, jnp.asarray(keep))


SORT_ROWS = 512


def _sort_kernel(xc_ref, xl_ref, mod_ref, gffn_ref, wr_ref, br_ref,
                 xs_ref, dk_ref, np_ref, off_ref):
    j = pl.program_id(0)
    x = jnp.where(j < _N_CTX_BLOCKS, xc_ref[...], xl_ref[...])
    shift, scale = mod_ref[0, 3:4, :], mod_ref[0, 4:5, :]
    hb = _norm_mod(x, gffn_ref[...], scale, shift).astype(BF16)
    logits = _dot_nt(wr_ref[...], hb) + br_ref[...]
    eio = lax.broadcasted_iota(jnp.int32, logits.shape, 0)
    work = logits
    sels, vals = [], []
    for _ in range(TOP_K):
        m = jnp.max(work, axis=0, keepdims=True)
        idx = jnp.min(jnp.where(work == m, eio, N_EXPERTS), axis=0, keepdims=True)
        sel = eio == idx
        sels.append(sel)
        vals.append(m)
        work = jnp.where(sel, -jnp.inf, work)
    exps = [jnp.exp(v - vals[0]) for v in vals]
    den = exps[0] + exps[1] + exps[2] + exps[3]
    mask = jnp.zeros(logits.shape, F32)
    for sel in sels:
        mask = mask + jnp.where(sel, 1.0, 0.0)
    t_row = lax.broadcasted_iota(jnp.int32, (TOK_BLOCK, TOK_BLOCK), 0)
    t_col = lax.broadcasted_iota(jnp.int32, (TOK_BLOCK, TOK_BLOCK), 1)
    before = jnp.where(t_row < t_col, 1.0, 0.0).astype(BF16)
    rank = _dot(mask.astype(BF16), before)
    cnt = jnp.sum(mask, axis=1, keepdims=True)
    np16 = jnp.floor((cnt + (CHUNK - 1.0)) * (1.0 / CHUNK))
    e_row = lax.broadcasted_iota(jnp.int32, (N_EXPERTS, N_EXPERTS), 0)
    e_col = lax.broadcasted_iota(jnp.int32, (N_EXPERTS, N_EXPERTS), 1)
    lower = jnp.where(e_col < e_row, 1.0, 0.0).astype(BF16)
    np16_b = jnp.broadcast_to(np16, (N_EXPERTS, 128))
    off16 = _dot(lower, np16_b.astype(BF16))
    np_ref[0] = np16_b
    off_ref[0] = off16
    dest = off16[:, 0:1] * float(CHUNK) + rank
    dests = []
    for k in range(TOP_K):
        dk = jnp.sum(jnp.where(sels[k], dest, 0.0), axis=0, keepdims=True)
        dk_ref[0, k:k + 1, :] = dk
        dk_ref[0, TOP_K + k:TOP_K + k + 1, :] = exps[k] / den
        dests.append(dk.astype(jnp.int32))
    rio = lax.broadcasted_iota(jnp.int32, (SORT_ROWS, TOK_BLOCK), 0)
    for c in range(BLOCK_CAP // SORT_ROWS):
        onehot = jnp.zeros((SORT_ROWS, TOK_BLOCK), F32)
        for dk in dests:
            onehot = jnp.where(rio == dk - c * SORT_ROWS, 1.0, onehot)
        xs_ref[c * SORT_ROWS:(c + 1) * SORT_ROWS, :] = _dot(onehot.astype(BF16), hb).astype(BF16)


def _block_mod_index(j):
    return jnp.where(j < _N_CTX_BLOCKS, 0, 1 + (j - _N_CTX_BLOCKS) // (DEC_SEQ // TOK_BLOCK))


def _token_specs():
    return [pl.BlockSpec((TOK_BLOCK, D_MODEL), lambda j: (jnp.minimum(j, _N_CTX_BLOCKS - 1), 0)),
            pl.BlockSpec((TOK_BLOCK, D_MODEL), lambda j: (jnp.maximum(j - _N_CTX_BLOCKS, 0), 0)),
            pl.BlockSpec((1, N_MOD, D_MODEL), lambda j: (_block_mod_index(j), 0, 0))]


def _route_sort(x1c, x1l, modv, g_ffn, wr_t, br):
    tbl = jax.ShapeDtypeStruct((_N_BLOCKS, N_EXPERTS, 128), F32)
    tbl_spec = pl.BlockSpec((1, N_EXPERTS, 128), lambda j: (j, 0, 0))
    return pl.pallas_call(
        _sort_kernel,
        grid=(_N_BLOCKS,),
        in_specs=_token_specs() + [_const_spec((1, D_MODEL)),
                                   _const_spec((N_EXPERTS, D_MODEL)),
                                   _const_spec((N_EXPERTS, 1))],
        out_specs=[pl.BlockSpec((BLOCK_CAP, D_MODEL), lambda j: (j, 0)),
                   pl.BlockSpec((1, 2 * TOP_K, TOK_BLOCK), lambda j: (j, 0, 0)),
                   tbl_spec, tbl_spec],
        out_shape=[jax.ShapeDtypeStruct((_N_BLOCKS * BLOCK_CAP, D_MODEL), BF16),
                   jax.ShapeDtypeStruct((_N_BLOCKS, 2 * TOP_K, TOK_BLOCK), F32),
                   tbl, tbl],
        compiler_params=pltpu.CompilerParams(dimension_semantics=("arbitrary",),
                                             vmem_limit_bytes=V7X_VMEM_LIMIT),
        name="route_sort",
    )(x1c, x1l, modv, g_ffn, wr_t, br)


EXPERT_BUF_ROWS = 2048
PASS_CHUNKS = EXPERT_BUF_ROWS // CHUNK
N_ROW_BUFS = 3


def _expert_kernel(np_ref, off_ref, wu_ref, bu_ref, wd_ref, bd_ref, xs_hbm, ys_hbm,
                   buf, wu_s, wd_s, gsem, ssem, pend):
    del xs_hbm
    e = pl.program_id(0)
    last = pl.num_programs(0) - 1
    slot = e % N_ROW_BUFS
    nxt = (e + 1) % N_ROW_BUFS

    def move_chunks(ex, q_lo, slot_, gather):
        def block_body(j, q0):
            n = np_ref[j * N_EXPERTS + ex]
            off = off_ref[j * N_EXPERTS + ex]
            c_lo = jnp.clip(q_lo - q0, 0, n)
            m = jnp.clip(q_lo + PASS_CHUNKS - q0, 0, n) - c_lo

            @pl.when(m > 0)
            def _():
                rows = pl.multiple_of(m * CHUNK, CHUNK)
                row0 = pl.multiple_of(j * BLOCK_CAP + (off + c_lo) * CHUNK, CHUNK)
                brow0 = pl.multiple_of((q0 + c_lo - q_lo) * CHUNK, CHUNK)
                hbm = ys_hbm.at[pl.ds(row0, rows), :]
                vm = buf.at[slot_, pl.ds(brow0, rows), :]
                if gather:
                    pltpu.make_async_copy(hbm, vm, gsem.at[slot_]).start()
                else:
                    pltpu.make_async_copy(vm, hbm, ssem.at[slot_]).start()

            return q0 + n

        lax.fori_loop(0, _N_BLOCKS, block_body, 0)

    def wait_chunks(sem, n, slot_):
        @pl.when(n > 0)
        def _():
            rows = pl.multiple_of(n * CHUNK, CHUNK)
            pltpu.make_async_copy(ys_hbm.at[pl.ds(0, rows), :], buf.at[slot_, pl.ds(0, rows), :], sem).wait()

    def mlp_rows(r0, rows):
        x = buf[slot, pl.ds(r0, rows), :]
        gu = _dot(x, wu_s[...]) + bu_ref[0]
        gate = jnp.minimum(gu[:, :D_FF], SWIGLU_LIMIT)
        up = jnp.clip(gu[:, D_FF:], -SWIGLU_LIMIT, SWIGLU_LIMIT)
        glu = gate * _sigmoid(SWIGLU_ALPHA * gate)
        y = _dot(((up + 1.0) * glu).astype(BF16), wd_s[...]) + bd_ref[0]
        buf[slot, pl.ds(r0, rows), :] = y.astype(BF16)

    def compute(n):
        rows = (n * CHUNK + ROW_STEP - 1) // ROW_STEP * ROW_STEP
        n_main = (jnp.maximum(rows - LAST_ROWS_MAX, 0) + EXPERT_ROW_TILE - 1) // EXPERT_ROW_TILE

        def tile_body(t, carry):
            mlp_rows(pl.multiple_of(t * EXPERT_ROW_TILE, ROW_STEP), EXPERT_ROW_TILE)
            return carry

        lax.fori_loop(0, n_main, tile_body, 0)
        r0 = pl.multiple_of(n_main * EXPERT_ROW_TILE, ROW_STEP)
        for size in range(ROW_STEP, LAST_ROWS_MAX + ROW_STEP, ROW_STEP):
            pl.when(rows - r0 == size)(functools.partial(mlp_rows, r0, size))

    def chunks_of(ex):
        return lax.fori_loop(0, _N_BLOCKS, lambda j, acc: acc + np_ref[j * N_EXPERTS + ex], 0)

    @pl.when(e == 0)
    def _():
        buf[...] = jnp.zeros_like(buf)
        for b in range(N_ROW_BUFS):
            pend[b] = 0
        move_chunks(0, 0, 0, True)

    wait_chunks(ssem.at[nxt], pend[nxt], nxt)
    pend[nxt] = 0

    @pl.when(e < last)
    def _():
        move_chunks(e + 1, 0, nxt, True)

    wu_s[...] = wu_ref[0].astype(BF16)
    wd_s[...] = wd_ref[0].astype(BF16)

    total = chunks_of(e)
    n0 = jnp.minimum(total, PASS_CHUNKS)
    wait_chunks(gsem.at[slot], n0, slot)
    compute(n0)
    move_chunks(e, 0, slot, False)
    pend[slot] = n0

    def pass_body(p, carry):
        wait_chunks(ssem.at[slot], pend[slot], slot)
        lo = p * PASS_CHUNKS
        n = jnp.minimum(total - lo, PASS_CHUNKS)
        move_chunks(e, lo, slot, True)
        wait_chunks(gsem.at[slot], n, slot)
        compute(n)
        move_chunks(e, lo, slot, False)
        pend[slot] = n
        return carry

    lax.fori_loop(1, (total + PASS_CHUNKS - 1) // PASS_CHUNKS, pass_body, 0)

    @pl.when(e == last)
    def _():
        for b in range(N_ROW_BUFS):
            wait_chunks(ssem.at[b], pend[b], b)
            pend[b] = 0


def _experts(np16, off16, w_up, b_up, w_down, b_down, xs):
    grid_spec = pltpu.PrefetchScalarGridSpec(
        num_scalar_prefetch=2,
        grid=(N_EXPERTS,),
        in_specs=[pl.BlockSpec((1, D_MODEL, 2 * D_FF), lambda e, *_: (e, 0, 0)),
                  pl.BlockSpec((1, 1, 2 * D_FF), lambda e, *_: (e, 0, 0)),
                  pl.BlockSpec((1, D_FF, D_MODEL), lambda e, *_: (e, 0, 0)),
                  pl.BlockSpec((1, 1, D_MODEL), lambda e, *_: (e, 0, 0)),
                  pl.BlockSpec(memory_space=pl.ANY)],
        out_specs=pl.BlockSpec(memory_space=pl.ANY),
        scratch_shapes=[pltpu.VMEM((N_ROW_BUFS, EXPERT_BUF_ROWS, D_MODEL), BF16),
                        pltpu.VMEM((D_MODEL, 2 * D_FF), BF16),
                        pltpu.VMEM((D_FF, D_MODEL), BF16),
                        pltpu.SemaphoreType.DMA((N_ROW_BUFS,)),
                        pltpu.SemaphoreType.DMA((N_ROW_BUFS,)),
                        pltpu.SMEM((N_ROW_BUFS,), jnp.int32)],
    )
    return pl.pallas_call(
        _expert_kernel,
        grid_spec=grid_spec,
        out_shape=jax.ShapeDtypeStruct(xs.shape, xs.dtype),
        input_output_aliases={6: 0},
        compiler_params=pltpu.CompilerParams(dimension_semantics=("arbitrary",),
                                             vmem_limit_bytes=V7X_VMEM_LIMIT),
        name="experts",
    )(np16, off16, w_up, b_up.reshape(N_EXPERTS, 1, 2 * D_FF), w_down,
      b_down.reshape(N_EXPERTS, 1, D_MODEL), xs)


def _combine_kernel(ys_ref, dk_ref, xc_ref, xl_ref, mod_ref, gfin_ref, yc_ref, yl_ref, y_s):
    j = pl.program_id(0)
    half = TOK_BLOCK // 2
    cio = lax.broadcasted_iota(jnp.int32, (half, SORT_ROWS), 1).astype(F32)
    halves = [slice(h * half, (h + 1) * half) for h in range(2)]
    accs = [jnp.zeros((half, D_MODEL), F32) for _ in halves]
    for c in range(BLOCK_CAP // SORT_ROWS):
        ys = ys_ref[c * SORT_ROWS:(c + 1) * SORT_ROWS, :]
        for h, rows in enumerate(halves):
            d = dk_ref[0, rows, :]
            w = jnp.zeros((half, SORT_ROWS), F32)
            for k in range(TOP_K):
                w = jnp.where(cio == d[:, k:k + 1] - float(c * SORT_ROWS), d[:, TOP_K + k:TOP_K + k + 1], w)
            accs[h] = accs[h] + _dot(w.astype(BF16), ys)
    for h, rows in enumerate(halves):
        acc = accs[h]
        x1 = jnp.where(j < _N_CTX_BLOCKS, xc_ref[rows, :], xl_ref[rows, :])
        x2 = x1 + mod_ref[0, 5:6, :] * acc
        ms = jnp.mean(x2 * x2, axis=-1, keepdims=True)
        y_s[rows, :] = x2 * lax.rsqrt(ms + RMS_EPS) * gfin_ref[...]

    @pl.when(j < _N_CTX_BLOCKS)
    def _():
        yc_ref[...] = y_s[...]

    @pl.when(j >= _N_CTX_BLOCKS)
    def _():
        yl_ref[...] = y_s[...]


def _combine(ys, dk_t, x1c, x1l, modv, g_final):
    return pl.pallas_call(
        _combine_kernel,
        grid=(_N_BLOCKS,),
        in_specs=[pl.BlockSpec((BLOCK_CAP, D_MODEL), lambda j: (j, 0)),
                  pl.BlockSpec((1, TOK_BLOCK, 2 * TOP_K), lambda j: (j, 0, 0))] + _token_specs() + [
                  _const_spec((1, D_MODEL))],
        out_specs=[pl.BlockSpec((TOK_BLOCK, D_MODEL), lambda j: (jnp.minimum(j, _N_CTX_BLOCKS - 1), 0)),
                   pl.BlockSpec((TOK_BLOCK, D_MODEL), lambda j: (jnp.maximum(j - _N_CTX_BLOCKS, 0), 0))],
        out_shape=[jax.ShapeDtypeStruct(x1c.shape, F32), jax.ShapeDtypeStruct(x1l.shape, F32)],
        scratch_shapes=[pltpu.VMEM((TOK_BLOCK, D_MODEL), F32)],
        compiler_params=pltpu.CompilerParams(dimension_semantics=("arbitrary",),
                                             vmem_limit_bytes=V7X_VMEM_LIMIT),
        name="combine",
    )(ys, dk_t, x1c, x1l, modv, g_final)


def kernel(x_prompt, x_sample, cache_k, cache_v, c, c_ctx, w_ada, b_ada, g_mix, w_in, w_pool, pool_scale,
           w_pa, w_pb, rpb, w_out, g_ffn, w_router, b_router, w_up, b_up, w_down, b_down, g_final):
    assert w_ada.shape[0] == 1, "single trunk layer"
    batch, seq, d = x_prompt.shape
    dec_batch, dec_seq, _ = x_sample.shape
    assert (seq, dec_seq, d) == (SEQ, DEC_SEQ, D_MODEL)
    assert batch * seq == _N_CTX_BLOCKS * TOK_BLOCK and dec_batch * dec_seq == _N_LAT_BLOCKS * TOK_BLOCK

    cmat = jnp.concatenate([c_ctx[None, :], c, jnp.zeros((8 - 1 - dec_batch, d), F32)], axis=0)
    modv = _modulation(cmat, w_ada[0], b_ada[0]).reshape(8, N_MOD, d)

    weights = (g_mix[0][None, :], w_in[0].astype(BF16), w_pool[0].astype(BF16), pool_scale[0][None, :],
               w_pa[0].astype(BF16), w_pb[0].astype(BF16), w_out[0].astype(BF16))

    def by_pair(cache):
        z = cache[:, 0].reshape(dec_batch, N_PAIRS, 2, PAST_LEN, HEAD_DIM)
        return z.transpose(0, 1, 3, 2, 4).reshape(dec_batch, N_PAIRS, PAST_LEN, PAIR_W)

    x1c, new_k, new_v = _ctx_mixer(x_prompt.reshape(batch * seq, d), modv, weights)
    x1l = _lat_mixer(x_sample.reshape(dec_batch * dec_seq, d), modv, weights,
                     by_pair(cache_k), by_pair(cache_v), _bias_tables(rpb[0]))

    xs, dk, np16, off16 = _route_sort(x1c, x1l, modv, g_ffn[0][None, :],
                                      w_router[0].T.astype(BF16), b_router[0][:, None])
    np16_i = np16[:, :, 0].astype(jnp.int32).reshape(-1)
    off16_i = off16[:, :, 0].astype(jnp.int32).reshape(-1)
    ys = _experts(np16_i, off16_i, w_up[0], b_up[0], w_down[0], b_down[0], xs)
    yc, yl = _combine(ys, dk.transpose(0, 2, 1), x1c, x1l, modv, g_final[None, :])
    return (yc.reshape(batch, seq, d), yl.reshape(dec_batch, dec_seq, d), new_k, new_v)
```

```python
import functools

import jax
import jax.numpy as jnp
import numpy as np
from jax import lax
from jax.experimental import pallas as pl
from jax.experimental.pallas import tpu as pltpu

F32 = jnp.float32
BF16 = jnp.bfloat16

D_MODEL = 1024
SEQ = 256
DEC_SEQ = 1024
GRID_W = 64
ROWS = DEC_SEQ // GRID_W
N_HEADS = 8
HEAD_DIM = 64
N_PAIRS = N_HEADS // 2
PAIR_W = 2 * HEAD_DIM
PAST_LEN = 512
POOL_DIM = 512
POOL_WINDOWS = (2, 4, 8, 16)
POOL_GROUP_DIM = 128
ATTN_DIM = 512
WIN_R = 8
WIN_C = 16
N_EXPERTS = 32
TOP_K = 4
D_FF = 1024
SWIGLU_LIMIT = 7.0
SWIGLU_ALPHA = 1.702
N_MOD = 6
RMS_EPS = 1e-6
NEG_INF = -1e30
ATTN_SCALE = HEAD_DIM ** -0.5
IN_DIM = POOL_DIM + 3 * ATTN_DIM + 2 * D_MODEL

TOK_BLOCK = 512
CHUNK = 16
BLOCK_CAP = TOK_BLOCK * TOP_K + N_EXPERTS * CHUNK
ROW_STEP = 128
EXPERT_ROW_TILE = 384
LAST_ROWS_MAX = 512
V7X_VMEM_LIMIT = 60 * 1024 * 1024

_N_CTX_BLOCKS = 8
_N_LAT_BLOCKS = 4
_N_BLOCKS = _N_CTX_BLOCKS + _N_LAT_BLOCKS

_NT = (((1,), (1,)), ((), ()))


def _dot(a, b):
    return jnp.dot(a, b, preferred_element_type=F32)


def _dot_nt(a, b):
    return lax.dot_general(a, b, _NT, preferred_element_type=F32)


def _sigmoid(x):
    return 1.0 / (1.0 + jnp.exp(-x))


def _norm_mod(x, gain, scale, shift):
    ms = jnp.mean(x * x, axis=-1, keepdims=True)
    return (x * lax.rsqrt(ms + RMS_EPS) * gain) * (1.0 + scale) + shift


def _const_spec(shape):
    zeros = (0,) * len(shape)
    return pl.BlockSpec(shape, lambda *_: zeros, pipeline_mode=pl.Buffered(1))


MOD_COLS = 1536


def _mod_kernel(c_ref, w_ref, b_ref, o_ref):
    c = c_ref[...]
    s = c * _sigmoid(c)
    s_hi = s.astype(BF16)
    s_lo = (s - s_hi.astype(F32)).astype(BF16)
    r = _dot(jnp.concatenate([s_hi, s_lo], axis=0), w_ref[...].astype(BF16))
    o_ref[...] = r[:8] + r[8:] + b_ref[...]


def _modulation(cmat, w_ada, b_ada):
    n = w_ada.shape[1]
    return pl.pallas_call(
        _mod_kernel,
        grid=(n // MOD_COLS,),
        in_specs=[pl.BlockSpec((8, D_MODEL), lambda i: (0, 0)),
                  pl.BlockSpec((D_MODEL, MOD_COLS), lambda i: (0, i)),
                  pl.BlockSpec((1, MOD_COLS), lambda i: (0, i))],
        out_specs=pl.BlockSpec((8, MOD_COLS), lambda i: (0, i)),
        out_shape=jax.ShapeDtypeStruct((8, n), F32),
        name="modulation",
    )(cmat, w_ada, b_ada.reshape(1, n))


def _pool_mix(u, pos, seq):
    n = u.shape[0]

    def down(x, d):
        return jnp.where(pos >= d, pltpu.roll(x, d, 0), 0.0)

    def up(x, d):
        return jnp.where(pos < seq - d, pltpu.roll(x, n - d, 0), 0.0)

    return down, up


def _pool_group(u, pos, seq, w):
    down, up = _pool_mix(u, pos, seq)
    hw = w // 2
    back = u
    fwd = u
    d = 1
    while d < hw:
        back = back + down(back, d)
        fwd = fwd + up(fwd, d)
        d *= 2
    s = down(back, 1) + fwd
    posf = pos.astype(F32)
    cnt = jnp.minimum(posf + hw, float(seq)) - jnp.maximum(posf - hw, 0.0)
    return s / cnt - u


def _pass_rows(tile):
    return tile if tile <= TOK_BLOCK else tile // 4


def _mixer_front(x_ref, mod_ref, gmix_ref, win_ref, hb_s, u_s, q_s, k_s, v_s):
    shift, scale = mod_ref[0, 0:1, :], mod_ref[0, 1:2, :]
    step = _pass_rows(x_ref.shape[0])
    for c in range(x_ref.shape[0] // step):
        rows = slice(c * step, (c + 1) * step)
        hb = _norm_mod(x_ref[rows, :], gmix_ref[...], scale, shift).astype(BF16)
        hb_s[rows, :] = hb
        u_s[rows, :] = _dot(hb, win_ref[:, 0:POOL_DIM])
        for dst, base in ((q_s, POOL_DIM), (k_s, POOL_DIM + ATTN_DIM), (v_s, POOL_DIM + 2 * ATTN_DIM)):
            z = _dot(hb, win_ref[:, base:base + ATTN_DIM])
            if dst is q_s:
                z = z * ATTN_SCALE
            for g in range(N_PAIRS):
                dst[g, rows, :] = z[:, g * PAIR_W:(g + 1) * PAIR_W].astype(dst.dtype)


def _mixer_back(x_ref, mod_ref, win_ref, wpool_ref, ps_ref, wpa_ref, wpb_ref, wout_ref, x1_ref,
                hb_s, u_s, o_s, pg_s, seq):
    tile = x_ref.shape[0]
    gate = mod_ref[0, 2:3, :]
    pos = lax.broadcasted_iota(jnp.int32, (tile, 1), 0) % seq
    for g, w in enumerate(POOL_WINDOWS):
        cols = slice(g * POOL_GROUP_DIM, (g + 1) * POOL_GROUP_DIM)
        pg_s[:, cols] = _pool_group(u_s[:, cols], pos, seq, w).astype(BF16)
    step = _pass_rows(tile)
    for c in range(tile // step):
        rows = slice(c * step, (c + 1) * step)
        ys = []
        for g in range(len(POOL_WINDOWS)):
            cols = slice(g * POOL_GROUP_DIM, (g + 1) * POOL_GROUP_DIM)
            ys.append((_dot(pg_s[rows, cols], wpool_ref[g]) * ps_ref[:, cols]).astype(BF16))
        a = _dot(jnp.concatenate(ys, axis=1), wpa_ref[...])
        ob = _dot(jnp.concatenate([o_s[g, rows, :] for g in range(N_PAIRS)], axis=1), wpb_ref[...])
        gab = _dot(hb_s[rows, :], win_ref[:, POOL_DIM + 3 * ATTN_DIM:IN_DIM])
        merged = _sigmoid(gab[:, :D_MODEL]) * a + _sigmoid(gab[:, D_MODEL:]) * ob
        mix = _dot(merged.astype(BF16), wout_ref[...])
        x1_ref[rows, :] = x_ref[rows, :] + gate * mix


def _ctx_mixer_kernel(x_ref, mod_ref, gmix_ref, win_ref, wpool_ref, ps_ref, wpa_ref, wpb_ref, wout_ref,
                      x1_ref, ko_ref, vo_ref, hb_s, u_s, q_s, k_s, v_s, o_s, pg_s):
    _mixer_front(x_ref, mod_ref, gmix_ref, win_ref, hb_s, u_s, q_s, k_s, v_s)
    even = lax.broadcasted_iota(jnp.int32, (1, PAIR_W), 1) < HEAD_DIM
    tile = x_ref.shape[0]
    for s in range(tile // SEQ):
        rows = slice(s * SEQ, (s + 1) * SEQ)
        for g in range(N_PAIRS):
            q2, k2, v2 = q_s[g, rows, :], k_s[g, rows, :], v_s[g, rows, :]
            ko_ref[s, 0, 2 * g] = k2[:, :HEAD_DIM]
            ko_ref[s, 0, 2 * g + 1] = k2[:, HEAD_DIM:]
            vo_ref[s, 0, 2 * g] = v2[:, :HEAD_DIM]
            vo_ref[s, 0, 2 * g + 1] = v2[:, HEAD_DIM:]
            kb, vb = k2.astype(BF16), v2.astype(BF16)
            outs = []
            for par in range(2):
                qm = jnp.where(even if par == 0 else jnp.logical_not(even), q2, 0.0).astype(BF16)
                sc = _dot_nt(qm, kb)
                m = jnp.max(sc, axis=-1, keepdims=True)
                p = jnp.exp(sc - m)
                l = jnp.sum(p, axis=-1, keepdims=True)
                outs.append(_dot(p.astype(BF16), vb) / l)
            o_s[g, rows, :] = jnp.where(even, outs[0], outs[1]).astype(BF16)
    _mixer_back(x_ref, mod_ref, win_ref, wpool_ref, ps_ref, wpa_ref, wpb_ref, wout_ref, x1_ref,
                hb_s, u_s, o_s, pg_s, SEQ)


def _row_window(r):
    rs = min(max(r - WIN_R // 2, 0), ROWS - WIN_R)
    return rs, rs - r + WIN_R - 1


def _lat_mixer_kernel(x_ref, mod_ref, gmix_ref, win_ref, wpool_ref, ps_ref, wpa_ref, wpb_ref, wout_ref,
                      kc_ref, vc_ref, tb_ref, x1_ref, hb_s, u_s, q_s, k_s, v_s, o_s, pg_s):
    _mixer_front(x_ref, mod_ref, gmix_ref, win_ref, hb_s, u_s, q_s, k_s, v_s)
    even = lax.broadcasted_iota(jnp.int32, (1, PAIR_W), 1) < HEAD_DIM
    nk = WIN_R * GRID_W

    def pair_body(g, carry):
        q2 = q_s[g]
        kb, vb = k_s[g].astype(BF16), v_s[g].astype(BF16)
        kcb, vcb = kc_ref[0, g].astype(BF16), vc_ref[0, g].astype(BF16)
        outs = []
        for par in range(2):
            qm = jnp.where(even if par == 0 else jnp.logical_not(even), q2, 0.0).astype(BF16)
            s_ctx = _dot_nt(qm, kcb)
            slabs = []
            for r in range(ROWS):
                rs, rho = _row_window(r)
                bias = tb_ref[2 * g + par, rho % 2, :, (rho - rho % 2) * GRID_W:(rho - rho % 2) * GRID_W + nk]
                sl = _dot_nt(qm[r * GRID_W:(r + 1) * GRID_W, :], kb[rs * GRID_W:rs * GRID_W + nk, :])
                slabs.append(sl + bias)
            s_loc = jnp.concatenate(slabs, axis=0)
            m = jnp.maximum(jnp.max(s_loc, axis=-1, keepdims=True), jnp.max(s_ctx, axis=-1, keepdims=True))
            p_loc = jnp.exp(s_loc - m)
            p_ctx = jnp.exp(s_ctx - m)
            l = jnp.sum(p_loc, axis=-1, keepdims=True) + jnp.sum(p_ctx, axis=-1, keepdims=True)
            p_locb = p_loc.astype(BF16)
            o_rows = []
            for r in range(ROWS):
                rs, _ = _row_window(r)
                o_rows.append(_dot(p_locb[r * GRID_W:(r + 1) * GRID_W, :], vb[rs * GRID_W:rs * GRID_W + nk, :]))
            o = jnp.concatenate(o_rows, axis=0) + _dot(p_ctx.astype(BF16), vcb)
            outs.append(o / l)
        o_s[g] = jnp.where(even, outs[0], outs[1]).astype(BF16)
        return carry

    lax.fori_loop(0, N_PAIRS, pair_body, 0)
    _mixer_back(x_ref, mod_ref, win_ref, wpool_ref, ps_ref, wpa_ref, wpb_ref, wout_ref, x1_ref,
                hb_s, u_s, o_s, pg_s, DEC_SEQ)


def _mixer_scratch(tile, kv_dtype):
    return [pltpu.VMEM((tile, D_MODEL), BF16),
            pltpu.VMEM((tile, POOL_DIM), F32),
            pltpu.VMEM((N_PAIRS, tile, PAIR_W), BF16),
            pltpu.VMEM((N_PAIRS, tile, PAIR_W), kv_dtype),
            pltpu.VMEM((N_PAIRS, tile, PAIR_W), kv_dtype),
            pltpu.VMEM((N_PAIRS, tile, PAIR_W), BF16),
            pltpu.VMEM((tile, POOL_DIM), BF16)]


def _weight_specs():
    return [_const_spec((1, D_MODEL)),
            _const_spec((D_MODEL, IN_DIM)),
            _const_spec((len(POOL_WINDOWS), POOL_GROUP_DIM, POOL_GROUP_DIM)),
            _const_spec((1, POOL_DIM)),
            _const_spec((POOL_DIM, D_MODEL)),
            _const_spec((ATTN_DIM, D_MODEL)),
            _const_spec((D_MODEL, D_MODEL))]


def _ctx_mixer(x, modv, weights):
    n = x.shape[0]
    nseq = TOK_BLOCK // SEQ
    cache = jax.ShapeDtypeStruct((n // SEQ, 1, N_HEADS, SEQ, HEAD_DIM), F32)
    cache_spec = pl.BlockSpec((nseq, 1, N_HEADS, SEQ, HEAD_DIM), lambda i: (i, 0, 0, 0, 0))
    return pl.pallas_call(
        _ctx_mixer_kernel,
        grid=(n // TOK_BLOCK,),
        in_specs=[pl.BlockSpec((TOK_BLOCK, D_MODEL), lambda i: (i, 0)),
                  pl.BlockSpec((1, N_MOD, D_MODEL), lambda i: (0, 0, 0))] + _weight_specs(),
        out_specs=[pl.BlockSpec((TOK_BLOCK, D_MODEL), lambda i: (i, 0)), cache_spec, cache_spec],
        out_shape=[jax.ShapeDtypeStruct((n, D_MODEL), F32), cache, cache],
        scratch_shapes=_mixer_scratch(TOK_BLOCK, F32),
        compiler_params=pltpu.CompilerParams(dimension_semantics=("arbitrary",),
                                             vmem_limit_bytes=V7X_VMEM_LIMIT),
        name="ctx_mixer",
    )(x, modv, *weights)


def _lat_mixer(x, modv, weights, kc, vc, tb):
    n = x.shape[0]
    return pl.pallas_call(
        _lat_mixer_kernel,
        grid=(n // DEC_SEQ,),
        in_specs=[pl.BlockSpec((DEC_SEQ, D_MODEL), lambda i: (i, 0)),
                  pl.BlockSpec((1, N_MOD, D_MODEL), lambda i: (i + 1, 0, 0))] + _weight_specs() + [
                  pl.BlockSpec((1, N_PAIRS, PAST_LEN, PAIR_W), lambda i: (i, 0, 0, 0)),
                  pl.BlockSpec((1, N_PAIRS, PAST_LEN, PAIR_W), lambda i: (i, 0, 0, 0)),
                  _const_spec((N_HEADS, 2, GRID_W, ROWS * GRID_W))],
        out_specs=pl.BlockSpec((DEC_SEQ, D_MODEL), lambda i: (i, 0)),
        out_shape=jax.ShapeDtypeStruct((n, D_MODEL), F32),
        scratch_shapes=_mixer_scratch(DEC_SEQ, BF16),
        compiler_params=pltpu.CompilerParams(dimension_semantics=("arbitrary",),
                                             vmem_limit_bytes=V7X_VMEM_LIMIT),
        name="lat_mixer",
    )(x, modv, *weights, kc, vc, tb)


N_RPB_ROWS = 2 * WIN_R - 1
N_RPB_COLS = 2 * WIN_C - 1
TABLE_W = ROWS * GRID_W


def _bias_kernel(v_ref, keep_ref, o_ref):
    for h in range(N_HEADS):
        for par in range(2):
            x = jnp.broadcast_to(v_ref[h, par:par + 1, :], (GRID_W, TABLE_W))
            shifted = pltpu.roll(x, TABLE_W - (WIN_C - 1), 1, stride=1, stride_axis=0)
            o_ref[h, par] = jnp.where(keep_ref[par] > 0.0, shifted, NEG_INF)


def _bias_tables(rpb):
    col = np.arange(GRID_W)
    cs = np.clip(col - WIN_C // 2, 0, GRID_W - WIN_C)
    in_win = (col[None, :] >= cs[:, None]) & (col[None, :] < cs[:, None] + WIN_C)
    keep = np.tile(in_win.astype(np.float32), (2, 1, ROWS))
    keep[0, :, N_RPB_ROWS * GRID_W:] = 0.0
    keep[1, :, (N_RPB_ROWS - 1) * GRID_W:] = 0.0
    rp = jnp.pad(rpb.astype(F32), ((0, 0), (0, ROWS + 1 - N_RPB_ROWS), (0, GRID_W - N_RPB_COLS)))
    v = jnp.stack([rp[:, :ROWS].reshape(N_HEADS, TABLE_W), rp[:, 1:].reshape(N_HEADS, TABLE_W)], axis=1)
    return pl.pallas_call(
        _bias_kernel,
        out_shape=jax.ShapeDtypeStruct((N_HEADS, 2, GRID_W, TABLE_W), F32),
        name="bias_tables",
    )(v, jnp.asarray(keep))


SORT_ROWS = 512


def _sort_kernel(xc_ref, xl_ref, mod_ref, gffn_ref, wr_ref, br_ref,
                 xs_ref, dk_ref, np_ref, off_ref):
    j = pl.program_id(0)
    x = jnp.where(j < _N_CTX_BLOCKS, xc_ref[...], xl_ref[...])
    shift, scale = mod_ref[0, 3:4, :], mod_ref[0, 4:5, :]
    hb = _norm_mod(x, gffn_ref[...], scale, shift).astype(BF16)
    logits = _dot_nt(wr_ref[...], hb) + br_ref[...]
    eio = lax.broadcasted_iota(jnp.int32, logits.shape, 0)
    work = logits
    sels, vals = [], []
    for _ in range(TOP_K):
        m = jnp.max(work, axis=0, keepdims=True)
        idx = jnp.min(jnp.where(work == m, eio, N_EXPERTS), axis=0, keepdims=True)
        sel = eio == idx
        sels.append(sel)
        vals.append(m)
        work = jnp.where(sel, -jnp.inf, work)
    exps = [jnp.exp(v - vals[0]) for v in vals]
    den = exps[0] + exps[1] + exps[2] + exps[3]
    mask = jnp.zeros(logits.shape, F32)
    for sel in sels:
        mask = mask + jnp.where(sel, 1.0, 0.0)
    t_row = lax.broadcasted_iota(jnp.int32, (TOK_BLOCK, TOK_BLOCK), 0)
    t_col = lax.broadcasted_iota(jnp.int32, (TOK_BLOCK, TOK_BLOCK), 1)
    before = jnp.where(t_row < t_col, 1.0, 0.0).astype(BF16)
    rank = _dot(mask.astype(BF16), before)
    cnt = jnp.sum(mask, axis=1, keepdims=True)
    np16 = jnp.floor((cnt + (CHUNK - 1.0)) * (1.0 / CHUNK))
    e_row = lax.broadcasted_iota(jnp.int32, (N_EXPERTS, N_EXPERTS), 0)
    e_col = lax.broadcasted_iota(jnp.int32, (N_EXPERTS, N_EXPERTS), 1)
    lower = jnp.where(e_col < e_row, 1.0, 0.0).astype(BF16)
    np16_b = jnp.broadcast_to(np16, (N_EXPERTS, 128))
    off16 = _dot(lower, np16_b.astype(BF16))
    np_ref[0] = np16_b
    off_ref[0] = off16
    dest = off16[:, 0:1] * float(CHUNK) + rank
    dests = []
    for k in range(TOP_K):
        dk = jnp.sum(jnp.where(sels[k], dest, 0.0), axis=0, keepdims=True)
        dk_ref[0, k:k + 1, :] = dk
        dk_ref[0, TOP_K + k:TOP_K + k + 1, :] = exps[k] / den
        dests.append(dk.astype(jnp.int32))
    rio = lax.broadcasted_iota(jnp.int32, (SORT_ROWS, TOK_BLOCK), 0)
    for c in range(BLOCK_CAP // SORT_ROWS):
        onehot = jnp.zeros((SORT_ROWS, TOK_BLOCK), F32)
        for dk in dests:
            onehot = jnp.where(rio == dk - c * SORT_ROWS, 1.0, onehot)
        xs_ref[c * SORT_ROWS:(c + 1) * SORT_ROWS, :] = _dot(onehot.astype(BF16), hb).astype(BF16)


def _block_mod_index(j):
    return jnp.where(j < _N_CTX_BLOCKS, 0, 1 + (j - _N_CTX_BLOCKS) // (DEC_SEQ // TOK_BLOCK))


def _token_specs():
    return [pl.BlockSpec((TOK_BLOCK, D_MODEL), lambda j: (jnp.minimum(j, _N_CTX_BLOCKS - 1), 0)),
            pl.BlockSpec((TOK_BLOCK, D_MODEL), lambda j: (jnp.maximum(j - _N_CTX_BLOCKS, 0), 0)),
            pl.BlockSpec((1, N_MOD, D_MODEL), lambda j: (_block_mod_index(j), 0, 0))]


def _route_sort(x1c, x1l, modv, g_ffn, wr_t, br):
    tbl = jax.ShapeDtypeStruct((_N_BLOCKS, N_EXPERTS, 128), F32)
    tbl_spec = pl.BlockSpec((1, N_EXPERTS, 128), lambda j: (j, 0, 0))
    return pl.pallas_call(
        _sort_kernel,
        grid=(_N_BLOCKS,),
        in_specs=_token_specs() + [_const_spec((1, D_MODEL)),
                                   _const_spec((N_EXPERTS, D_MODEL)),
                                   _const_spec((N_EXPERTS, 1))],
        out_specs=[pl.BlockSpec((BLOCK_CAP, D_MODEL), lambda j: (j, 0)),
                   pl.BlockSpec((1, 2 * TOP_K, TOK_BLOCK), lambda j: (j, 0, 0)),
                   tbl_spec, tbl_spec],
        out_shape=[jax.ShapeDtypeStruct((_N_BLOCKS * BLOCK_CAP, D_MODEL), BF16),
                   jax.ShapeDtypeStruct((_N_BLOCKS, 2 * TOP_K, TOK_BLOCK), F32),
                   tbl, tbl],
        compiler_params=pltpu.CompilerParams(dimension_semantics=("arbitrary",),
                                             vmem_limit_bytes=V7X_VMEM_LIMIT),
        name="route_sort",
    )(x1c, x1l, modv, g_ffn, wr_t, br)


EXPERT_BUF_ROWS = 2048
PASS_CHUNKS = EXPERT_BUF_ROWS // CHUNK
N_ROW_BUFS = 3


def _expert_kernel(np_ref, off_ref, wu_ref, bu_ref, wd_ref, bd_ref, xs_hbm, ys_hbm,
                   buf, wu_s, wd_s, gsem, ssem, pend):
    del xs_hbm
    e = pl.program_id(0)
    last = pl.num_programs(0) - 1
    slot = e % N_ROW_BUFS
    nxt = (e + 1) % N_ROW_BUFS

    def move_chunks(ex, q_lo, slot_, gather):
        def block_body(j, q0):
            n = np_ref[j * N_EXPERTS + ex]
            off = off_ref[j * N_EXPERTS + ex]
            c_lo = jnp.clip(q_lo - q0, 0, n)
            m = jnp.clip(q_lo + PASS_CHUNKS - q0, 0, n) - c_lo

            @pl.when(m > 0)
            def _():
                rows = pl.multiple_of(m * CHUNK, CHUNK)
                row0 = pl.multiple_of(j * BLOCK_CAP + (off + c_lo) * CHUNK, CHUNK)
                brow0 = pl.multiple_of((q0 + c_lo - q_lo) * CHUNK, CHUNK)
                hbm = ys_hbm.at[pl.ds(row0, rows), :]
                vm = buf.at[slot_, pl.ds(brow0, rows), :]
                if gather:
                    pltpu.make_async_copy(hbm, vm, gsem.at[slot_]).start()
                else:
                    pltpu.make_async_copy(vm, hbm, ssem.at[slot_]).start()

            return q0 + n

        lax.fori_loop(0, _N_BLOCKS, block_body, 0)

    def wait_chunks(sem, n, slot_):
        @pl.when(n > 0)
        def _():
            rows = pl.multiple_of(n * CHUNK, CHUNK)
            pltpu.make_async_copy(ys_hbm.at[pl.ds(0, rows), :], buf.at[slot_, pl.ds(0, rows), :], sem).wait()

    def mlp_rows(r0, rows):
        x = buf[slot, pl.ds(r0, rows), :]
        gu = _dot(x, wu_s[...]) + bu_ref[0]
        gate = jnp.minimum(gu[:, :D_FF], SWIGLU_LIMIT)
        up = jnp.clip(gu[:, D_FF:], -SWIGLU_LIMIT, SWIGLU_LIMIT)
        glu = gate * _sigmoid(SWIGLU_ALPHA * gate)
        y = _dot(((up + 1.0) * glu).astype(BF16), wd_s[...]) + bd_ref[0]
        buf[slot, pl.ds(r0, rows), :] = y.astype(BF16)

    def compute(n):
        rows = (n * CHUNK + ROW_STEP - 1) // ROW_STEP * ROW_STEP
        n_main = (jnp.maximum(rows - LAST_ROWS_MAX, 0) + EXPERT_ROW_TILE - 1) // EXPERT_ROW_TILE

        def tile_body(t, carry):
            mlp_rows(pl.multiple_of(t * EXPERT_ROW_TILE, ROW_STEP), EXPERT_ROW_TILE)
            return carry

        lax.fori_loop(0, n_main, tile_body, 0)
        r0 = pl.multiple_of(n_main * EXPERT_ROW_TILE, ROW_STEP)
        for size in range(ROW_STEP, LAST_ROWS_MAX + ROW_STEP, ROW_STEP):
            pl.when(rows - r0 == size)(functools.partial(mlp_rows, r0, size))

    def chunks_of(ex):
        return lax.fori_loop(0, _N_BLOCKS, lambda j, acc: acc + np_ref[j * N_EXPERTS + ex], 0)

    @pl.when(e == 0)
    def _():
        buf[...] = jnp.zeros_like(buf)
        for b in range(N_ROW_BUFS):
            pend[b] = 0
        move_chunks(0, 0, 0, True)

    wait_chunks(ssem.at[nxt], pend[nxt], nxt)
    pend[nxt] = 0

    @pl.when(e < last)
    def _():
        move_chunks(e + 1, 0, nxt, True)

    wu_s[...] = wu_ref[0].astype(BF16)
    wd_s[...] = wd_ref[0].astype(BF16)

    total = chunks_of(e)
    n0 = jnp.minimum(total, PASS_CHUNKS)
    wait_chunks(gsem.at[slot], n0, slot)
    compute(n0)
    move_chunks(e, 0, slot, False)
    pend[slot] = n0

    def pass_body(p, carry):
        wait_chunks(ssem.at[slot], pend[slot], slot)
        lo = p * PASS_CHUNKS
        n = jnp.minimum(total - lo, PASS_CHUNKS)
        move_chunks(e, lo, slot, True)
        wait_chunks(gsem.at[slot], n, slot)
        compute(n)
        move_chunks(e, lo, slot, False)
        pend[slot] = n
        return carry

    lax.fori_loop(1, (total + PASS_CHUNKS - 1) // PASS_CHUNKS, pass_body, 0)

    @pl.when(e == last)
    def _():
        for b in range(N_ROW_BUFS):
            wait_chunks(ssem.at[b], pend[b], b)
            pend[b] = 0


def _experts(np16, off16, w_up, b_up, w_down, b_down, xs):
    grid_spec = pltpu.PrefetchScalarGridSpec(
        num_scalar_prefetch=2,
        grid=(N_EXPERTS,),
        in_specs=[pl.BlockSpec((1, D_MODEL, 2 * D_FF), lambda e, *_: (e, 0, 0)),
                  pl.BlockSpec((1, 1, 2 * D_FF), lambda e, *_: (e, 0, 0)),
                  pl.BlockSpec((1, D_FF, D_MODEL), lambda e, *_: (e, 0, 0)),
                  pl.BlockSpec((1, 1, D_MODEL), lambda e, *_: (e, 0, 0)),
                  pl.BlockSpec(memory_space=pl.ANY)],
        out_specs=pl.BlockSpec(memory_space=pl.ANY),
        scratch_shapes=[pltpu.VMEM((N_ROW_BUFS, EXPERT_BUF_ROWS, D_MODEL), BF16),
                        pltpu.VMEM((D_MODEL, 2 * D_FF), BF16),
                        pltpu.VMEM((D_FF, D_MODEL), BF16),
                        pltpu.SemaphoreType.DMA((N_ROW_BUFS,)),
                        pltpu.SemaphoreType.DMA((N_ROW_BUFS,)),
                        pltpu.SMEM((N_ROW_BUFS,), jnp.int32)],
    )
    return pl.pallas_call(
        _expert_kernel,
        grid_spec=grid_spec,
        out_shape=jax.ShapeDtypeStruct(xs.shape, xs.dtype),
        input_output_aliases={6: 0},
        compiler_params=pltpu.CompilerParams(dimension_semantics=("arbitrary",),
                                             vmem_limit_bytes=V7X_VMEM_LIMIT),
        name="experts",
    )(np16, off16, w_up, b_up.reshape(N_EXPERTS, 1, 2 * D_FF), w_down,
      b_down.reshape(N_EXPERTS, 1, D_MODEL), xs)


def _combine_kernel(ys_ref, dk_ref, xc_ref, xl_ref, mod_ref, gfin_ref, yc_ref, yl_ref, y_s):
    j = pl.program_id(0)
    half = TOK_BLOCK // 2
    cio = lax.broadcasted_iota(jnp.int32, (half, SORT_ROWS), 1).astype(F32)
    halves = [slice(h * half, (h + 1) * half) for h in range(2)]
    accs = [jnp.zeros((half, D_MODEL), F32) for _ in halves]
    for c in range(BLOCK_CAP // SORT_ROWS):
        ys = ys_ref[c * SORT_ROWS:(c + 1) * SORT_ROWS, :]
        for h, rows in enumerate(halves):
            d = dk_ref[0, rows, :]
            w = jnp.zeros((half, SORT_ROWS), F32)
            for k in range(TOP_K):
                w = jnp.where(cio == d[:, k:k + 1] - float(c * SORT_ROWS), d[:, TOP_K + k:TOP_K + k + 1], w)
            accs[h] = accs[h] + _dot(w.astype(BF16), ys)
    for h, rows in enumerate(halves):
        acc = accs[h]
        x1 = jnp.where(j < _N_CTX_BLOCKS, xc_ref[rows, :], xl_ref[rows, :])
        x2 = x1 + mod_ref[0, 5:6, :] * acc
        ms = jnp.mean(x2 * x2, axis=-1, keepdims=True)
        y_s[rows, :] = x2 * lax.rsqrt(ms + RMS_EPS) * gfin_ref[...]

    @pl.when(j < _N_CTX_BLOCKS)
    def _():
        yc_ref[...] = y_s[...]

    @pl.when(j >= _N_CTX_BLOCKS)
    def _():
        yl_ref[...] = y_s[...]


def _combine(ys, dk_t, x1c, x1l, modv, g_final):
    return pl.pallas_call(
        _combine_kernel,
        grid=(_N_BLOCKS,),
        in_specs=[pl.BlockSpec((BLOCK_CAP, D_MODEL), lambda j: (j, 0)),
                  pl.BlockSpec((1, TOK_BLOCK, 2 * TOP_K), lambda j: (j, 0, 0))] + _token_specs() + [
                  _const_spec((1, D_MODEL))],
        out_specs=[pl.BlockSpec((TOK_BLOCK, D_MODEL), lambda j: (jnp.minimum(j, _N_CTX_BLOCKS - 1), 0)),
                   pl.BlockSpec((TOK_BLOCK, D_MODEL), lambda j: (jnp.maximum(j - _N_CTX_BLOCKS, 0), 0))],
        out_shape=[jax.ShapeDtypeStruct(x1c.shape, F32), jax.ShapeDtypeStruct(x1l.shape, F32)],
        scratch_shapes=[pltpu.VMEM((TOK_BLOCK, D_MODEL), F32)],
        compiler_params=pltpu.CompilerParams(dimension_semantics=("arbitrary",),
                                             vmem_limit_bytes=V7X_VMEM_LIMIT),
        name="combine",
    )(ys, dk_t, x1c, x1l, modv, g_final)


def kernel(x_prompt, x_sample, cache_k, cache_v, c, c_ctx, w_ada, b_ada, g_mix, w_in, w_pool, pool_scale,
           w_pa, w_pb, rpb, w_out, g_ffn, w_router, b_router, w_up, b_up, w_down, b_down, g_final):
    assert w_ada.shape[0] == 1, "single trunk layer"
    batch, seq, d = x_prompt.shape
    dec_batch, dec_seq, _ = x_sample.shape
    assert (seq, dec_seq, d) == (SEQ, DEC_SEQ, D_MODEL)
    assert batch * seq == _N_CTX_BLOCKS * TOK_BLOCK and dec_batch * dec_seq == _N_LAT_BLOCKS * TOK_BLOCK

    cmat = jnp.concatenate([c_ctx[None, :], c, jnp.zeros((8 - 1 - dec_batch, d), F32)], axis=0)
    modv = _modulation(cmat, w_ada[0], b_ada[0]).reshape(8, N_MOD, d)

    weights = (g_mix[0][None, :], w_in[0].astype(BF16), w_pool[0].astype(BF16), pool_scale[0][None, :],
               w_pa[0].astype(BF16), w_pb[0].astype(BF16), w_out[0].astype(BF16))

    def by_pair(cache):
        z = cache[:, 0].reshape(dec_batch, N_PAIRS, 2, PAST_LEN, HEAD_DIM)
        return z.transpose(0, 1, 3, 2, 4).reshape(dec_batch, N_PAIRS, PAST_LEN, PAIR_W)

    x1c, new_k, new_v = _ctx_mixer(x_prompt.reshape(batch * seq, d), modv, weights)
    x1l = _lat_mixer(x_sample.reshape(dec_batch * dec_seq, d), modv, weights,
                     by_pair(cache_k), by_pair(cache_v), _bias_tables(rpb[0]))

    xs, dk, np16, off16 = _route_sort(x1c, x1l, modv, g_ffn[0][None, :],
                                      w_router[0].T.astype(BF16), b_router[0][:, None])
    np16_i = np16[:, :, 0].astype(jnp.int32).reshape(-1)
    off16_i = off16[:, :, 0].astype(jnp.int32).reshape(-1)
    ys = _experts(np16_i, off16_i, w_up[0], b_up[0], w_down[0], b_down[0], xs)
    yc, yl = _combine(ys, dk.transpose(0, 2, 1), x1c, x1l, modv, g_final[None, :])
    return (yc.reshape(batch, seq, d), yl.reshape(dec_batch, dec_seq, d), new_k, new_v)
```

```python
import functools

import jax
import jax.numpy as jnp
import numpy as np
from jax import lax
from jax.experimental import pallas as pl
from jax.experimental.pallas import tpu as pltpu

F32 = jnp.float32
BF16 = jnp.bfloat16

D_MODEL = 1024
SEQ = 256
DEC_SEQ = 1024
GRID_W = 64
ROWS = DEC_SEQ // GRID_W
N_HEADS = 8
HEAD_DIM = 64
N_PAIRS = N_HEADS // 2
PAIR_W = 2 * HEAD_DIM
PAST_LEN = 512
POOL_DIM = 512
POOL_WINDOWS = (2, 4, 8, 16)
POOL_GROUP_DIM = 128
ATTN_DIM = 512
WIN_R = 8
WIN_C = 16
N_EXPERTS = 32
TOP_K = 4
D_FF = 1024
SWIGLU_LIMIT = 7.0
SWIGLU_ALPHA = 1.702
N_MOD = 6
RMS_EPS = 1e-6
NEG_INF = -1e30
ATTN_SCALE = HEAD_DIM ** -0.5
IN_DIM = POOL_DIM + 3 * ATTN_DIM + 2 * D_MODEL

TOK_BLOCK = 512
CHUNK = 16
BLOCK_CAP = TOK_BLOCK * TOP_K + N_EXPERTS * CHUNK
ROW_STEP = 128
EXPERT_ROW_TILE = 384
LAST_ROWS_MAX = 512
V7X_VMEM_LIMIT = 60 * 1024 * 1024

_N_CTX_BLOCKS = 8
_N_LAT_BLOCKS = 4
_N_BLOCKS = _N_CTX_BLOCKS + _N_LAT_BLOCKS

_NT = (((1,), (1,)), ((), ()))


def _dot(a, b):
    return jnp.dot(a, b, preferred_element_type=F32)


def _dot_nt(a, b):
    return lax.dot_general(a, b, _NT, preferred_element_type=F32)


def _sigmoid(x):
    return 1.0 / (1.0 + jnp.exp(-x))


def _norm_mod(x, gain, scale, shift):
    ms = jnp.mean(x * x, axis=-1, keepdims=True)
    return (x * lax.rsqrt(ms + RMS_EPS) * gain) * (1.0 + scale) + shift


def _const_spec(shape):
    zeros = (0,) * len(shape)
    return pl.BlockSpec(shape, lambda *_: zeros, pipeline_mode=pl.Buffered(1))


MOD_COLS = 1536


def _mod_kernel(c_ref, w_ref, b_ref, o_ref):
    c = c_ref[...]
    s = c * _sigmoid(c)
    s_hi = s.astype(BF16)
    s_lo = (s - s_hi.astype(F32)).astype(BF16)
    r = _dot(jnp.concatenate([s_hi, s_lo], axis=0), w_ref[...].astype(BF16))
    o_ref[...] = r[:8] + r[8:] + b_ref[...]


def _modulation(cmat, w_ada, b_ada):
    n = w_ada.shape[1]
    return pl.pallas_call(
        _mod_kernel,
        grid=(n // MOD_COLS,),
        in_specs=[pl.BlockSpec((8, D_MODEL), lambda i: (0, 0)),
                  pl.BlockSpec((D_MODEL, MOD_COLS), lambda i: (0, i)),
                  pl.BlockSpec((1, MOD_COLS), lambda i: (0, i))],
        out_specs=pl.BlockSpec((8, MOD_COLS), lambda i: (0, i)),
        out_shape=jax.ShapeDtypeStruct((8, n), F32),
        name="modulation",
    )(cmat, w_ada, b_ada.reshape(1, n))


def _pool_mix(u, pos, seq):
    n = u.shape[0]

    def down(x, d):
        return jnp.where(pos >= d, pltpu.roll(x, d, 0), 0.0)

    def up(x, d):
        return jnp.where(pos < seq - d, pltpu.roll(x, n - d, 0), 0.0)

    return down, up


def _pool_group(u, pos, seq, w):
    down, up = _pool_mix(u, pos, seq)
    hw = w // 2
    back = u
    fwd = u
    d = 1
    while d < hw:
        back = back + down(back, d)
        fwd = fwd + up(fwd, d)
        d *= 2
    s = down(back, 1) + fwd
    posf = pos.astype(F32)
    cnt = jnp.minimum(posf + hw, float(seq)) - jnp.maximum(posf - hw, 0.0)
    return s / cnt - u


def _pass_rows(tile):
    return tile if tile <= TOK_BLOCK else tile // 4


def _mixer_front(x_ref, mod_ref, gmix_ref, win_ref, hb_s, u_s, q_s, k_s, v_s):
    shift, scale = mod_ref[0, 0:1, :], mod_ref[0, 1:2, :]
    step = _pass_rows(x_ref.shape[0])
    for c in range(x_ref.shape[0] // step):
        rows = slice(c * step, (c + 1) * step)
        hb = _norm_mod(x_ref[rows, :], gmix_ref[...], scale, shift).astype(BF16)
        hb_s[rows, :] = hb
        u_s[rows, :] = _dot(hb, win_ref[:, 0:POOL_DIM])
        for dst, base in ((q_s, POOL_DIM), (k_s, POOL_DIM + ATTN_DIM), (v_s, POOL_DIM + 2 * ATTN_DIM)):
            z = _dot(hb, win_ref[:, base:base + ATTN_DIM])
            if dst is q_s:
                z = z * ATTN_SCALE
            for g in range(N_PAIRS):
                dst[g, rows, :] = z[:, g * PAIR_W:(g + 1) * PAIR_W].astype(dst.dtype)


def _mixer_back(x_ref, mod_ref, win_ref, wpool_ref, ps_ref, wpa_ref, wpb_ref, wout_ref, x1_ref,
                hb_s, u_s, o_s, pg_s, seq):
    tile = x_ref.shape[0]
    gate = mod_ref[0, 2:3, :]
    pos = lax.broadcasted_iota(jnp.int32, (tile, 1), 0) % seq
    for g, w in enumerate(POOL_WINDOWS):
        cols = slice(g * POOL_GROUP_DIM, (g + 1) * POOL_GROUP_DIM)
        pg_s[:, cols] = _pool_group(u_s[:, cols], pos, seq, w).astype(BF16)
    step = _pass_rows(tile)
    for c in range(tile // step):
        rows = slice(c * step, (c + 1) * step)
        ys = []
        for g in range(len(POOL_WINDOWS)):
            cols = slice(g * POOL_GROUP_DIM, (g + 1) * POOL_GROUP_DIM)
            ys.append((_dot(pg_s[rows, cols], wpool_ref[g]) * ps_ref[:, cols]).astype(BF16))
        a = _dot(jnp.concatenate(ys, axis=1), wpa_ref[...])
        ob = _dot(jnp.concatenate([o_s[g, rows, :] for g in range(N_PAIRS)], axis=1), wpb_ref[...])
        gab = _dot(hb_s[rows, :], win_ref[:, POOL_DIM + 3 * ATTN_DIM:IN_DIM])
        merged = _sigmoid(gab[:, :D_MODEL]) * a + _sigmoid(gab[:, D_MODEL:]) * ob
        mix = _dot(merged.astype(BF16), wout_ref[...])
        x1_ref[rows, :] = x_ref[rows, :] + gate * mix


def _ctx_mixer_kernel(x_ref, mod_ref, gmix_ref, win_ref, wpool_ref, ps_ref, wpa_ref, wpb_ref, wout_ref,
                      x1_ref, ko_ref, vo_ref, hb_s, u_s, q_s, k_s, v_s, o_s, pg_s):
    _mixer_front(x_ref, mod_ref, gmix_ref, win_ref, hb_s, u_s, q_s, k_s, v_s)
    even = lax.broadcasted_iota(jnp.int32, (1, PAIR_W), 1) < HEAD_DIM
    tile = x_ref.shape[0]
    for s in range(tile // SEQ):
        rows = slice(s * SEQ, (s + 1) * SEQ)
        for g in range(N_PAIRS):
            q2, k2, v2 = q_s[g, rows, :], k_s[g, rows, :], v_s[g, rows, :]
            ko_ref[s, 0, 2 * g] = k2[:, :HEAD_DIM]
            ko_ref[s, 0, 2 * g + 1] = k2[:, HEAD_DIM:]
            vo_ref[s, 0, 2 * g] = v2[:, :HEAD_DIM]
            vo_ref[s, 0, 2 * g + 1] = v2[:, HEAD_DIM:]
            kb, vb = k2.astype(BF16), v2.astype(BF16)
            outs = []
            for par in range(2):
                qm = jnp.where(even if par == 0 else jnp.logical_not(even), q2, 0.0).astype(BF16)
                sc = _dot_nt(qm, kb)
                m = jnp.max(sc, axis=-1, keepdims=True)
                p = jnp.exp(sc - m)
                l = jnp.sum(p, axis=-1, keepdims=True)
                outs.append(_dot(p.astype(BF16), vb) / l)
            o_s[g, rows, :] = jnp.where(even, outs[0], outs[1]).astype(BF16)
    _mixer_back(x_ref, mod_ref, win_ref, wpool_ref, ps_ref, wpa_ref, wpb_ref, wout_ref, x1_ref,
                hb_s, u_s, o_s, pg_s, SEQ)


Q_ROWS = 8


def _row_window(r):
    rs = min(max(r - WIN_R // 2, 0), ROWS - WIN_R)
    return rs, rs - r + WIN_R - 1


def _lat_mixer_kernel(x_ref, mod_ref, gmix_ref, win_ref, wpool_ref, ps_ref, wpa_ref, wpb_ref, wout_ref,
                      kc_ref, vc_ref, tb_ref, x1_ref, hb_s, u_s, q_s, k_s, v_s, o_s, pg_s):
    _mixer_front(x_ref, mod_ref, gmix_ref, win_ref, hb_s, u_s, q_s, k_s, v_s)
    even = lax.broadcasted_iota(jnp.int32, (1, PAIR_W), 1) < HEAD_DIM
    nk = WIN_R * GRID_W

    def pair_body(g, carry):
        kb, vb = k_s[g].astype(BF16), v_s[g].astype(BF16)
        kcb, vcb = kc_ref[0, g].astype(BF16), vc_ref[0, g].astype(BF16)
        for r0 in range(0, ROWS, Q_ROWS):
            rows = slice(r0 * GRID_W, (r0 + Q_ROWS) * GRID_W)
            q2 = q_s[g, rows, :]
            outs = []
            for par in range(2):
                qm = jnp.where(even if par == 0 else jnp.logical_not(even), q2, 0.0)
                s_ctx = _dot_nt(qm, kcb)
                slabs = []
                for i in range(Q_ROWS):
                    rs, rho = _row_window(r0 + i)
                    bias = tb_ref[2 * g + par, rho % 2, :, (rho - rho % 2) * GRID_W:(rho - rho % 2) * GRID_W + nk]
                    sl = _dot_nt(qm[i * GRID_W:(i + 1) * GRID_W, :], kb[rs * GRID_W:rs * GRID_W + nk, :])
                    slabs.append(sl + bias)
                s_loc = jnp.concatenate(slabs, axis=0)
                m = jnp.maximum(jnp.max(s_loc, axis=-1, keepdims=True), jnp.max(s_ctx, axis=-1, keepdims=True))
                p_loc = jnp.exp(s_loc - m)
                p_ctx = jnp.exp(s_ctx - m)
                l = jnp.sum(p_loc, axis=-1, keepdims=True) + jnp.sum(p_ctx, axis=-1, keepdims=True)
                p_locb = p_loc.astype(BF16)
                o_rows = []
                for i in range(Q_ROWS):
                    rs, _ = _row_window(r0 + i)
                    o_rows.append(_dot(p_locb[i * GRID_W:(i + 1) * GRID_W, :], vb[rs * GRID_W:rs * GRID_W + nk, :]))
                o = jnp.concatenate(o_rows, axis=0) + _dot(p_ctx.astype(BF16), vcb)
                outs.append(o / l)
            o_s[g, rows, :] = jnp.where(even, outs[0], outs[1]).astype(BF16)
        return carry

    lax.fori_loop(0, N_PAIRS, pair_body, 0)
    _mixer_back(x_ref, mod_ref, win_ref, wpool_ref, ps_ref, wpa_ref, wpb_ref, wout_ref, x1_ref,
                hb_s, u_s, o_s, pg_s, DEC_SEQ)


def _mixer_scratch(tile, kv_dtype):
    return [pltpu.VMEM((tile, D_MODEL), BF16),
            pltpu.VMEM((tile, POOL_DIM), F32),
            pltpu.VMEM((N_PAIRS, tile, PAIR_W), BF16),
            pltpu.VMEM((N_PAIRS, tile, PAIR_W), kv_dtype),
            pltpu.VMEM((N_PAIRS, tile, PAIR_W), kv_dtype),
            pltpu.VMEM((N_PAIRS, tile, PAIR_W), BF16),
            pltpu.VMEM((tile, POOL_DIM), BF16)]


def _weight_specs():
    return [_const_spec((1, D_MODEL)),
            _const_spec((D_MODEL, IN_DIM)),
            _const_spec((len(POOL_WINDOWS), POOL_GROUP_DIM, POOL_GROUP_DIM)),
            _const_spec((1, POOL_DIM)),
            _const_spec((POOL_DIM, D_MODEL)),
            _const_spec((ATTN_DIM, D_MODEL)),
            _const_spec((D_MODEL, D_MODEL))]


def _ctx_mixer(x, modv, weights):
    n = x.shape[0]
    nseq = TOK_BLOCK // SEQ
    cache = jax.ShapeDtypeStruct((n // SEQ, 1, N_HEADS, SEQ, HEAD_DIM), F32)
    cache_spec = pl.BlockSpec((nseq, 1, N_HEADS, SEQ, HEAD_DIM), lambda i: (i, 0, 0, 0, 0))
    return pl.pallas_call(
        _ctx_mixer_kernel,
        grid=(n // TOK_BLOCK,),
        in_specs=[pl.BlockSpec((TOK_BLOCK, D_MODEL), lambda i: (i, 0)),
                  pl.BlockSpec((1, N_MOD, D_MODEL), lambda i: (0, 0, 0))] + _weight_specs(),
        out_specs=[pl.BlockSpec((TOK_BLOCK, D_MODEL), lambda i: (i, 0)), cache_spec, cache_spec],
        out_shape=[jax.ShapeDtypeStruct((n, D_MODEL), F32), cache, cache],
        scratch_shapes=_mixer_scratch(TOK_BLOCK, F32),
        compiler_params=pltpu.CompilerParams(dimension_semantics=("arbitrary",),
                                             vmem_limit_bytes=V7X_VMEM_LIMIT),
        name="ctx_mixer",
    )(x, modv, *weights)


def _lat_mixer(x, modv, weights, kc, vc, tb):
    n = x.shape[0]
    return pl.pallas_call(
        _lat_mixer_kernel,
        grid=(n // DEC_SEQ,),
        in_specs=[pl.BlockSpec((DEC_SEQ, D_MODEL), lambda i: (i, 0)),
                  pl.BlockSpec((1, N_MOD, D_MODEL), lambda i: (i + 1, 0, 0))] + _weight_specs() + [
                  pl.BlockSpec((1, N_PAIRS, PAST_LEN, PAIR_W), lambda i: (i, 0, 0, 0)),
                  pl.BlockSpec((1, N_PAIRS, PAST_LEN, PAIR_W), lambda i: (i, 0, 0, 0)),
                  _const_spec((N_HEADS, 2, GRID_W, ROWS * GRID_W))],
        out_specs=pl.BlockSpec((DEC_SEQ, D_MODEL), lambda i: (i, 0)),
        out_shape=jax.ShapeDtypeStruct((n, D_MODEL), F32),
        scratch_shapes=_mixer_scratch(DEC_SEQ, BF16),
        compiler_params=pltpu.CompilerParams(dimension_semantics=("arbitrary",),
                                             vmem_limit_bytes=V7X_VMEM_LIMIT),
        name="lat_mixer",
    )(x, modv, *weights, kc, vc, tb)


N_RPB_ROWS = 2 * WIN_R - 1
N_RPB_COLS = 2 * WIN_C - 1
TABLE_W = ROWS * GRID_W


def _bias_kernel(v_ref, keep_ref, o_ref):
    for h in range(N_HEADS):
        for par in range(2):
            x = jnp.broadcast_to(v_ref[h, par:par + 1, :], (GRID_W, TABLE_W))
            shifted = pltpu.roll(x, TABLE_W - (WIN_C - 1), 1, stride=1, stride_axis=0)
            o_ref[h, par] = jnp.where(keep_ref[par] > 0.0, shifted, NEG_INF)


def _bias_tables(rpb):
    col = np.arange(GRID_W)
    cs = np.clip(col - WIN_C // 2, 0, GRID_W - WIN_C)
    in_win = (col[None, :] >= cs[:, None]) & (col[None, :] < cs[:, None] + WIN_C)
    keep = np.tile(in_win.astype(np.float32), (2, 1, ROWS))
    keep[0, :, N_RPB_ROWS * GRID_W:] = 0.0
    keep[1, :, (N_RPB_ROWS - 1) * GRID_W:] = 0.0
    rp = jnp.pad(rpb.astype(F32), ((0, 0), (0, ROWS + 1 - N_RPB_ROWS), (0, GRID_W - N_RPB_COLS)))
    v = jnp.stack([rp[:, :ROWS].reshape(N_HEADS, TABLE_W), rp[:, 1:].reshape(N_HEADS, TABLE_W)], axis=1)
    return pl.pallas_call(
        _bias_kernel,
        out_shape=jax.ShapeDtypeStruct((N_HEADS, 2, GRID_W, TABLE_W), F32),
        name="bias_tables",
    )(v, jnp.asarray(keep))


SORT_ROWS = 512


def _sort_kernel(xc_ref, xl_ref, mod_ref, gffn_ref, wr_ref, br_ref,
                 xs_ref, dk_ref, np_ref, off_ref):
    j = pl.program_id(0)
    x = jnp.where(j < _N_CTX_BLOCKS, xc_ref[...], xl_ref[...])
    shift, scale = mod_ref[0, 3:4, :], mod_ref[0, 4:5, :]
    hb = _norm_mod(x, gffn_ref[...], scale, shift).astype(BF16)
    logits = _dot_nt(wr_ref[...], hb) + br_ref[...]
    eio = lax.broadcasted_iota(jnp.int32, logits.shape, 0)
    work = logits
    sels, vals = [], []
    for _ in range(TOP_K):
        m = jnp.max(work, axis=0, keepdims=True)
        idx = jnp.min(jnp.where(work == m, eio, N_EXPERTS), axis=0, keepdims=True)
        sel = eio == idx
        sels.append(sel)
        vals.append(m)
        work = jnp.where(sel, -jnp.inf, work)
    exps = [jnp.exp(v - vals[0]) for v in vals]
    den = exps[0] + exps[1] + exps[2] + exps[3]
    mask = jnp.zeros(logits.shape, F32)
    for sel in sels:
        mask = mask + jnp.where(sel, 1.0, 0.0)
    t_row = lax.broadcasted_iota(jnp.int32, (TOK_BLOCK, TOK_BLOCK), 0)
    t_col = lax.broadcasted_iota(jnp.int32, (TOK_BLOCK, TOK_BLOCK), 1)
    before = jnp.where(t_row < t_col, 1.0, 0.0).astype(BF16)
    rank = _dot(mask.astype(BF16), before)
    cnt = jnp.sum(mask, axis=1, keepdims=True)
    np16 = jnp.floor((cnt + (CHUNK - 1.0)) * (1.0 / CHUNK))
    e_row = lax.broadcasted_iota(jnp.int32, (N_EXPERTS, N_EXPERTS), 0)
    e_col = lax.broadcasted_iota(jnp.int32, (N_EXPERTS, N_EXPERTS), 1)
    lower = jnp.where(e_col < e_row, 1.0, 0.0).astype(BF16)
    np16_b = jnp.broadcast_to(np16, (N_EXPERTS, 128))
    off16 = _dot(lower, np16_b.astype(BF16))
    np_ref[0] = np16_b
    off_ref[0] = off16
    dest = off16[:, 0:1] * float(CHUNK) + rank
    dests = []
    for k in range(TOP_K):
        dk = jnp.sum(jnp.where(sels[k], dest, 0.0), axis=0, keepdims=True)
        dk_ref[0, k:k + 1, :] = dk
        dk_ref[0, TOP_K + k:TOP_K + k + 1, :] = exps[k] / den
        dests.append(dk.astype(jnp.int32))
    rio = lax.broadcasted_iota(jnp.int32, (SORT_ROWS, TOK_BLOCK), 0)
    for c in range(BLOCK_CAP // SORT_ROWS):
        onehot = jnp.zeros((SORT_ROWS, TOK_BLOCK), F32)
        for dk in dests:
            onehot = jnp.where(rio == dk - c * SORT_ROWS, 1.0, onehot)
        xs_ref[c * SORT_ROWS:(c + 1) * SORT_ROWS, :] = _dot(onehot.astype(BF16), hb).astype(BF16)


def _block_mod_index(j):
    return jnp.where(j < _N_CTX_BLOCKS, 0, 1 + (j - _N_CTX_BLOCKS) // (DEC_SEQ // TOK_BLOCK))


def _token_specs():
    return [pl.BlockSpec((TOK_BLOCK, D_MODEL), lambda j: (jnp.minimum(j, _N_CTX_BLOCKS - 1), 0)),
            pl.BlockSpec((TOK_BLOCK, D_MODEL), lambda j: (jnp.maximum(j - _N_CTX_BLOCKS, 0), 0)),
            pl.BlockSpec((1, N_MOD, D_MODEL), lambda j: (_block_mod_index(j), 0, 0))]


def _route_sort(x1c, x1l, modv, g_ffn, wr_t, br):
    tbl = jax.ShapeDtypeStruct((_N_BLOCKS, N_EXPERTS, 128), F32)
    tbl_spec = pl.BlockSpec((1, N_EXPERTS, 128), lambda j: (j, 0, 0))
    return pl.pallas_call(
        _sort_kernel,
        grid=(_N_BLOCKS,),
        in_specs=_token_specs() + [_const_spec((1, D_MODEL)),
                                   _const_spec((N_EXPERTS, D_MODEL)),
                                   _const_spec((N_EXPERTS, 1))],
        out_specs=[pl.BlockSpec((BLOCK_CAP, D_MODEL), lambda j: (j, 0)),
                   pl.BlockSpec((1, 2 * TOP_K, TOK_BLOCK), lambda j: (j, 0, 0)),
                   tbl_spec, tbl_spec],
        out_shape=[jax.ShapeDtypeStruct((_N_BLOCKS * BLOCK_CAP, D_MODEL), BF16),
                   jax.ShapeDtypeStruct((_N_BLOCKS, 2 * TOP_K, TOK_BLOCK), F32),
                   tbl, tbl],
        compiler_params=pltpu.CompilerParams(dimension_semantics=("arbitrary",),
                                             vmem_limit_bytes=V7X_VMEM_LIMIT),
        name="route_sort",
    )(x1c, x1l, modv, g_ffn, wr_t, br)


EXPERT_BUF_ROWS = 2048
PASS_CHUNKS = EXPERT_BUF_ROWS // CHUNK
N_ROW_BUFS = 3


def _expert_kernel(np_ref, off_ref, wu_ref, bu_ref, wd_ref, bd_ref, xs_hbm, ys_hbm,
                   buf, wu_s, wd_s, gsem, ssem, pend):
    del xs_hbm
    e = pl.program_id(0)
    last = pl.num_programs(0) - 1
    slot = e % N_ROW_BUFS
    nxt = (e + 1) % N_ROW_BUFS

    def move_chunks(ex, q_lo, slot_, gather):
        def block_body(j, q0):
            n = np_ref[j * N_EXPERTS + ex]
            off = off_ref[j * N_EXPERTS + ex]
            c_lo = jnp.clip(q_lo - q0, 0, n)
            m = jnp.clip(q_lo + PASS_CHUNKS - q0, 0, n) - c_lo

            @pl.when(m > 0)
            def _():
                rows = pl.multiple_of(m * CHUNK, CHUNK)
                row0 = pl.multiple_of(j * BLOCK_CAP + (off + c_lo) * CHUNK, CHUNK)
                brow0 = pl.multiple_of((q0 + c_lo - q_lo) * CHUNK, CHUNK)
                hbm = ys_hbm.at[pl.ds(row0, rows), :]
                vm = buf.at[slot_, pl.ds(brow0, rows), :]
                if gather:
                    pltpu.make_async_copy(hbm, vm, gsem.at[slot_]).start()
                else:
                    pltpu.make_async_copy(vm, hbm, ssem.at[slot_]).start()

            return q0 + n

        lax.fori_loop(0, _N_BLOCKS, block_body, 0)

    def wait_chunks(sem, n, slot_):
        @pl.when(n > 0)
        def _():
            rows = pl.multiple_of(n * CHUNK, CHUNK)
            pltpu.make_async_copy(ys_hbm.at[pl.ds(0, rows), :], buf.at[slot_, pl.ds(0, rows), :], sem).wait()

    def mlp_rows(r0, rows):
        x = buf[slot, pl.ds(r0, rows), :]
        gu = _dot(x, wu_s[...]) + bu_ref[0]
        gate = jnp.minimum(gu[:, :D_FF], SWIGLU_LIMIT)
        up = jnp.clip(gu[:, D_FF:], -SWIGLU_LIMIT, SWIGLU_LIMIT)
        glu = gate * _sigmoid(SWIGLU_ALPHA * gate)
        y = _dot(((up + 1.0) * glu).astype(BF16), wd_s[...]) + bd_ref[0]
        buf[slot, pl.ds(r0, rows), :] = y.astype(BF16)

    def compute(n):
        rows = (n * CHUNK + ROW_STEP - 1) // ROW_STEP * ROW_STEP
        n_main = (jnp.maximum(rows - LAST_ROWS_MAX, 0) + EXPERT_ROW_TILE - 1) // EXPERT_ROW_TILE

        def tile_body(t, carry):
            mlp_rows(pl.multiple_of(t * EXPERT_ROW_TILE, ROW_STEP), EXPERT_ROW_TILE)
            return carry

        lax.fori_loop(0, n_main, tile_body, 0)
        r0 = pl.multiple_of(n_main * EXPERT_ROW_TILE, ROW_STEP)
        for size in range(ROW_STEP, LAST_ROWS_MAX + ROW_STEP, ROW_STEP):
            pl.when(rows - r0 == size)(functools.partial(mlp_rows, r0, size))

    def chunks_of(ex):
        return lax.fori_loop(0, _N_BLOCKS, lambda j, acc: acc + np_ref[j * N_EXPERTS + ex], 0)

    @pl.when(e == 0)
    def _():
        buf[...] = jnp.zeros_like(buf)
        for b in range(N_ROW_BUFS):
            pend[b] = 0
        move_chunks(0, 0, 0, True)

    wait_chunks(ssem.at[nxt], pend[nxt], nxt)
    pend[nxt] = 0

    @pl.when(e < last)
    def _():
        move_chunks(e + 1, 0, nxt, True)

    wu_s[...] = wu_ref[0].astype(BF16)
    wd_s[...] = wd_ref[0].astype(BF16)

    total = chunks_of(e)
    n0 = jnp.minimum(total, PASS_CHUNKS)
    wait_chunks(gsem.at[slot], n0, slot)
    compute(n0)
    move_chunks(e, 0, slot, False)
    pend[slot] = n0

    def pass_body(p, carry):
        wait_chunks(ssem.at[slot], pend[slot], slot)
        lo = p * PASS_CHUNKS
        n = jnp.minimum(total - lo, PASS_CHUNKS)
        move_chunks(e, lo, slot, True)
        wait_chunks(gsem.at[slot], n, slot)
        compute(n)
        move_chunks(e, lo, slot, False)
        pend[slot] = n
        return carry

    lax.fori_loop(1, (total + PASS_CHUNKS - 1) // PASS_CHUNKS, pass_body, 0)

    @pl.when(e == last)
    def _():
        for b in range(N_ROW_BUFS):
            wait_chunks(ssem.at[b], pend[b], b)
            pend[b] = 0


def _experts(np16, off16, w_up, b_up, w_down, b_down, xs):
    grid_spec = pltpu.PrefetchScalarGridSpec(
        num_scalar_prefetch=2,
        grid=(N_EXPERTS,),
        in_specs=[pl.BlockSpec((1, D_MODEL, 2 * D_FF), lambda e, *_: (e, 0, 0)),
                  pl.BlockSpec((1, 1, 2 * D_FF), lambda e, *_: (e, 0, 0)),
                  pl.BlockSpec((1, D_FF, D_MODEL), lambda e, *_: (e, 0, 0)),
                  pl.BlockSpec((1, 1, D_MODEL), lambda e, *_: (e, 0, 0)),
                  pl.BlockSpec(memory_space=pl.ANY)],
        out_specs=pl.BlockSpec(memory_space=pl.ANY),
        scratch_shapes=[pltpu.VMEM((N_ROW_BUFS, EXPERT_BUF_ROWS, D_MODEL), BF16),
                        pltpu.VMEM((D_MODEL, 2 * D_FF), BF16),
                        pltpu.VMEM((D_FF, D_MODEL), BF16),
                        pltpu.SemaphoreType.DMA((N_ROW_BUFS,)),
                        pltpu.SemaphoreType.DMA((N_ROW_BUFS,)),
                        pltpu.SMEM((N_ROW_BUFS,), jnp.int32)],
    )
    return pl.pallas_call(
        _expert_kernel,
        grid_spec=grid_spec,
        out_shape=jax.ShapeDtypeStruct(xs.shape, xs.dtype),
        input_output_aliases={6: 0},
        compiler_params=pltpu.CompilerParams(dimension_semantics=("arbitrary",),
                                             vmem_limit_bytes=V7X_VMEM_LIMIT),
        name="experts",
    )(np16, off16, w_up, b_up.reshape(N_EXPERTS, 1, 2 * D_FF), w_down,
      b_down.reshape(N_EXPERTS, 1, D_MODEL), xs)


def _combine_kernel(ys_ref, dk_ref, xc_ref, xl_ref, mod_ref, gfin_ref, yc_ref, yl_ref, y_s):
    j = pl.program_id(0)
    half = TOK_BLOCK // 2
    cio = lax.broadcasted_iota(jnp.int32, (half, SORT_ROWS), 1).astype(F32)
    halves = [slice(h * half, (h + 1) * half) for h in range(2)]
    accs = [jnp.zeros((half, D_MODEL), F32) for _ in halves]
    for c in range(BLOCK_CAP // SORT_ROWS):
        ys = ys_ref[c * SORT_ROWS:(c + 1) * SORT_ROWS, :]
        for h, rows in enumerate(halves):
            d = dk_ref[0, rows, :]
            w = jnp.zeros((half, SORT_ROWS), F32)
            for k in range(TOP_K):
                w = jnp.where(cio == d[:, k:k + 1] - float(c * SORT_ROWS), d[:, TOP_K + k:TOP_K + k + 1], w)
            accs[h] = accs[h] + _dot(w.astype(BF16), ys)
    for h, rows in enumerate(halves):
        acc = accs[h]
        x1 = jnp.where(j < _N_CTX_BLOCKS, xc_ref[rows, :], xl_ref[rows, :])
        x2 = x1 + mod_ref[0, 5:6, :] * acc
        ms = jnp.mean(x2 * x2, axis=-1, keepdims=True)
        y_s[rows, :] = x2 * lax.rsqrt(ms + RMS_EPS) * gfin_ref[...]

    @pl.when(j < _N_CTX_BLOCKS)
    def _():
        yc_ref[...] = y_s[...]

    @pl.when(j >= _N_CTX_BLOCKS)
    def _():
        yl_ref[...] = y_s[...]


def _combine(ys, dk_t, x1c, x1l, modv, g_final):
    return pl.pallas_call(
        _combine_kernel,
        grid=(_N_BLOCKS,),
        in_specs=[pl.BlockSpec((BLOCK_CAP, D_MODEL), lambda j: (j, 0)),
                  pl.BlockSpec((1, TOK_BLOCK, 2 * TOP_K), lambda j: (j, 0, 0))] + _token_specs() + [
                  _const_spec((1, D_MODEL))],
        out_specs=[pl.BlockSpec((TOK_BLOCK, D_MODEL), lambda j: (jnp.minimum(j, _N_CTX_BLOCKS - 1), 0)),
                   pl.BlockSpec((TOK_BLOCK, D_MODEL), lambda j: (jnp.maximum(j - _N_CTX_BLOCKS, 0), 0))],
        out_shape=[jax.ShapeDtypeStruct(x1c.shape, F32), jax.ShapeDtypeStruct(x1l.shape, F32)],
        scratch_shapes=[pltpu.VMEM((TOK_BLOCK, D_MODEL), F32)],
        compiler_params=pltpu.CompilerParams(dimension_semantics=("arbitrary",),
                                             vmem_limit_bytes=V7X_VMEM_LIMIT),
        name="combine",
    )(ys, dk_t, x1c, x1l, modv, g_final)


def kernel(x_prompt, x_sample, cache_k, cache_v, c, c_ctx, w_ada, b_ada, g_mix, w_in, w_pool, pool_scale,
           w_pa, w_pb, rpb, w_out, g_ffn, w_router, b_router, w_up, b_up, w_down, b_down, g_final):
    assert w_ada.shape[0] == 1, "single trunk layer"
    batch, seq, d = x_prompt.shape
    dec_batch, dec_seq, _ = x_sample.shape
    assert (seq, dec_seq, d) == (SEQ, DEC_SEQ, D_MODEL)
    assert batch * seq == _N_CTX_BLOCKS * TOK_BLOCK and dec_batch * dec_seq == _N_LAT_BLOCKS * TOK_BLOCK

    cmat = jnp.concatenate([c_ctx[None, :], c, jnp.zeros((8 - 1 - dec_batch, d), F32)], axis=0)
    modv = _modulation(cmat, w_ada[0], b_ada[0]).reshape(8, N_MOD, d)

    weights = (g_mix[0][None, :], w_in[0].astype(BF16), w_pool[0].astype(BF16), pool_scale[0][None, :],
               w_pa[0].astype(BF16), w_pb[0].astype(BF16), w_out[0].astype(BF16))

    def by_pair(cache):
        z = cache[:, 0].reshape(dec_batch, N_PAIRS, 2, PAST_LEN, HEAD_DIM)
        return z.transpose(0, 1, 3, 2, 4).reshape(dec_batch, N_PAIRS, PAST_LEN, PAIR_W)

    x1c, new_k, new_v = _ctx_mixer(x_prompt.reshape(batch * seq, d), modv, weights)
    x1l = _lat_mixer(x_sample.reshape(dec_batch * dec_seq, d), modv, weights,
                     by_pair(cache_k), by_pair(cache_v), _bias_tables(rpb[0]))

    xs, dk, np16, off16 = _route_sort(x1c, x1l, modv, g_ffn[0][None, :],
                                      w_router[0].T.astype(BF16), b_router[0][:, None])
    np16_i = np16[:, :, 0].astype(jnp.int32).reshape(-1)
    off16_i = off16[:, :, 0].astype(jnp.int32).reshape(-1)
    ys = _experts(np16_i, off16_i, w_up[0], b_up[0], w_down[0], b_down[0], xs)
    yc, yl = _combine(ys, dk.transpose(0, 2, 1), x1c, x1l, modv, g_final[None, :])
    return (yc.reshape(batch, seq, d), yl.reshape(dec_batch, dec_seq, d), new_k, new_v)
```

```python
import functools

import jax
import jax.numpy as jnp
import numpy as np
from jax import lax
from jax.experimental import pallas as pl
from jax.experimental.pallas import tpu as pltpu

F32 = jnp.float32
BF16 = jnp.bfloat16

D_MODEL = 1024
SEQ = 256
DEC_SEQ = 1024
GRID_W = 64
ROWS = DEC_SEQ // GRID_W
N_HEADS = 8
HEAD_DIM = 64
N_PAIRS = N_HEADS // 2
PAIR_W = 2 * HEAD_DIM
PAST_LEN = 512
POOL_DIM = 512
POOL_WINDOWS = (2, 4, 8, 16)
POOL_GROUP_DIM = 128
ATTN_DIM = 512
WIN_R = 8
WIN_C = 16
N_EXPERTS = 32
TOP_K = 4
D_FF = 1024
SWIGLU_LIMIT = 7.0
SWIGLU_ALPHA = 1.702
N_MOD = 6
RMS_EPS = 1e-6
NEG_INF = -1e30
ATTN_SCALE = HEAD_DIM ** -0.5
IN_DIM = POOL_DIM + 3 * ATTN_DIM + 2 * D_MODEL

TOK_BLOCK = 512
CHUNK = 16
BLOCK_CAP = TOK_BLOCK * TOP_K + N_EXPERTS * CHUNK
ROW_STEP = 128
EXPERT_ROW_TILE = 384
LAST_ROWS_MAX = 512
V7X_VMEM_LIMIT = 60 * 1024 * 1024

_N_CTX_BLOCKS = 8
_N_LAT_BLOCKS = 4
_N_BLOCKS = _N_CTX_BLOCKS + _N_LAT_BLOCKS

_NT = (((1,), (1,)), ((), ()))


def _dot(a, b):
    return jnp.dot(a, b, preferred_element_type=F32)


def _dot_nt(a, b):
    return lax.dot_general(a, b, _NT, preferred_element_type=F32)


def _sigmoid(x):
    return 1.0 / (1.0 + jnp.exp(-x))


def _norm_mod(x, gain, scale, shift):
    ms = jnp.mean(x * x, axis=-1, keepdims=True)
    return (x * lax.rsqrt(ms + RMS_EPS) * gain) * (1.0 + scale) + shift


def _const_spec(shape):
    zeros = (0,) * len(shape)
    return pl.BlockSpec(shape, lambda *_: zeros, pipeline_mode=pl.Buffered(1))


MOD_COLS = 1536


def _mod_kernel(c_ref, w_ref, b_ref, o_ref):
    c = c_ref[...]
    s = c * _sigmoid(c)
    s_hi = s.astype(BF16)
    s_lo = (s - s_hi.astype(F32)).astype(BF16)
    r = _dot(jnp.concatenate([s_hi, s_lo], axis=0), w_ref[...].astype(BF16))
    o_ref[...] = r[:8] + r[8:] + b_ref[...]


def _modulation(cmat, w_ada, b_ada):
    n = w_ada.shape[1]
    return pl.pallas_call(
        _mod_kernel,
        grid=(n // MOD_COLS,),
        in_specs=[pl.BlockSpec((8, D_MODEL), lambda i: (0, 0)),
                  pl.BlockSpec((D_MODEL, MOD_COLS), lambda i: (0, i)),
                  pl.BlockSpec((1, MOD_COLS), lambda i: (0, i))],
        out_specs=pl.BlockSpec((8, MOD_COLS), lambda i: (0, i)),
        out_shape=jax.ShapeDtypeStruct((8, n), F32),
        name="modulation",
    )(cmat, w_ada, b_ada.reshape(1, n))


def _pool_mix(u, pos, seq):
    n = u.shape[0]

    def down(x, d):
        return jnp.where(pos >= d, pltpu.roll(x, d, 0), 0.0)

    def up(x, d):
        return jnp.where(pos < seq - d, pltpu.roll(x, n - d, 0), 0.0)

    return down, up


def _pool_group(u, pos, seq, w):
    down, up = _pool_mix(u, pos, seq)
    hw = w // 2
    back = u
    fwd = u
    d = 1
    while d < hw:
        back = back + down(back, d)
        fwd = fwd + up(fwd, d)
        d *= 2
    s = down(back, 1) + fwd
    posf = pos.astype(F32)
    cnt = jnp.minimum(posf + hw, float(seq)) - jnp.maximum(posf - hw, 0.0)
    return s / cnt - u


def _pass_rows(tile):
    return tile if tile <= TOK_BLOCK else tile // 4


def _mixer_front(x_ref, mod_ref, gmix_ref, win_ref, hb_s, u_s, q_s, k_s, v_s):
    shift, scale = mod_ref[0, 0:1, :], mod_ref[0, 1:2, :]
    step = _pass_rows(x_ref.shape[0])
    for c in range(x_ref.shape[0] // step):
        rows = slice(c * step, (c + 1) * step)
        hb = _norm_mod(x_ref[rows, :], gmix_ref[...], scale, shift).astype(BF16)
        hb_s[rows, :] = hb
        u_s[rows, :] = _dot(hb, win_ref[:, 0:POOL_DIM])
        for dst, base in ((q_s, POOL_DIM), (k_s, POOL_DIM + ATTN_DIM), (v_s, POOL_DIM + 2 * ATTN_DIM)):
            z = _dot(hb, win_ref[:, base:base + ATTN_DIM])
            if dst is q_s:
                z = z * ATTN_SCALE
            for g in range(N_PAIRS):
                dst[g, rows, :] = z[:, g * PAIR_W:(g + 1) * PAIR_W].astype(dst.dtype)


def _mixer_back(x_ref, mod_ref, win_ref, wpool_ref, ps_ref, wpa_ref, wpb_ref, wout_ref, x1_ref,
                hb_s, u_s, o_s, pg_s, seq):
    tile = x_ref.shape[0]
    gate = mod_ref[0, 2:3, :]
    pos = lax.broadcasted_iota(jnp.int32, (tile, 1), 0) % seq
    for g, w in enumerate(POOL_WINDOWS):
        cols = slice(g * POOL_GROUP_DIM, (g + 1) * POOL_GROUP_DIM)
        pg_s[:, cols] = _pool_group(u_s[:, cols], pos, seq, w).astype(BF16)
    step = _pass_rows(tile)
    for c in range(tile // step):
        rows = slice(c * step, (c + 1) * step)
        ys = []
        for g in range(len(POOL_WINDOWS)):
            cols = slice(g * POOL_GROUP_DIM, (g + 1) * POOL_GROUP_DIM)
            ys.append((_dot(pg_s[rows, cols], wpool_ref[g]) * ps_ref[:, cols]).astype(BF16))
        a = _dot(jnp.concatenate(ys, axis=1), wpa_ref[...])
        ob = _dot(jnp.concatenate([o_s[g, rows, :] for g in range(N_PAIRS)], axis=1), wpb_ref[...])
        gab = _dot(hb_s[rows, :], win_ref[:, POOL_DIM + 3 * ATTN_DIM:IN_DIM])
        merged = _sigmoid(gab[:, :D_MODEL]) * a + _sigmoid(gab[:, D_MODEL:]) * ob
        mix = _dot(merged.astype(BF16), wout_ref[...])
        x1_ref[rows, :] = x_ref[rows, :] + gate * mix


def _ctx_mixer_kernel(x_ref, mod_ref, gmix_ref, win_ref, wpool_ref, ps_ref, wpa_ref, wpb_ref, wout_ref,
                      x1_ref, ko_ref, vo_ref, hb_s, u_s, q_s, k_s, v_s, o_s, pg_s):
    _mixer_front(x_ref, mod_ref, gmix_ref, win_ref, hb_s, u_s, q_s, k_s, v_s)
    even = lax.broadcasted_iota(jnp.int32, (1, PAIR_W), 1) < HEAD_DIM
    tile = x_ref.shape[0]
    for s in range(tile // SEQ):
        rows = slice(s * SEQ, (s + 1) * SEQ)
        for g in range(N_PAIRS):
            q2, k2, v2 = q_s[g, rows, :], k_s[g, rows, :], v_s[g, rows, :]
            ko_ref[s, 0, 2 * g] = k2[:, :HEAD_DIM]
            ko_ref[s, 0, 2 * g + 1] = k2[:, HEAD_DIM:]
            vo_ref[s, 0, 2 * g] = v2[:, :HEAD_DIM]
            vo_ref[s, 0, 2 * g + 1] = v2[:, HEAD_DIM:]
            kb, vb = k2.astype(BF16), v2.astype(BF16)
            outs = []
            for par in range(2):
                qm = jnp.where(even if par == 0 else jnp.logical_not(even), q2, 0.0).astype(BF16)
                sc = _dot_nt(qm, kb)
                m = jnp.max(sc, axis=-1, keepdims=True)
                p = jnp.exp(sc - m)
                l = jnp.sum(p, axis=-1, keepdims=True)
                outs.append(_dot(p.astype(BF16), vb) / l)
            o_s[g, rows, :] = jnp.where(even, outs[0], outs[1]).astype(BF16)
    _mixer_back(x_ref, mod_ref, win_ref, wpool_ref, ps_ref, wpa_ref, wpb_ref, wout_ref, x1_ref,
                hb_s, u_s, o_s, pg_s, SEQ)


def _row_window(r):
    rs = min(max(r - WIN_R // 2, 0), ROWS - WIN_R)
    return rs, rs - r + WIN_R - 1


def _lat_mixer_kernel(x_ref, mod_ref, gmix_ref, win_ref, wpool_ref, ps_ref, wpa_ref, wpb_ref, wout_ref,
                      kc_ref, vc_ref, bv_ref, keep_ref, x1_ref, hb_s, u_s, q_s, k_s, v_s, o_s, pg_s, tb_ref):
    @pl.when(pl.program_id(0) == 0)
    def _():
        _bias_kernel(bv_ref, keep_ref, tb_ref)

    _mixer_front(x_ref, mod_ref, gmix_ref, win_ref, hb_s, u_s, q_s, k_s, v_s)
    even = lax.broadcasted_iota(jnp.int32, (1, PAIR_W), 1) < HEAD_DIM
    nk = WIN_R * GRID_W

    def pair_body(g, carry):
        q2 = q_s[g]
        kb, vb = k_s[g].astype(BF16), v_s[g].astype(BF16)
        kcb, vcb = kc_ref[0, g].astype(BF16), vc_ref[0, g].astype(BF16)
        outs = []
        for par in range(2):
            qm = jnp.where(even if par == 0 else jnp.logical_not(even), q2, 0.0).astype(BF16)
            s_ctx = _dot_nt(qm, kcb)
            slabs = []
            for r in range(ROWS):
                rs, rho = _row_window(r)
                bias = tb_ref[2 * g + par, rho % 2, :, (rho - rho % 2) * GRID_W:(rho - rho % 2) * GRID_W + nk]
                sl = _dot_nt(qm[r * GRID_W:(r + 1) * GRID_W, :], kb[rs * GRID_W:rs * GRID_W + nk, :])
                slabs.append(sl + bias)
            s_loc = jnp.concatenate(slabs, axis=0)
            m = jnp.maximum(jnp.max(s_loc, axis=-1, keepdims=True), jnp.max(s_ctx, axis=-1, keepdims=True))
            p_loc = jnp.exp(s_loc - m)
            p_ctx = jnp.exp(s_ctx - m)
            l = jnp.sum(p_loc, axis=-1, keepdims=True) + jnp.sum(p_ctx, axis=-1, keepdims=True)
            p_locb = p_loc.astype(BF16)
            o_rows = []
            for r in range(ROWS):
                rs, _ = _row_window(r)
                o_rows.append(_dot(p_locb[r * GRID_W:(r + 1) * GRID_W, :], vb[rs * GRID_W:rs * GRID_W + nk, :]))
            o = jnp.concatenate(o_rows, axis=0) + _dot(p_ctx.astype(BF16), vcb)
            outs.append(o / l)
        o_s[g] = jnp.where(even, outs[0], outs[1]).astype(BF16)
        return carry

    lax.fori_loop(0, N_PAIRS, pair_body, 0)
    _mixer_back(x_ref, mod_ref, win_ref, wpool_ref, ps_ref, wpa_ref, wpb_ref, wout_ref, x1_ref,
                hb_s, u_s, o_s, pg_s, DEC_SEQ)


def _mixer_scratch(tile, kv_dtype):
    return [pltpu.VMEM((tile, D_MODEL), BF16),
            pltpu.VMEM((tile, POOL_DIM), F32),
            pltpu.VMEM((N_PAIRS, tile, PAIR_W), BF16),
            pltpu.VMEM((N_PAIRS, tile, PAIR_W), kv_dtype),
            pltpu.VMEM((N_PAIRS, tile, PAIR_W), kv_dtype),
            pltpu.VMEM((N_PAIRS, tile, PAIR_W), BF16),
            pltpu.VMEM((tile, POOL_DIM), BF16)]


def _weight_specs():
    return [_const_spec((1, D_MODEL)),
            _const_spec((D_MODEL, IN_DIM)),
            _const_spec((len(POOL_WINDOWS), POOL_GROUP_DIM, POOL_GROUP_DIM)),
            _const_spec((1, POOL_DIM)),
            _const_spec((POOL_DIM, D_MODEL)),
            _const_spec((ATTN_DIM, D_MODEL)),
            _const_spec((D_MODEL, D_MODEL))]


def _ctx_mixer(x, modv, weights):
    n = x.shape[0]
    nseq = TOK_BLOCK // SEQ
    cache = jax.ShapeDtypeStruct((n // SEQ, 1, N_HEADS, SEQ, HEAD_DIM), F32)
    cache_spec = pl.BlockSpec((nseq, 1, N_HEADS, SEQ, HEAD_DIM), lambda i: (i, 0, 0, 0, 0))
    return pl.pallas_call(
        _ctx_mixer_kernel,
        grid=(n // TOK_BLOCK,),
        in_specs=[pl.BlockSpec((TOK_BLOCK, D_MODEL), lambda i: (i, 0)),
                  pl.BlockSpec((1, N_MOD, D_MODEL), lambda i: (0, 0, 0))] + _weight_specs(),
        out_specs=[pl.BlockSpec((TOK_BLOCK, D_MODEL), lambda i: (i, 0)), cache_spec, cache_spec],
        out_shape=[jax.ShapeDtypeStruct((n, D_MODEL), F32), cache, cache],
        scratch_shapes=_mixer_scratch(TOK_BLOCK, F32),
        compiler_params=pltpu.CompilerParams(dimension_semantics=("arbitrary",),
                                             vmem_limit_bytes=V7X_VMEM_LIMIT),
        name="ctx_mixer",
    )(x, modv, *weights)


def _lat_mixer(x, modv, weights, kc, vc, tb):
    n = x.shape[0]
    return pl.pallas_call(
        _lat_mixer_kernel,
        grid=(n // DEC_SEQ,),
        in_specs=[pl.BlockSpec((DEC_SEQ, D_MODEL), lambda i: (i, 0)),
                  pl.BlockSpec((1, N_MOD, D_MODEL), lambda i: (i + 1, 0, 0))] + _weight_specs() + [
                  pl.BlockSpec((1, N_PAIRS, PAST_LEN, PAIR_W), lambda i: (i, 0, 0, 0)),
                  pl.BlockSpec((1, N_PAIRS, PAST_LEN, PAIR_W), lambda i: (i, 0, 0, 0)),
                  _const_spec((N_HEADS, 2, ROWS * GRID_W)),
                  _const_spec((2, GRID_W, ROWS * GRID_W))],
        out_specs=pl.BlockSpec((DEC_SEQ, D_MODEL), lambda i: (i, 0)),
        out_shape=jax.ShapeDtypeStruct((n, D_MODEL), F32),
        scratch_shapes=_mixer_scratch(DEC_SEQ, BF16) + [pltpu.VMEM((N_HEADS, 2, GRID_W, ROWS * GRID_W), F32)],
        compiler_params=pltpu.CompilerParams(dimension_semantics=("arbitrary",),
                                             vmem_limit_bytes=V7X_VMEM_LIMIT),
        name="lat_mixer",
    )(x, modv, *weights, kc, vc, *tb)


N_RPB_ROWS = 2 * WIN_R - 1
N_RPB_COLS = 2 * WIN_C - 1
TABLE_W = ROWS * GRID_W


def _bias_kernel(v_ref, keep_ref, o_ref):
    for h in range(N_HEADS):
        for par in range(2):
            x = jnp.broadcast_to(v_ref[h, par:par + 1, :], (GRID_W, TABLE_W))
            shifted = pltpu.roll(x, TABLE_W - (WIN_C - 1), 1, stride=1, stride_axis=0)
            o_ref[h, par] = jnp.where(keep_ref[par] > 0.0, shifted, NEG_INF)


def _bias_inputs(rpb):
    col = np.arange(GRID_W)
    cs = np.clip(col - WIN_C // 2, 0, GRID_W - WIN_C)
    in_win = (col[None, :] >= cs[:, None]) & (col[None, :] < cs[:, None] + WIN_C)
    keep = np.tile(in_win.astype(np.float32), (2, 1, ROWS))
    keep[0, :, N_RPB_ROWS * GRID_W:] = 0.0
    keep[1, :, (N_RPB_ROWS - 1) * GRID_W:] = 0.0
    rp = jnp.pad(rpb.astype(F32), ((0, 0), (0, ROWS + 1 - N_RPB_ROWS), (0, GRID_W - N_RPB_COLS)))
    v = jnp.stack([rp[:, :ROWS].reshape(N_HEADS, TABLE_W), rp[:, 1:].reshape(N_HEADS, TABLE_W)], axis=1)
    return v, jnp.asarray(keep)


SORT_ROWS = 512


def _sort_kernel(xc_ref, xl_ref, mod_ref, gffn_ref, wr_ref, br_ref,
                 xs_ref, dk_ref, np_ref, off_ref):
    j = pl.program_id(0)
    x = jnp.where(j < _N_CTX_BLOCKS, xc_ref[...], xl_ref[...])
    shift, scale = mod_ref[0, 3:4, :], mod_ref[0, 4:5, :]
    hb = _norm_mod(x, gffn_ref[...], scale, shift).astype(BF16)
    logits = _dot_nt(wr_ref[...], hb) + br_ref[...]
    eio = lax.broadcasted_iota(jnp.int32, logits.shape, 0)
    work = logits
    sels, vals = [], []
    for _ in range(TOP_K):
        m = jnp.max(work, axis=0, keepdims=True)
        idx = jnp.min(jnp.where(work == m, eio, N_EXPERTS), axis=0, keepdims=True)
        sel = eio == idx
        sels.append(sel)
        vals.append(m)
        work = jnp.where(sel, -jnp.inf, work)
    exps = [jnp.exp(v - vals[0]) for v in vals]
    den = exps[0] + exps[1] + exps[2] + exps[3]
    mask = jnp.zeros(logits.shape, F32)
    for sel in sels:
        mask = mask + jnp.where(sel, 1.0, 0.0)
    t_row = lax.broadcasted_iota(jnp.int32, (TOK_BLOCK, TOK_BLOCK), 0)
    t_col = lax.broadcasted_iota(jnp.int32, (TOK_BLOCK, TOK_BLOCK), 1)
    before = jnp.where(t_row < t_col, 1.0, 0.0).astype(BF16)
    rank = _dot(mask.astype(BF16), before)
    cnt = jnp.sum(mask, axis=1, keepdims=True)
    np16 = jnp.floor((cnt + (CHUNK - 1.0)) * (1.0 / CHUNK))
    e_row = lax.broadcasted_iota(jnp.int32, (N_EXPERTS, N_EXPERTS), 0)
    e_col = lax.broadcasted_iota(jnp.int32, (N_EXPERTS, N_EXPERTS), 1)
    lower = jnp.where(e_col < e_row, 1.0, 0.0).astype(BF16)
    np16_b = jnp.broadcast_to(np16, (N_EXPERTS, 128))
    off16 = _dot(lower, np16_b.astype(BF16))
    np_ref[0] = np16_b
    off_ref[0] = off16
    dest = off16[:, 0:1] * float(CHUNK) + rank
    dests = []
    for k in range(TOP_K):
        dk = jnp.sum(jnp.where(sels[k], dest, 0.0), axis=0, keepdims=True)
        dk_ref[0, k:k + 1, :] = dk
        dk_ref[0, TOP_K + k:TOP_K + k + 1, :] = exps[k] / den
        dests.append(dk.astype(jnp.int32))
    rio = lax.broadcasted_iota(jnp.int32, (SORT_ROWS, TOK_BLOCK), 0)
    for c in range(BLOCK_CAP // SORT_ROWS):
        onehot = jnp.zeros((SORT_ROWS, TOK_BLOCK), F32)
        for dk in dests:
            onehot = jnp.where(rio == dk - c * SORT_ROWS, 1.0, onehot)
        xs_ref[c * SORT_ROWS:(c + 1) * SORT_ROWS, :] = _dot(onehot.astype(BF16), hb).astype(BF16)


def _block_mod_index(j):
    return jnp.where(j < _N_CTX_BLOCKS, 0, 1 + (j - _N_CTX_BLOCKS) // (DEC_SEQ // TOK_BLOCK))


def _token_specs():
    return [pl.BlockSpec((TOK_BLOCK, D_MODEL), lambda j: (jnp.minimum(j, _N_CTX_BLOCKS - 1), 0)),
            pl.BlockSpec((TOK_BLOCK, D_MODEL), lambda j: (jnp.maximum(j - _N_CTX_BLOCKS, 0), 0)),
            pl.BlockSpec((1, N_MOD, D_MODEL), lambda j: (_block_mod_index(j), 0, 0))]


def _route_sort(x1c, x1l, modv, g_ffn, wr_t, br):
    tbl = jax.ShapeDtypeStruct((_N_BLOCKS, N_EXPERTS, 128), F32)
    tbl_spec = pl.BlockSpec((1, N_EXPERTS, 128), lambda j: (j, 0, 0))
    return pl.pallas_call(
        _sort_kernel,
        grid=(_N_BLOCKS,),
        in_specs=_token_specs() + [_const_spec((1, D_MODEL)),
                                   _const_spec((N_EXPERTS, D_MODEL)),
                                   _const_spec((N_EXPERTS, 1))],
        out_specs=[pl.BlockSpec((BLOCK_CAP, D_MODEL), lambda j: (j, 0)),
                   pl.BlockSpec((1, 2 * TOP_K, TOK_BLOCK), lambda j: (j, 0, 0)),
                   tbl_spec, tbl_spec],
        out_shape=[jax.ShapeDtypeStruct((_N_BLOCKS * BLOCK_CAP, D_MODEL), BF16),
                   jax.ShapeDtypeStruct((_N_BLOCKS, 2 * TOP_K, TOK_BLOCK), F32),
                   tbl, tbl],
        compiler_params=pltpu.CompilerParams(dimension_semantics=("arbitrary",),
                                             vmem_limit_bytes=V7X_VMEM_LIMIT),
        name="route_sort",
    )(x1c, x1l, modv, g_ffn, wr_t, br)


EXPERT_BUF_ROWS = 2048
PASS_CHUNKS = EXPERT_BUF_ROWS // CHUNK
N_ROW_BUFS = 3


def _expert_kernel(np_ref, off_ref, wu_ref, bu_ref, wd_ref, bd_ref, xs_hbm, ys_hbm,
                   buf, wu_s, wd_s, gsem, ssem, pend):
    del xs_hbm
    e = pl.program_id(0)
    last = pl.num_programs(0) - 1
    slot = e % N_ROW_BUFS
    nxt = (e + 1) % N_ROW_BUFS

    def move_chunks(ex, q_lo, slot_, gather):
        def block_body(j, q0):
            n = np_ref[j * N_EXPERTS + ex]
            off = off_ref[j * N_EXPERTS + ex]
            c_lo = jnp.clip(q_lo - q0, 0, n)
            m = jnp.clip(q_lo + PASS_CHUNKS - q0, 0, n) - c_lo

            @pl.when(m > 0)
            def _():
                rows = pl.multiple_of(m * CHUNK, CHUNK)
                row0 = pl.multiple_of(j * BLOCK_CAP + (off + c_lo) * CHUNK, CHUNK)
                brow0 = pl.multiple_of((q0 + c_lo - q_lo) * CHUNK, CHUNK)
                hbm = ys_hbm.at[pl.ds(row0, rows), :]
                vm = buf.at[slot_, pl.ds(brow0, rows), :]
                if gather:
                    pltpu.make_async_copy(hbm, vm, gsem.at[slot_]).start()
                else:
                    pltpu.make_async_copy(vm, hbm, ssem.at[slot_]).start()

            return q0 + n

        lax.fori_loop(0, _N_BLOCKS, block_body, 0)

    def wait_chunks(sem, n, slot_):
        @pl.when(n > 0)
        def _():
            rows = pl.multiple_of(n * CHUNK, CHUNK)
            pltpu.make_async_copy(ys_hbm.at[pl.ds(0, rows), :], buf.at[slot_, pl.ds(0, rows), :], sem).wait()

    def mlp_rows(r0, rows):
        x = buf[slot, pl.ds(r0, rows), :]
        gu = _dot(x, wu_s[...]) + bu_ref[0]
        gate = jnp.minimum(gu[:, :D_FF], SWIGLU_LIMIT)
        up = jnp.clip(gu[:, D_FF:], -SWIGLU_LIMIT, SWIGLU_LIMIT)
        glu = gate * _sigmoid(SWIGLU_ALPHA * gate)
        y = _dot(((up + 1.0) * glu).astype(BF16), wd_s[...]) + bd_ref[0]
        buf[slot, pl.ds(r0, rows), :] = y.astype(BF16)

    def compute(n):
        rows = (n * CHUNK + ROW_STEP - 1) // ROW_STEP * ROW_STEP
        n_main = (jnp.maximum(rows - LAST_ROWS_MAX, 0) + EXPERT_ROW_TILE - 1) // EXPERT_ROW_TILE

        def tile_body(t, carry):
            mlp_rows(pl.multiple_of(t * EXPERT_ROW_TILE, ROW_STEP), EXPERT_ROW_TILE)
            return carry

        lax.fori_loop(0, n_main, tile_body, 0)
        r0 = pl.multiple_of(n_main * EXPERT_ROW_TILE, ROW_STEP)
        for size in range(ROW_STEP, LAST_ROWS_MAX + ROW_STEP, ROW_STEP):
            pl.when(rows - r0 == size)(functools.partial(mlp_rows, r0, size))

    def chunks_of(ex):
        return lax.fori_loop(0, _N_BLOCKS, lambda j, acc: acc + np_ref[j * N_EXPERTS + ex], 0)

    @pl.when(e == 0)
    def _():
        buf[...] = jnp.zeros_like(buf)
        for b in range(N_ROW_BUFS):
            pend[b] = 0
        move_chunks(0, 0, 0, True)

    wait_chunks(ssem.at[nxt], pend[nxt], nxt)
    pend[nxt] = 0

    @pl.when(e < last)
    def _():
        move_chunks(e + 1, 0, nxt, True)

    wu_s[...] = wu_ref[0].astype(BF16)
    wd_s[...] = wd_ref[0].astype(BF16)

    total = chunks_of(e)
    n0 = jnp.minimum(total, PASS_CHUNKS)
    wait_chunks(gsem.at[slot], n0, slot)
    compute(n0)
    move_chunks(e, 0, slot, False)
    pend[slot] = n0

    def pass_body(p, carry):
        wait_chunks(ssem.at[slot], pend[slot], slot)
        lo = p * PASS_CHUNKS
        n = jnp.minimum(total - lo, PASS_CHUNKS)
        move_chunks(e, lo, slot, True)
        wait_chunks(gsem.at[slot], n, slot)
        compute(n)
        move_chunks(e, lo, slot, False)
        pend[slot] = n
        return carry

    lax.fori_loop(1, (total + PASS_CHUNKS - 1) // PASS_CHUNKS, pass_body, 0)

    @pl.when(e == last)
    def _():
        for b in range(N_ROW_BUFS):
            wait_chunks(ssem.at[b], pend[b], b)
            pend[b] = 0


def _experts(np16, off16, w_up, b_up, w_down, b_down, xs):
    grid_spec = pltpu.PrefetchScalarGridSpec(
        num_scalar_prefetch=2,
        grid=(N_EXPERTS,),
        in_specs=[pl.BlockSpec((1, D_MODEL, 2 * D_FF), lambda e, *_: (e, 0, 0)),
                  pl.BlockSpec((1, 1, 2 * D_FF), lambda e, *_: (e, 0, 0)),
                  pl.BlockSpec((1, D_FF, D_MODEL), lambda e, *_: (e, 0, 0)),
                  pl.BlockSpec((1, 1, D_MODEL), lambda e, *_: (e, 0, 0)),
                  pl.BlockSpec(memory_space=pl.ANY)],
        out_specs=pl.BlockSpec(memory_space=pl.ANY),
        scratch_shapes=[pltpu.VMEM((N_ROW_BUFS, EXPERT_BUF_ROWS, D_MODEL), BF16),
                        pltpu.VMEM((D_MODEL, 2 * D_FF), BF16),
                        pltpu.VMEM((D_FF, D_MODEL), BF16),
                        pltpu.SemaphoreType.DMA((N_ROW_BUFS,)),
                        pltpu.SemaphoreType.DMA((N_ROW_BUFS,)),
                        pltpu.SMEM((N_ROW_BUFS,), jnp.int32)],
    )
    return pl.pallas_call(
        _expert_kernel,
        grid_spec=grid_spec,
        out_shape=jax.ShapeDtypeStruct(xs.shape, xs.dtype),
        input_output_aliases={6: 0},
        compiler_params=pltpu.CompilerParams(dimension_semantics=("arbitrary",),
                                             vmem_limit_bytes=V7X_VMEM_LIMIT),
        name="experts",
    )(np16, off16, w_up, b_up.reshape(N_EXPERTS, 1, 2 * D_FF), w_down,
      b_down.reshape(N_EXPERTS, 1, D_MODEL), xs)


def _combine_kernel(ys_ref, dk_ref, xc_ref, xl_ref, mod_ref, gfin_ref, yc_ref, yl_ref, y_s):
    j = pl.program_id(0)
    half = TOK_BLOCK // 2
    cio = lax.broadcasted_iota(jnp.int32, (half, SORT_ROWS), 1).astype(F32)
    halves = [slice(h * half, (h + 1) * half) for h in range(2)]
    accs = [jnp.zeros((half, D_MODEL), F32) for _ in halves]
    for c in range(BLOCK_CAP // SORT_ROWS):
        ys = ys_ref[c * SORT_ROWS:(c + 1) * SORT_ROWS, :]
        for h, rows in enumerate(halves):
            d = dk_ref[0, rows, :]
            w = jnp.zeros((half, SORT_ROWS), F32)
            for k in range(TOP_K):
                w = jnp.where(cio == d[:, k:k + 1] - float(c * SORT_ROWS), d[:, TOP_K + k:TOP_K + k + 1], w)
            accs[h] = accs[h] + _dot(w.astype(BF16), ys)
    for h, rows in enumerate(halves):
        acc = accs[h]
        x1 = jnp.where(j < _N_CTX_BLOCKS, xc_ref[rows, :], xl_ref[rows, :])
        x2 = x1 + mod_ref[0, 5:6, :] * acc
        ms = jnp.mean(x2 * x2, axis=-1, keepdims=True)
        y_s[rows, :] = x2 * lax.rsqrt(ms + RMS_EPS) * gfin_ref[...]

    @pl.when(j < _N_CTX_BLOCKS)
    def _():
        yc_ref[...] = y_s[...]

    @pl.when(j >= _N_CTX_BLOCKS)
    def _():
        yl_ref[...] = y_s[...]


def _combine(ys, dk_t, x1c, x1l, modv, g_final):
    return pl.pallas_call(
        _combine_kernel,
        grid=(_N_BLOCKS,),
        in_specs=[pl.BlockSpec((BLOCK_CAP, D_MODEL), lambda j: (j, 0)),
                  pl.BlockSpec((1, TOK_BLOCK, 2 * TOP_K), lambda j: (j, 0, 0))] + _token_specs() + [
                  _const_spec((1, D_MODEL))],
        out_specs=[pl.BlockSpec((TOK_BLOCK, D_MODEL), lambda j: (jnp.minimum(j, _N_CTX_BLOCKS - 1), 0)),
                   pl.BlockSpec((TOK_BLOCK, D_MODEL), lambda j: (jnp.maximum(j - _N_CTX_BLOCKS, 0), 0))],
        out_shape=[jax.ShapeDtypeStruct(x1c.shape, F32), jax.ShapeDtypeStruct(x1l.shape, F32)],
        scratch_shapes=[pltpu.VMEM((TOK_BLOCK, D_MODEL), F32)],
        compiler_params=pltpu.CompilerParams(dimension_semantics=("arbitrary",),
                                             vmem_limit_bytes=V7X_VMEM_LIMIT),
        name="combine",
    )(ys, dk_t, x1c, x1l, modv, g_final)


def kernel(x_prompt, x_sample, cache_k, cache_v, c, c_ctx, w_ada, b_ada, g_mix, w_in, w_pool, pool_scale,
           w_pa, w_pb, rpb, w_out, g_ffn, w_router, b_router, w_up, b_up, w_down, b_down, g_final):
    assert w_ada.shape[0] == 1, "single trunk layer"
    batch, seq, d = x_prompt.shape
    dec_batch, dec_seq, _ = x_sample.shape
    assert (seq, dec_seq, d) == (SEQ, DEC_SEQ, D_MODEL)
    assert batch * seq == _N_CTX_BLOCKS * TOK_BLOCK and dec_batch * dec_seq == _N_LAT_BLOCKS * TOK_BLOCK

    cmat = jnp.concatenate([c_ctx[None, :], c, jnp.zeros((8 - 1 - dec_batch, d), F32)], axis=0)
    modv = _modulation(cmat, w_ada[0], b_ada[0]).reshape(8, N_MOD, d)

    weights = (g_mix[0][None, :], w_in[0].astype(BF16), w_pool[0].astype(BF16), pool_scale[0][None, :],
               w_pa[0].astype(BF16), w_pb[0].astype(BF16), w_out[0].astype(BF16))

    def by_pair(cache):
        z = cache[:, 0].reshape(dec_batch, N_PAIRS, 2, PAST_LEN, HEAD_DIM)
        return z.transpose(0, 1, 3, 2, 4).reshape(dec_batch, N_PAIRS, PAST_LEN, PAIR_W)

    x1c, new_k, new_v = _ctx_mixer(x_prompt.reshape(batch * seq, d), modv, weights)
    x1l = _lat_mixer(x_sample.reshape(dec_batch * dec_seq, d), modv, weights,
                     by_pair(cache_k), by_pair(cache_v), _bias_inputs(rpb[0]))

    xs, dk, np16, off16 = _route_sort(x1c, x1l, modv, g_ffn[0][None, :],
                                      w_router[0].T.astype(BF16), b_router[0][:, None])
    np16_i = np16[:, :, 0].astype(jnp.int32).reshape(-1)
    off16_i = off16[:, :, 0].astype(jnp.int32).reshape(-1)
    ys = _experts(np16_i, off16_i, w_up[0], b_up[0], w_down[0], b_down[0], xs)
    yc, yl = _combine(ys, dk.transpose(0, 2, 1), x1c, x1l, modv, g_final[None, :])
    return (yc.reshape(batch, seq, d), yl.reshape(dec_batch, dec_seq, d), new_k, new_v)
```

```python
import functools

import jax
import jax.numpy as jnp
import numpy as np
from jax import lax
from jax.experimental import pallas as pl
from jax.experimental.pallas import tpu as pltpu

F32 = jnp.float32
BF16 = jnp.bfloat16

D_MODEL = 1024
SEQ = 256
DEC_SEQ = 1024
GRID_W = 64
ROWS = DEC_SEQ // GRID_W
N_HEADS = 8
HEAD_DIM = 64
N_PAIRS = N_HEADS // 2
PAIR_W = 2 * HEAD_DIM
PAST_LEN = 512
POOL_DIM = 512
POOL_WINDOWS = (2, 4, 8, 16)
POOL_GROUP_DIM = 128
ATTN_DIM = 512
WIN_R = 8
WIN_C = 16
N_EXPERTS = 32
TOP_K = 4
D_FF = 1024
SWIGLU_LIMIT = 7.0
SWIGLU_ALPHA = 1.702
N_MOD = 6
RMS_EPS = 1e-6
NEG_INF = -1e30
ATTN_SCALE = HEAD_DIM ** -0.5
IN_DIM = POOL_DIM + 3 * ATTN_DIM + 2 * D_MODEL

TOK_BLOCK = 512
CHUNK = 16
BLOCK_CAP = TOK_BLOCK * TOP_K + N_EXPERTS * CHUNK
ROW_STEP = 128
EXPERT_ROW_TILE = 384
LAST_ROWS_MAX = 512
V7X_VMEM_LIMIT = 60 * 1024 * 1024

_N_CTX_BLOCKS = 8
_N_LAT_BLOCKS = 4
_N_BLOCKS = _N_CTX_BLOCKS + _N_LAT_BLOCKS

_NT = (((1,), (1,)), ((), ()))


def _dot(a, b):
    return jnp.dot(a, b, preferred_element_type=F32)


def _dot_nt(a, b):
    return lax.dot_general(a, b, _NT, preferred_element_type=F32)


def _sigmoid(x):
    return 1.0 / (1.0 + jnp.exp(-x))


def _norm_mod(x, gain, scale, shift):
    ms = jnp.mean(x * x, axis=-1, keepdims=True)
    return (x * lax.rsqrt(ms + RMS_EPS) * gain) * (1.0 + scale) + shift


def _const_spec(shape):
    zeros = (0,) * len(shape)
    return pl.BlockSpec(shape, lambda *_: zeros, pipeline_mode=pl.Buffered(1))


MOD_COLS = 1536


def _mod_kernel(c_ref, w_ref, b_ref, o_ref):
    c = c_ref[...]
    s = c * _sigmoid(c)
    s_hi = s.astype(BF16)
    s_lo = (s - s_hi.astype(F32)).astype(BF16)
    r = _dot(jnp.concatenate([s_hi, s_lo], axis=0), w_ref[...].astype(BF16))
    o_ref[...] = r[:8] + r[8:] + b_ref[...]


def _modulation(cmat, w_ada, b_ada):
    n = w_ada.shape[1]
    return pl.pallas_call(
        _mod_kernel,
        grid=(n // MOD_COLS,),
        in_specs=[pl.BlockSpec((8, D_MODEL), lambda i: (0, 0)),
                  pl.BlockSpec((D_MODEL, MOD_COLS), lambda i: (0, i)),
                  pl.BlockSpec((1, MOD_COLS), lambda i: (0, i))],
        out_specs=pl.BlockSpec((8, MOD_COLS), lambda i: (0, i)),
        out_shape=jax.ShapeDtypeStruct((8, n), F32),
        name="modulation",
    )(cmat, w_ada, b_ada.reshape(1, n))


def _pool_mix(u, pos, seq):
    n = u.shape[0]

    def down(x, d):
        return jnp.where(pos >= d, pltpu.roll(x, d, 0), 0.0)

    def up(x, d):
        return jnp.where(pos < seq - d, pltpu.roll(x, n - d, 0), 0.0)

    return down, up


def _pool_group(u, pos, seq, w):
    down, up = _pool_mix(u, pos, seq)
    hw = w // 2
    back = u
    fwd = u
    d = 1
    while d < hw:
        back = back + down(back, d)
        fwd = fwd + up(fwd, d)
        d *= 2
    s = down(back, 1) + fwd
    posf = pos.astype(F32)
    cnt = jnp.minimum(posf + hw, float(seq)) - jnp.maximum(posf - hw, 0.0)
    return s / cnt - u


def _pass_rows(tile):
    return tile if tile <= TOK_BLOCK else tile // 4


def _mixer_front(x_ref, mod_ref, gmix_ref, win_ref, hb_s, u_s, q_s, k_s, v_s):
    shift, scale = mod_ref[0, 0:1, :], mod_ref[0, 1:2, :]
    step = _pass_rows(x_ref.shape[0])
    for c in range(x_ref.shape[0] // step):
        rows = slice(c * step, (c + 1) * step)
        hb = _norm_mod(x_ref[rows, :], gmix_ref[...], scale, shift).astype(BF16)
        hb_s[rows, :] = hb
        u_s[rows, :] = _dot(hb, win_ref[:, 0:POOL_DIM])
        for dst, base in ((q_s, POOL_DIM), (k_s, POOL_DIM + ATTN_DIM), (v_s, POOL_DIM + 2 * ATTN_DIM)):
            z = _dot(hb, win_ref[:, base:base + ATTN_DIM])
            if dst is q_s:
                z = z * ATTN_SCALE
            for g in range(N_PAIRS):
                dst[g, rows, :] = z[:, g * PAIR_W:(g + 1) * PAIR_W].astype(dst.dtype)


def _mixer_back(x_ref, mod_ref, win_ref, wpool_ref, ps_ref, wpa_ref, wpb_ref, wout_ref, x1_ref,
                hb_s, u_s, o_s, pg_s, seq):
    tile = x_ref.shape[0]
    gate = mod_ref[0, 2:3, :]
    pos = lax.broadcasted_iota(jnp.int32, (tile, 1), 0) % seq
    for g, w in enumerate(POOL_WINDOWS):
        cols = slice(g * POOL_GROUP_DIM, (g + 1) * POOL_GROUP_DIM)
        pg_s[:, cols] = _pool_group(u_s[:, cols], pos, seq, w).astype(BF16)
    step = _pass_rows(tile)
    for c in range(tile // step):
        rows = slice(c * step, (c + 1) * step)
        ys = []
        for g in range(len(POOL_WINDOWS)):
            cols = slice(g * POOL_GROUP_DIM, (g + 1) * POOL_GROUP_DIM)
            ys.append((_dot(pg_s[rows, cols], wpool_ref[g]) * ps_ref[:, cols]).astype(BF16))
        a = _dot(jnp.concatenate(ys, axis=1), wpa_ref[...])
        ob = _dot(jnp.concatenate([o_s[g, rows, :] for g in range(N_PAIRS)], axis=1), wpb_ref[...])
        gab = _dot(hb_s[rows, :], win_ref[:, POOL_DIM + 3 * ATTN_DIM:IN_DIM])
        merged = _sigmoid(gab[:, :D_MODEL]) * a + _sigmoid(gab[:, D_MODEL:]) * ob
        mix = _dot(merged.astype(BF16), wout_ref[...])
        x1_ref[rows, :] = x_ref[rows, :] + gate * mix


def _ctx_mixer_kernel(x_ref, mod_ref, gmix_ref, win_ref, wpool_ref, ps_ref, wpa_ref, wpb_ref, wout_ref,
                      x1_ref, ko_ref, vo_ref, hb_s, u_s, q_s, k_s, v_s, o_s, pg_s):
    _mixer_front(x_ref, mod_ref, gmix_ref, win_ref, hb_s, u_s, q_s, k_s, v_s)
    even = lax.broadcasted_iota(jnp.int32, (1, PAIR_W), 1) < HEAD_DIM
    tile = x_ref.shape[0]
    for s in range(tile // SEQ):
        rows = slice(s * SEQ, (s + 1) * SEQ)
        for g in range(N_PAIRS):
            q2, k2, v2 = q_s[g, rows, :], k_s[g, rows, :], v_s[g, rows, :]
            ko_ref[s, 0, 2 * g] = k2[:, :HEAD_DIM]
            ko_ref[s, 0, 2 * g + 1] = k2[:, HEAD_DIM:]
            vo_ref[s, 0, 2 * g] = v2[:, :HEAD_DIM]
            vo_ref[s, 0, 2 * g + 1] = v2[:, HEAD_DIM:]
            kb, vb = k2.astype(BF16), v2.astype(BF16)
            outs = []
            for par in range(2):
                qm = jnp.where(even if par == 0 else jnp.logical_not(even), q2, 0.0).astype(BF16)
                sc = _dot_nt(qm, kb)
                m = jnp.max(sc, axis=-1, keepdims=True)
                p = jnp.exp(sc - m)
                l = jnp.sum(p, axis=-1, keepdims=True)
                outs.append(_dot(p.astype(BF16), vb) / l)
            o_s[g, rows, :] = jnp.where(even, outs[0], outs[1]).astype(BF16)
    _mixer_back(x_ref, mod_ref, win_ref, wpool_ref, ps_ref, wpa_ref, wpb_ref, wout_ref, x1_ref,
                hb_s, u_s, o_s, pg_s, SEQ)


def _row_window(r):
    rs = min(max(r - WIN_R // 2, 0), ROWS - WIN_R)
    return rs, rs - r + WIN_R - 1


def _lat_mixer_kernel(x_ref, mod_ref, gmix_ref, win_ref, wpool_ref, ps_ref, wpa_ref, wpb_ref, wout_ref,
                      kc_ref, vc_ref, bv_ref, keep_ref, x1_ref, hb_s, u_s, q_s, k_s, v_s, o_s, pg_s, tb_ref):
    @pl.when(pl.program_id(0) == 0)
    def _():
        _bias_kernel(bv_ref, keep_ref, tb_ref)

    _mixer_front(x_ref, mod_ref, gmix_ref, win_ref, hb_s, u_s, q_s, k_s, v_s)
    even = lax.broadcasted_iota(jnp.int32, (1, PAIR_W), 1) < HEAD_DIM
    nk = WIN_R * GRID_W

    def pair_body(g, carry):
        q2 = q_s[g]
        kb, vb = k_s[g].astype(BF16), v_s[g].astype(BF16)
        kcb, vcb = kc_ref[0, g].astype(BF16), vc_ref[0, g].astype(BF16)
        outs = []
        for par in range(2):
            qm = jnp.where(even if par == 0 else jnp.logical_not(even), q2, 0.0).astype(BF16)
            s_ctx = _dot_nt(qm, kcb)
            slabs = []
            for r in range(ROWS):
                rs, rho = _row_window(r)
                bias = tb_ref[2 * g + par, rho % 2, :, (rho - rho % 2) * GRID_W:(rho - rho % 2) * GRID_W + nk]
                sl = _dot_nt(qm[r * GRID_W:(r + 1) * GRID_W, :], kb[rs * GRID_W:rs * GRID_W + nk, :])
                slabs.append(sl + bias)
            s_loc = jnp.concatenate(slabs, axis=0)
            m = jnp.maximum(jnp.max(s_loc, axis=-1, keepdims=True), jnp.max(s_ctx, axis=-1, keepdims=True))
            p_loc = jnp.exp(s_loc - m)
            p_ctx = jnp.exp(s_ctx - m)
            l = jnp.sum(p_loc, axis=-1, keepdims=True) + jnp.sum(p_ctx, axis=-1, keepdims=True)
            p_locb = p_loc.astype(BF16)
            o_rows = []
            for r in range(ROWS):
                rs, _ = _row_window(r)
                o_rows.append(_dot(p_locb[r * GRID_W:(r + 1) * GRID_W, :], vb[rs * GRID_W:rs * GRID_W + nk, :]))
            o = jnp.concatenate(o_rows, axis=0) + _dot(p_ctx.astype(BF16), vcb)
            outs.append(o / l)
        o_s[g] = jnp.where(even, outs[0], outs[1]).astype(BF16)
        return carry

    lax.fori_loop(0, N_PAIRS, pair_body, 0)
    _mixer_back(x_ref, mod_ref, win_ref, wpool_ref, ps_ref, wpa_ref, wpb_ref, wout_ref, x1_ref,
                hb_s, u_s, o_s, pg_s, DEC_SEQ)


def _mixer_scratch(tile, kv_dtype):
    return [pltpu.VMEM((tile, D_MODEL), BF16),
            pltpu.VMEM((tile, POOL_DIM), F32),
            pltpu.VMEM((N_PAIRS, tile, PAIR_W), BF16),
            pltpu.VMEM((N_PAIRS, tile, PAIR_W), kv_dtype),
            pltpu.VMEM((N_PAIRS, tile, PAIR_W), kv_dtype),
            pltpu.VMEM((N_PAIRS, tile, PAIR_W), BF16),
            pltpu.VMEM((tile, POOL_DIM), BF16)]


def _weight_specs():
    return [_const_spec((1, D_MODEL)),
            _const_spec((D_MODEL, IN_DIM)),
            _const_spec((len(POOL_WINDOWS), POOL_GROUP_DIM, POOL_GROUP_DIM)),
            _const_spec((1, POOL_DIM)),
            _const_spec((POOL_DIM, D_MODEL)),
            _const_spec((ATTN_DIM, D_MODEL)),
            _const_spec((D_MODEL, D_MODEL))]


def _ctx_mixer(x, modv, weights):
    n = x.shape[0]
    nseq = TOK_BLOCK // SEQ
    cache = jax.ShapeDtypeStruct((n // SEQ, 1, N_HEADS, SEQ, HEAD_DIM), F32)
    cache_spec = pl.BlockSpec((nseq, 1, N_HEADS, SEQ, HEAD_DIM), lambda i: (i, 0, 0, 0, 0))
    return pl.pallas_call(
        _ctx_mixer_kernel,
        grid=(n // TOK_BLOCK,),
        in_specs=[pl.BlockSpec((TOK_BLOCK, D_MODEL), lambda i: (i, 0)),
                  pl.BlockSpec((1, N_MOD, D_MODEL), lambda i: (0, 0, 0))] + _weight_specs(),
        out_specs=[pl.BlockSpec((TOK_BLOCK, D_MODEL), lambda i: (i, 0)), cache_spec, cache_spec],
        out_shape=[jax.ShapeDtypeStruct((n, D_MODEL), F32), cache, cache],
        scratch_shapes=_mixer_scratch(TOK_BLOCK, F32),
        compiler_params=pltpu.CompilerParams(dimension_semantics=("arbitrary",),
                                             vmem_limit_bytes=V7X_VMEM_LIMIT),
        name="ctx_mixer",
    )(x, modv, *weights)


def _lat_mixer(x, modv, weights, kc, vc, tb):
    n = x.shape[0]
    return pl.pallas_call(
        _lat_mixer_kernel,
        grid=(n // DEC_SEQ,),
        in_specs=[pl.BlockSpec((DEC_SEQ, D_MODEL), lambda i: (i, 0)),
                  pl.BlockSpec((1, N_MOD, D_MODEL), lambda i: (i + 1, 0, 0))] + _weight_specs() + [
                  pl.BlockSpec((1, N_PAIRS, PAST_LEN, PAIR_W), lambda i: (i, 0, 0, 0)),
                  pl.BlockSpec((1, N_PAIRS, PAST_LEN, PAIR_W), lambda i: (i, 0, 0, 0)),
                  _const_spec((N_HEADS, 2, ROWS * GRID_W)),
                  _const_spec((2, GRID_W, ROWS * GRID_W))],
        out_specs=pl.BlockSpec((DEC_SEQ, D_MODEL), lambda i: (i, 0)),
        out_shape=jax.ShapeDtypeStruct((n, D_MODEL), F32),
        scratch_shapes=_mixer_scratch(DEC_SEQ, BF16) + [pltpu.VMEM((N_HEADS, 2, GRID_W, ROWS * GRID_W), F32)],
        compiler_params=pltpu.CompilerParams(dimension_semantics=("arbitrary",),
                                             vmem_limit_bytes=V7X_VMEM_LIMIT),
        name="lat_mixer",
    )(x, modv, *weights, kc, vc, *tb)


N_RPB_ROWS = 2 * WIN_R - 1
N_RPB_COLS = 2 * WIN_C - 1
TABLE_W = ROWS * GRID_W


def _bias_kernel(v_ref, keep_ref, o_ref):
    for h in range(N_HEADS):
        for par in range(2):
            x = jnp.broadcast_to(v_ref[h, par:par + 1, :], (GRID_W, TABLE_W))
            shifted = pltpu.roll(x, TABLE_W - (WIN_C - 1), 1, stride=1, stride_axis=0)
            o_ref[h, par] = jnp.where(keep_ref[par] > 0.0, shifted, NEG_INF)


def _bias_inputs(rpb):
    col = np.arange(GRID_W)
    cs = np.clip(col - WIN_C // 2, 0, GRID_W - WIN_C)
    in_win = (col[None, :] >= cs[:, None]) & (col[None, :] < cs[:, None] + WIN_C)
    keep = np.tile(in_win.astype(np.float32), (2, 1, ROWS))
    keep[0, :, N_RPB_ROWS * GRID_W:] = 0.0
    keep[1, :, (N_RPB_ROWS - 1) * GRID_W:] = 0.0
    rp = jnp.pad(rpb.astype(F32), ((0, 0), (0, ROWS + 1 - N_RPB_ROWS), (0, GRID_W - N_RPB_COLS)))
    v = jnp.stack([rp[:, :ROWS].reshape(N_HEADS, TABLE_W), rp[:, 1:].reshape(N_HEADS, TABLE_W)], axis=1)
    return v, jnp.asarray(keep)


SORT_ROWS = 512


def _sort_kernel(xc_ref, xl_ref, mod_ref, gffn_ref, wr_ref, br_ref,
                 xs_ref, dk_ref, np_ref, off_ref):
    j = pl.program_id(0)
    x = jnp.where(j < _N_CTX_BLOCKS, xc_ref[...], xl_ref[...])
    shift, scale = mod_ref[0, 3:4, :], mod_ref[0, 4:5, :]
    hb = _norm_mod(x, gffn_ref[...], scale, shift).astype(BF16)
    logits = _dot_nt(wr_ref[...], hb) + br_ref[...]
    eio = lax.broadcasted_iota(jnp.int32, logits.shape, 0)
    work = logits
    sels, vals = [], []
    for _ in range(TOP_K):
        m = jnp.max(work, axis=0, keepdims=True)
        idx = jnp.min(jnp.where(work == m, eio, N_EXPERTS), axis=0, keepdims=True)
        sel = eio == idx
        sels.append(sel)
        vals.append(m)
        work = jnp.where(sel, -jnp.inf, work)
    exps = [jnp.exp(v - vals[0]) for v in vals]
    den = exps[0] + exps[1] + exps[2] + exps[3]
    mask = jnp.zeros(logits.shape, F32)
    for sel in sels:
        mask = mask + jnp.where(sel, 1.0, 0.0)
    t_row = lax.broadcasted_iota(jnp.int32, (TOK_BLOCK, TOK_BLOCK), 0)
    t_col = lax.broadcasted_iota(jnp.int32, (TOK_BLOCK, TOK_BLOCK), 1)
    before = jnp.where(t_row < t_col, 1.0, 0.0).astype(BF16)
    rank = _dot(mask.astype(BF16), before)
    cnt = jnp.sum(mask, axis=1, keepdims=True)
    np16 = jnp.floor((cnt + (CHUNK - 1.0)) * (1.0 / CHUNK))
    e_row = lax.broadcasted_iota(jnp.int32, (N_EXPERTS, N_EXPERTS), 0)
    e_col = lax.broadcasted_iota(jnp.int32, (N_EXPERTS, N_EXPERTS), 1)
    lower = jnp.where(e_col < e_row, 1.0, 0.0).astype(BF16)
    np16_b = jnp.broadcast_to(np16, (N_EXPERTS, 128))
    off16 = _dot(lower, np16_b.astype(BF16))
    np_ref[0] = np16_b
    off_ref[0] = off16
    dest = off16[:, 0:1] * float(CHUNK) + rank
    dests = []
    for k in range(TOP_K):
        dk = jnp.sum(jnp.where(sels[k], dest, 0.0), axis=0, keepdims=True)
        dk_ref[0, k:k + 1, :] = dk
        dk_ref[0, TOP_K + k:TOP_K + k + 1, :] = exps[k] / den
        dests.append(dk.astype(jnp.int32))
    rio = lax.broadcasted_iota(jnp.int32, (SORT_ROWS, TOK_BLOCK), 0)
    for c in range(BLOCK_CAP // SORT_ROWS):
        onehot = jnp.zeros((SORT_ROWS, TOK_BLOCK), F32)
        for dk in dests:
            onehot = jnp.where(rio == dk - c * SORT_ROWS, 1.0, onehot)
        xs_ref[c * SORT_ROWS:(c + 1) * SORT_ROWS, :] = _dot(onehot.astype(BF16), hb).astype(BF16)


def _block_mod_index(j):
    return jnp.where(j < _N_CTX_BLOCKS, 0, 1 + (j - _N_CTX_BLOCKS) // (DEC_SEQ // TOK_BLOCK))


def _token_specs():
    return [pl.BlockSpec((TOK_BLOCK, D_MODEL), lambda j: (jnp.minimum(j, _N_CTX_BLOCKS - 1), 0)),
            pl.BlockSpec((TOK_BLOCK, D_MODEL), lambda j: (jnp.maximum(j - _N_CTX_BLOCKS, 0), 0)),
            pl.BlockSpec((1, N_MOD, D_MODEL), lambda j: (_block_mod_index(j), 0, 0))]


def _route_sort(x1c, x1l, modv, g_ffn, wr_t, br):
    tbl = jax.ShapeDtypeStruct((_N_BLOCKS, N_EXPERTS, 128), F32)
    tbl_spec = pl.BlockSpec((1, N_EXPERTS, 128), lambda j: (j, 0, 0))
    return pl.pallas_call(
        _sort_kernel,
        grid=(_N_BLOCKS,),
        in_specs=_token_specs() + [_const_spec((1, D_MODEL)),
                                   _const_spec((N_EXPERTS, D_MODEL)),
                                   _const_spec((N_EXPERTS, 1))],
        out_specs=[pl.BlockSpec((BLOCK_CAP, D_MODEL), lambda j: (j, 0)),
                   pl.BlockSpec((1, 2 * TOP_K, TOK_BLOCK), lambda j: (j, 0, 0)),
                   tbl_spec, tbl_spec],
        out_shape=[jax.ShapeDtypeStruct((_N_BLOCKS * BLOCK_CAP, D_MODEL), BF16),
                   jax.ShapeDtypeStruct((_N_BLOCKS, 2 * TOP_K, TOK_BLOCK), F32),
                   tbl, tbl],
        compiler_params=pltpu.CompilerParams(dimension_semantics=("arbitrary",),
                                             vmem_limit_bytes=V7X_VMEM_LIMIT),
        name="route_sort",
    )(x1c, x1l, modv, g_ffn, wr_t, br)


EXPERT_BUF_ROWS = 2048
PASS_CHUNKS = EXPERT_BUF_ROWS // CHUNK
N_ROW_BUFS = 3


def _expert_kernel(np_ref, off_ref, wu_ref, bu_ref, wd_ref, bd_ref, xs_hbm, ys_hbm,
                   buf, wu_s, wd_s, gsem, ssem, pend):
    del xs_hbm
    e = pl.program_id(0)
    last = pl.num_programs(0) - 1
    slot = e % N_ROW_BUFS
    nxt = (e + 1) % N_ROW_BUFS

    def move_chunks(ex, q_lo, slot_, gather):
        def one_block(j, q0, priority):
            n = np_ref[j * N_EXPERTS + ex]
            off = off_ref[j * N_EXPERTS + ex]
            c_lo = jnp.clip(q_lo - q0, 0, n)
            m = jnp.clip(q_lo + PASS_CHUNKS - q0, 0, n) - c_lo

            @pl.when(m > 0)
            def _():
                rows = pl.multiple_of(m * CHUNK, CHUNK)
                row0 = pl.multiple_of(j * BLOCK_CAP + (off + c_lo) * CHUNK, CHUNK)
                brow0 = pl.multiple_of((q0 + c_lo - q_lo) * CHUNK, CHUNK)
                hbm = ys_hbm.at[pl.ds(row0, rows), :]
                vm = buf.at[slot_, pl.ds(brow0, rows), :]
                if gather:
                    pltpu.make_async_copy(hbm, vm, gsem.at[slot_]).start(priority=priority)
                else:
                    pltpu.make_async_copy(vm, hbm, ssem.at[slot_]).start(priority=priority)

            return q0 + n

        def pair_of_blocks(i, q0):
            return one_block(2 * i + 1, one_block(2 * i, q0, 0), 1)

        lax.fori_loop(0, _N_BLOCKS // 2, pair_of_blocks, 0)

    def wait_chunks(sem, n, slot_):
        @pl.when(n > 0)
        def _():
            rows = pl.multiple_of(n * CHUNK, CHUNK)
            pltpu.make_async_copy(ys_hbm.at[pl.ds(0, rows), :], buf.at[slot_, pl.ds(0, rows), :], sem).wait()

    def mlp_rows(r0, rows):
        x = buf[slot, pl.ds(r0, rows), :]
        gu = _dot(x, wu_s[...]) + bu_ref[0]
        gate = jnp.minimum(gu[:, :D_FF], SWIGLU_LIMIT)
        up = jnp.clip(gu[:, D_FF:], -SWIGLU_LIMIT, SWIGLU_LIMIT)
        glu = gate * _sigmoid(SWIGLU_ALPHA * gate)
        y = _dot(((up + 1.0) * glu).astype(BF16), wd_s[...]) + bd_ref[0]
        buf[slot, pl.ds(r0, rows), :] = y.astype(BF16)

    def compute(n):
        rows = (n * CHUNK + ROW_STEP - 1) // ROW_STEP * ROW_STEP
        n_main = (jnp.maximum(rows - LAST_ROWS_MAX, 0) + EXPERT_ROW_TILE - 1) // EXPERT_ROW_TILE

        def tile_body(t, carry):
            mlp_rows(pl.multiple_of(t * EXPERT_ROW_TILE, ROW_STEP), EXPERT_ROW_TILE)
            return carry

        lax.fori_loop(0, n_main, tile_body, 0)
        r0 = pl.multiple_of(n_main * EXPERT_ROW_TILE, ROW_STEP)
        for size in range(ROW_STEP, LAST_ROWS_MAX + ROW_STEP, ROW_STEP):
            pl.when(rows - r0 == size)(functools.partial(mlp_rows, r0, size))

    def chunks_of(ex):
        return lax.fori_loop(0, _N_BLOCKS, lambda j, acc: acc + np_ref[j * N_EXPERTS + ex], 0)

    @pl.when(e == 0)
    def _():
        buf[...] = jnp.zeros_like(buf)
        for b in range(N_ROW_BUFS):
            pend[b] = 0
        move_chunks(0, 0, 0, True)

    wait_chunks(ssem.at[nxt], pend[nxt], nxt)
    pend[nxt] = 0

    @pl.when(e < last)
    def _():
        move_chunks(e + 1, 0, nxt, True)

    wu_s[...] = wu_ref[0].astype(BF16)
    wd_s[...] = wd_ref[0].astype(BF16)

    total = chunks_of(e)
    n0 = jnp.minimum(total, PASS_CHUNKS)
    wait_chunks(gsem.at[slot], n0, slot)
    compute(n0)
    move_chunks(e, 0, slot, False)
    pend[slot] = n0

    def pass_body(p, carry):
        wait_chunks(ssem.at[slot], pend[slot], slot)
        lo = p * PASS_CHUNKS
        n = jnp.minimum(total - lo, PASS_CHUNKS)
        move_chunks(e, lo, slot, True)
        wait_chunks(gsem.at[slot], n, slot)
        compute(n)
        move_chunks(e, lo, slot, False)
        pend[slot] = n
        return carry

    lax.fori_loop(1, (total + PASS_CHUNKS - 1) // PASS_CHUNKS, pass_body, 0)

    @pl.when(e == last)
    def _():
        for b in range(N_ROW_BUFS):
            wait_chunks(ssem.at[b], pend[b], b)
            pend[b] = 0


def _experts(np16, off16, w_up, b_up, w_down, b_down, xs):
    grid_spec = pltpu.PrefetchScalarGridSpec(
        num_scalar_prefetch=2,
        grid=(N_EXPERTS,),
        in_specs=[pl.BlockSpec((1, D_MODEL, 2 * D_FF), lambda e, *_: (e, 0, 0)),
                  pl.BlockSpec((1, 1, 2 * D_FF), lambda e, *_: (e, 0, 0)),
                  pl.BlockSpec((1, D_FF, D_MODEL), lambda e, *_: (e, 0, 0)),
                  pl.BlockSpec((1, 1, D_MODEL), lambda e, *_: (e, 0, 0)),
                  pl.BlockSpec(memory_space=pl.ANY)],
        out_specs=pl.BlockSpec(memory_space=pl.ANY),
        scratch_shapes=[pltpu.VMEM((N_ROW_BUFS, EXPERT_BUF_ROWS, D_MODEL), BF16),
                        pltpu.VMEM((D_MODEL, 2 * D_FF), BF16),
                        pltpu.VMEM((D_FF, D_MODEL), BF16),
                        pltpu.SemaphoreType.DMA((N_ROW_BUFS,)),
                        pltpu.SemaphoreType.DMA((N_ROW_BUFS,)),
                        pltpu.SMEM((N_ROW_BUFS,), jnp.int32)],
    )
    return pl.pallas_call(
        _expert_kernel,
        grid_spec=grid_spec,
        out_shape=jax.ShapeDtypeStruct(xs.shape, xs.dtype),
        input_output_aliases={6: 0},
        compiler_params=pltpu.CompilerParams(dimension_semantics=("arbitrary",),
                                             vmem_limit_bytes=V7X_VMEM_LIMIT),
        name="experts",
    )(np16, off16, w_up, b_up.reshape(N_EXPERTS, 1, 2 * D_FF), w_down,
      b_down.reshape(N_EXPERTS, 1, D_MODEL), xs)


def _combine_kernel(ys_ref, dk_ref, xc_ref, xl_ref, mod_ref, gfin_ref, yc_ref, yl_ref, y_s):
    j = pl.program_id(0)
    half = TOK_BLOCK // 2
    cio = lax.broadcasted_iota(jnp.int32, (half, SORT_ROWS), 1).astype(F32)
    halves = [slice(h * half, (h + 1) * half) for h in range(2)]
    accs = [jnp.zeros((half, D_MODEL), F32) for _ in halves]
    for c in range(BLOCK_CAP // SORT_ROWS):
        ys = ys_ref[c * SORT_ROWS:(c + 1) * SORT_ROWS, :]
        for h, rows in enumerate(halves):
            d = dk_ref[0, rows, :]
            w = jnp.zeros((half, SORT_ROWS), F32)
            for k in range(TOP_K):
                w = jnp.where(cio == d[:, k:k + 1] - float(c * SORT_ROWS), d[:, TOP_K + k:TOP_K + k + 1], w)
            accs[h] = accs[h] + _dot(w.astype(BF16), ys)
    for h, rows in enumerate(halves):
        acc = accs[h]
        x1 = jnp.where(j < _N_CTX_BLOCKS, xc_ref[rows, :], xl_ref[rows, :])
        x2 = x1 + mod_ref[0, 5:6, :] * acc
        ms = jnp.mean(x2 * x2, axis=-1, keepdims=True)
        y_s[rows, :] = x2 * lax.rsqrt(ms + RMS_EPS) * gfin_ref[...]

    @pl.when(j < _N_CTX_BLOCKS)
    def _():
        yc_ref[...] = y_s[...]

    @pl.when(j >= _N_CTX_BLOCKS)
    def _():
        yl_ref[...] = y_s[...]


def _combine(ys, dk_t, x1c, x1l, modv, g_final):
    return pl.pallas_call(
        _combine_kernel,
        grid=(_N_BLOCKS,),
        in_specs=[pl.BlockSpec((BLOCK_CAP, D_MODEL), lambda j: (j, 0)),
                  pl.BlockSpec((1, TOK_BLOCK, 2 * TOP_K), lambda j: (j, 0, 0))] + _token_specs() + [
                  _const_spec((1, D_MODEL))],
        out_specs=[pl.BlockSpec((TOK_BLOCK, D_MODEL), lambda j: (jnp.minimum(j, _N_CTX_BLOCKS - 1), 0)),
                   pl.BlockSpec((TOK_BLOCK, D_MODEL), lambda j: (jnp.maximum(j - _N_CTX_BLOCKS, 0), 0))],
        out_shape=[jax.ShapeDtypeStruct(x1c.shape, F32), jax.ShapeDtypeStruct(x1l.shape, F32)],
        scratch_shapes=[pltpu.VMEM((TOK_BLOCK, D_MODEL), F32)],
        compiler_params=pltpu.CompilerParams(dimension_semantics=("arbitrary",),
                                             vmem_limit_bytes=V7X_VMEM_LIMIT),
        name="combine",
    )(ys, dk_t, x1c, x1l, modv, g_final)


def kernel(x_prompt, x_sample, cache_k, cache_v, c, c_ctx, w_ada, b_ada, g_mix, w_in, w_pool, pool_scale,
           w_pa, w_pb, rpb, w_out, g_ffn, w_router, b_router, w_up, b_up, w_down, b_down, g_final):
    assert w_ada.shape[0] == 1, "single trunk layer"
    batch, seq, d = x_prompt.shape
    dec_batch, dec_seq, _ = x_sample.shape
    assert (seq, dec_seq, d) == (SEQ, DEC_SEQ, D_MODEL)
    assert batch * seq == _N_CTX_BLOCKS * TOK_BLOCK and dec_batch * dec_seq == _N_LAT_BLOCKS * TOK_BLOCK

    cmat = jnp.concatenate([c_ctx[None, :], c, jnp.zeros((8 - 1 - dec_batch, d), F32)], axis=0)
    modv = _modulation(cmat, w_ada[0], b_ada[0]).reshape(8, N_MOD, d)

    weights = (g_mix[0][None, :], w_in[0].astype(BF16), w_pool[0].astype(BF16), pool_scale[0][None, :],
               w_pa[0].astype(BF16), w_pb[0].astype(BF16), w_out[0].astype(BF16))

    def by_pair(cache):
        z = cache[:, 0].reshape(dec_batch, N_PAIRS, 2, PAST_LEN, HEAD_DIM)
        return z.transpose(0, 1, 3, 2, 4).reshape(dec_batch, N_PAIRS, PAST_LEN, PAIR_W)

    x1c, new_k, new_v = _ctx_mixer(x_prompt.reshape(batch * seq, d), modv, weights)
    x1l = _lat_mixer(x_sample.reshape(dec_batch * dec_seq, d), modv, weights,
                     by_pair(cache_k), by_pair(cache_v), _bias_inputs(rpb[0]))

    xs, dk, np16, off16 = _route_sort(x1c, x1l, modv, g_ffn[0][None, :],
                                      w_router[0].T.astype(BF16), b_router[0][:, None])
    np16_i = np16[:, :, 0].astype(jnp.int32).reshape(-1)
    off16_i = off16[:, :, 0].astype(jnp.int32).reshape(-1)
    ys = _experts(np16_i, off16_i, w_up[0], b_up[0], w_down[0], b_down[0], xs)
    yc, yl = _combine(ys, dk.transpose(0, 2, 1), x1c, x1l, modv, g_final[None, :])
    return (yc.reshape(batch, seq, d), yl.reshape(dec_batch, dec_seq, d), new_k, new_v)
```
